```python
import math
import jax, jax.numpy as jnp
from jax import lax
import numpy as np

D_MODEL = 1024
BATCH = 4
SEQ = 4096
DEPTH = 2
DEC_BATCH = 32
DEC_SEQ = 8
PAST_LEN = 8192
PAGE_SIZE = 128

N_MIXERS = 2
N_ATTN_LAYERS = (DEPTH + N_MIXERS - 1) // N_MIXERS
N_SSM_LAYERS = DEPTH // N_MIXERS

ATT_HEADS = 16
ATT_HEAD_DIM = 64
ATT_WIDTH = ATT_HEADS * ATT_HEAD_DIM
DIL_GROUPS = ((128, 1), (512, 4), (2048, 16))
N_DIL = len(DIL_GROUPS)
BAND_BLOCK = 128
ATT_IN = 3 * N_DIL * ATT_WIDTH + ATT_WIDTH

SSM_D_INNER = 2 * D_MODEL
SSM_HEAD_DIM = 64
SSM_HEADS = SSM_D_INNER // SSM_HEAD_DIM
SSM_STATE = 128
SSM_GROUPS = 4
SSM_HEADS_PER_GROUP = SSM_HEADS // SSM_GROUPS
SSM_CONV = 4
SSM_CHUNK = 128
SSM_CONV_DIM = SSM_D_INNER + 2 * SSM_GROUPS * SSM_STATE
SSM_IN = 2 * SSM_D_INNER + 2 * SSM_GROUPS * SSM_STATE + SSM_HEADS

NORM_EPS = 1e-6
GATE_NORM_EPS = 1e-5

kernel_name = "dilated_swa_mamba2_hybrid_step"


def rms_norm(x, w, eps=NORM_EPS):
    xf = x.astype(jnp.float32)
    y = xf * lax.rsqrt(jnp.mean(xf * xf, axis=-1, keepdims=True) + eps)
    return (y * w.astype(jnp.float32)).astype(x.dtype)


def attn_project(x, norm_w, w_in, q_gain, k_gain):
    b, s, _ = x.shape
    h = rms_norm(x, norm_w)
    proj = h @ w_in
    qkv = proj[..., :3 * N_DIL * ATT_WIDTH].reshape(b, s, N_DIL, 3, ATT_HEADS, ATT_HEAD_DIM)
    gate = proj[..., 3 * N_DIL * ATT_WIDTH:]
    q = rms_norm(qkv[:, :, :, 0], q_gain[:, None, :]) * (ATT_HEAD_DIM ** -0.5)
    k = rms_norm(qkv[:, :, :, 1], k_gain[:, None, :])
    v = qkv[:, :, :, 2]
    return q, k, v, gate


def dilated_band_prompt(q, k, v, window, dilation):
    b, s, nh, dh = q.shape
    wn = window // dilation
    span = dilation * BAND_BLOCK
    sp = -(-s // span) * span
    nb = sp // span
    pad = ((0, 0), (0, sp - s), (0, 0), (0, 0))

    def to_blocks(t):
        t = jnp.pad(t.astype(jnp.float32), pad).reshape(b, sp // dilation, dilation, nh, dh)
        return t.transpose(0, 2, 1, 3, 4).reshape(b, dilation, nb, BAND_BLOCK, nh, dh)

    def with_prev(t):
        prev = jnp.pad(t, ((0, 0), (0, 0), (1, 0), (0, 0), (0, 0), (0, 0)))[:, :, :-1]
        return jnp.concatenate([prev, t], axis=3)

    qb, kb, vb = to_blocks(q), to_blocks(k), to_blocks(v)
    kk, vv = with_prev(kb), with_prev(vb)
    scores = jnp.einsum('brnqhe,brnkhe->brnhqk', qb, kk)
    blk = jnp.arange(nb)[:, None, None]
    qi = jnp.arange(BAND_BLOCK)[None, :, None]
    kj = jnp.arange(2 * BAND_BLOCK)[None, None, :]
    dist = BAND_BLOCK + qi - kj
    mask = (dist >= 0) & (dist <= wn) & ((blk - 1) * BAND_BLOCK + kj >= 0)
    scores = jnp.where(mask[None, None, :, None], scores, -jnp.inf)
    m = jnp.max(scores, axis=-1, keepdims=True)
    p = jnp.exp(scores - m)
    den = jnp.sum(p, axis=-1, keepdims=True)
    o = jnp.einsum('brnhqk,brnkhe->brnqhe', p / den, vv)
    lse = (m + jnp.log(den))[..., 0]
    o = o.reshape(b, dilation, sp // dilation, nh, dh).transpose(0, 2, 1, 3, 4).reshape(b, sp, nh, dh)[:, :s]
    lse = lse.transpose(0, 1, 2, 4, 3).reshape(b, dilation, sp // dilation, nh)
    lse = lse.transpose(0, 2, 1, 3).reshape(b, sp, nh)[:, :s]
    return o, lse


def dilated_gather_sample(q, k_all, v_all, buf_len, window, dilation):
    t = q.shape[1]
    wn = window // dilation
    idx = buf_len + jnp.arange(t)[:, None] - dilation * jnp.arange(wn + 1)[None, :]
    valid = idx >= 0
    idx = jnp.maximum(idx, 0)
    kg = k_all[:, idx].astype(jnp.float32)
    vg = v_all[:, idx].astype(jnp.float32)
    scores = jnp.einsum('bthe,btkhe->bthk', q.astype(jnp.float32), kg)
    scores = jnp.where(valid[None, :, None, :], scores, -jnp.inf)
    m = jnp.max(scores, axis=-1, keepdims=True)
    p = jnp.exp(scores - m)
    den = jnp.sum(p, axis=-1, keepdims=True)
    o = jnp.einsum('bthk,btkhe->bthe', p / den, vg)
    lse = (m + jnp.log(den))[..., 0]
    return o, lse


def attn_merge(outs, lses, gate, w_out):
    wts = jax.nn.softmax(lses, axis=0)
    o = jnp.einsum('gbsh,gbshe->bshe', wts, outs)
    b, s = o.shape[:2]
    o = o.reshape(b, s, ATT_WIDTH).astype(gate.dtype) * jax.nn.silu(gate)
    return o @ w_out


def attn_layer_prompt(x, norm_w, w_in, q_gain, k_gain, w_out):
    q, k, v, gate = attn_project(x, norm_w, w_in, q_gain, k_gain)
    s = x.shape[1]
    outs, lses, new_kv = [], [], []
    for g, (window, dilation) in enumerate(DIL_GROUPS):
        o, l = dilated_band_prompt(q[:, :, g], k[:, :, g], v[:, :, g], window, dilation)
        outs.append(o)
        lses.append(l)
        keep = min(window, s)
        new_kv.append(jnp.stack([k[:, s - keep:, g], v[:, s - keep:, g]], axis=2))
    y = x + attn_merge(jnp.stack(outs), jnp.stack(lses), gate, w_out).astype(x.dtype)
    return y, new_kv


def attn_layer_sample(x, kv_bufs, norm_w, w_in, q_gain, k_gain, w_out):
    q, k, v, gate = attn_project(x, norm_w, w_in, q_gain, k_gain)
    outs, lses, new_kv = [], [], []
    for g, (window, dilation) in enumerate(DIL_GROUPS):
        buf = kv_bufs[g]
        k_all = jnp.concatenate([buf[:, :, 0], k[:, :, g].astype(buf.dtype)], axis=1)
        v_all = jnp.concatenate([buf[:, :, 1], v[:, :, g].astype(buf.dtype)], axis=1)
        o, l = dilated_gather_sample(q[:, :, g], k_all, v_all, buf.shape[1], window, dilation)
        outs.append(o)
        lses.append(l)
        new_kv.append(jnp.stack([k[:, :, g], v[:, :, g]], axis=2))
    y = x + attn_merge(jnp.stack(outs), jnp.stack(lses), gate, w_out).astype(x.dtype)
    return y, new_kv


def causal_conv(xbc, conv_state, conv_w, conv_b):
    L = xbc.shape[1]
    xpad = jnp.concatenate([conv_state.astype(xbc.dtype), xbc], axis=1)
    out = conv_b + xpad[:, 0:L] * conv_w[0]
    for j in range(1, SSM_CONV):
        out = out + xpad[:, j:j + L] * conv_w[j]
    return jax.nn.silu(out), xpad[:, L:]


def ssd_chunked(x, dt, A, Bm, Cm, h0, chunk):
    f32 = jnp.float32
    b, L, nh, p = x.shape
    c = L // chunk
    G, R, N = SSM_GROUPS, SSM_HEADS_PER_GROUP, SSM_STATE
    x = x.astype(f32).reshape(b, c, chunk, G, R, p)
    dt = dt.astype(f32).reshape(b, c, chunk, G, R)
    Bm = Bm.astype(f32).reshape(b, c, chunk, G, N)
    Cm = Cm.astype(f32).reshape(b, c, chunk, G, N)
    a_cs = jnp.cumsum(dt * A.astype(f32).reshape(G, R), axis=2)
    xdt = x * dt[..., None]
    a_t = a_cs.transpose(0, 1, 3, 4, 2)
    seg = a_t[..., :, None] - a_t[..., None, :]
    causal = jnp.tril(jnp.ones((chunk, chunk), dtype=bool))
    decay = jnp.exp(jnp.where(causal, seg, -jnp.inf))
    cb = jnp.einsum('bclgn,bcsgn->bcgls', Cm, Bm)
    y_diag = jnp.einsum('bcgls,bcgrls,bcsgrp->bclgrp', cb, decay, xdt)
    decay_to_end = jnp.exp(a_cs[:, :, -1:] - a_cs)
    states = jnp.einsum('bclgn,bclgr,bclgrp->bcgrpn', Bm, decay_to_end, xdt)
    chunk_decay = jnp.exp(a_cs[:, :, -1])

    def step(h, inp):
        s_c, d_c = inp
        return h * d_c[..., None, None] + s_c, h

    h_init = h0.astype(f32).reshape(b, G, R, p, N)
    h_final, h_prev = lax.scan(step, h_init, (states.transpose(1, 0, 2, 3, 4, 5), chunk_decay.transpose(1, 0, 2, 3)))
    h_prev = h_prev.transpose(1, 0, 2, 3, 4, 5)
    y_off = jnp.einsum('bclgn,bcgrpn,bclgr->bclgrp', Cm, h_prev, jnp.exp(a_cs))
    y = (y_diag + y_off).reshape(b, L, nh, p)
    return y, h_final.reshape(b, nh, p, N)


def ssm_layer(x, conv_state, ssm_state, chunk, norm_w, w_in, conv_w, conv_b, dt_bias, A_log, D, gate_norm, w_out):
    f32 = jnp.float32
    b, L, _ = x.shape
    h = rms_norm(x, norm_w)
    proj = h @ w_in
    z = proj[..., :SSM_D_INNER]
    xbc = proj[..., SSM_D_INNER:SSM_D_INNER + SSM_CONV_DIM]
    dt = proj[..., SSM_D_INNER + SSM_CONV_DIM:]
    xbc, new_conv = causal_conv(xbc, conv_state, conv_w, conv_b)
    gn = SSM_GROUPS * SSM_STATE
    xs = xbc[..., :SSM_D_INNER].reshape(b, L, SSM_HEADS, SSM_HEAD_DIM)
    Bm = xbc[..., SSM_D_INNER:SSM_D_INNER + gn].reshape(b, L, SSM_GROUPS, SSM_STATE)
    Cm = xbc[..., SSM_D_INNER + gn:].reshape(b, L, SSM_GROUPS, SSM_STATE)
    dt = jax.nn.softplus(dt.astype(f32) + dt_bias.astype(f32))
    A = -jnp.exp(A_log.astype(f32))
    y, new_ssm = ssd_chunked(xs, dt, A, Bm, Cm, ssm_state, chunk)
    y = y + D.astype(f32)[:, None] * xs.astype(f32)
    y = y.reshape(b, L, SSM_D_INNER) * jax.nn.silu(z.astype(f32))
    yg = y.reshape(b, L, SSM_GROUPS, SSM_D_INNER // SSM_GROUPS)
    yg = yg * lax.rsqrt(jnp.mean(yg * yg, axis=-1, keepdims=True) + GATE_NORM_EPS)
    y = yg.reshape(b, L, SSM_D_INNER) * gate_norm.astype(f32)
    out = x + (y.astype(x.dtype) @ w_out).astype(x.dtype)
    return out, new_conv, new_ssm


def setup_inputs(seed: int = 0) -> dict:
    key = jax.random.key(seed)
    ks = jax.random.split(key, 24)
    f32 = jnp.float32

    def nrm(k, shape, scale):
        return scale * jax.random.normal(k, shape, f32)

    NA, NS = N_ATTN_LAYERS, N_SSM_LAYERS
    buf = [min(w, PAST_LEN) for (w, _) in DIL_GROUPS]
    x_prompt = nrm(ks[0], (BATCH, SEQ, D_MODEL), 1.0)
    x_sample = nrm(ks[1], (DEC_BATCH, DEC_SEQ, D_MODEL), 1.0)
    cache_kv_g0 = nrm(ks[2], (NA, DEC_BATCH, buf[0], 2, ATT_HEADS, ATT_HEAD_DIM), 1.0)
    cache_kv_g1 = nrm(ks[3], (NA, DEC_BATCH, buf[1], 2, ATT_HEADS, ATT_HEAD_DIM), 1.0)
    cache_kv_g2 = nrm(ks[4], (NA, DEC_BATCH, buf[2], 2, ATT_HEADS, ATT_HEAD_DIM), 1.0)
    state_conv = nrm(ks[5], (NS, DEC_BATCH, SSM_CONV - 1, SSM_CONV_DIM), 1.0)
    state_ssm = nrm(ks[6], (NS, DEC_BATCH, SSM_HEADS, SSM_HEAD_DIM, SSM_STATE), 0.1)
    attn_norm = 1.0 + nrm(ks[7], (NA, D_MODEL), 0.02)
    attn_w_in = nrm(ks[8], (NA, D_MODEL, ATT_IN), D_MODEL ** -0.5)
    attn_q_gain = 1.0 + nrm(ks[9], (NA, N_DIL, ATT_HEAD_DIM), 0.02)
    attn_k_gain = 1.0 + nrm(ks[10], (NA, N_DIL, ATT_HEAD_DIM), 0.02)
    attn_w_out = nrm(ks[11], (NA, ATT_WIDTH, D_MODEL), ATT_WIDTH ** -0.5)
    ssm_norm = 1.0 + nrm(ks[12], (NS, D_MODEL), 0.02)
    ssm_w_in = nrm(ks[13], (NS, D_MODEL, SSM_IN), D_MODEL ** -0.5)
    ssm_conv_w = nrm(ks[14], (NS, SSM_CONV, SSM_CONV_DIM), SSM_CONV ** -0.5)
    ssm_conv_b = nrm(ks[15], (NS, SSM_CONV_DIM), 0.02)
    dt0 = jnp.exp(jax.random.uniform(ks[16], (NS, SSM_HEADS), f32, math.log(1e-3), math.log(1e-1)))
    ssm_dt_bias = dt0 + jnp.log(-jnp.expm1(-dt0))
    ssm_A_log = jnp.log(jax.random.uniform(ks[17], (NS, SSM_HEADS), f32, 1.0, 16.0))
    ssm_D = 1.0 + nrm(ks[18], (NS, SSM_HEADS), 0.02)
    ssm_gate_norm = 1.0 + nrm(ks[19], (NS, SSM_D_INNER), 0.02)
    ssm_w_out = nrm(ks[20], (NS, SSM_D_INNER, D_MODEL), SSM_D_INNER ** -0.5)
    return {
        "x_prompt": x_prompt, "x_sample": x_sample,
        "cache_kv_g0": cache_kv_g0, "cache_kv_g1": cache_kv_g1, "cache_kv_g2": cache_kv_g2,
        "state_conv": state_conv, "state_ssm": state_ssm,
        "attn_norm": attn_norm, "attn_w_in": attn_w_in, "attn_q_gain": attn_q_gain,
        "attn_k_gain": attn_k_gain, "attn_w_out": attn_w_out,
        "ssm_norm": ssm_norm, "ssm_w_in": ssm_w_in, "ssm_conv_w": ssm_conv_w, "ssm_conv_b": ssm_conv_b,
        "ssm_dt_bias": ssm_dt_bias, "ssm_A_log": ssm_A_log, "ssm_D": ssm_D,
        "ssm_gate_norm": ssm_gate_norm, "ssm_w_out": ssm_w_out,
    }


def reference(x_prompt, x_sample, cache_kv_g0, cache_kv_g1, cache_kv_g2, state_conv, state_ssm,
              attn_norm, attn_w_in, attn_q_gain, attn_k_gain, attn_w_out,
              ssm_norm, ssm_w_in, ssm_conv_w, ssm_conv_b, ssm_dt_bias, ssm_A_log, ssm_D,
              ssm_gate_norm, ssm_w_out):
    kv_caches = (cache_kv_g0, cache_kv_g1, cache_kv_g2)
    xp, xs = x_prompt, x_sample
    kv_p = [[] for _ in range(N_DIL)]
    kv_s = [[] for _ in range(N_DIL)]
    conv_p, conv_s, ssm_p, ssm_s = [], [], [], []
    for i in range(DEPTH):
        j = i // N_MIXERS
        if i % N_MIXERS == 0:
            params = (attn_norm[j], attn_w_in[j], attn_q_gain[j], attn_k_gain[j], attn_w_out[j])
            xp, new_p = attn_layer_prompt(xp, *params)
            xs, new_s = attn_layer_sample(xs, [c[j] for c in kv_caches], *params)
            for g in range(N_DIL):
                kv_p[g].append(new_p[g])
                kv_s[g].append(new_s[g])
        else:
            params = (ssm_norm[j], ssm_w_in[j], ssm_conv_w[j], ssm_conv_b[j], ssm_dt_bias[j],
                      ssm_A_log[j], ssm_D[j], ssm_gate_norm[j], ssm_w_out[j])
            b = xp.shape[0]
            zero_conv = jnp.zeros((b, SSM_CONV - 1, SSM_CONV_DIM), xp.dtype)
            zero_ssm = jnp.zeros((b, SSM_HEADS, SSM_HEAD_DIM, SSM_STATE), jnp.float32)
            xp, c_p, h_p = ssm_layer(xp, zero_conv, zero_ssm, SSM_CHUNK, *params)
            xs, c_s, h_s = ssm_layer(xs, state_conv[j], state_ssm[j], xs.shape[1], *params)
            conv_p.append(c_p)
            conv_s.append(c_s)
            ssm_p.append(h_p)
            ssm_s.append(h_s)
    new_kv_g0_prompt = jnp.stack(kv_p[0])
    new_kv_g1_prompt = jnp.stack(kv_p[1])
    new_kv_g2_prompt = jnp.stack(kv_p[2])
    new_kv_g0_sample = jnp.stack(kv_s[0])
    new_kv_g1_sample = jnp.stack(kv_s[1])
    new_kv_g2_sample = jnp.stack(kv_s[2])
    new_conv_prompt = jnp.stack(conv_p)
    new_conv_sample = jnp.stack(conv_s)
    new_ssm_prompt = jnp.stack(ssm_p)
    new_ssm_sample = jnp.stack(ssm_s)
    return (xp, xs, new_kv_g0_prompt, new_kv_g1_prompt, new_kv_g2_prompt,
            new_kv_g0_sample, new_kv_g1_sample, new_kv_g2_sample,
            new_conv_prompt, new_conv_sample, new_ssm_prompt, new_ssm_sample)
```

```python
import functools

import jax
import jax.numpy as jnp
from jax import lax
from jax.experimental import pallas as pl
from jax.experimental.pallas import tpu as pltpu

F32 = jnp.float32
BF16 = jnp.bfloat16

D_MODEL = 1024
N_HEADS = 16
HEAD_DIM = 64
ATT_WIDTH = N_HEADS * HEAD_DIM
DILATIONS = (1, 4, 16)
WINDOW_STEPS = 128
ATT_IN = 10 * ATT_WIDTH
PERM = 16

SSM_D_INNER = 2048
SSM_HEADS = 32
SSM_STATE = 128
SSM_GROUPS = 4
SSM_GROUP_WIDTH = SSM_D_INNER // SSM_GROUPS
SSM_CONV = 4
SSM_CONV_DIM = SSM_D_INNER + 2 * SSM_GROUPS * SSM_STATE
SSM_MAIN = SSM_D_INNER + SSM_CONV_DIM
SSM_CHUNK = 128

NORM_EPS = 1e-6
GATE_NORM_EPS = 1e-5
MASKED = -1e30

LANES = 128
VMEM_LIMIT = 56 * 1024 * 1024


def _params(semantics):
    return pltpu.CompilerParams(dimension_semantics=semantics, vmem_limit_bytes=VMEM_LIMIT)


def _rms(x, w, eps):
    return x * lax.rsqrt(jnp.mean(x * x, axis=-1, keepdims=True) + eps) * w


def _sigmoid(x):
    return 1.0 / (1.0 + jnp.exp(-x))


def _split2(v):
    hi = v.astype(BF16)
    lo = (v - hi.astype(F32)).astype(BF16)
    return hi, lo


def _dot(a, b):
    return jnp.dot(a, b, preferred_element_type=F32)


def _dot2(v, e):
    hi, lo = _split2(v)
    return _dot(hi, e) + _dot(lo, e)


def _head_norm(x, eps):
    lo = lax.broadcasted_iota(jnp.int32, (1, LANES), 1) < HEAD_DIM
    parts = []
    for j in range(x.shape[1] // LANES):
        t = x[:, j * LANES:(j + 1) * LANES]
        t2 = t * t
        s_lo = jnp.sum(jnp.where(lo, t2, 0.0), axis=-1, keepdims=True)
        s_hi = jnp.sum(jnp.where(lo, 0.0, t2), axis=-1, keepdims=True)
        r = jnp.where(lo, lax.rsqrt(s_lo * (1.0 / HEAD_DIM) + eps), lax.rsqrt(s_hi * (1.0 / HEAD_DIM) + eps))
        parts.append(t * r)
    return jnp.concatenate(parts, axis=-1)


def _proj_kernel(x_ref, nw_ref, w_ref, *rest, perm, with_dt):
    if with_dt:
        wdt_ref, o_ref, dt_ref, h_ref = rest
    elif perm:
        o_ref, h_ref, slab_ref = rest
    else:
        o_ref, h_ref = rest

    @pl.when(pl.program_id(1) == 0)
    def _():
        xn = _rms(x_ref[...], nw_ref[...], NORM_EPS)
        if perm:
            rows = x_ref.shape[0] // PERM
            for c in range(x_ref.shape[1] // LANES):
                cols = slice(c * LANES, (c + 1) * LANES)
                slab_ref[...] = xn[:, cols]
                for r in range(PERM):
                    h_ref[r * rows:(r + 1) * rows, cols] = slab_ref[pl.ds(r, rows, stride=PERM), :].astype(BF16)
        else:
            h_ref[...] = xn.astype(BF16)
        if with_dt:
            dt_ref[...] = _dot(h_ref[...], wdt_ref[...])

    res = _dot(h_ref[...], w_ref[...]).astype(o_ref.dtype)
    o_ref[...] = res.reshape(o_ref.shape)


def _proj(x, norm_w, w, *, tm, tn, out_dtype, perm_seq=None, w_dt=None):
    m, dm = x.shape
    n = w.shape[1]
    grid = (m // tm, n // tn)
    in_specs = [pl.BlockSpec((tm, dm), lambda i, j: (i, 0)),
                pl.BlockSpec((1, dm), lambda i, j: (0, 0)),
                pl.BlockSpec((dm, tn), lambda i, j: (0, j))]
    args = [x, norm_w.reshape(1, dm), w]
    if perm_seq is not None:
        per_b = perm_seq // tm
        rows = tm // PERM
        out_shape = [jax.ShapeDtypeStruct((m // perm_seq, PERM, perm_seq // PERM, n), out_dtype)]
        out_specs = [pl.BlockSpec((None, PERM, rows, tn), lambda i, j: (i // per_b, 0, i % per_b, j))]
    else:
        out_shape = [jax.ShapeDtypeStruct((m, n), out_dtype)]
        out_specs = [pl.BlockSpec((tm, tn), lambda i, j: (i, j))]
    if w_dt is not None:
        in_specs.append(pl.BlockSpec((dm, LANES), lambda i, j: (0, 0)))
        args.append(w_dt)
        out_shape.append(jax.ShapeDtypeStruct((m, LANES), F32))
        out_specs.append(pl.BlockSpec((tm, LANES), lambda i, j: (i, 0)))
    outs = pl.pallas_call(
        functools.partial(_proj_kernel, perm=perm_seq is not None, with_dt=w_dt is not None),
        grid=grid, in_specs=in_specs, out_specs=out_specs, out_shape=out_shape,
        scratch_shapes=[pltpu.VMEM((tm, dm), BF16)] + ([pltpu.VMEM((tm, LANES), F32)] if perm_seq else []),
        compiler_params=_params(("parallel", "arbitrary")),
        name="norm_proj",
    )(*args)
    return outs if w_dt is not None else outs[0]


def _outproj_kernel(a_ref, w_ref, x_ref, o_ref, *scratch, unperm):
    tm = o_ref.shape[0]
    a = a_ref[...].reshape(tm, a_ref.shape[-1]).astype(BF16)
    res = _dot(a, w_ref[...])
    if unperm:
        slab_ref, = scratch
        rows = tm // PERM
        for c in range(o_ref.shape[1] // LANES):
            cols = slice(c * LANES, (c + 1) * LANES)
            for r in range(PERM):
                slab_ref[pl.ds(r, rows, stride=PERM), :] = res[r * rows:(r + 1) * rows, cols]
            o_ref[:, cols] = x_ref[:, cols] + slab_ref[...]
    else:
        o_ref[...] = x_ref[...] + res


def _outproj(a, w, x, *, tm, perm_seq=None):
    m, n = x.shape
    k = w.shape[0]
    if perm_seq is not None:
        per_b = perm_seq // tm
        a_spec = pl.BlockSpec((None, PERM, tm // PERM, k), lambda i: (i // per_b, 0, i % per_b, 0))
    else:
        a_spec = pl.BlockSpec((tm, k), lambda i: (i, 0))
    return pl.pallas_call(
        functools.partial(_outproj_kernel, unperm=perm_seq is not None),
        grid=(m // tm,),
        in_specs=[a_spec, pl.BlockSpec((k, n), lambda i: (0, 0)), pl.BlockSpec((tm, n), lambda i: (i, 0))],
        out_specs=pl.BlockSpec((tm, n), lambda i: (i, 0)),
        out_shape=jax.ShapeDtypeStruct((m, n), F32),
        scratch_shapes=[pltpu.VMEM((tm, LANES), F32)] if perm_seq else [],
        compiler_params=_params(("parallel",)),
        name="out_proj",
    )(a, w, x)


def _attn_group_kernel(*refs, n_chunks, first, last):
    it = iter(refs)
    q_ref, k_ref, v_ref = next(it), next(it), next(it)
    gate_ref = next(it) if last else None
    qg_ref, kg_ref = next(it), next(it)
    if not first:
        acc_in, m_in, l_in = next(it), next(it), next(it)
    kv_out = next(it)
    if last:
        o_out = next(it)
    else:
        acc_out, m_out, l_out = next(it), next(it), next(it)
    ks_ref, vs_ref = next(it), next(it)

    rc = q_ref.shape[1]
    rb = n_chunks * rc
    n = pl.program_id(2)

    @pl.when(n == 0)
    def _():
        ks_ref[0:rb, :] = jnp.zeros((rb, ATT_WIDTH), BF16)
        vs_ref[0:rb, :] = jnp.zeros((rb, ATT_WIDTH), BF16)

    q = q_ref[...].reshape(rb, ATT_WIDTH).astype(F32)
    k = k_ref[...].reshape(rb, ATT_WIDTH).astype(F32)
    v = v_ref[...].reshape(rb, ATT_WIDTH)
    qn = (_head_norm(q, NORM_EPS) * qg_ref[...]).astype(BF16)
    kn = _head_norm(k, NORM_EPS) * kg_ref[...]
    ks_ref[rb:2 * rb, :] = kn.astype(BF16)
    vs_ref[rb:2 * rb, :] = v

    @pl.when(n == pl.num_programs(2) - 1)
    def _():
        kv_out[:, :, 0:ATT_WIDTH] = kn.reshape(n_chunks, rc, ATT_WIDTH)
        kv_out[:, :, ATT_WIDTH:2 * ATT_WIDTH] = v.astype(F32).reshape(n_chunks, rc, ATT_WIDTH)

    qi = lax.broadcasted_iota(jnp.int32, (rb, 2 * rb), 0)
    kj = lax.broadcasted_iota(jnp.int32, (rb, 2 * rb), 1)
    kc = jnp.where(kj >= rb, kj - rb, kj)
    tq = n_chunks * (qi % rc) + qi // rc
    tk = n_chunks * (kc % rc) + kc // rc + jnp.where(kj >= rb, 0, -rb)
    dist = tq - tk
    valid = (dist >= 0) & (dist <= WINDOW_STEPS) & ((kj >= rb) | (n > 0))
    bias = jnp.where(valid, 0.0, MASKED)

    lane = lax.broadcasted_iota(jnp.int32, (1, LANES), 1)
    lo = lane < HEAD_DIM
    if not first:
        m_prev_all = m_in[...].reshape(rb, LANES)
        l_prev_all = l_in[...].reshape(rb, LANES)
    m_all = jnp.zeros((rb, LANES), F32)
    l_all = jnp.zeros((rb, LANES), F32)
    for j in range(N_HEADS // 2):
        cols = slice(j * LANES, (j + 1) * LANES)
        qp = qn[:, cols]
        kp = ks_ref[:, cols]
        vp = vs_ref[:, cols]
        if not first:
            acc_prev = acc_in[:, :, cols].reshape(rb, LANES)
        new = []
        lsum = []
        for e in range(2):
            h = 2 * j + e
            sel = lo if e == 0 else jnp.logical_not(lo)
            qm = jnp.where(sel, qp, jnp.zeros_like(qp))
            s = lax.dot_general(qm, kp, (((1,), (1,)), ((), ())), preferred_element_type=F32) + bias
            m_cur = jnp.max(s, axis=-1, keepdims=True)
            if first:
                m_new = m_cur
            else:
                m_prev = m_prev_all[:, h:h + 1]
                m_new = jnp.maximum(m_prev, m_cur)
                alpha = jnp.exp(m_prev - m_new)
            p = jnp.exp(s - m_new)
            l_new = jnp.sum(p, axis=-1, keepdims=True)
            pv = _dot(p.astype(BF16), vp)
            if not first:
                l_new = alpha * l_prev_all[:, h:h + 1] + l_new
                pv = alpha * acc_prev + pv
            new.append(pv)
            lsum.append(l_new)
            if not last:
                m_all = jnp.where(lane == h, m_new, m_all)
                l_all = jnp.where(lane == h, l_new, l_all)
        acc_new = jnp.where(lo, new[0], new[1])
        if last:
            o = acc_new / jnp.where(lo, lsum[0], lsum[1])
            g = gate_ref[:, :, cols].reshape(rb, LANES).astype(F32)
            o_out[:, :, cols] = (o * (g * _sigmoid(g))).astype(BF16).reshape(n_chunks, rc, LANES)
        else:
            acc_out[:, :, cols] = acc_new.reshape(n_chunks, rc, LANES)
    if not last:
        m_out[...] = m_all.reshape(n_chunks, rc, LANES)
        l_out[...] = l_all.reshape(n_chunks, rc, LANES)
    ks_ref[0:rb, :] = ks_ref[rb:2 * rb, :]
    vs_ref[0:rb, :] = vs_ref[rb:2 * rb, :]


def _attn_group(qkvg, g, q_gain, k_gain, state, *, rb, first, last):
    b, _, t, _ = qkvg.shape
    d = DILATIONS[g]
    n_chunks = PERM // d
    rc = rb // n_chunks
    nb = t // rc
    n_col = ATT_IN // ATT_WIDTH
    view = lambda a: a.reshape(b, n_chunks, d, t, a.shape[-1])

    def spec(width, col=None):
        if col is None:
            return pl.BlockSpec((None, n_chunks, None, rc, width), lambda bb, r, n: (bb, 0, r, n, 0))
        return pl.BlockSpec((None, n_chunks, None, rc, width), lambda bb, r, n: (bb, 0, r, n, col))

    qkvg5 = view(qkvg)
    in_specs = [spec(ATT_WIDTH, 3 * g), spec(ATT_WIDTH, 3 * g + 1), spec(ATT_WIDTH, 3 * g + 2)]
    args = [qkvg5, qkvg5, qkvg5]
    if last:
        in_specs.append(spec(ATT_WIDTH, n_col - 1))
        args.append(qkvg5)
    vec = pl.BlockSpec((1, ATT_WIDTH), lambda bb, r, n: (0, 0))
    in_specs += [vec, vec]
    args += [q_gain, k_gain]
    if not first:
        acc, m, l = state
        in_specs += [spec(ATT_WIDTH), spec(LANES), spec(LANES)]
        args += [view(acc), view(m), view(l)]
    out_shape = [jax.ShapeDtypeStruct((b, n_chunks, d, rc, 2 * ATT_WIDTH), F32)]
    out_specs = [pl.BlockSpec((None, n_chunks, None, rc, 2 * ATT_WIDTH), lambda bb, r, n: (bb, 0, r, 0, 0))]
    if last:
        out_shape.append(jax.ShapeDtypeStruct((b, n_chunks, d, t, ATT_WIDTH), BF16))
        out_specs.append(spec(ATT_WIDTH))
    else:
        out_shape += [jax.ShapeDtypeStruct((b, n_chunks, d, t, ATT_WIDTH), F32),
                      jax.ShapeDtypeStruct((b, n_chunks, d, t, LANES), F32),
                      jax.ShapeDtypeStruct((b, n_chunks, d, t, LANES), F32)]
        out_specs += [spec(ATT_WIDTH), spec(LANES), spec(LANES)]
    outs = pl.pallas_call(
        functools.partial(_attn_group_kernel, n_chunks=n_chunks, first=first, last=last),
        grid=(b, d, nb), in_specs=in_specs, out_specs=out_specs, out_shape=out_shape,
        scratch_shapes=[pltpu.VMEM((2 * rb, ATT_WIDTH), BF16), pltpu.VMEM((2 * rb, ATT_WIDTH), BF16)],
        compiler_params=_params(("parallel", "parallel", "arbitrary")),
        name=f"attn_group{g}",
    )(*args)
    unview = lambda a: a.reshape(b, PERM, t, a.shape[-1])
    kv_tail = outs[0]
    if last:
        return kv_tail, unview(outs[1])
    return kv_tail, tuple(unview(a) for a in outs[1:])


SAMPLE_STEPS = 2


def _sample_attn_kernel(p_ref, c0_ref, c1_ref, c2_ref, qg_ref, kg_ref,
                        o_ref, kv0_ref, kv1_ref, kv2_ref, q_s, k_s, v_s, acc_s, m_s, l_s):
    caches = (c0_ref, c1_ref, c2_ref)
    kv_outs = (kv0_ref, kv1_ref, kv2_ref)
    t_new = p_ref.shape[0]
    th = t_new * N_HEADS
    step = pl.program_id(1)

    @pl.when(step == 0)
    def _():
        for g in range(len(DILATIONS)):
            q = _rms(p_ref[:, 3 * g], qg_ref[g:g + 1, :][None], NORM_EPS)
            k = _rms(p_ref[:, 3 * g + 1], kg_ref[g:g + 1, :][None], NORM_EPS)
            v = p_ref[:, 3 * g + 2]
            kv_outs[g][:, 0] = k
            kv_outs[g][:, 1] = v
            q_s[g] = q.reshape(th, HEAD_DIM)
            k_s[g] = k.reshape(th, HEAD_DIM)
            v_s[g] = v.reshape(th, HEAD_DIM)

    def block(g, cache_ref, idx, res, n_tok):
        d = DILATIONS[g]
        buf = cache_ref.shape[0] * N_HEADS
        kc = cache_ref[:, idx, 0].reshape(buf, HEAD_DIM).astype(BF16)
        vc = cache_ref[:, idx, 1].reshape(buf, HEAD_DIM).astype(BF16)
        starts = [pl.multiple_of((res + d * i) * N_HEADS, N_HEADS) for i in range(n_tok)]
        q = jnp.concatenate([q_s[g, pl.ds(s0, N_HEADS), :] for s0 in starts], axis=0).astype(BF16)
        nt = (((1,), (1,)), ((), ()))
        s1 = lax.dot_general(q, kc, nt, preferred_element_type=F32)
        s2 = lax.dot_general(q, k_s[g].astype(BF16), nt, preferred_element_type=F32)
        rows = n_tok * N_HEADS
        ri = lax.broadcasted_iota(jnp.int32, (rows, 1), 0)
        c1 = lax.broadcasted_iota(jnp.int32, (1, buf), 1)
        c2 = lax.broadcasted_iota(jnp.int32, (1, th), 1)
        same_head1 = (c1 % N_HEADS) == (ri % N_HEADS)
        same_head2 = (c2 % N_HEADS) == (ri % N_HEADS)
        t_q = res + d * (ri // N_HEADS)
        t_k = c2 // N_HEADS
        valid1 = same_head1 & ((c1 // N_HEADS) >= (ri // N_HEADS))
        valid2 = same_head2 & (t_k <= t_q) & ((t_k % d) == res)
        s1 = jnp.where(valid1, s1, MASKED)
        s2 = jnp.where(valid2, s2, MASKED)
        m = jnp.maximum(jnp.max(s1, axis=-1, keepdims=True), jnp.max(s2, axis=-1, keepdims=True))
        p1 = jnp.exp(s1 - m)
        p2 = jnp.exp(s2 - m)
        l = jnp.sum(p1, axis=-1, keepdims=True) + jnp.sum(p2, axis=-1, keepdims=True)
        o = _dot(p1.astype(BF16), vc) + _dot(p2.astype(BF16), v_s[g].astype(BF16))
        for i, s0 in enumerate(starts):
            rs = slice(i * N_HEADS, (i + 1) * N_HEADS)
            acc_s[g, pl.ds(s0, N_HEADS), :] = o[rs]
            m_s[g, pl.ds(s0, N_HEADS), :] = jnp.broadcast_to(m[rs], (N_HEADS, HEAD_DIM))
            l_s[g, pl.ds(s0, N_HEADS), :] = jnp.broadcast_to(l[rs], (N_HEADS, HEAD_DIM))

    for g, d in enumerate(DILATIONS):
        n_res = min(d, t_new)
        per_step = caches[g].shape[1]
        n_tok = t_new // n_res
        if n_res < SAMPLE_STEPS:
            @pl.when(step == 0)
            def _(g=g, n_tok=n_tok):
                block(g, caches[g], 0, 0, n_tok)
        else:
            for idx in range(per_step):
                block(g, caches[g], idx, step * per_step + idx, n_tok)

    @pl.when(step == SAMPLE_STEPS - 1)
    def _():
        m_max = jnp.maximum(jnp.maximum(m_s[0], m_s[1]), m_s[2])
        f = [jnp.exp(m_s[g] - m_max) for g in range(3)]
        den = f[0] * l_s[0] + f[1] * l_s[1] + f[2] * l_s[2]
        o = (f[0] * acc_s[0] + f[1] * acc_s[1] + f[2] * acc_s[2]) / den
        gate = p_ref[:, 9].reshape(th, HEAD_DIM)
        o_ref[...] = (o * (gate * _sigmoid(gate))).reshape(t_new, N_HEADS, HEAD_DIM)


def _sample_attn(proj, caches, q_gain, k_gain):
    b = caches[0].shape[0]
    t_new = proj.shape[0] // b
    th = t_new * N_HEADS
    views, specs = [], []
    for g, d in enumerate(DILATIONS):
        rows = caches[g].shape[1] // d
        views.append(caches[g].reshape(b, rows, d, 2, N_HEADS, HEAD_DIM))
        n_res = min(d, t_new)
        if n_res < SAMPLE_STEPS:
            specs.append(pl.BlockSpec((None, rows, n_res, 2, N_HEADS, HEAD_DIM),
                                      lambda bb, s: (bb, 0, 0, 0, 0, 0)))
        else:
            specs.append(pl.BlockSpec((None, rows, n_res // SAMPLE_STEPS, 2, N_HEADS, HEAD_DIM),
                                      lambda bb, s: (bb, 0, s, 0, 0, 0)))
    gain_spec = pl.BlockSpec((3, HEAD_DIM), lambda bb, s: (0, 0))
    kv_spec = pl.BlockSpec((None, t_new, 2, N_HEADS, HEAD_DIM), lambda bb, s: (bb, 0, 0, 0, 0))
    kv_shape = jax.ShapeDtypeStruct((b, t_new, 2, N_HEADS, HEAD_DIM), F32)
    table = pltpu.VMEM((3, th, HEAD_DIM), F32)
    return pl.pallas_call(
        _sample_attn_kernel,
        grid=(b, SAMPLE_STEPS),
        in_specs=[pl.BlockSpec((t_new, 10, N_HEADS, HEAD_DIM), lambda bb, s: (bb, 0, 0, 0))] + specs
        + [gain_spec, gain_spec],
        out_specs=[pl.BlockSpec((t_new, N_HEADS, HEAD_DIM), lambda bb, s: (bb, 0, 0)), kv_spec, kv_spec, kv_spec],
        out_shape=[jax.ShapeDtypeStruct((b * t_new, N_HEADS, HEAD_DIM), F32), kv_shape, kv_shape, kv_shape],
        scratch_shapes=[table] * 6,
        compiler_params=_params(("parallel", "arbitrary")),
        name="sample_attn",
    )(proj, *views, q_gain, k_gain)


def _softplus(x):
    return jnp.maximum(x, 0.0) + jnp.log(1.0 + jnp.exp(-jnp.abs(x)))


def _conv_silu(zx_ref, xpad_ref, cw_ref, cb_ref, lc):
    pad = 8
    xpad_ref[pad:pad + lc, :] = zx_ref[:, SSM_D_INNER:SSM_MAIN].astype(F32)
    conv = cb_ref[...] + xpad_ref[pad - 3:pad - 3 + lc, :] * cw_ref[0:1, :]
    for j in range(1, SSM_CONV):
        conv = conv + xpad_ref[pad - 3 + j:pad - 3 + j + lc, :] * cw_ref[j:j + 1, :]
    tail = xpad_ref[pad + lc - 3:pad + lc, :]
    xpad_ref[pad - 3:pad, :] = tail
    return conv * _sigmoid(conv), tail


def _gate_norm(y, z, gn):
    y = y * (z * _sigmoid(z))
    parts = []
    for g in range(SSM_GROUPS):
        yg = y[:, g * SSM_GROUP_WIDTH:(g + 1) * SSM_GROUP_WIDTH]
        parts.append(yg * lax.rsqrt(jnp.mean(yg * yg, axis=-1, keepdims=True) + GATE_NORM_EPS))
    return jnp.concatenate(parts, axis=-1) * gn


def _cumsum_rows(x):
    rows = x.shape[0]
    tril = (lax.broadcasted_iota(jnp.int32, (rows, rows), 0)
            >= lax.broadcasted_iota(jnp.int32, (rows, rows), 1)).astype(BF16)
    h1 = x.astype(BF16)
    r1 = x - h1.astype(F32)
    h2 = r1.astype(BF16)
    h3 = (r1 - h2.astype(F32)).astype(BF16)
    return _dot(tril, h1) + _dot(tril, h2) + _dot(tril, h3)


def _ssd_prompt_kernel(zx_ref, dt_ref, cw_ref, cb_ref, dtb_ref, a_ref, dskip_ref, gn_ref, e_ref,
                       y_ref, nconv_ref, nssm_ref, xpad_ref, st_ref):
    lc = zx_ref.shape[0]
    c = pl.program_id(1)
    last = pl.num_programs(1) - 1

    @pl.when(c == 0)
    def _():
        xpad_ref[0:8, :] = jnp.zeros((8, SSM_CONV_DIM), F32)
        st_ref[...] = jnp.zeros(st_ref.shape, F32)

    xbc, tail = _conv_silu(zx_ref, xpad_ref, cw_ref, cb_ref, lc)

    @pl.when(c == last)
    def _():
        nconv_ref[...] = tail

    xs = xbc[:, 0:SSM_D_INNER]
    bm = xbc[:, SSM_D_INNER:SSM_D_INNER + SSM_GROUPS * SSM_STATE].astype(BF16)
    cm = xbc[:, SSM_D_INNER + SSM_GROUPS * SSM_STATE:].astype(BF16)
    e = e_ref[...]
    dt = _softplus(dt_ref[...] + dtb_ref[...])
    a_cs = _cumsum_rows(dt * a_ref[...])
    a_t = a_cs.T
    a_last = a_cs[lc - 1:lc, :]
    xdt = xs * _dot2(dt, e)
    xdt_b = xdt.astype(BF16)
    xdte = (xdt * _dot2(jnp.exp(a_last - a_cs), e)).astype(BF16)
    ea_e = _dot2(jnp.exp(a_cs), e)
    cd_e = _dot2(jnp.broadcast_to(jnp.exp(a_last), (8, LANES)), e)[0:1, :]
    s_prev = st_ref[...]
    s_b = s_prev.astype(BF16)
    causal = (lax.broadcasted_iota(jnp.int32, (lc, lc), 0) >= lax.broadcasted_iota(jnp.int32, (lc, lc), 1))
    lo = lax.broadcasted_iota(jnp.int32, (1, LANES), 1) < 64
    heads_per_group = SSM_HEADS // SSM_GROUPS
    y_parts = []
    for g in range(SSM_GROUPS):
        gs = slice(g * SSM_STATE, (g + 1) * SSM_STATE)
        gw = slice(g * SSM_GROUP_WIDTH, (g + 1) * SSM_GROUP_WIDTH)
        cb = lax.dot_general(cm[:, gs], bm[:, gs], (((1,), (1,)), ((), ())), preferred_element_type=F32)
        y_off = _dot(cm[:, gs], s_b[:, gw]) * ea_e[:, gw]
        s_new = lax.dot_general(bm[:, gs], xdte[:, gw], (((0,), (0,)), ((), ())), preferred_element_type=F32)
        st_ref[:, gw] = s_prev[:, gw] * cd_e[:, gw] + s_new
        pair_out = []
        for pr in range(heads_per_group // 2):
            h0 = g * heads_per_group + 2 * pr
            ms = []
            for h in (h0, h0 + 1):
                seg = a_cs[:, h:h + 1] - a_t[h:h + 1, :]
                ms.append((cb * jnp.where(causal, jnp.exp(seg), 0.0)).astype(BF16))
            xp = xdt_b[:, (h0 // 2) * LANES:(h0 // 2 + 1) * LANES]
            yy = _dot(jnp.concatenate(ms, axis=0), xp)
            pair_out.append(jnp.where(lo, yy[0:lc], yy[lc:2 * lc]))
        y_parts.append(jnp.concatenate(pair_out, axis=-1) + y_off)
    y = jnp.concatenate(y_parts, axis=-1) + dskip_ref[...] * xs
    z = zx_ref[:, 0:SSM_D_INNER].astype(F32)
    y_ref[...] = _gate_norm(y, z, gn_ref[...]).astype(y_ref.dtype)

    @pl.when(c == last)
    def _():
        nssm_ref[...] = st_ref[...].T


def _ssd_prompt(zx, dt, b, params):
    m = zx.shape[0]
    nc = m // b // SSM_CHUNK
    cw, cb, dtb, a, dskip, gn, e = params
    full = lambda x: pl.BlockSpec(x.shape, lambda bb, c: (0, 0))
    row = lambda w: pl.BlockSpec((SSM_CHUNK, w), lambda bb, c: (bb * nc + c, 0))
    return pl.pallas_call(
        _ssd_prompt_kernel,
        grid=(b, nc),
        in_specs=[row(SSM_MAIN), row(LANES)] + [full(x) for x in params],
        out_specs=[row(SSM_D_INNER),
                   pl.BlockSpec((None, SSM_CONV - 1, SSM_CONV_DIM), lambda bb, c: (bb, 0, 0)),
                   pl.BlockSpec((None, SSM_D_INNER, SSM_STATE), lambda bb, c: (bb, 0, 0))],
        out_shape=[jax.ShapeDtypeStruct((m, SSM_D_INNER), BF16),
                   jax.ShapeDtypeStruct((b, SSM_CONV - 1, SSM_CONV_DIM), F32),
                   jax.ShapeDtypeStruct((b, SSM_D_INNER, SSM_STATE), F32)],
        scratch_shapes=[pltpu.VMEM((8 + SSM_CHUNK, SSM_CONV_DIM), F32),
                        pltpu.VMEM((SSM_STATE, SSM_D_INNER), F32)],
        compiler_params=_params(("parallel", "arbitrary")),
        name="ssd_prompt",
    )(zx, dt, *params)


def _ssd_sample_kernel(zx_ref, dt_ref, conv0_ref, s0_ref, cw_ref, cb_ref, dtb_ref, a_ref, dskip_ref, gn_ref,
                       e_ref, eg_ref, y_ref, nconv_ref, nssm_ref, xpad_ref):
    lc = zx_ref.shape[0]
    xpad_ref[0:5, :] = jnp.zeros((5, SSM_CONV_DIM), F32)
    xpad_ref[5:8, :] = conv0_ref[...]
    xbc, tail = _conv_silu(zx_ref, xpad_ref, cw_ref, cb_ref, lc)
    nconv_ref[...] = tail

    xs = xbc[:, 0:SSM_D_INNER]
    bm = xbc[:, SSM_D_INNER:SSM_D_INNER + SSM_GROUPS * SSM_STATE]
    cm = xbc[:, SSM_D_INNER + SSM_GROUPS * SSM_STATE:]
    e = e_ref[...]
    dt = _softplus(dt_ref[...] + dtb_ref[...])
    dta = dt * a_ref[...]
    rows = [dta[0:1, :]]
    for i in range(1, lc):
        rows.append(rows[-1] + dta[i:i + 1, :])
    a_cs = jnp.concatenate(rows, axis=0)
    a_last = rows[-1]
    xdt = xs * _dot2(dt, e)
    xdte = (xdt * _dot2(jnp.exp(a_last - a_cs), e)).astype(BF16)
    ea_e = _dot2(jnp.exp(a_cs), e)
    cd_e = _dot2(jnp.broadcast_to(jnp.exp(a_last), (8, LANES)), e)[0:1, :]

    a_l = jnp.concatenate([a_cs] * lc, axis=0)
    a_s = jnp.concatenate([jnp.broadcast_to(a_cs[s:s + 1, :], (lc, LANES)) for s in range(lc)], axis=0)
    l_idx = lax.broadcasted_iota(jnp.int32, (lc * lc, 1), 0) % lc
    s_idx = lax.broadcasted_iota(jnp.int32, (lc * lc, 1), 0) // lc
    decay = jnp.where(l_idx >= s_idx, jnp.exp(a_l - a_s), 0.0)
    cb_prod = jnp.concatenate([cm * bm[s:s + 1, :] for s in range(lc)], axis=0)
    mix = _dot2(_dot2(cb_prod, eg_ref[...]) * decay, e)
    y = ea_e * 0.0
    for s in range(lc):
        y = y + mix[s * lc:(s + 1) * lc, :] * xdt[s:s + 1, :]

    s_prev = s0_ref[...].T
    s_b = s_prev.astype(BF16)
    bm_b = bm.astype(BF16)
    cm_b = cm.astype(BF16)
    y_off, s_parts = [], []
    for g in range(SSM_GROUPS):
        gs = slice(g * SSM_STATE, (g + 1) * SSM_STATE)
        gw = slice(g * SSM_GROUP_WIDTH, (g + 1) * SSM_GROUP_WIDTH)
        y_off.append(_dot(cm_b[:, gs], s_b[:, gw]))
        s_parts.append(lax.dot_general(bm_b[:, gs], xdte[:, gw], (((0,), (0,)), ((), ())),
                                       preferred_element_type=F32))
    y = y + jnp.concatenate(y_off, axis=-1) * ea_e + dskip_ref[...] * xs
    s_new = s_prev * cd_e + jnp.concatenate(s_parts, axis=-1)
    nssm_ref[...] = s_new.T
    z = zx_ref[:, 0:SSM_D_INNER]
    y_ref[...] = _gate_norm(y, z, gn_ref[...])


def _ssd_sample(zx, dt, conv0, s0, params):
    b = conv0.shape[0]
    t_new = zx.shape[0] // b
    full = lambda x: pl.BlockSpec(x.shape, lambda bb: (0, 0))
    row = lambda w: pl.BlockSpec((t_new, w), lambda bb: (bb, 0))
    conv_spec = pl.BlockSpec((None, SSM_CONV - 1, SSM_CONV_DIM), lambda bb: (bb, 0, 0))
    state_spec = pl.BlockSpec((None, SSM_D_INNER, SSM_STATE), lambda bb: (bb, 0, 0))
    return pl.pallas_call(
        _ssd_sample_kernel,
        grid=(b,),
        in_specs=[row(SSM_MAIN), row(LANES), conv_spec, state_spec] + [full(x) for x in params],
        out_specs=[row(SSM_D_INNER), conv_spec, state_spec],
        out_shape=[jax.ShapeDtypeStruct((b * t_new, SSM_D_INNER), F32),
                   jax.ShapeDtypeStruct((b, SSM_CONV - 1, SSM_CONV_DIM), F32),
                   jax.ShapeDtypeStruct((b, SSM_D_INNER, SSM_STATE), F32)],
        scratch_shapes=[pltpu.VMEM((8 + t_new, SSM_CONV_DIM), F32)],
        compiler_params=_params(("parallel",)),
        name="ssd_sample",
    )(zx, dt, conv0, s0, *params)


def _one_hot_expand(n_rows, n_cols, group):
    r = lax.broadcasted_iota(jnp.int32, (n_rows, n_cols), 0)
    c = lax.broadcasted_iota(jnp.int32, (n_rows, n_cols), 1)
    return (c // group == r).astype(BF16)


def _unpermute_tail(tail, keep):
    b, n_chunks, d, rc, w = tail.shape
    x = jnp.transpose(tail, (0, 3, 1, 2, 4)).reshape(b, rc * n_chunks * d, w)
    return x[:, x.shape[1] - keep:]


def kernel(x_prompt, x_sample, cache_kv_g0, cache_kv_g1, cache_kv_g2, state_conv, state_ssm, attn_norm, attn_w_in, attn_q_gain, attn_k_gain, attn_w_out, ssm_norm, ssm_w_in, ssm_conv_w, ssm_conv_b, ssm_dt_bias, ssm_A_log, ssm_D, ssm_gate_norm, ssm_w_out):
    b, seq, dm = x_prompt.shape
    sb, st, _ = x_sample.shape
    caches = (cache_kv_g0[0], cache_kv_g1[0], cache_kv_g2[0])
    xp = x_prompt.reshape(b * seq, dm)
    xs = x_sample.reshape(sb * st, dm)

    w_in = attn_w_in[0].astype(BF16)
    w_out = attn_w_out[0].astype(BF16)
    scale = HEAD_DIM ** -0.5
    q_gain3 = jnp.tile(attn_q_gain[0], (1, N_HEADS)) * scale
    k_gain3 = jnp.tile(attn_k_gain[0], (1, N_HEADS))

    qkvg = _proj(xp, attn_norm[0], w_in, tm=1024, tn=1024, out_dtype=BF16, perm_seq=seq)
    tails = [None] * 3
    tails[2], state = _attn_group(qkvg, 2, q_gain3[2:3], k_gain3[2:3], None, rb=128, first=True, last=False)
    tails[1], state = _attn_group(qkvg, 1, q_gain3[1:2], k_gain3[1:2], state, rb=128, first=False, last=False)
    tails[0], o_gated = _attn_group(qkvg, 0, q_gain3[0:1], k_gain3[0:1], state, rb=256, first=False, last=True)
    y1p = _outproj(o_gated, w_out, xp, tm=1024, perm_seq=seq)
    kv_p = []
    for g, d in enumerate(DILATIONS):
        keep = min(d * WINDOW_STEPS, seq)
        kv_p.append(_unpermute_tail(tails[g], keep).reshape(1, b, keep, 2, N_HEADS, HEAD_DIM))

    proj_s = _proj(xs, attn_norm[0], w_in, tm=sb * st, tn=1024, out_dtype=F32)
    o_s, kv0, kv1, kv2 = _sample_attn(proj_s.reshape(sb * st, ATT_IN // ATT_WIDTH, N_HEADS, HEAD_DIM), caches,
                                      attn_q_gain[0] * scale, attn_k_gain[0])
    y1s = _outproj(o_s.reshape(sb * st, ATT_WIDTH), w_out, xs, tm=sb * st)
    kv_s = [a[None] for a in (kv0, kv1, kv2)]

    w_in2 = ssm_w_in[0]
    w_main = w_in2[:, :SSM_MAIN].astype(BF16)
    w_dt = jnp.pad(w_in2[:, SSM_MAIN:], ((0, 0), (0, LANES - SSM_HEADS))).astype(BF16)
    w_out2 = ssm_w_out[0].astype(BF16)
    pad_h = lambda v: jnp.pad(v.astype(F32), (0, LANES - SSM_HEADS)).reshape(1, LANES)
    e32 = _one_hot_expand(LANES, SSM_D_INNER, SSM_D_INNER // SSM_HEADS)
    ssm_params = (ssm_conv_w[0], ssm_conv_b[0].reshape(1, -1), pad_h(ssm_dt_bias[0]),
                  pad_h(-jnp.exp(ssm_A_log[0].astype(F32))),
                  jnp.repeat(ssm_D[0].astype(F32), SSM_D_INNER // SSM_HEADS).reshape(1, -1),
                  ssm_gate_norm[0].reshape(1, -1), e32)

    zx_p, dt_p = _proj(y1p, ssm_norm[0], w_main, tm=1024, tn=1024, out_dtype=BF16, w_dt=w_dt)
    yg_p, conv_p, ssm_p = _ssd_prompt(zx_p, dt_p, b, ssm_params)
    y2p = _outproj(yg_p, w_out2, y1p, tm=1024)

    eg = (lax.broadcasted_iota(jnp.int32, (SSM_GROUPS * SSM_STATE, LANES), 0) // SSM_STATE
          == lax.broadcasted_iota(jnp.int32, (SSM_GROUPS * SSM_STATE, LANES), 1) // (SSM_HEADS // SSM_GROUPS))
    eg = (eg & (lax.broadcasted_iota(jnp.int32, eg.shape, 1) < SSM_HEADS)).astype(BF16)
    zx_s, dt_s = _proj(y1s, ssm_norm[0], w_main, tm=sb * st, tn=1024, out_dtype=F32, w_dt=w_dt)
    yg_s, conv_s, ssm_s = _ssd_sample(zx_s, dt_s, state_conv[0], state_ssm[0].reshape(sb, SSM_D_INNER, SSM_STATE),
                                      ssm_params + (eg,))
    y2s = _outproj(yg_s, w_out2, y1s, tm=sb * st)

    hp = SSM_D_INNER // SSM_HEADS
    return (y2p.reshape(b, seq, dm), y2s.reshape(sb, st, dm),
            kv_p[0], kv_p[1], kv_p[2], kv_s[0], kv_s[1], kv_s[2],
            conv_p[None], conv_s[None],
            ssm_p.reshape(1, b, SSM_HEADS, hp, SSM_STATE), ssm_s.reshape(1, sb, SSM_HEADS, hp, SSM_STATE))
```

```python
import functools

import jax
import jax.numpy as jnp
from jax import lax
from jax.experimental import pallas as pl
from jax.experimental.pallas import tpu as pltpu

F32 = jnp.float32
BF16 = jnp.bfloat16

D_MODEL = 1024
N_HEADS = 16
HEAD_DIM = 64
ATT_WIDTH = N_HEADS * HEAD_DIM
DILATIONS = (1, 4, 16)
WINDOW_STEPS = 128
ATT_IN = 10 * ATT_WIDTH
PERM = 16

SSM_D_INNER = 2048
SSM_HEADS = 32
SSM_STATE = 128
SSM_GROUPS = 4
SSM_GROUP_WIDTH = SSM_D_INNER // SSM_GROUPS
SSM_CONV = 4
SSM_CONV_DIM = SSM_D_INNER + 2 * SSM_GROUPS * SSM_STATE
SSM_MAIN = SSM_D_INNER + SSM_CONV_DIM
SSM_CHUNK = 128

NORM_EPS = 1e-6
GATE_NORM_EPS = 1e-5
MASKED = -1e30

LANES = 128
VMEM_LIMIT = 56 * 1024 * 1024

NT = (((1,), (1,)), ((), ()))
TN = (((0,), (0,)), ((), ()))


def _params(semantics):
    return pltpu.CompilerParams(dimension_semantics=semantics, vmem_limit_bytes=VMEM_LIMIT)


def _rms(x, w, eps):
    return x * lax.rsqrt(jnp.mean(x * x, axis=-1, keepdims=True) + eps) * w


def _sigmoid(x):
    return 1.0 / (1.0 + jnp.exp(-x))


def _split2(v):
    hi = v.astype(BF16)
    lo = (v - hi.astype(F32)).astype(BF16)
    return hi, lo


def _dot(a, b):
    return jnp.dot(a, b, preferred_element_type=F32)


def _dot_nt(a, b):
    return lax.dot_general(a, b, NT, preferred_element_type=F32)


def _dot2(v, e):
    hi, lo = _split2(v)
    return _dot(hi, e) + _dot(lo, e)


def _low_half():
    return lax.broadcasted_iota(jnp.int32, (1, LANES), 1) < HEAD_DIM


def _head_norm(x, eps):
    lo = _low_half()
    parts = []
    for j in range(x.shape[1] // LANES):
        t = x[:, j * LANES:(j + 1) * LANES]
        t2 = t * t
        s_lo = jnp.sum(jnp.where(lo, t2, 0.0), axis=-1, keepdims=True)
        s_hi = jnp.sum(jnp.where(lo, 0.0, t2), axis=-1, keepdims=True)
        r = jnp.where(lo, lax.rsqrt(s_lo * (1.0 / HEAD_DIM) + eps), lax.rsqrt(s_hi * (1.0 / HEAD_DIM) + eps))
        parts.append(t * r)
    return jnp.concatenate(parts, axis=-1)


def _split_heads(qp):
    lo = _low_half()
    zero = jnp.zeros_like(qp)
    return jnp.concatenate([jnp.where(lo, qp, zero), jnp.where(lo, zero, qp)], axis=0)


def _join_heads(x2):
    rows = x2.shape[0] // 2
    return jnp.where(_low_half(), x2[0:rows], x2[rows:])


def _proj_kernel(x_ref, nw_ref, w_ref, *rest, perm, with_dt):
    if with_dt:
        wdt_ref, o_ref, dt_ref, h_ref = rest
    elif perm:
        o_ref, h_ref, slab_ref = rest
    else:
        o_ref, h_ref = rest

    @pl.when(pl.program_id(1) == 0)
    def _():
        xn = _rms(x_ref[...], nw_ref[...], NORM_EPS)
        if perm:
            rows = x_ref.shape[0] // PERM
            for c in range(x_ref.shape[1] // LANES):
                cols = slice(c * LANES, (c + 1) * LANES)
                slab_ref[...] = xn[:, cols]
                for r in range(PERM):
                    h_ref[r * rows:(r + 1) * rows, cols] = slab_ref[pl.ds(r, rows, stride=PERM), :].astype(BF16)
        else:
            h_ref[...] = xn.astype(BF16)
        if with_dt:
            dt_ref[...] = _dot(h_ref[...], wdt_ref[...])

    res = _dot(h_ref[...], w_ref[...]).astype(o_ref.dtype)
    o_ref[...] = res.reshape(o_ref.shape)


def _proj(x, norm_w, w, *, tm, tn, out_dtype, perm_seq=None, w_dt=None):
    m, dm = x.shape
    n = w.shape[1]
    grid = (m // tm, n // tn)
    in_specs = [pl.BlockSpec((tm, dm), lambda i, j: (i, 0)),
                pl.BlockSpec((1, dm), lambda i, j: (0, 0)),
                pl.BlockSpec((dm, tn), lambda i, j: (0, j))]
    args = [x, norm_w.reshape(1, dm), w]
    if perm_seq is not None:
        per_b = perm_seq // tm
        rows = tm // PERM
        out_shape = [jax.ShapeDtypeStruct((m // perm_seq, PERM, perm_seq // PERM, n), out_dtype)]
        out_specs = [pl.BlockSpec((None, PERM, rows, tn), lambda i, j: (i // per_b, 0, i % per_b, j))]
    else:
        out_shape = [jax.ShapeDtypeStruct((m, n), out_dtype)]
        out_specs = [pl.BlockSpec((tm, tn), lambda i, j: (i, j))]
    if w_dt is not None:
        in_specs.append(pl.BlockSpec((dm, LANES), lambda i, j: (0, 0)))
        args.append(w_dt)
        out_shape.append(jax.ShapeDtypeStruct((m, LANES), F32))
        out_specs.append(pl.BlockSpec((tm, LANES), lambda i, j: (i, 0)))
    outs = pl.pallas_call(
        functools.partial(_proj_kernel, perm=perm_seq is not None, with_dt=w_dt is not None),
        grid=grid, in_specs=in_specs, out_specs=out_specs, out_shape=out_shape,
        scratch_shapes=[pltpu.VMEM((tm, dm), BF16)] + ([pltpu.VMEM((tm, LANES), F32)] if perm_seq else []),
        compiler_params=_params(("parallel", "arbitrary")),
        name="norm_proj",
    )(*args)
    return outs if w_dt is not None else outs[0]


def _outproj_kernel(a_ref, w_ref, x_ref, o_ref, *scratch, unperm):
    tm = o_ref.shape[0]
    a = a_ref[...].reshape(tm, a_ref.shape[-1]).astype(BF16)
    res = _dot(a, w_ref[...])
    if unperm:
        slab_ref, = scratch
        rows = tm // PERM
        for c in range(o_ref.shape[1] // LANES):
            cols = slice(c * LANES, (c + 1) * LANES)
            for r in range(PERM):
                slab_ref[pl.ds(r, rows, stride=PERM), :] = res[r * rows:(r + 1) * rows, cols]
            o_ref[:, cols] = x_ref[:, cols] + slab_ref[...]
    else:
        o_ref[...] = x_ref[...] + res


def _outproj(a, w, x, *, tm, perm_seq=None):
    m, n = x.shape
    k = w.shape[0]
    if perm_seq is not None:
        per_b = perm_seq // tm
        a_spec = pl.BlockSpec((None, PERM, tm // PERM, k), lambda i: (i // per_b, 0, i % per_b, 0))
    else:
        a_spec = pl.BlockSpec((tm, k), lambda i: (i, 0))
    return pl.pallas_call(
        functools.partial(_outproj_kernel, unperm=perm_seq is not None),
        grid=(m // tm,),
        in_specs=[a_spec, pl.BlockSpec((k, n), lambda i: (0, 0)), pl.BlockSpec((tm, n), lambda i: (i, 0))],
        out_specs=pl.BlockSpec((tm, n), lambda i: (i, 0)),
        out_shape=jax.ShapeDtypeStruct((m, n), F32),
        scratch_shapes=[pltpu.VMEM((tm, LANES), F32)] if perm_seq else [],
        compiler_params=_params(("parallel",)),
        name="out_proj",
    )(a, w, x)


def _attn_group_kernel(*refs, n_chunks, merge):
    it = iter(refs)
    q_ref, k_ref, v_ref, qg_ref, kg_ref = (next(it) for _ in range(5))
    if merge:
        gate_ref = next(it)
        others = [(next(it), next(it)) for _ in range(len(DILATIONS) - 1)]
        o_out = next(it)
    else:
        o_out, lse_out = next(it), next(it)
    ks_ref, vs_ref = next(it), next(it)

    rc = q_ref.shape[1]
    rb = n_chunks * rc
    n = pl.program_id(2)

    @pl.when(n == 0)
    def _():
        ks_ref[0:rb, :] = jnp.zeros((rb, ATT_WIDTH), BF16)
        vs_ref[0:rb, :] = jnp.zeros((rb, ATT_WIDTH), BF16)

    q = q_ref[...].reshape(rb, ATT_WIDTH).astype(F32)
    k = k_ref[...].reshape(rb, ATT_WIDTH).astype(F32)
    qn = (_head_norm(q, NORM_EPS) * qg_ref[...]).astype(BF16)
    ks_ref[rb:2 * rb, :] = (_head_norm(k, NORM_EPS) * kg_ref[...]).astype(BF16)
    vs_ref[rb:2 * rb, :] = v_ref[...].reshape(rb, ATT_WIDTH)

    qi = lax.broadcasted_iota(jnp.int32, (rb, 2 * rb), 0)
    kj = lax.broadcasted_iota(jnp.int32, (rb, 2 * rb), 1)
    kc = jnp.where(kj >= rb, kj - rb, kj)
    tq = n_chunks * (qi % rc) + qi // rc
    tk = n_chunks * (kc % rc) + kc // rc + jnp.where(kj >= rb, 0, -rb)
    dist = tq - tk
    valid = (dist >= 0) & (dist <= WINDOW_STEPS) & ((kj >= rb) | (n > 0))
    bias = jnp.where(valid, 0.0, MASKED)
    bias2 = jnp.concatenate([bias, bias], axis=0)

    for j in range(N_HEADS // 2):
        cols = slice(j * LANES, (j + 1) * LANES)
        s = _dot_nt(_split_heads(qn[:, cols]), ks_ref[:, cols]) + bias2
        m = jnp.max(s, axis=-1, keepdims=True)
        p = jnp.exp(s - m)
        l = jnp.sum(p, axis=-1, keepdims=True)
        pv = _dot(p.astype(BF16), vs_ref[:, cols])
        o = _join_heads(pv / l)
        lse = _join_heads(jnp.broadcast_to(m + jnp.log(l), (2 * rb, LANES)))
        shape3 = (n_chunks, rc, LANES)
        if merge:
            o_g = [o] + [o_ref[:, :, cols].reshape(rb, LANES).astype(F32) for o_ref, _ in others]
            lse_g = [lse] + [l_ref[:, :, cols].reshape(rb, LANES) for _, l_ref in others]
            top = jnp.maximum(jnp.maximum(lse_g[0], lse_g[1]), lse_g[2])
            w = [jnp.exp(x - top) for x in lse_g]
            o = (w[0] * o_g[0] + w[1] * o_g[1] + w[2] * o_g[2]) / (w[0] + w[1] + w[2])
            g = gate_ref[:, :, cols].reshape(rb, LANES).astype(F32)
            o_out[:, :, cols] = (o * (g * _sigmoid(g))).astype(BF16).reshape(shape3)
        else:
            o_out[:, :, cols] = o.astype(BF16).reshape(shape3)
            lse_out[:, :, cols] = lse.reshape(shape3)
    ks_ref[0:rb, :] = ks_ref[rb:2 * rb, :]
    vs_ref[0:rb, :] = vs_ref[rb:2 * rb, :]


def _attn_group(qkvg, g, q_gain, k_gain, *, rb, others=None):
    b, _, t, _ = qkvg.shape
    d = DILATIONS[g]
    n_chunks = PERM // d
    rc = rb // n_chunks
    nb = t // rc
    merge = others is not None
    view = lambda a: a.reshape(b, n_chunks, d, t, a.shape[-1])

    def spec(col=0):
        return pl.BlockSpec((None, n_chunks, None, rc, ATT_WIDTH), lambda bb, r, n: (bb, 0, r, n, col))

    qkvg5 = view(qkvg)
    vec = pl.BlockSpec((1, ATT_WIDTH), lambda bb, r, n: (0, 0))
    in_specs = [spec(3 * g), spec(3 * g + 1), spec(3 * g + 2), vec, vec]
    args = [qkvg5, qkvg5, qkvg5, q_gain, k_gain]
    full = jax.ShapeDtypeStruct((b, n_chunks, d, t, ATT_WIDTH), BF16)
    if merge:
        in_specs.append(spec(ATT_IN // ATT_WIDTH - 1))
        args.append(qkvg5)
        for o, lse in others:
            in_specs += [spec(), spec()]
            args += [view(o), view(lse)]
        out_shape, out_specs = [full], [spec()]
    else:
        out_shape = [full, jax.ShapeDtypeStruct(full.shape, F32)]
        out_specs = [spec(), spec()]
    outs = pl.pallas_call(
        functools.partial(_attn_group_kernel, n_chunks=n_chunks, merge=merge),
        grid=(b, d, nb), in_specs=in_specs, out_specs=out_specs, out_shape=out_shape,
        scratch_shapes=[pltpu.VMEM((2 * rb, ATT_WIDTH), BF16), pltpu.VMEM((2 * rb, ATT_WIDTH), BF16)],
        compiler_params=_params(("parallel", "parallel", "arbitrary")),
        name=f"attn_group{g}",
    )(*args)
    outs = [a.reshape(b, PERM, t, ATT_WIDTH) for a in outs]
    return outs[0] if merge else tuple(outs)


def _kv_tail_kernel(x_ref, nw_ref, wt_ref, kg_ref, o_ref):
    h = _rms(x_ref[...], nw_ref[...], NORM_EPS).astype(BF16)
    res = _dot_nt(wt_ref[...], h)
    tokens = res.shape[1]
    k = res[0:ATT_WIDTH].reshape(N_HEADS, HEAD_DIM, tokens)
    r = lax.rsqrt(jnp.mean(k * k, axis=1, keepdims=True) + NORM_EPS)
    o_ref[0:ATT_WIDTH, :] = (k * r).reshape(ATT_WIDTH, tokens) * kg_ref[...]
    o_ref[ATT_WIDTH:2 * ATT_WIDTH, :] = res[ATT_WIDTH:2 * ATT_WIDTH]


def _kv_tail(x, norm_w, wt, k_gain, *, b, keep, tm):
    seq = x.shape[0] // b
    first = (seq - keep) // tm
    kg = jnp.broadcast_to(jnp.tile(k_gain, N_HEADS)[:, None], (ATT_WIDTH, tm))
    return pl.pallas_call(
        _kv_tail_kernel,
        grid=(b, keep // tm),
        in_specs=[pl.BlockSpec((tm, D_MODEL), lambda bb, i: (bb * (seq // tm) + first + i, 0)),
                  pl.BlockSpec((1, D_MODEL), lambda bb, i: (0, 0)),
                  pl.BlockSpec((2 * ATT_WIDTH, D_MODEL), lambda bb, i: (0, 0)),
                  pl.BlockSpec((ATT_WIDTH, tm), lambda bb, i: (0, 0))],
        out_specs=pl.BlockSpec((None, 2 * ATT_WIDTH, tm), lambda bb, i: (bb, 0, i)),
        out_shape=jax.ShapeDtypeStruct((b, 2 * ATT_WIDTH, keep), F32),
        compiler_params=_params(("parallel", "parallel")),
        name="kv_tail",
    )(x, norm_w.reshape(1, D_MODEL), wt, kg)


SAMPLE_HEADS = 8


def _sample_attn_kernel(*refs):
    n_g = len(DILATIONS)
    qkv = [refs[3 * g:3 * g + 3] for g in range(n_g)]
    gate_ref = refs[3 * n_g]
    c_refs = refs[3 * n_g + 1:4 * n_g + 1]
    qg_ref, kg_ref = refs[4 * n_g + 1:4 * n_g + 3]
    o_ref = refs[4 * n_g + 3]
    kn_refs = refs[4 * n_g + 4:]
    t_new = o_ref.shape[0]
    t_row = lax.broadcasted_iota(jnp.int32, (2 * t_new, 1), 0) % t_new
    t_col = lax.broadcasted_iota(jnp.int32, (1, t_new), 1)

    for j in range(SAMPLE_HEADS // 2):
        cols = slice(j * LANES, (j + 1) * LANES)
        o_g, m_g, l_g = [], [], []
        for g, d in enumerate(DILATIONS):
            q_ref, k_ref, v_ref = qkv[g]
            qm = _split_heads(_head_norm(q_ref[:, cols], NORM_EPS) * qg_ref[g:g + 1, :]).astype(BF16)
            kn = _head_norm(k_ref[:, cols], NORM_EPS) * kg_ref[g:g + 1, :]
            kn_refs[g][:, cols] = kn
            length = c_refs[g].shape[2]
            s_c = _dot(qm, c_refs[g][0, cols, :].astype(BF16))
            s_n = _dot_nt(qm, kn.astype(BF16))
            back = length + t_row - lax.broadcasted_iota(jnp.int32, (1, length), 1)
            s_c = jnp.where((back % d == 0) & (back <= WINDOW_STEPS * d), s_c, MASKED)
            back = t_row - t_col
            s_n = jnp.where((back >= 0) & (back % d == 0) & (back <= WINDOW_STEPS * d), s_n, MASKED)
            m = jnp.maximum(jnp.max(s_c, axis=-1, keepdims=True), jnp.max(s_n, axis=-1, keepdims=True))
            p_c = jnp.exp(s_c - m)
            p_n = jnp.exp(s_n - m)
            l_g.append(jnp.sum(p_c, axis=-1, keepdims=True) + jnp.sum(p_n, axis=-1, keepdims=True))
            o_g.append(_dot_nt(p_c.astype(BF16), c_refs[g][1, cols, :].astype(BF16))
                       + _dot(p_n.astype(BF16), v_ref[:, cols].astype(BF16)))
            m_g.append(m)
        top = jnp.maximum(jnp.maximum(m_g[0], m_g[1]), m_g[2])
        f = [jnp.exp(m - top) for m in m_g]
        den = f[0] * l_g[0] + f[1] * l_g[1] + f[2] * l_g[2]
        o = _join_heads((f[0] * o_g[0] + f[1] * o_g[1] + f[2] * o_g[2]) / den)
        gate = gate_ref[:, cols]
        o_ref[:, cols] = o * (gate * _sigmoid(gate))


def _sample_attn(proj, caches_t, q_gain, k_gain):
    b = caches_t[0].shape[0]
    t_new = proj.shape[0] // b
    width = SAMPLE_HEADS * HEAD_DIM
    per_chunk = ATT_WIDTH // width
    col = lambda c: pl.BlockSpec((t_new, width), lambda bb, hc: (bb, c * per_chunk + hc))
    in_specs, args = [], []
    for g in range(len(DILATIONS)):
        in_specs += [col(3 * g), col(3 * g + 1), col(3 * g + 2)]
        args += [proj, proj, proj]
    in_specs.append(col(ATT_IN // ATT_WIDTH - 1))
    args.append(proj)
    for c in caches_t:
        in_specs.append(pl.BlockSpec((None, 2, width, c.shape[3]), lambda bb, hc: (bb, 0, hc, 0)))
        args.append(c)
    gain_spec = pl.BlockSpec((len(DILATIONS), LANES), lambda bb, hc: (0, 0))
    out_spec = pl.BlockSpec((t_new, width), lambda bb, hc: (bb, hc))
    out_sds = jax.ShapeDtypeStruct((b * t_new, ATT_WIDTH), F32)
    return pl.pallas_call(
        _sample_attn_kernel,
        grid=(b, per_chunk),
        in_specs=in_specs + [gain_spec, gain_spec],
        out_specs=[out_spec] * 4,
        out_shape=[out_sds] * 4,
        compiler_params=_params(("parallel", "parallel")),
        name="sample_attn",
    )(*args, q_gain, k_gain)


def _softplus(x):
    return jnp.maximum(x, 0.0) + jnp.log(1.0 + jnp.exp(-jnp.abs(x)))


def _conv_silu(zx_ref, xpad_ref, cw_ref, cb_ref, lc):
    pad = 8
    xpad_ref[pad:pad + lc, :] = zx_ref[:, SSM_D_INNER:SSM_MAIN].astype(F32)
    conv = cb_ref[...] + xpad_ref[pad - 3:pad - 3 + lc, :] * cw_ref[0:1, :]
    for j in range(1, SSM_CONV):
        conv = conv + xpad_ref[pad - 3 + j:pad - 3 + j + lc, :] * cw_ref[j:j + 1, :]
    tail = xpad_ref[pad + lc - 3:pad + lc, :]
    xpad_ref[pad - 3:pad, :] = tail
    return conv * _sigmoid(conv), tail


def _gate_norm(y, z, gn):
    y = y * (z * _sigmoid(z))
    parts = []
    for g in range(SSM_GROUPS):
        yg = y[:, g * SSM_GROUP_WIDTH:(g + 1) * SSM_GROUP_WIDTH]
        parts.append(yg * lax.rsqrt(jnp.mean(yg * yg, axis=-1, keepdims=True) + GATE_NORM_EPS))
    return jnp.concatenate(parts, axis=-1) * gn


def _cumsum_rows(x):
    rows = x.shape[0]
    tril = (lax.broadcasted_iota(jnp.int32, (rows, rows), 0)
            >= lax.broadcasted_iota(jnp.int32, (rows, rows), 1)).astype(BF16)
    h1 = x.astype(BF16)
    r1 = x - h1.astype(F32)
    h2 = r1.astype(BF16)
    h3 = (r1 - h2.astype(F32)).astype(BF16)
    return _dot(tril, h1) + _dot(tril, h2) + _dot(tril, h3)


def _ssd_prompt_kernel(zx_ref, dt_ref, cw_ref, cb_ref, dtb_ref, a_ref, dskip_ref, gn_ref, e_ref,
                       y_ref, nconv_ref, nssm_ref, xpad_ref, st_ref):
    lc = zx_ref.shape[0]
    c = pl.program_id(1)
    last = pl.num_programs(1) - 1

    @pl.when(c == 0)
    def _():
        xpad_ref[0:8, :] = jnp.zeros((8, SSM_CONV_DIM), F32)
        st_ref[...] = jnp.zeros(st_ref.shape, F32)

    xbc, tail = _conv_silu(zx_ref, xpad_ref, cw_ref, cb_ref, lc)

    @pl.when(c == last)
    def _():
        nconv_ref[...] = tail

    xs = xbc[:, 0:SSM_D_INNER]
    bm = xbc[:, SSM_D_INNER:SSM_D_INNER + SSM_GROUPS * SSM_STATE].astype(BF16)
    cm = xbc[:, SSM_D_INNER + SSM_GROUPS * SSM_STATE:].astype(BF16)
    e = e_ref[...]
    dt = _softplus(dt_ref[...] + dtb_ref[...])
    a_cs = _cumsum_rows(dt * a_ref[...])
    a_t = a_cs.T
    a_last = a_cs[lc - 1:lc, :]
    xdt = xs * _dot2(dt, e)
    xdt_b = xdt.astype(BF16)
    xdte = (xdt * _dot2(jnp.exp(a_last - a_cs), e)).astype(BF16)
    ea_e = _dot2(jnp.exp(a_cs), e)
    cd_e = _dot2(jnp.broadcast_to(jnp.exp(a_last), (8, LANES)), e)[0:1, :]
    s_prev = st_ref[...]
    s_b = s_prev.astype(BF16)
    causal = (lax.broadcasted_iota(jnp.int32, (lc, lc), 0) >= lax.broadcasted_iota(jnp.int32, (lc, lc), 1))
    heads_per_group = SSM_HEADS // SSM_GROUPS
    y_parts = []
    for g in range(SSM_GROUPS):
        gs = slice(g * SSM_STATE, (g + 1) * SSM_STATE)
        gw = slice(g * SSM_GROUP_WIDTH, (g + 1) * SSM_GROUP_WIDTH)
        cb = _dot_nt(cm[:, gs], bm[:, gs])
        y_off = _dot(cm[:, gs], s_b[:, gw]) * ea_e[:, gw]
        s_new = lax.dot_general(bm[:, gs], xdte[:, gw], TN, preferred_element_type=F32)
        st_ref[:, gw] = s_prev[:, gw] * cd_e[:, gw] + s_new
        pair_out = []
        for pr in range(heads_per_group // 2):
            h0 = g * heads_per_group + 2 * pr
            ms = []
            for h in (h0, h0 + 1):
                seg = a_cs[:, h:h + 1] - a_t[h:h + 1, :]
                ms.append((cb * jnp.where(causal, jnp.exp(seg), 0.0)).astype(BF16))
            xp = xdt_b[:, (h0 // 2) * LANES:(h0 // 2 + 1) * LANES]
            pair_out.append(_join_heads(_dot(jnp.concatenate(ms, axis=0), xp)))
        y_parts.append(jnp.concatenate(pair_out, axis=-1) + y_off)
    y = jnp.concatenate(y_parts, axis=-1) + dskip_ref[...] * xs
    z = zx_ref[:, 0:SSM_D_INNER].astype(F32)
    y_ref[...] = _gate_norm(y, z, gn_ref[...]).astype(y_ref.dtype)

    @pl.when(c == last)
    def _():
        nssm_ref[...] = st_ref[...].T


def _ssd_prompt(zx, dt, b, params):
    m = zx.shape[0]
    nc = m // b // SSM_CHUNK
    full = lambda x: pl.BlockSpec(x.shape, lambda bb, c: (0, 0))
    row = lambda w: pl.BlockSpec((SSM_CHUNK, w), lambda bb, c: (bb * nc + c, 0))
    return pl.pallas_call(
        _ssd_prompt_kernel,
        grid=(b, nc),
        in_specs=[row(SSM_MAIN), row(LANES)] + [full(x) for x in params],
        out_specs=[row(SSM_D_INNER),
                   pl.BlockSpec((None, SSM_CONV - 1, SSM_CONV_DIM), lambda bb, c: (bb, 0, 0)),
                   pl.BlockSpec((None, SSM_D_INNER, SSM_STATE), lambda bb, c: (bb, 0, 0))],
        out_shape=[jax.ShapeDtypeStruct((m, SSM_D_INNER), BF16),
                   jax.ShapeDtypeStruct((b, SSM_CONV - 1, SSM_CONV_DIM), F32),
                   jax.ShapeDtypeStruct((b, SSM_D_INNER, SSM_STATE), F32)],
        scratch_shapes=[pltpu.VMEM((8 + SSM_CHUNK, SSM_CONV_DIM), F32),
                        pltpu.VMEM((SSM_STATE, SSM_D_INNER), F32)],
        compiler_params=_params(("parallel", "arbitrary")),
        name="ssd_prompt",
    )(zx, dt, *params)


def _ssd_sample_kernel(zx_ref, dt_ref, conv0_ref, s0_ref, cw_ref, cb_ref, dtb_ref, a_ref, dskip_ref, gn_ref,
                       e_ref, eg_ref, y_ref, nconv_ref, nssm_ref, xpad_ref):
    lc = zx_ref.shape[0]
    xpad_ref[0:5, :] = jnp.zeros((5, SSM_CONV_DIM), F32)
    xpad_ref[5:8, :] = conv0_ref[...]
    xbc, tail = _conv_silu(zx_ref, xpad_ref, cw_ref, cb_ref, lc)
    nconv_ref[...] = tail

    xs = xbc[:, 0:SSM_D_INNER]
    bm = xbc[:, SSM_D_INNER:SSM_D_INNER + SSM_GROUPS * SSM_STATE]
    cm = xbc[:, SSM_D_INNER + SSM_GROUPS * SSM_STATE:]
    e = e_ref[...]
    dt = _softplus(dt_ref[...] + dtb_ref[...])
    dta = dt * a_ref[...]
    rows = [dta[0:1, :]]
    for i in range(1, lc):
        rows.append(rows[-1] + dta[i:i + 1, :])
    a_cs = jnp.concatenate(rows, axis=0)
    a_last = rows[-1]
    xdt = xs * _dot2(dt, e)
    xdte = (xdt * _dot2(jnp.exp(a_last - a_cs), e)).astype(BF16)
    ea_e = _dot2(jnp.exp(a_cs), e)
    cd_e = _dot2(jnp.broadcast_to(jnp.exp(a_last), (8, LANES)), e)[0:1, :]

    a_l = jnp.concatenate([a_cs] * lc, axis=0)
    a_s = jnp.concatenate([jnp.broadcast_to(a_cs[s:s + 1, :], (lc, LANES)) for s in range(lc)], axis=0)
    l_idx = lax.broadcasted_iota(jnp.int32, (lc * lc, 1), 0) % lc
    s_idx = lax.broadcasted_iota(jnp.int32, (lc * lc, 1), 0) // lc
    decay = jnp.where(l_idx >= s_idx, jnp.exp(a_l - a_s), 0.0)
    cb_prod = jnp.concatenate([cm * bm[s:s + 1, :] for s in range(lc)], axis=0)
    mix = _dot2(_dot2(cb_prod, eg_ref[...]) * decay, e)
    y = ea_e * 0.0
    for s in range(lc):
        y = y + mix[s * lc:(s + 1) * lc, :] * xdt[s:s + 1, :]

    s_prev = s0_ref[...].T
    s_b = s_prev.astype(BF16)
    bm_b = bm.astype(BF16)
    cm_b = cm.astype(BF16)
    y_off, s_parts = [], []
    for g in range(SSM_GROUPS):
        gs = slice(g * SSM_STATE, (g + 1) * SSM_STATE)
        gw = slice(g * SSM_GROUP_WIDTH, (g + 1) * SSM_GROUP_WIDTH)
        y_off.append(_dot(cm_b[:, gs], s_b[:, gw]))
        s_parts.append(lax.dot_general(bm_b[:, gs], xdte[:, gw], TN, preferred_element_type=F32))
    y = y + jnp.concatenate(y_off, axis=-1) * ea_e + dskip_ref[...] * xs
    s_new = s_prev * cd_e + jnp.concatenate(s_parts, axis=-1)
    nssm_ref[...] = s_new.T
    z = zx_ref[:, 0:SSM_D_INNER]
    y_ref[...] = _gate_norm(y, z, gn_ref[...])


def _ssd_sample(zx, dt, conv0, s0, params):
    b = conv0.shape[0]
    t_new = zx.shape[0] // b
    full = lambda x: pl.BlockSpec(x.shape, lambda bb: (0, 0))
    row = lambda w: pl.BlockSpec((t_new, w), lambda bb: (bb, 0))
    conv_spec = pl.BlockSpec((None, SSM_CONV - 1, SSM_CONV_DIM), lambda bb: (bb, 0, 0))
    state_spec = pl.BlockSpec((None, SSM_D_INNER, SSM_STATE), lambda bb: (bb, 0, 0))
    return pl.pallas_call(
        _ssd_sample_kernel,
        grid=(b,),
        in_specs=[row(SSM_MAIN), row(LANES), conv_spec, state_spec] + [full(x) for x in params],
        out_specs=[row(SSM_D_INNER), conv_spec, state_spec],
        out_shape=[jax.ShapeDtypeStruct((b * t_new, SSM_D_INNER), F32),
                   jax.ShapeDtypeStruct((b, SSM_CONV - 1, SSM_CONV_DIM), F32),
                   jax.ShapeDtypeStruct((b, SSM_D_INNER, SSM_STATE), F32)],
        scratch_shapes=[pltpu.VMEM((8 + t_new, SSM_CONV_DIM), F32)],
        compiler_params=_params(("parallel",)),
        name="ssd_sample",
    )(zx, dt, conv0, s0, *params)


def _one_hot_expand(n_rows, n_cols, group):
    r = lax.broadcasted_iota(jnp.int32, (n_rows, n_cols), 0)
    c = lax.broadcasted_iota(jnp.int32, (n_rows, n_cols), 1)
    return (c // group == r).astype(BF16)


def _kv_rows(feature_major, b):
    tokens = feature_major.shape[-1]
    x = feature_major.reshape(1, b, 2, N_HEADS, HEAD_DIM, tokens)
    return jnp.transpose(x, (0, 1, 5, 2, 3, 4))


def kernel(x_prompt, x_sample, cache_kv_g0, cache_kv_g1, cache_kv_g2, state_conv, state_ssm, attn_norm, attn_w_in, attn_q_gain, attn_k_gain, attn_w_out, ssm_norm, ssm_w_in, ssm_conv_w, ssm_conv_b, ssm_dt_bias, ssm_A_log, ssm_D, ssm_gate_norm, ssm_w_out):
    b, seq, dm = x_prompt.shape
    sb, st, _ = x_sample.shape
    xp = x_prompt.reshape(b * seq, dm)
    xs = x_sample.reshape(sb * st, dm)

    w_in = attn_w_in[0].astype(BF16)
    w_out = attn_w_out[0].astype(BF16)
    scale = HEAD_DIM ** -0.5
    q_gain3 = jnp.tile(attn_q_gain[0], (1, N_HEADS)) * scale
    k_gain3 = jnp.tile(attn_k_gain[0], (1, N_HEADS))

    qkvg = _proj(xp, attn_norm[0], w_in, tm=1024, tn=1024, out_dtype=BF16, perm_seq=seq)
    part2 = _attn_group(qkvg, 2, q_gain3[2:3], k_gain3[2:3], rb=128)
    part1 = _attn_group(qkvg, 1, q_gain3[1:2], k_gain3[1:2], rb=128)
    o_gated = _attn_group(qkvg, 0, q_gain3[0:1], k_gain3[0:1], rb=256, others=[part1, part2])
    y1p = _outproj(o_gated, w_out, xp, tm=1024, perm_seq=seq)
    kv_p = []
    for g, d in enumerate(DILATIONS):
        keep = min(d * WINDOW_STEPS, seq)
        wt = w_in[:, (3 * g + 1) * ATT_WIDTH:(3 * g + 3) * ATT_WIDTH].T
        kv_p.append(_kv_rows(_kv_tail(xp, attn_norm[0], wt, attn_k_gain[0, g], b=b, keep=keep,
                                      tm=min(keep, 512)), b))

    proj_s = _proj(xs, attn_norm[0], w_in, tm=sb * st, tn=1024, out_dtype=F32)
    caches_t = [jnp.transpose(c[0], (0, 2, 3, 4, 1)).reshape(sb, 2, ATT_WIDTH, c.shape[2])
                for c in (cache_kv_g0, cache_kv_g1, cache_kv_g2)]
    o_s, kn0, kn1, kn2 = _sample_attn(proj_s, caches_t, q_gain3[:, :LANES], k_gain3[:, :LANES])
    y1s = _outproj(o_s, w_out, xs, tm=sb * st)
    kv_s = []
    for g, kn in enumerate((kn0, kn1, kn2)):
        v = proj_s[:, (3 * g + 2) * ATT_WIDTH:(3 * g + 3) * ATT_WIDTH]
        kv_s.append(jnp.stack([kn, v], axis=1).reshape(1, sb, st, 2, N_HEADS, HEAD_DIM))

    w_in2 = ssm_w_in[0]
    w_main = w_in2[:, :SSM_MAIN].astype(BF16)
    w_dt = jnp.pad(w_in2[:, SSM_MAIN:], ((0, 0), (0, LANES - SSM_HEADS))).astype(BF16)
    w_out2 = ssm_w_out[0].astype(BF16)
    pad_h = lambda v: jnp.pad(v.astype(F32), (0, LANES - SSM_HEADS)).reshape(1, LANES)
    e32 = _one_hot_expand(LANES, SSM_D_INNER, SSM_D_INNER // SSM_HEADS)
    ssm_params = (ssm_conv_w[0], ssm_conv_b[0].reshape(1, -1), pad_h(ssm_dt_bias[0]),
                  pad_h(-jnp.exp(ssm_A_log[0].astype(F32))),
                  jnp.repeat(ssm_D[0].astype(F32), SSM_D_INNER // SSM_HEADS).reshape(1, -1),
                  ssm_gate_norm[0].reshape(1, -1), e32)

    zx_p, dt_p = _proj(y1p, ssm_norm[0], w_main, tm=1024, tn=1024, out_dtype=BF16, w_dt=w_dt)
    yg_p, conv_p, ssm_p = _ssd_prompt(zx_p, dt_p, b, ssm_params)
    y2p = _outproj(yg_p, w_out2, y1p, tm=1024)

    eg = (lax.broadcasted_iota(jnp.int32, (SSM_GROUPS * SSM_STATE, LANES), 0) // SSM_STATE
          == lax.broadcasted_iota(jnp.int32, (SSM_GROUPS * SSM_STATE, LANES), 1) // (SSM_HEADS // SSM_GROUPS))
    eg = (eg & (lax.broadcasted_iota(jnp.int32, eg.shape, 1) < SSM_HEADS)).astype(BF16)
    zx_s, dt_s = _proj(y1s, ssm_norm[0], w_main, tm=sb * st, tn=1024, out_dtype=F32, w_dt=w_dt)
    yg_s, conv_s, ssm_s = _ssd_sample(zx_s, dt_s, state_conv[0], state_ssm[0].reshape(sb, SSM_D_INNER, SSM_STATE),
                                      ssm_params + (eg,))
    y2s = _outproj(yg_s, w_out2, y1s, tm=sb * st)

    hp = SSM_D_INNER // SSM_HEADS
    return (y2p.reshape(b, seq, dm), y2s.reshape(sb, st, dm),
            kv_p[0], kv_p[1], kv_p[2], kv_s[0], kv_s[1], kv_s[2],
            conv_p[None], conv_s[None],
            ssm_p.reshape(1, b, SSM_HEADS, hp, SSM_STATE), ssm_s.reshape(1, sb, SSM_HEADS, hp, SSM_STATE))
```

```python
import functools

import jax
import jax.numpy as jnp
from jax import lax
from jax.experimental import pallas as pl
from jax.experimental.pallas import tpu as pltpu

F32 = jnp.float32
BF16 = jnp.bfloat16

D_MODEL = 1024
N_HEADS = 16
HEAD_DIM = 64
ATT_WIDTH = N_HEADS * HEAD_DIM
DILATIONS = (1, 4, 16)
WINDOW_STEPS = 128
ATT_IN = 10 * ATT_WIDTH
PERM = 16

SSM_D_INNER = 2048
SSM_HEADS = 32
SSM_STATE = 128
SSM_GROUPS = 4
SSM_GROUP_WIDTH = SSM_D_INNER // SSM_GROUPS
SSM_CONV = 4
SSM_CONV_DIM = SSM_D_INNER + 2 * SSM_GROUPS * SSM_STATE
SSM_MAIN = SSM_D_INNER + SSM_CONV_DIM
SSM_CHUNK = 128

NORM_EPS = 1e-6
GATE_NORM_EPS = 1e-5
MASKED = -1e30

LANES = 128
VMEM_LIMIT = 56 * 1024 * 1024

NT = (((1,), (1,)), ((), ()))
TN = (((0,), (0,)), ((), ()))


def _params(semantics):
    return pltpu.CompilerParams(dimension_semantics=semantics, vmem_limit_bytes=VMEM_LIMIT)


def _rms(x, w, eps):
    return x * lax.rsqrt(jnp.mean(x * x, axis=-1, keepdims=True) + eps) * w


def _sigmoid(x):
    return 1.0 / (1.0 + jnp.exp(-x))


def _split2(v):
    hi = v.astype(BF16)
    lo = (v - hi.astype(F32)).astype(BF16)
    return hi, lo


def _dot(a, b):
    return jnp.dot(a, b, preferred_element_type=F32)


def _dot_nt(a, b):
    return lax.dot_general(a, b, NT, preferred_element_type=F32)


def _dot2(v, e):
    hi, lo = _split2(v)
    return _dot(hi, e) + _dot(lo, e)


def _low_half():
    return lax.broadcasted_iota(jnp.int32, (1, LANES), 1) < HEAD_DIM


def _head_norm(x, eps):
    lo = _low_half()
    parts = []
    for j in range(x.shape[1] // LANES):
        t = x[:, j * LANES:(j + 1) * LANES]
        t2 = t * t
        s_lo = jnp.sum(jnp.where(lo, t2, 0.0), axis=-1, keepdims=True)
        s_hi = jnp.sum(jnp.where(lo, 0.0, t2), axis=-1, keepdims=True)
        r = jnp.where(lo, lax.rsqrt(s_lo * (1.0 / HEAD_DIM) + eps), lax.rsqrt(s_hi * (1.0 / HEAD_DIM) + eps))
        parts.append(t * r)
    return jnp.concatenate(parts, axis=-1)


def _split_heads(qp):
    lo = _low_half()
    zero = jnp.zeros_like(qp)
    return jnp.concatenate([jnp.where(lo, qp, zero), jnp.where(lo, zero, qp)], axis=0)


def _join_heads(x2):
    rows = x2.shape[0] // 2
    return jnp.where(_low_half(), x2[0:rows], x2[rows:])


def _proj_kernel(x_ref, nw_ref, w_ref, *rest, perm, with_dt, qk_norm):
    rest = list(rest)
    gain_ref = rest.pop(0) if qk_norm else None
    wdt_ref = rest.pop(0) if with_dt else None
    o_ref = rest.pop(0)
    dt_ref = rest.pop(0) if with_dt else None
    h_ref = rest.pop(0)
    slab_ref = rest.pop(0) if perm else None

    @pl.when(pl.program_id(1) == 0)
    def _():
        xn = _rms(x_ref[...], nw_ref[...], NORM_EPS)
        if perm:
            rows = x_ref.shape[0] // PERM
            for c in range(x_ref.shape[1] // LANES):
                cols = slice(c * LANES, (c + 1) * LANES)
                slab_ref[...] = xn[:, cols]
                for r in range(PERM):
                    h_ref[r * rows:(r + 1) * rows, cols] = slab_ref[pl.ds(r, rows, stride=PERM), :].astype(BF16)
        else:
            h_ref[...] = xn.astype(BF16)
        if with_dt:
            dt_ref[...] = _dot(h_ref[...], wdt_ref[...])

    if qk_norm:
        j = pl.program_id(1)
        is_qk = (j < 3 * len(DILATIONS)) & (j % 3 < 2)

        @pl.when(is_qk)
        def _():
            res = _head_norm(_dot(h_ref[...], w_ref[...]), NORM_EPS) * gain_ref[0]
            o_ref[...] = res.astype(o_ref.dtype).reshape(o_ref.shape)

        @pl.when(jnp.logical_not(is_qk))
        def _():
            o_ref[...] = _dot(h_ref[...], w_ref[...]).astype(o_ref.dtype).reshape(o_ref.shape)
    else:
        o_ref[...] = _dot(h_ref[...], w_ref[...]).astype(o_ref.dtype).reshape(o_ref.shape)


def _proj(x, norm_w, w, *, tm, tn, out_dtype, perm_seq=None, w_dt=None, qk_gain=None):
    m, dm = x.shape
    n = w.shape[1]
    grid = (m // tm, n // tn)
    in_specs = [pl.BlockSpec((tm, dm), lambda i, j: (i, 0)),
                pl.BlockSpec((1, dm), lambda i, j: (0, 0)),
                pl.BlockSpec((dm, tn), lambda i, j: (0, j))]
    args = [x, norm_w.reshape(1, dm), w]
    if qk_gain is not None:
        assert tn == ATT_WIDTH
        in_specs.append(pl.BlockSpec((1, 1, tn), lambda i, j: (j, 0, 0)))
        args.append(qk_gain)
    if perm_seq is not None:
        per_b = perm_seq // tm
        rows = tm // PERM
        out_shape = [jax.ShapeDtypeStruct((m // perm_seq, PERM, perm_seq // PERM, n), out_dtype)]
        out_specs = [pl.BlockSpec((None, PERM, rows, tn), lambda i, j: (i // per_b, 0, i % per_b, j))]
    else:
        out_shape = [jax.ShapeDtypeStruct((m, n), out_dtype)]
        out_specs = [pl.BlockSpec((tm, tn), lambda i, j: (i, j))]
    if w_dt is not None:
        in_specs.append(pl.BlockSpec((dm, LANES), lambda i, j: (0, 0)))
        args.append(w_dt)
        out_shape.append(jax.ShapeDtypeStruct((m, LANES), F32))
        out_specs.append(pl.BlockSpec((tm, LANES), lambda i, j: (i, 0)))
    outs = pl.pallas_call(
        functools.partial(_proj_kernel, perm=perm_seq is not None, with_dt=w_dt is not None,
                          qk_norm=qk_gain is not None),
        grid=grid, in_specs=in_specs, out_specs=out_specs, out_shape=out_shape,
        scratch_shapes=[pltpu.VMEM((tm, dm), BF16)] + ([pltpu.VMEM((tm, LANES), F32)] if perm_seq else []),
        compiler_params=_params(("parallel", "arbitrary")),
        name="norm_proj",
    )(*args)
    return outs if w_dt is not None else outs[0]


def _outproj_kernel(a_ref, w_ref, x_ref, o_ref, *scratch, unperm):
    tm = o_ref.shape[0]
    a = a_ref[...].reshape(tm, a_ref.shape[-1]).astype(BF16)
    res = _dot(a, w_ref[...])
    if unperm:
        slab_ref, = scratch
        rows = tm // PERM
        for c in range(o_ref.shape[1] // LANES):
            cols = slice(c * LANES, (c + 1) * LANES)
            for r in range(PERM):
                slab_ref[pl.ds(r, rows, stride=PERM), :] = res[r * rows:(r + 1) * rows, cols]
            o_ref[:, cols] = x_ref[:, cols] + slab_ref[...]
    else:
        o_ref[...] = x_ref[...] + res


def _outproj(a, w, x, *, tm, perm_seq=None):
    m, n = x.shape
    k = w.shape[0]
    if perm_seq is not None:
        per_b = perm_seq // tm
        a_spec = pl.BlockSpec((None, PERM, tm // PERM, k), lambda i: (i // per_b, 0, i % per_b, 0))
    else:
        a_spec = pl.BlockSpec((tm, k), lambda i: (i, 0))
    return pl.pallas_call(
        functools.partial(_outproj_kernel, unperm=perm_seq is not None),
        grid=(m // tm,),
        in_specs=[a_spec, pl.BlockSpec((k, n), lambda i: (0, 0)), pl.BlockSpec((tm, n), lambda i: (i, 0))],
        out_specs=pl.BlockSpec((tm, n), lambda i: (i, 0)),
        out_shape=jax.ShapeDtypeStruct((m, n), F32),
        scratch_shapes=[pltpu.VMEM((tm, LANES), F32)] if perm_seq else [],
        compiler_params=_params(("parallel",)),
        name="out_proj",
    )(a, w, x)


ATT_SUB = 128


def _attn_group_kernel(*refs, n_chunks, n_sub, merge):
    it = iter(refs)
    q_ref, k_ref, v_ref = next(it), next(it), next(it)
    if merge:
        gate_ref = next(it)
        others = [(next(it), next(it)) for _ in range(len(DILATIONS) - 1)]
        o_out = next(it)
    else:
        o_out, lse_out = next(it), next(it)
    ks_ref, vs_ref = next(it), next(it)

    rc = q_ref.shape[1] // n_sub
    assert n_chunks * rc == ATT_SUB
    n = pl.program_id(2)
    narrow = rc % 16 != 0

    def subs(ref, cols=slice(None)):
        x = ref[:, :, cols]
        x = x.astype(F32) if narrow and x.dtype == BF16 else x
        return [x[:, u * rc:(u + 1) * rc, :].reshape(ATT_SUB, x.shape[-1]) for u in range(n_sub)]

    def unsubs(parts, dtype):
        parts = [p.reshape(n_chunks, rc, p.shape[-1]) for p in parts]
        if narrow:
            return jnp.concatenate(parts, axis=1).astype(dtype)
        return jnp.concatenate([p.astype(dtype) for p in parts], axis=1)

    @pl.when(n == 0)
    def _():
        ks_ref[0:ATT_SUB, :] = jnp.zeros((ATT_SUB, ATT_WIDTH), BF16)
        vs_ref[0:ATT_SUB, :] = jnp.zeros((ATT_SUB, ATT_WIDTH), BF16)

    q_sub = [x.astype(BF16) for x in subs(q_ref)]
    for u, (k, v) in enumerate(zip(subs(k_ref), subs(v_ref))):
        ks_ref[(u + 1) * ATT_SUB:(u + 2) * ATT_SUB, :] = k.astype(BF16)
        vs_ref[(u + 1) * ATT_SUB:(u + 2) * ATT_SUB, :] = v.astype(BF16)

    qi = lax.broadcasted_iota(jnp.int32, (ATT_SUB, 2 * ATT_SUB), 0)
    kj = lax.broadcasted_iota(jnp.int32, (ATT_SUB, 2 * ATT_SUB), 1)
    kc = jnp.where(kj >= ATT_SUB, kj - ATT_SUB, kj)
    tq = n_chunks * (qi % rc) + qi // rc
    tk = n_chunks * (kc % rc) + kc // rc + jnp.where(kj >= ATT_SUB, 0, -ATT_SUB)
    dist = tq - tk
    in_band = (dist >= 0) & (dist <= WINDOW_STEPS)
    band = jnp.where(in_band, 0.0, MASKED)
    band_first = jnp.where(in_band & ((kj >= ATT_SUB) | (n > 0)), 0.0, MASKED)
    bias = [jnp.concatenate([x, x], axis=0) for x in (band_first, band)]

    for j in range(N_HEADS // 2):
        cols = slice(j * LANES, (j + 1) * LANES)
        o_parts, lse_parts = [], []
        for u in range(n_sub):
            keys = slice(u * ATT_SUB, (u + 2) * ATT_SUB)
            s = _dot_nt(_split_heads(q_sub[u][:, cols]), ks_ref[keys, cols]) + bias[min(u, 1)]
            m = jnp.max(s, axis=-1, keepdims=True)
            p = jnp.exp2(s - m)
            l = jnp.sum(p, axis=-1, keepdims=True)
            pv = _dot(p.astype(BF16), vs_ref[keys, cols])
            o_parts.append(_join_heads(pv / l))
            lse_parts.append(_join_heads(jnp.broadcast_to(m + jnp.log2(l), (2 * ATT_SUB, LANES))))
        if merge:
            gate = subs(gate_ref, cols)
            o_g = [o_parts] + [subs(o_ref, cols) for o_ref, _ in others]
            lse_g = [lse_parts] + [subs(l_ref, cols) for _, l_ref in others]
            outs = []
            for u in range(n_sub):
                top = jnp.maximum(jnp.maximum(lse_g[0][u], lse_g[1][u]), lse_g[2][u])
                w = [jnp.exp2(lse_g[i][u] - top) for i in range(3)]
                o = sum(w[i] * o_g[i][u].astype(F32) for i in range(3)) / (w[0] + w[1] + w[2])
                g = gate[u].astype(F32)
                outs.append(o * (g * _sigmoid(g)))
            o_out[:, :, cols] = unsubs(outs, BF16)
        else:
            o_out[:, :, cols] = unsubs(o_parts, BF16)
            lse_out[:, :, cols] = unsubs(lse_parts, F32)
    ks_ref[0:ATT_SUB, :] = ks_ref[n_sub * ATT_SUB:(n_sub + 1) * ATT_SUB, :]
    vs_ref[0:ATT_SUB, :] = vs_ref[n_sub * ATT_SUB:(n_sub + 1) * ATT_SUB, :]


def _attn_group(qkvg, g, *, n_sub, others=None):
    b, _, t, _ = qkvg.shape
    d = DILATIONS[g]
    n_chunks = PERM // d
    rc = ATT_SUB // n_chunks * n_sub
    nb = t // rc
    merge = others is not None
    view = lambda a: a.reshape(b, n_chunks, d, t, a.shape[-1])

    def spec(col=0):
        return pl.BlockSpec((None, n_chunks, None, rc, ATT_WIDTH), lambda bb, r, n: (bb, 0, r, n, col))

    qkvg5 = view(qkvg)
    in_specs = [spec(3 * g), spec(3 * g + 1), spec(3 * g + 2)]
    args = [qkvg5, qkvg5, qkvg5]
    full = jax.ShapeDtypeStruct((b, n_chunks, d, t, ATT_WIDTH), BF16)
    if merge:
        in_specs.append(spec(ATT_IN // ATT_WIDTH - 1))
        args.append(qkvg5)
        for o, lse in others:
            in_specs += [spec(), spec()]
            args += [view(o), view(lse)]
        out_shape, out_specs = [full], [spec()]
    else:
        out_shape = [full, jax.ShapeDtypeStruct(full.shape, F32)]
        out_specs = [spec(), spec()]
    outs = pl.pallas_call(
        functools.partial(_attn_group_kernel, n_chunks=n_chunks, n_sub=n_sub, merge=merge),
        grid=(b, d, nb), in_specs=in_specs, out_specs=out_specs, out_shape=out_shape,
        scratch_shapes=[pltpu.VMEM(((n_sub + 1) * ATT_SUB, ATT_WIDTH), BF16)] * 2,
        compiler_params=_params(("parallel", "parallel", "arbitrary")),
        name=f"attn_group{g}",
    )(*args)
    outs = [a.reshape(b, PERM, t, ATT_WIDTH) for a in outs]
    return outs[0] if merge else tuple(outs)


def _kv_tail_kernel(x_ref, nw_ref, wt_ref, kg_ref, o_ref):
    h = _rms(x_ref[...], nw_ref[...], NORM_EPS).astype(BF16)
    res = _dot_nt(wt_ref[...], h)
    tokens = res.shape[1]
    k = res[0:ATT_WIDTH].reshape(N_HEADS, HEAD_DIM, tokens)
    r = lax.rsqrt(jnp.mean(k * k, axis=1, keepdims=True) + NORM_EPS)
    o_ref[0:ATT_WIDTH, :] = (k * r).reshape(ATT_WIDTH, tokens) * kg_ref[...]
    o_ref[ATT_WIDTH:2 * ATT_WIDTH, :] = res[ATT_WIDTH:2 * ATT_WIDTH]


def _kv_tail(x, norm_w, wt, k_gain, *, b, keep, tm):
    seq = x.shape[0] // b
    first = (seq - keep) // tm
    kg = jnp.broadcast_to(jnp.tile(k_gain, N_HEADS)[:, None], (ATT_WIDTH, tm))
    return pl.pallas_call(
        _kv_tail_kernel,
        grid=(b, keep // tm),
        in_specs=[pl.BlockSpec((tm, D_MODEL), lambda bb, i: (bb * (seq // tm) + first + i, 0)),
                  pl.BlockSpec((1, D_MODEL), lambda bb, i: (0, 0)),
                  pl.BlockSpec((2 * ATT_WIDTH, D_MODEL), lambda bb, i: (0, 0)),
                  pl.BlockSpec((ATT_WIDTH, tm), lambda bb, i: (0, 0))],
        out_specs=pl.BlockSpec((None, 2 * ATT_WIDTH, tm), lambda bb, i: (bb, 0, i)),
        out_shape=jax.ShapeDtypeStruct((b, 2 * ATT_WIDTH, keep), F32),
        compiler_params=_params(("parallel", "parallel")),
        name="kv_tail",
    )(x, norm_w.reshape(1, D_MODEL), wt, kg)


SAMPLE_HEADS = 8


def _sample_attn_kernel(*refs):
    n_g = len(DILATIONS)
    qkv = [refs[3 * g:3 * g + 3] for g in range(n_g)]
    gate_ref = refs[3 * n_g]
    c_refs = refs[3 * n_g + 1:4 * n_g + 1]
    qg_ref, kg_ref = refs[4 * n_g + 1:4 * n_g + 3]
    o_ref = refs[4 * n_g + 3]
    kn_refs = refs[4 * n_g + 4:]
    t_new = o_ref.shape[0]
    t_row = lax.broadcasted_iota(jnp.int32, (2 * t_new, 1), 0) % t_new
    t_col = lax.broadcasted_iota(jnp.int32, (1, t_new), 1)

    for j in range(SAMPLE_HEADS // 2):
        cols = slice(j * LANES, (j + 1) * LANES)
        o_g, m_g, l_g = [], [], []
        for g, d in enumerate(DILATIONS):
            q_ref, k_ref, v_ref = qkv[g]
            qm = _split_heads(_head_norm(q_ref[:, cols], NORM_EPS) * qg_ref[g:g + 1, :]).astype(BF16)
            kn = _head_norm(k_ref[:, cols], NORM_EPS) * kg_ref[g:g + 1, :]
            kn_refs[g][:, cols] = kn
            length = c_refs[g].shape[2]
            s_c = _dot(qm, c_refs[g][0, cols, :].astype(BF16))
            s_n = _dot_nt(qm, kn.astype(BF16))
            back = length + t_row - lax.broadcasted_iota(jnp.int32, (1, length), 1)
            s_c = jnp.where((back % d == 0) & (back <= WINDOW_STEPS * d), s_c, MASKED)
            back = t_row - t_col
            s_n = jnp.where((back >= 0) & (back % d == 0) & (back <= WINDOW_STEPS * d), s_n, MASKED)
            m = jnp.maximum(jnp.max(s_c, axis=-1, keepdims=True), jnp.max(s_n, axis=-1, keepdims=True))
            p_c = jnp.exp(s_c - m)
            p_n = jnp.exp(s_n - m)
            l_g.append(jnp.sum(p_c, axis=-1, keepdims=True) + jnp.sum(p_n, axis=-1, keepdims=True))
            o_g.append(_dot_nt(p_c.astype(BF16), c_refs[g][1, cols, :].astype(BF16))
                       + _dot(p_n.astype(BF16), v_ref[:, cols].astype(BF16)))
            m_g.append(m)
        top = jnp.maximum(jnp.maximum(m_g[0], m_g[1]), m_g[2])
        f = [jnp.exp(m - top) for m in m_g]
        den = f[0] * l_g[0] + f[1] * l_g[1] + f[2] * l_g[2]
        o = _join_heads((f[0] * o_g[0] + f[1] * o_g[1] + f[2] * o_g[2]) / den)
        gate = gate_ref[:, cols]
        o_ref[:, cols] = o * (gate * _sigmoid(gate))


def _sample_attn(proj, caches_t, q_gain, k_gain):
    b = caches_t[0].shape[0]
    t_new = proj.shape[0] // b
    width = SAMPLE_HEADS * HEAD_DIM
    per_chunk = ATT_WIDTH // width
    col = lambda c: pl.BlockSpec((t_new, width), lambda bb, hc: (bb, c * per_chunk + hc))
    in_specs, args = [], []
    for g in range(len(DILATIONS)):
        in_specs += [col(3 * g), col(3 * g + 1), col(3 * g + 2)]
        args += [proj, proj, proj]
    in_specs.append(col(ATT_IN // ATT_WIDTH - 1))
    args.append(proj)
    for c in caches_t:
        in_specs.append(pl.BlockSpec((None, 2, width, c.shape[3]), lambda bb, hc: (bb, 0, hc, 0)))
        args.append(c)
    gain_spec = pl.BlockSpec((len(DILATIONS), LANES), lambda bb, hc: (0, 0))
    out_spec = pl.BlockSpec((t_new, width), lambda bb, hc: (bb, hc))
    out_sds = jax.ShapeDtypeStruct((b * t_new, ATT_WIDTH), F32)
    return pl.pallas_call(
        _sample_attn_kernel,
        grid=(b, per_chunk),
        in_specs=in_specs + [gain_spec, gain_spec],
        out_specs=[out_spec] * 4,
        out_shape=[out_sds] * 4,
        compiler_params=_params(("parallel", "parallel")),
        name="sample_attn",
    )(*args, q_gain, k_gain)


def _softplus(x):
    return jnp.maximum(x, 0.0) + jnp.log(1.0 + jnp.exp(-jnp.abs(x)))


def _conv_silu(zx_ref, xpad_ref, cw_ref, cb_ref, lc):
    pad = 8
    xpad_ref[pad:pad + lc, :] = zx_ref[:, SSM_D_INNER:SSM_MAIN].astype(F32)
    conv = cb_ref[...] + xpad_ref[pad - 3:pad - 3 + lc, :] * cw_ref[0:1, :]
    for j in range(1, SSM_CONV):
        conv = conv + xpad_ref[pad - 3 + j:pad - 3 + j + lc, :] * cw_ref[j:j + 1, :]
    tail = xpad_ref[pad + lc - 3:pad + lc, :]
    xpad_ref[pad - 3:pad, :] = tail
    return conv * _sigmoid(conv), tail


def _gate_norm(y, z, gn):
    y = y * (z * _sigmoid(z))
    parts = []
    for g in range(SSM_GROUPS):
        yg = y[:, g * SSM_GROUP_WIDTH:(g + 1) * SSM_GROUP_WIDTH]
        parts.append(yg * lax.rsqrt(jnp.mean(yg * yg, axis=-1, keepdims=True) + GATE_NORM_EPS))
    return jnp.concatenate(parts, axis=-1) * gn


def _cumsum_rows(x):
    rows = x.shape[0]
    tril = (lax.broadcasted_iota(jnp.int32, (rows, rows), 0)
            >= lax.broadcasted_iota(jnp.int32, (rows, rows), 1)).astype(BF16)
    h1 = x.astype(BF16)
    r1 = x - h1.astype(F32)
    h2 = r1.astype(BF16)
    h3 = (r1 - h2.astype(F32)).astype(BF16)
    return _dot(tril, h1) + _dot(tril, h2) + _dot(tril, h3)


def _ssd_prompt_kernel(zx_ref, dt_ref, cw_ref, cb_ref, dtb_ref, a_ref, dskip_ref, gn_ref, e_ref,
                       y_ref, nconv_ref, nssm_ref, carry_ref, st_ref):
    lc = zx_ref.shape[0]
    c = pl.program_id(1)
    last = pl.num_programs(1) - 1

    @pl.when(c == 0)
    def _():
        carry_ref[...] = jnp.zeros(carry_ref.shape, BF16)
        st_ref[...] = jnp.zeros(st_ref.shape, F32)

    n_carry = carry_ref.shape[0]
    raw = zx_ref[:, SSM_D_INNER:SSM_MAIN]
    ext = jnp.concatenate([carry_ref[...], raw], axis=0)
    src = lax.broadcasted_iota(jnp.int32, (lc, n_carry + lc), 1) - n_carry
    dst = lax.broadcasted_iota(jnp.int32, (lc, n_carry + lc), 0)
    taps = SSM_CONV - 1
    shift = jnp.concatenate([(src == dst - (taps - j)).astype(BF16) for j in range(taps)], axis=0)
    shifted = _dot(shift, ext)
    conv = cb_ref[...] + shifted[0:lc] * cw_ref[0:1, :]
    for j in range(1, taps):
        conv = conv + shifted[j * lc:(j + 1) * lc] * cw_ref[j:j + 1, :]
    conv = conv + raw.astype(F32) * cw_ref[taps:taps + 1, :]
    xbc = conv * _sigmoid(conv)
    carry_ref[...] = raw[lc - n_carry:lc]

    @pl.when(c == last)
    def _():
        nconv_ref[...] = raw[lc - 8:lc].astype(F32)[8 - taps:8]

    xs = xbc[:, 0:SSM_D_INNER]
    bm = xbc[:, SSM_D_INNER:SSM_D_INNER + SSM_GROUPS * SSM_STATE].astype(BF16)
    cm = xbc[:, SSM_D_INNER + SSM_GROUPS * SSM_STATE:].astype(BF16)
    e = e_ref[...]
    dt = _softplus(dt_ref[...] + dtb_ref[...])
    a_cs = _cumsum_rows(dt * a_ref[...])
    a_t = a_cs.T
    a_last = a_cs[lc - 1:lc, :]
    xdt = xs * _dot2(dt, e)
    xdt_b = xdt.astype(BF16)
    xdte = (xdt * _dot2(jnp.exp(a_last - a_cs), e)).astype(BF16)
    ea_e = _dot2(jnp.exp(a_cs), e)
    cd_e = _dot2(jnp.broadcast_to(jnp.exp(a_last), (8, LANES)), e)[0:1, :]
    s_prev = st_ref[...]
    s_b = s_prev.astype(BF16)
    causal = (lax.broadcasted_iota(jnp.int32, (lc, lc), 0) >= lax.broadcasted_iota(jnp.int32, (lc, lc), 1))
    heads_per_group = SSM_HEADS // SSM_GROUPS
    y_parts = []
    for g in range(SSM_GROUPS):
        gs = slice(g * SSM_STATE, (g + 1) * SSM_STATE)
        gw = slice(g * SSM_GROUP_WIDTH, (g + 1) * SSM_GROUP_WIDTH)
        cb = _dot_nt(cm[:, gs], bm[:, gs])
        y_off = _dot(cm[:, gs], s_b[:, gw]) * ea_e[:, gw]
        s_new = lax.dot_general(bm[:, gs], xdte[:, gw], TN, preferred_element_type=F32)
        st_ref[:, gw] = s_prev[:, gw] * cd_e[:, gw] + s_new
        pair_out = []
        for pr in range(heads_per_group // 2):
            h0 = g * heads_per_group + 2 * pr
            ms = []
            for h in (h0, h0 + 1):
                seg = a_cs[:, h:h + 1] - a_t[h:h + 1, :]
                ms.append((cb * jnp.where(causal, jnp.exp(seg), 0.0)).astype(BF16))
            xp = xdt_b[:, (h0 // 2) * LANES:(h0 // 2 + 1) * LANES]
            pair_out.append(_join_heads(_dot(jnp.concatenate(ms, axis=0), xp)))
        y_parts.append(jnp.concatenate(pair_out, axis=-1) + y_off)
    y = jnp.concatenate(y_parts, axis=-1) + dskip_ref[...] * xs
    z = zx_ref[:, 0:SSM_D_INNER].astype(F32)
    y_ref[...] = _gate_norm(y, z, gn_ref[...]).astype(y_ref.dtype)

    @pl.when(c == last)
    def _():
        nssm_ref[...] = st_ref[...].T


def _ssd_prompt(zx, dt, b, params):
    m = zx.shape[0]
    nc = m // b // SSM_CHUNK
    full = lambda x: pl.BlockSpec(x.shape, lambda bb, c: (0, 0))
    row = lambda w: pl.BlockSpec((SSM_CHUNK, w), lambda bb, c: (bb * nc + c, 0))
    return pl.pallas_call(
        _ssd_prompt_kernel,
        grid=(b, nc),
        in_specs=[row(SSM_MAIN), row(LANES)] + [full(x) for x in params],
        out_specs=[row(SSM_D_INNER),
                   pl.BlockSpec((None, SSM_CONV - 1, SSM_CONV_DIM), lambda bb, c: (bb, 0, 0)),
                   pl.BlockSpec((None, SSM_D_INNER, SSM_STATE), lambda bb, c: (bb, 0, 0))],
        out_shape=[jax.ShapeDtypeStruct((m, SSM_D_INNER), BF16),
                   jax.ShapeDtypeStruct((b, SSM_CONV - 1, SSM_CONV_DIM), F32),
                   jax.ShapeDtypeStruct((b, SSM_D_INNER, SSM_STATE), F32)],
        scratch_shapes=[pltpu.VMEM((16, SSM_CONV_DIM), BF16),
                        pltpu.VMEM((SSM_STATE, SSM_D_INNER), F32)],
        compiler_params=_params(("parallel", "arbitrary")),
        name="ssd_prompt",
    )(zx, dt, *params)


def _ssd_sample_kernel(zx_ref, dt_ref, conv0_ref, s0_ref, cw_ref, cb_ref, dtb_ref, a_ref, dskip_ref, gn_ref,
                       e_ref, eg_ref, y_ref, nconv_ref, nssm_ref, xpad_ref):
    lc = zx_ref.shape[0]
    xpad_ref[0:5, :] = jnp.zeros((5, SSM_CONV_DIM), F32)
    xpad_ref[5:8, :] = conv0_ref[...]
    xbc, tail = _conv_silu(zx_ref, xpad_ref, cw_ref, cb_ref, lc)
    nconv_ref[...] = tail

    xs = xbc[:, 0:SSM_D_INNER]
    bm = xbc[:, SSM_D_INNER:SSM_D_INNER + SSM_GROUPS * SSM_STATE]
    cm = xbc[:, SSM_D_INNER + SSM_GROUPS * SSM_STATE:]
    e = e_ref[...]
    dt = _softplus(dt_ref[...] + dtb_ref[...])
    dta = dt * a_ref[...]
    rows = [dta[0:1, :]]
    for i in range(1, lc):
        rows.append(rows[-1] + dta[i:i + 1, :])
    a_cs = jnp.concatenate(rows, axis=0)
    a_last = rows[-1]
    xdt = xs * _dot2(dt, e)
    xdte = (xdt * _dot2(jnp.exp(a_last - a_cs), e)).astype(BF16)
    ea_e = _dot2(jnp.exp(a_cs), e)
    cd_e = _dot2(jnp.broadcast_to(jnp.exp(a_last), (8, LANES)), e)[0:1, :]

    a_l = jnp.concatenate([a_cs] * lc, axis=0)
    a_s = jnp.concatenate([jnp.broadcast_to(a_cs[s:s + 1, :], (lc, LANES)) for s in range(lc)], axis=0)
    l_idx = lax.broadcasted_iota(jnp.int32, (lc * lc, 1), 0) % lc
    s_idx = lax.broadcasted_iota(jnp.int32, (lc * lc, 1), 0) // lc
    decay = jnp.where(l_idx >= s_idx, jnp.exp(a_l - a_s), 0.0)
    cb_prod = jnp.concatenate([cm * bm[s:s + 1, :] for s in range(lc)], axis=0)
    mix = _dot2(_dot2(cb_prod, eg_ref[...]) * decay, e)
    y = ea_e * 0.0
    for s in range(lc):
        y = y + mix[s * lc:(s + 1) * lc, :] * xdt[s:s + 1, :]

    s_prev = s0_ref[...].T
    s_b = s_prev.astype(BF16)
    bm_b = bm.astype(BF16)
    cm_b = cm.astype(BF16)
    y_off, s_parts = [], []
    for g in range(SSM_GROUPS):
        gs = slice(g * SSM_STATE, (g + 1) * SSM_STATE)
        gw = slice(g * SSM_GROUP_WIDTH, (g + 1) * SSM_GROUP_WIDTH)
        y_off.append(_dot(cm_b[:, gs], s_b[:, gw]))
        s_parts.append(lax.dot_general(bm_b[:, gs], xdte[:, gw], TN, preferred_element_type=F32))
    y = y + jnp.concatenate(y_off, axis=-1) * ea_e + dskip_ref[...] * xs
    s_new = s_prev * cd_e + jnp.concatenate(s_parts, axis=-1)
    nssm_ref[...] = s_new.T
    z = zx_ref[:, 0:SSM_D_INNER]
    y_ref[...] = _gate_norm(y, z, gn_ref[...])


def _ssd_sample(zx, dt, conv0, s0, params):
    b = conv0.shape[0]
    t_new = zx.shape[0] // b
    full = lambda x: pl.BlockSpec(x.shape, lambda bb: (0, 0))
    row = lambda w: pl.BlockSpec((t_new, w), lambda bb: (bb, 0))
    conv_spec = pl.BlockSpec((None, SSM_CONV - 1, SSM_CONV_DIM), lambda bb: (bb, 0, 0))
    state_spec = pl.BlockSpec((None, SSM_D_INNER, SSM_STATE), lambda bb: (bb, 0, 0))
    return pl.pallas_call(
        _ssd_sample_kernel,
        grid=(b,),
        in_specs=[row(SSM_MAIN), row(LANES), conv_spec, state_spec] + [full(x) for x in params],
        out_specs=[row(SSM_D_INNER), conv_spec, state_spec],
        out_shape=[jax.ShapeDtypeStruct((b * t_new, SSM_D_INNER), F32),
                   jax.ShapeDtypeStruct((b, SSM_CONV - 1, SSM_CONV_DIM), F32),
                   jax.ShapeDtypeStruct((b, SSM_D_INNER, SSM_STATE), F32)],
        scratch_shapes=[pltpu.VMEM((8 + t_new, SSM_CONV_DIM), F32)],
        compiler_params=_params(("parallel",)),
        name="ssd_sample",
    )(zx, dt, conv0, s0, *params)


def _one_hot_expand(n_rows, n_cols, group):
    r = lax.broadcasted_iota(jnp.int32, (n_rows, n_cols), 0)
    c = lax.broadcasted_iota(jnp.int32, (n_rows, n_cols), 1)
    return (c // group == r).astype(BF16)


def _kv_rows(feature_major, b):
    tokens = feature_major.shape[-1]
    x = feature_major.reshape(1, b, 2, N_HEADS, HEAD_DIM, tokens)
    return jnp.transpose(x, (0, 1, 5, 2, 3, 4))


def kernel(x_prompt, x_sample, cache_kv_g0, cache_kv_g1, cache_kv_g2, state_conv, state_ssm, attn_norm, attn_w_in, attn_q_gain, attn_k_gain, attn_w_out, ssm_norm, ssm_w_in, ssm_conv_w, ssm_conv_b, ssm_dt_bias, ssm_A_log, ssm_D, ssm_gate_norm, ssm_w_out):
    b, seq, dm = x_prompt.shape
    sb, st, _ = x_sample.shape
    xp = x_prompt.reshape(b * seq, dm)
    xs = x_sample.reshape(sb * st, dm)

    w_in = attn_w_in[0].astype(BF16)
    w_out = attn_w_out[0].astype(BF16)
    scale = HEAD_DIM ** -0.5
    q_gain3 = jnp.tile(attn_q_gain[0], (1, N_HEADS)) * scale
    k_gain3 = jnp.tile(attn_k_gain[0], (1, N_HEADS))

    log2e = 1.4426950408889634
    tile_gain = jnp.ones((ATT_IN // ATT_WIDTH, 1, ATT_WIDTH), F32)
    for g in range(len(DILATIONS)):
        tile_gain = tile_gain.at[3 * g, 0].set(q_gain3[g] * log2e).at[3 * g + 1, 0].set(k_gain3[g])
    qkvg = _proj(xp, attn_norm[0], w_in, tm=1024, tn=ATT_WIDTH, out_dtype=BF16, perm_seq=seq, qk_gain=tile_gain)
    part2 = _attn_group(qkvg, 2, n_sub=2)
    part1 = _attn_group(qkvg, 1, n_sub=2)
    o_gated = _attn_group(qkvg, 0, n_sub=2, others=[part1, part2])
    y1p = _outproj(o_gated, w_out, xp, tm=1024, perm_seq=seq)
    kv_p = []
    for g, d in enumerate(DILATIONS):
        keep = min(d * WINDOW_STEPS, seq)
        wt = attn_w_in[0][:, (3 * g + 1) * ATT_WIDTH:(3 * g + 3) * ATT_WIDTH].astype(BF16).T
        kv_p.append(_kv_rows(_kv_tail(xp, attn_norm[0], wt, attn_k_gain[0, g], b=b, keep=keep,
                                      tm=min(keep, 512)), b))

    proj_s = _proj(xs, attn_norm[0], w_in, tm=sb * st, tn=1024, out_dtype=F32)
    caches_t = [jnp.transpose(c[0], (0, 2, 3, 4, 1)).reshape(sb, 2, ATT_WIDTH, c.shape[2])
                for c in (cache_kv_g0, cache_kv_g1, cache_kv_g2)]
    o_s, kn0, kn1, kn2 = _sample_attn(proj_s, caches_t, q_gain3[:, :LANES], k_gain3[:, :LANES])
    y1s = _outproj(o_s, w_out, xs, tm=sb * st)
    kv_s = []
    for g, kn in enumerate((kn0, kn1, kn2)):
        v = proj_s[:, (3 * g + 2) * ATT_WIDTH:(3 * g + 3) * ATT_WIDTH]
        kv_s.append(jnp.stack([kn, v], axis=1).reshape(1, sb, st, 2, N_HEADS, HEAD_DIM))

    w_in2 = ssm_w_in[0]
    w_main = w_in2[:, :SSM_MAIN].astype(BF16)
    w_dt = jnp.pad(w_in2[:, SSM_MAIN:], ((0, 0), (0, LANES - SSM_HEADS))).astype(BF16)
    w_out2 = ssm_w_out[0].astype(BF16)
    pad_h = lambda v: jnp.pad(v.astype(F32), (0, LANES - SSM_HEADS)).reshape(1, LANES)
    e32 = _one_hot_expand(LANES, SSM_D_INNER, SSM_D_INNER // SSM_HEADS)
    ssm_params = (ssm_conv_w[0], ssm_conv_b[0].reshape(1, -1), pad_h(ssm_dt_bias[0]),
                  pad_h(-jnp.exp(ssm_A_log[0].astype(F32))),
                  jnp.repeat(ssm_D[0].astype(F32), SSM_D_INNER // SSM_HEADS).reshape(1, -1),
                  ssm_gate_norm[0].reshape(1, -1), e32)

    zx_p, dt_p = _proj(y1p, ssm_norm[0], w_main, tm=1024, tn=1024, out_dtype=BF16, w_dt=w_dt)
    yg_p, conv_p, ssm_p = _ssd_prompt(zx_p, dt_p, b, ssm_params)
    y2p = _outproj(yg_p, w_out2, y1p, tm=1024)

    eg = (lax.broadcasted_iota(jnp.int32, (SSM_GROUPS * SSM_STATE, LANES), 0) // SSM_STATE
          == lax.broadcasted_iota(jnp.int32, (SSM_GROUPS * SSM_STATE, LANES), 1) // (SSM_HEADS // SSM_GROUPS))
    eg = (eg & (lax.broadcasted_iota(jnp.int32, eg.shape, 1) < SSM_HEADS)).astype(BF16)
    zx_s, dt_s = _proj(y1s, ssm_norm[0], w_main, tm=sb * st, tn=1024, out_dtype=F32, w_dt=w_dt)
    yg_s, conv_s, ssm_s = _ssd_sample(zx_s, dt_s, state_conv[0], state_ssm[0].reshape(sb, SSM_D_INNER, SSM_STATE),
                                      ssm_params + (eg,))
    y2s = _outproj(yg_s, w_out2, y1s, tm=sb * st)

    hp = SSM_D_INNER // SSM_HEADS
    return (y2p.reshape(b, seq, dm), y2s.reshape(sb, st, dm),
            kv_p[0], kv_p[1], kv_p[2], kv_s[0], kv_s[1], kv_s[2],
            conv_p[None], conv_s[None],
            ssm_p.reshape(1, b, SSM_HEADS, hp, SSM_STATE), ssm_s.reshape(1, sb, SSM_HEADS, hp, SSM_STATE))
```

```python
import functools

import jax
import jax.numpy as jnp
from jax import lax
from jax.experimental import pallas as pl
from jax.experimental.pallas import tpu as pltpu

F32 = jnp.float32
BF16 = jnp.bfloat16

D_MODEL = 1024
N_HEADS = 16
HEAD_DIM = 64
ATT_WIDTH = N_HEADS * HEAD_DIM
DILATIONS = (1, 4, 16)
WINDOW_STEPS = 128
ATT_IN = 10 * ATT_WIDTH
PERM = 16

SSM_D_INNER = 2048
SSM_HEADS = 32
SSM_STATE = 128
SSM_GROUPS = 4
SSM_GROUP_WIDTH = SSM_D_INNER // SSM_GROUPS
SSM_CONV = 4
SSM_CONV_DIM = SSM_D_INNER + 2 * SSM_GROUPS * SSM_STATE
SSM_MAIN = SSM_D_INNER + SSM_CONV_DIM
SSM_CHUNK = 128

NORM_EPS = 1e-6
GATE_NORM_EPS = 1e-5
MASKED = -1e30

LANES = 128
VMEM_LIMIT = 56 * 1024 * 1024

NT = (((1,), (1,)), ((), ()))
TN = (((0,), (0,)), ((), ()))


def _params(semantics):
    return pltpu.CompilerParams(dimension_semantics=semantics, vmem_limit_bytes=VMEM_LIMIT)


def _rms(x, w, eps):
    return x * lax.rsqrt(jnp.mean(x * x, axis=-1, keepdims=True) + eps) * w


def _sigmoid(x):
    return 1.0 / (1.0 + jnp.exp(-x))


def _split2(v):
    hi = v.astype(BF16)
    lo = (v - hi.astype(F32)).astype(BF16)
    return hi, lo


def _dot(a, b):
    return jnp.dot(a, b, preferred_element_type=F32)


def _dot_nt(a, b):
    return lax.dot_general(a, b, NT, preferred_element_type=F32)


def _dot2(v, e):
    hi, lo = _split2(v)
    return _dot(hi, e) + _dot(lo, e)


def _low_half():
    return lax.broadcasted_iota(jnp.int32, (1, LANES), 1) < HEAD_DIM


def _head_norm(x, eps):
    lo = _low_half()
    parts = []
    for j in range(x.shape[1] // LANES):
        t = x[:, j * LANES:(j + 1) * LANES]
        t2 = t * t
        s_lo = jnp.sum(jnp.where(lo, t2, 0.0), axis=-1, keepdims=True)
        s_hi = jnp.sum(jnp.where(lo, 0.0, t2), axis=-1, keepdims=True)
        r = jnp.where(lo, lax.rsqrt(s_lo * (1.0 / HEAD_DIM) + eps), lax.rsqrt(s_hi * (1.0 / HEAD_DIM) + eps))
        parts.append(t * r)
    return jnp.concatenate(parts, axis=-1)


def _split_heads(qp):
    lo = _low_half()
    zero = jnp.zeros_like(qp)
    return jnp.concatenate([jnp.where(lo, qp, zero), jnp.where(lo, zero, qp)], axis=0)


def _join_heads(x2):
    rows = x2.shape[0] // 2
    return jnp.where(_low_half(), x2[0:rows], x2[rows:])


def _proj_kernel(x_ref, nw_ref, w_ref, *rest, perm, with_dt, qk_norm):
    rest = list(rest)
    gain_ref = rest.pop(0) if qk_norm else None
    wdt_ref = rest.pop(0) if with_dt else None
    o_ref = rest.pop(0)
    dt_ref = rest.pop(0) if with_dt else None
    h_ref = rest.pop(0)
    slab_ref = rest.pop(0) if perm else None

    @pl.when(pl.program_id(1) == 0)
    def _():
        xn = _rms(x_ref[...], nw_ref[...], NORM_EPS)
        if perm:
            rows = x_ref.shape[0] // PERM
            for c in range(x_ref.shape[1] // LANES):
                cols = slice(c * LANES, (c + 1) * LANES)
                slab_ref[...] = xn[:, cols]
                for r in range(PERM):
                    h_ref[r * rows:(r + 1) * rows, cols] = slab_ref[pl.ds(r, rows, stride=PERM), :].astype(BF16)
        else:
            h_ref[...] = xn.astype(BF16)
        if with_dt:
            dt_ref[...] = _dot(h_ref[...], wdt_ref[...])

    if qk_norm:
        j = pl.program_id(1)
        is_qk = (j < 3 * len(DILATIONS)) & (j % 3 < 2)

        @pl.when(is_qk)
        def _():
            chunk = 2 * LANES
            lead = (slice(None),) * (len(o_ref.shape) - 1)
            for c in range(w_ref.shape[1] // chunk):
                cols = slice(c * chunk, (c + 1) * chunk)
                res = _head_norm(_dot(h_ref[...], w_ref[:, cols]), NORM_EPS) * gain_ref[0][:, cols]
                o_ref[lead + (cols,)] = res.astype(o_ref.dtype).reshape(o_ref.shape[:-1] + (chunk,))

        @pl.when(jnp.logical_not(is_qk))
        def _():
            o_ref[...] = _dot(h_ref[...], w_ref[...]).astype(o_ref.dtype).reshape(o_ref.shape)
    else:
        o_ref[...] = _dot(h_ref[...], w_ref[...]).astype(o_ref.dtype).reshape(o_ref.shape)


def _proj(x, norm_w, w, *, tm, tn, out_dtype, perm_seq=None, w_dt=None, qk_gain=None):
    m, dm = x.shape
    n = w.shape[1]
    grid = (m // tm, n // tn)
    in_specs = [pl.BlockSpec((tm, dm), lambda i, j: (i, 0)),
                pl.BlockSpec((1, dm), lambda i, j: (0, 0)),
                pl.BlockSpec((dm, tn), lambda i, j: (0, j))]
    args = [x, norm_w.reshape(1, dm), w]
    if qk_gain is not None:
        assert tn == ATT_WIDTH
        in_specs.append(pl.BlockSpec((1, 1, tn), lambda i, j: (j, 0, 0)))
        args.append(qk_gain)
    if perm_seq is not None:
        per_b = perm_seq // tm
        rows = tm // PERM
        out_shape = [jax.ShapeDtypeStruct((m // perm_seq, PERM, perm_seq // PERM, n), out_dtype)]
        out_specs = [pl.BlockSpec((None, PERM, rows, tn), lambda i, j: (i // per_b, 0, i % per_b, j))]
    else:
        out_shape = [jax.ShapeDtypeStruct((m, n), out_dtype)]
        out_specs = [pl.BlockSpec((tm, tn), lambda i, j: (i, j))]
    if w_dt is not None:
        in_specs.append(pl.BlockSpec((dm, LANES), lambda i, j: (0, 0)))
        args.append(w_dt)
        out_shape.append(jax.ShapeDtypeStruct((m, LANES), F32))
        out_specs.append(pl.BlockSpec((tm, LANES), lambda i, j: (i, 0)))
    outs = pl.pallas_call(
        functools.partial(_proj_kernel, perm=perm_seq is not None, with_dt=w_dt is not None,
                          qk_norm=qk_gain is not None),
        grid=grid, in_specs=in_specs, out_specs=out_specs, out_shape=out_shape,
        scratch_shapes=[pltpu.VMEM((tm, dm), BF16)] + ([pltpu.VMEM((tm, LANES), F32)] if perm_seq else []),
        compiler_params=_params(("parallel", "arbitrary")),
        name="norm_proj",
    )(*args)
    return outs if w_dt is not None else outs[0]


def _outproj_kernel(a_ref, w_ref, x_ref, o_ref, *scratch, unperm):
    tm = o_ref.shape[0]
    a = a_ref[...].reshape(tm, a_ref.shape[-1]).astype(BF16)
    res = _dot(a, w_ref[...])
    if unperm:
        slab_ref, = scratch
        rows = tm // PERM
        for c in range(o_ref.shape[1] // LANES):
            cols = slice(c * LANES, (c + 1) * LANES)
            for r in range(PERM):
                slab_ref[pl.ds(r, rows, stride=PERM), :] = res[r * rows:(r + 1) * rows, cols]
            o_ref[:, cols] = x_ref[:, cols] + slab_ref[...]
    else:
        o_ref[...] = x_ref[...] + res


def _outproj(a, w, x, *, tm, perm_seq=None):
    m, n = x.shape
    k = w.shape[0]
    if perm_seq is not None:
        per_b = perm_seq // tm
        a_spec = pl.BlockSpec((None, PERM, tm // PERM, k), lambda i: (i // per_b, 0, i % per_b, 0))
    else:
        a_spec = pl.BlockSpec((tm, k), lambda i: (i, 0))
    return pl.pallas_call(
        functools.partial(_outproj_kernel, unperm=perm_seq is not None),
        grid=(m // tm,),
        in_specs=[a_spec, pl.BlockSpec((k, n), lambda i: (0, 0)), pl.BlockSpec((tm, n), lambda i: (i, 0))],
        out_specs=pl.BlockSpec((tm, n), lambda i: (i, 0)),
        out_shape=jax.ShapeDtypeStruct((m, n), F32),
        scratch_shapes=[pltpu.VMEM((tm, LANES), F32)] if perm_seq else [],
        compiler_params=_params(("parallel",)),
        name="out_proj",
    )(a, w, x)


ATT_SUB = 128


def _attn_group_kernel(*refs, n_chunks, n_sub, merge):
    it = iter(refs)
    q_ref, k_ref, v_ref = next(it), next(it), next(it)
    if merge:
        gate_ref = next(it)
        others = [(next(it), next(it)) for _ in range(len(DILATIONS) - 1)]
        o_out = next(it)
    else:
        o_out, lse_out = next(it), next(it)
    ks_ref, vs_ref = next(it), next(it)

    rc = q_ref.shape[1] // n_sub
    assert n_chunks * rc == ATT_SUB
    n = pl.program_id(2)
    narrow = rc % 16 != 0

    def subs(ref, cols=slice(None)):
        x = ref[:, :, cols]
        x = x.astype(F32) if narrow and x.dtype == BF16 else x
        return [x[:, u * rc:(u + 1) * rc, :].reshape(ATT_SUB, x.shape[-1]) for u in range(n_sub)]

    def unsubs(parts, dtype):
        parts = [p.reshape(n_chunks, rc, p.shape[-1]) for p in parts]
        if narrow:
            return jnp.concatenate(parts, axis=1).astype(dtype)
        return jnp.concatenate([p.astype(dtype) for p in parts], axis=1)

    @pl.when(n == 0)
    def _():
        ks_ref[0:ATT_SUB, :] = jnp.zeros((ATT_SUB, ATT_WIDTH), BF16)
        vs_ref[0:ATT_SUB, :] = jnp.zeros((ATT_SUB, ATT_WIDTH), BF16)

    q_sub = [x.astype(BF16) for x in subs(q_ref)]
    for u, (k, v) in enumerate(zip(subs(k_ref), subs(v_ref))):
        ks_ref[(u + 1) * ATT_SUB:(u + 2) * ATT_SUB, :] = k.astype(BF16)
        vs_ref[(u + 1) * ATT_SUB:(u + 2) * ATT_SUB, :] = v.astype(BF16)

    qi = lax.broadcasted_iota(jnp.int32, (ATT_SUB, 2 * ATT_SUB), 0)
    kj = lax.broadcasted_iota(jnp.int32, (ATT_SUB, 2 * ATT_SUB), 1)
    kc = jnp.where(kj >= ATT_SUB, kj - ATT_SUB, kj)
    tq = n_chunks * (qi % rc) + qi // rc
    tk = n_chunks * (kc % rc) + kc // rc + jnp.where(kj >= ATT_SUB, 0, -ATT_SUB)
    dist = tq - tk
    in_band = (dist >= 0) & (dist <= WINDOW_STEPS)
    band = jnp.where(in_band, 0.0, MASKED)
    band_first = jnp.where(in_band & ((kj >= ATT_SUB) | (n > 0)), 0.0, MASKED)
    bias = [jnp.concatenate([x, x], axis=0) for x in (band_first, band)]

    for j in range(N_HEADS // 2):
        cols = slice(j * LANES, (j + 1) * LANES)
        o_parts, lse_parts = [], []
        for u in range(n_sub):
            keys = slice(u * ATT_SUB, (u + 2) * ATT_SUB)
            s = _dot_nt(_split_heads(q_sub[u][:, cols]), ks_ref[keys, cols]) + bias[min(u, 1)]
            m = jnp.max(s, axis=-1, keepdims=True)
            p = jnp.exp2(s - m)
            l = jnp.sum(p, axis=-1, keepdims=True)
            pv = _dot(p.astype(BF16), vs_ref[keys, cols])
            o_parts.append(_join_heads(pv / l))
            lse_parts.append(_join_heads(jnp.broadcast_to(m + jnp.log2(l), (2 * ATT_SUB, LANES))))
        if merge:
            gate = subs(gate_ref, cols)
            o_g = [o_parts] + [subs(o_ref, cols) for o_ref, _ in others]
            lse_g = [lse_parts] + [subs(l_ref, cols) for _, l_ref in others]
            outs = []
            for u in range(n_sub):
                top = jnp.maximum(jnp.maximum(lse_g[0][u], lse_g[1][u]), lse_g[2][u])
                w = [jnp.exp2(lse_g[i][u] - top) for i in range(3)]
                o = sum(w[i] * o_g[i][u].astype(F32) for i in range(3)) / (w[0] + w[1] + w[2])
                g = gate[u].astype(F32)
                outs.append(o * (g * _sigmoid(g)))
            o_out[:, :, cols] = unsubs(outs, BF16)
        else:
            o_out[:, :, cols] = unsubs(o_parts, BF16)
            lse_out[:, :, cols] = unsubs(lse_parts, F32)
    ks_ref[0:ATT_SUB, :] = ks_ref[n_sub * ATT_SUB:(n_sub + 1) * ATT_SUB, :]
    vs_ref[0:ATT_SUB, :] = vs_ref[n_sub * ATT_SUB:(n_sub + 1) * ATT_SUB, :]


def _attn_group(qkvg, g, *, n_sub, others=None):
    b, _, t, _ = qkvg.shape
    d = DILATIONS[g]
    n_chunks = PERM // d
    rc = ATT_SUB // n_chunks * n_sub
    nb = t // rc
    merge = others is not None
    view = lambda a: a.reshape(b, n_chunks, d, t, a.shape[-1])

    def spec(col=0):
        return pl.BlockSpec((None, n_chunks, None, rc, ATT_WIDTH), lambda bb, r, n: (bb, 0, r, n, col))

    qkvg5 = view(qkvg)
    in_specs = [spec(3 * g), spec(3 * g + 1), spec(3 * g + 2)]
    args = [qkvg5, qkvg5, qkvg5]
    full = jax.ShapeDtypeStruct((b, n_chunks, d, t, ATT_WIDTH), BF16)
    if merge:
        in_specs.append(spec(ATT_IN // ATT_WIDTH - 1))
        args.append(qkvg5)
        for o, lse in others:
            in_specs += [spec(), spec()]
            args += [view(o), view(lse)]
        out_shape, out_specs = [full], [spec()]
    else:
        out_shape = [full, jax.ShapeDtypeStruct(full.shape, F32)]
        out_specs = [spec(), spec()]
    outs = pl.pallas_call(
        functools.partial(_attn_group_kernel, n_chunks=n_chunks, n_sub=n_sub, merge=merge),
        grid=(b, d, nb), in_specs=in_specs, out_specs=out_specs, out_shape=out_shape,
        scratch_shapes=[pltpu.VMEM(((n_sub + 1) * ATT_SUB, ATT_WIDTH), BF16)] * 2,
        compiler_params=_params(("parallel", "parallel", "arbitrary")),
        name=f"attn_group{g}",
    )(*args)
    outs = [a.reshape(b, PERM, t, ATT_WIDTH) for a in outs]
    return outs[0] if merge else tuple(outs)


def _kv_tail_kernel(x_ref, nw_ref, wt_ref, kg_ref, o_ref):
    h = _rms(x_ref[...], nw_ref[...], NORM_EPS).astype(BF16)
    res = _dot_nt(wt_ref[...], h)
    tokens = res.shape[1]
    k = res[0:ATT_WIDTH].reshape(N_HEADS, HEAD_DIM, tokens)
    r = lax.rsqrt(jnp.mean(k * k, axis=1, keepdims=True) + NORM_EPS)
    o_ref[0:ATT_WIDTH, :] = (k * r).reshape(ATT_WIDTH, tokens) * kg_ref[...]
    o_ref[ATT_WIDTH:2 * ATT_WIDTH, :] = res[ATT_WIDTH:2 * ATT_WIDTH]


def _kv_tail(x, norm_w, wt, k_gain, *, b, keep, tm):
    seq = x.shape[0] // b
    first = (seq - keep) // tm
    kg = jnp.broadcast_to(jnp.tile(k_gain, N_HEADS)[:, None], (ATT_WIDTH, tm))
    return pl.pallas_call(
        _kv_tail_kernel,
        grid=(b, keep // tm),
        in_specs=[pl.BlockSpec((tm, D_MODEL), lambda bb, i: (bb * (seq // tm) + first + i, 0)),
                  pl.BlockSpec((1, D_MODEL), lambda bb, i: (0, 0)),
                  pl.BlockSpec((2 * ATT_WIDTH, D_MODEL), lambda bb, i: (0, 0)),
                  pl.BlockSpec((ATT_WIDTH, tm), lambda bb, i: (0, 0))],
        out_specs=pl.BlockSpec((None, 2 * ATT_WIDTH, tm), lambda bb, i: (bb, 0, i)),
        out_shape=jax.ShapeDtypeStruct((b, 2 * ATT_WIDTH, keep), F32),
        compiler_params=_params(("parallel", "parallel")),
        name="kv_tail",
    )(x, norm_w.reshape(1, D_MODEL), wt, kg)


SAMPLE_HEADS = 8


def _sample_attn_kernel(*refs):
    n_g = len(DILATIONS)
    qkv = [refs[3 * g:3 * g + 3] for g in range(n_g)]
    gate_ref = refs[3 * n_g]
    c_refs = refs[3 * n_g + 1:4 * n_g + 1]
    qg_ref, kg_ref = refs[4 * n_g + 1:4 * n_g + 3]
    o_ref = refs[4 * n_g + 3]
    kn_refs = refs[4 * n_g + 4:]
    t_new = o_ref.shape[0]
    t_row = lax.broadcasted_iota(jnp.int32, (2 * t_new, 1), 0) % t_new
    t_col = lax.broadcasted_iota(jnp.int32, (1, t_new), 1)

    o_cols = []
    kn_cols = [[] for _ in DILATIONS]
    for j in range(SAMPLE_HEADS // 2):
        cols = slice(j * LANES, (j + 1) * LANES)
        o_g, m_g, l_g = [], [], []
        for g, d in enumerate(DILATIONS):
            q_ref, k_ref, v_ref = qkv[g]
            qm = _split_heads(_head_norm(q_ref[:, cols], NORM_EPS) * qg_ref[g:g + 1, :]).astype(BF16)
            kn = _head_norm(k_ref[:, cols], NORM_EPS) * kg_ref[g:g + 1, :]
            kn_cols[g].append(kn)
            length = c_refs[g].shape[2]
            s_c = _dot(qm, c_refs[g][0, cols, :].astype(BF16))
            s_n = _dot_nt(qm, kn.astype(BF16))
            back = length + t_row - lax.broadcasted_iota(jnp.int32, (1, length), 1)
            s_c = jnp.where((back % d == 0) & (back <= WINDOW_STEPS * d), s_c, MASKED)
            back = t_row - t_col
            s_n = jnp.where((back >= 0) & (back % d == 0) & (back <= WINDOW_STEPS * d), s_n, MASKED)
            m = jnp.maximum(jnp.max(s_c, axis=-1, keepdims=True), jnp.max(s_n, axis=-1, keepdims=True))
            p_c = jnp.exp(s_c - m)
            p_n = jnp.exp(s_n - m)
            l_g.append(jnp.sum(p_c, axis=-1, keepdims=True) + jnp.sum(p_n, axis=-1, keepdims=True))
            o_g.append(_dot_nt(p_c.astype(BF16), c_refs[g][1, cols, :].astype(BF16))
                       + _dot(p_n.astype(BF16), v_ref[:, cols].astype(BF16)))
            m_g.append(m)
        top = jnp.maximum(jnp.maximum(m_g[0], m_g[1]), m_g[2])
        f = [jnp.exp(m - top) for m in m_g]
        den = f[0] * l_g[0] + f[1] * l_g[1] + f[2] * l_g[2]
        o = _join_heads((f[0] * o_g[0] + f[1] * o_g[1] + f[2] * o_g[2]) / den)
        gate = gate_ref[:, cols]
        o_cols.append(o * (gate * _sigmoid(gate)))
    o_ref[...] = jnp.concatenate(o_cols, axis=-1)
    for g in range(n_g):
        kn_refs[g][...] = jnp.concatenate(kn_cols[g], axis=-1)


def _sample_attn(proj, caches_t, q_gain, k_gain):
    b = caches_t[0].shape[0]
    t_new = proj.shape[0] // b
    width = SAMPLE_HEADS * HEAD_DIM
    per_chunk = ATT_WIDTH // width
    col = lambda c: pl.BlockSpec((t_new, width), lambda bb, hc: (bb, c * per_chunk + hc))
    in_specs, args = [], []
    for g in range(len(DILATIONS)):
        in_specs += [col(3 * g), col(3 * g + 1), col(3 * g + 2)]
        args += [proj, proj, proj]
    in_specs.append(col(ATT_IN // ATT_WIDTH - 1))
    args.append(proj)
    for c in caches_t:
        in_specs.append(pl.BlockSpec((None, 2, width, c.shape[3]), lambda bb, hc: (bb, 0, hc, 0)))
        args.append(c)
    gain_spec = pl.BlockSpec((len(DILATIONS), LANES), lambda bb, hc: (0, 0))
    out_spec = pl.BlockSpec((t_new, width), lambda bb, hc: (bb, hc))
    out_sds = jax.ShapeDtypeStruct((b * t_new, ATT_WIDTH), F32)
    return pl.pallas_call(
        _sample_attn_kernel,
        grid=(b, per_chunk),
        in_specs=in_specs + [gain_spec, gain_spec],
        out_specs=[out_spec] * 4,
        out_shape=[out_sds] * 4,
        compiler_params=_params(("parallel", "parallel")),
        name="sample_attn",
    )(*args, q_gain, k_gain)


def _softplus(x):
    return jnp.maximum(x, 0.0) + jnp.log(1.0 + jnp.exp(-jnp.abs(x)))


def _conv_silu(zx_ref, xpad_ref, cw_ref, cb_ref, lc):
    pad = 8
    xpad_ref[pad:pad + lc, :] = zx_ref[:, SSM_D_INNER:SSM_MAIN].astype(F32)
    conv = cb_ref[...] + xpad_ref[pad - 3:pad - 3 + lc, :] * cw_ref[0:1, :]
    for j in range(1, SSM_CONV):
        conv = conv + xpad_ref[pad - 3 + j:pad - 3 + j + lc, :] * cw_ref[j:j + 1, :]
    tail = xpad_ref[pad + lc - 3:pad + lc, :]
    xpad_ref[pad - 3:pad, :] = tail
    return conv * _sigmoid(conv), tail


def _gate_norm(y, z, gn):
    y = y * (z * _sigmoid(z))
    parts = []
    for g in range(SSM_GROUPS):
        yg = y[:, g * SSM_GROUP_WIDTH:(g + 1) * SSM_GROUP_WIDTH]
        parts.append(yg * lax.rsqrt(jnp.mean(yg * yg, axis=-1, keepdims=True) + GATE_NORM_EPS))
    return jnp.concatenate(parts, axis=-1) * gn


def _cumsum_rows(x):
    rows = x.shape[0]
    tril = (lax.broadcasted_iota(jnp.int32, (rows, rows), 0)
            >= lax.broadcasted_iota(jnp.int32, (rows, rows), 1)).astype(BF16)
    h1 = x.astype(BF16)
    r1 = x - h1.astype(F32)
    h2 = r1.astype(BF16)
    h3 = (r1 - h2.astype(F32)).astype(BF16)
    return _dot(tril, h1) + _dot(tril, h2) + _dot(tril, h3)


def _ssd_prompt_kernel(zx_ref, dt_ref, cw_ref, cb_ref, dtb_ref, a_ref, dskip_ref, gn_ref, e_ref,
                       y_ref, nconv_ref, nssm_ref, carry_ref, st_ref):
    lc = zx_ref.shape[0]
    c = pl.program_id(1)
    last = pl.num_programs(1) - 1

    @pl.when(c == 0)
    def _():
        carry_ref[...] = jnp.zeros(carry_ref.shape, BF16)
        st_ref[...] = jnp.zeros(st_ref.shape, F32)

    n_carry = carry_ref.shape[0]
    src = lax.broadcasted_iota(jnp.int32, (lc, n_carry + lc), 1) - n_carry
    dst = lax.broadcasted_iota(jnp.int32, (lc, n_carry + lc), 0)
    taps = SSM_CONV - 1
    shift = jnp.concatenate([(src == dst - (taps - j)).astype(BF16) for j in range(taps)], axis=0)

    def conv_silu(start, width):
        cols = slice(start, start + width)
        raw = zx_ref[:, SSM_D_INNER + start:SSM_D_INNER + start + width]
        shifted = _dot(shift, jnp.concatenate([carry_ref[:, cols], raw], axis=0))
        conv = cb_ref[:, cols] + shifted[0:lc] * cw_ref[0:1, cols]
        for j in range(1, taps):
            conv = conv + shifted[j * lc:(j + 1) * lc] * cw_ref[j:j + 1, cols]
        conv = conv + raw.astype(F32) * cw_ref[taps:taps + 1, cols]
        return conv * _sigmoid(conv)

    dt = _softplus(dt_ref[...] + dtb_ref[...])
    a_cs = _cumsum_rows(dt * a_ref[...])
    a_t = a_cs.T
    dt_t = dt.T
    a_last = a_cs[lc - 1:lc, :]
    to_end = (dt * jnp.exp(a_last - a_cs)).astype(BF16)
    from_start = jnp.exp(a_cs).astype(BF16)
    chunk_decay = jnp.broadcast_to(jnp.exp(a_last), (8, LANES))
    causal = (lax.broadcasted_iota(jnp.int32, (lc, lc), 0) >= lax.broadcasted_iota(jnp.int32, (lc, lc), 1))
    heads_per_group = SSM_HEADS // SSM_GROUPS
    head_w = SSM_D_INNER // SSM_HEADS
    quad = 4
    quad_w = quad * head_w
    lane_head = lax.broadcasted_iota(jnp.int32, (1, quad_w), 1) // head_w
    n_bc = SSM_GROUPS * SSM_STATE
    for g in range(SSM_GROUPS):
        gw = slice(g * SSM_GROUP_WIDTH, (g + 1) * SSM_GROUP_WIDTH)
        xs = conv_silu(g * SSM_GROUP_WIDTH, SSM_GROUP_WIDTH)
        bm = conv_silu(SSM_D_INNER + g * SSM_STATE, SSM_STATE).astype(BF16)
        cm = conv_silu(SSM_D_INNER + n_bc + g * SSM_STATE, SSM_STATE).astype(BF16)
        e = e_ref[:, gw]
        xs_b = xs.astype(BF16)
        xdte = (xs * _dot(to_end, e)).astype(BF16)
        s_prev = st_ref[:, gw]
        cb = _dot_nt(cm, bm)
        y = _dot(cm, s_prev.astype(BF16)) * _dot(from_start, e)
        st_ref[:, gw] = (s_prev * _dot2(chunk_decay, e)[0:1, :]
                         + lax.dot_general(bm, xdte, TN, preferred_element_type=F32))
        quad_out = []
        for qd in range(heads_per_group // quad):
            h0 = g * heads_per_group + quad * qd
            ms = []
            for h in range(h0, h0 + quad):
                seg = a_cs[:, h:h + 1] - a_t[h:h + 1, :]
                ms.append((cb * jnp.where(causal, jnp.exp(seg), 0.0) * dt_t[h:h + 1, :]).astype(BF16))
            xq = xs_b[:, qd * quad_w:(qd + 1) * quad_w]
            rhs = jnp.concatenate([jnp.where(lane_head == i, xq, jnp.zeros_like(xq)) for i in range(quad)], axis=0)
            quad_out.append(_dot(jnp.concatenate(ms, axis=1), rhs))
        y = y + jnp.concatenate(quad_out, axis=-1) + dskip_ref[:, gw] * xs
        z = zx_ref[:, gw].astype(F32)
        y = y * (z * _sigmoid(z))
        y = y * lax.rsqrt(jnp.mean(y * y, axis=-1, keepdims=True) + GATE_NORM_EPS) * gn_ref[:, gw]
        y_ref[:, gw] = y.astype(y_ref.dtype)

    tail = zx_ref[lc - n_carry:lc, SSM_D_INNER:SSM_MAIN]
    carry_ref[...] = tail

    @pl.when(c == last)
    def _():
        nconv_ref[...] = tail.astype(F32)[n_carry - taps:n_carry]

    @pl.when(c == last)
    def _():
        nssm_ref[...] = st_ref[...].T


def _ssd_prompt(zx, dt, b, params):
    m = zx.shape[0]
    nc = m // b // SSM_CHUNK
    full = lambda x: pl.BlockSpec(x.shape, lambda bb, c: (0, 0))
    row = lambda w: pl.BlockSpec((SSM_CHUNK, w), lambda bb, c: (bb * nc + c, 0))
    return pl.pallas_call(
        _ssd_prompt_kernel,
        grid=(b, nc),
        in_specs=[row(SSM_MAIN), row(LANES)] + [full(x) for x in params],
        out_specs=[row(SSM_D_INNER),
                   pl.BlockSpec((None, SSM_CONV - 1, SSM_CONV_DIM), lambda bb, c: (bb, 0, 0)),
                   pl.BlockSpec((None, SSM_D_INNER, SSM_STATE), lambda bb, c: (bb, 0, 0))],
        out_shape=[jax.ShapeDtypeStruct((m, SSM_D_INNER), BF16),
                   jax.ShapeDtypeStruct((b, SSM_CONV - 1, SSM_CONV_DIM), F32),
                   jax.ShapeDtypeStruct((b, SSM_D_INNER, SSM_STATE), F32)],
        scratch_shapes=[pltpu.VMEM((16, SSM_CONV_DIM), BF16),
                        pltpu.VMEM((SSM_STATE, SSM_D_INNER), F32)],
        compiler_params=_params(("parallel", "arbitrary")),
        name="ssd_prompt",
    )(zx, dt, *params)


def _ssd_sample_kernel(zx_ref, dt_ref, conv0_ref, s0_ref, cw_ref, cb_ref, dtb_ref, a_ref, dskip_ref, gn_ref,
                       e_ref, eg_ref, y_ref, nconv_ref, nssm_ref, xpad_ref):
    lc = zx_ref.shape[0]
    xpad_ref[0:5, :] = jnp.zeros((5, SSM_CONV_DIM), F32)
    xpad_ref[5:8, :] = conv0_ref[...]
    xbc, tail = _conv_silu(zx_ref, xpad_ref, cw_ref, cb_ref, lc)
    nconv_ref[...] = tail

    xs = xbc[:, 0:SSM_D_INNER]
    bm = xbc[:, SSM_D_INNER:SSM_D_INNER + SSM_GROUPS * SSM_STATE]
    cm = xbc[:, SSM_D_INNER + SSM_GROUPS * SSM_STATE:]
    e = e_ref[...]
    dt = _softplus(dt_ref[...] + dtb_ref[...])
    dta = dt * a_ref[...]
    rows = [dta[0:1, :]]
    for i in range(1, lc):
        rows.append(rows[-1] + dta[i:i + 1, :])
    a_cs = jnp.concatenate(rows, axis=0)
    a_last = rows[-1]
    xdt = xs * _dot2(dt, e)
    xdte = (xdt * _dot2(jnp.exp(a_last - a_cs), e)).astype(BF16)
    ea_e = _dot2(jnp.exp(a_cs), e)
    cd_e = _dot2(jnp.broadcast_to(jnp.exp(a_last), (8, LANES)), e)[0:1, :]

    a_l = jnp.concatenate([a_cs] * lc, axis=0)
    a_s = jnp.concatenate([jnp.broadcast_to(a_cs[s:s + 1, :], (lc, LANES)) for s in range(lc)], axis=0)
    l_idx = lax.broadcasted_iota(jnp.int32, (lc * lc, 1), 0) % lc
    s_idx = lax.broadcasted_iota(jnp.int32, (lc * lc, 1), 0) // lc
    decay = jnp.where(l_idx >= s_idx, jnp.exp(a_l - a_s), 0.0)
    cb_prod = jnp.concatenate([cm * bm[s:s + 1, :] for s in range(lc)], axis=0)
    mix = _dot2(_dot2(cb_prod, eg_ref[...]) * decay, e)
    y = ea_e * 0.0
    for s in range(lc):
        y = y + mix[s * lc:(s + 1) * lc, :] * xdt[s:s + 1, :]

    s_prev = s0_ref[...].T
    s_b = s_prev.astype(BF16)
    bm_b = bm.astype(BF16)
    cm_b = cm.astype(BF16)
    y_off, s_parts = [], []
    for g in range(SSM_GROUPS):
        gs = slice(g * SSM_STATE, (g + 1) * SSM_STATE)
        gw = slice(g * SSM_GROUP_WIDTH, (g + 1) * SSM_GROUP_WIDTH)
        y_off.append(_dot(cm_b[:, gs], s_b[:, gw]))
        s_parts.append(lax.dot_general(bm_b[:, gs], xdte[:, gw], TN, preferred_element_type=F32))
    y = y + jnp.concatenate(y_off, axis=-1) * ea_e + dskip_ref[...] * xs
    s_new = s_prev * cd_e + jnp.concatenate(s_parts, axis=-1)
    nssm_ref[...] = s_new.T
    z = zx_ref[:, 0:SSM_D_INNER]
    y_ref[...] = _gate_norm(y, z, gn_ref[...])


def _ssd_sample(zx, dt, conv0, s0, params):
    b = conv0.shape[0]
    t_new = zx.shape[0] // b
    full = lambda x: pl.BlockSpec(x.shape, lambda bb: (0, 0))
    row = lambda w: pl.BlockSpec((t_new, w), lambda bb: (bb, 0))
    conv_spec = pl.BlockSpec((None, SSM_CONV - 1, SSM_CONV_DIM), lambda bb: (bb, 0, 0))
    state_spec = pl.BlockSpec((None, SSM_D_INNER, SSM_STATE), lambda bb: (bb, 0, 0))
    return pl.pallas_call(
        _ssd_sample_kernel,
        grid=(b,),
        in_specs=[row(SSM_MAIN), row(LANES), conv_spec, state_spec] + [full(x) for x in params],
        out_specs=[row(SSM_D_INNER), conv_spec, state_spec],
        out_shape=[jax.ShapeDtypeStruct((b * t_new, SSM_D_INNER), F32),
                   jax.ShapeDtypeStruct((b, SSM_CONV - 1, SSM_CONV_DIM), F32),
                   jax.ShapeDtypeStruct((b, SSM_D_INNER, SSM_STATE), F32)],
        scratch_shapes=[pltpu.VMEM((8 + t_new, SSM_CONV_DIM), F32)],
        compiler_params=_params(("parallel",)),
        name="ssd_sample",
    )(zx, dt, conv0, s0, *params)


def _one_hot_expand(n_rows, n_cols, group):
    r = lax.broadcasted_iota(jnp.int32, (n_rows, n_cols), 0)
    c = lax.broadcasted_iota(jnp.int32, (n_rows, n_cols), 1)
    return (c // group == r).astype(BF16)


def _kv_rows(feature_major, b):
    tokens = feature_major.shape[-1]
    x = feature_major.reshape(1, b, 2, N_HEADS, HEAD_DIM, tokens)
    return jnp.transpose(x, (0, 1, 5, 2, 3, 4))


def kernel(x_prompt, x_sample, cache_kv_g0, cache_kv_g1, cache_kv_g2, state_conv, state_ssm, attn_norm, attn_w_in, attn_q_gain, attn_k_gain, attn_w_out, ssm_norm, ssm_w_in, ssm_conv_w, ssm_conv_b, ssm_dt_bias, ssm_A_log, ssm_D, ssm_gate_norm, ssm_w_out):
    b, seq, dm = x_prompt.shape
    sb, st, _ = x_sample.shape
    xp = x_prompt.reshape(b * seq, dm)
    xs = x_sample.reshape(sb * st, dm)

    w_in = attn_w_in[0].astype(BF16)
    w_out = attn_w_out[0].astype(BF16)
    scale = HEAD_DIM ** -0.5
    q_gain3 = jnp.tile(attn_q_gain[0], (1, N_HEADS)) * scale
    k_gain3 = jnp.tile(attn_k_gain[0], (1, N_HEADS))

    log2e = 1.4426950408889634
    tile_gain = jnp.ones((ATT_IN // ATT_WIDTH, 1, ATT_WIDTH), F32)
    for g in range(len(DILATIONS)):
        tile_gain = tile_gain.at[3 * g, 0].set(q_gain3[g] * log2e).at[3 * g + 1, 0].set(k_gain3[g])
    qkvg = _proj(xp, attn_norm[0], w_in, tm=1024, tn=ATT_WIDTH, out_dtype=BF16, perm_seq=seq, qk_gain=tile_gain)
    part2 = _attn_group(qkvg, 2, n_sub=2)
    part1 = _attn_group(qkvg, 1, n_sub=2)
    o_gated = _attn_group(qkvg, 0, n_sub=2, others=[part1, part2])
    y1p = _outproj(o_gated, w_out, xp, tm=1024, perm_seq=seq)
    kv_p = []
    for g, d in enumerate(DILATIONS):
        keep = min(d * WINDOW_STEPS, seq)
        wt = attn_w_in[0][:, (3 * g + 1) * ATT_WIDTH:(3 * g + 3) * ATT_WIDTH].astype(BF16).T
        kv_p.append(_kv_rows(_kv_tail(xp, attn_norm[0], wt, attn_k_gain[0, g], b=b, keep=keep,
                                      tm=min(keep, 512)), b))

    proj_s = _proj(xs, attn_norm[0], w_in, tm=sb * st, tn=1024, out_dtype=F32)
    caches_t = [jnp.transpose(c[0], (0, 2, 3, 4, 1)).reshape(sb, 2, ATT_WIDTH, c.shape[2])
                for c in (cache_kv_g0, cache_kv_g1, cache_kv_g2)]
    o_s, kn0, kn1, kn2 = _sample_attn(proj_s, caches_t, q_gain3[:, :LANES], k_gain3[:, :LANES])
    y1s = _outproj(o_s, w_out, xs, tm=sb * st)
    kv_s = []
    for g, kn in enumerate((kn0, kn1, kn2)):
        v = proj_s[:, (3 * g + 2) * ATT_WIDTH:(3 * g + 3) * ATT_WIDTH]
        kv_s.append(jnp.stack([kn, v], axis=1).reshape(1, sb, st, 2, N_HEADS, HEAD_DIM))

    w_in2 = ssm_w_in[0]
    w_main = w_in2[:, :SSM_MAIN].astype(BF16)
    w_dt = jnp.pad(w_in2[:, SSM_MAIN:], ((0, 0), (0, LANES - SSM_HEADS))).astype(BF16)
    w_out2 = ssm_w_out[0].astype(BF16)
    pad_h = lambda v: jnp.pad(v.astype(F32), (0, LANES - SSM_HEADS)).reshape(1, LANES)
    e32 = _one_hot_expand(LANES, SSM_D_INNER, SSM_D_INNER // SSM_HEADS)
    ssm_params = (ssm_conv_w[0], ssm_conv_b[0].reshape(1, -1), pad_h(ssm_dt_bias[0]),
                  pad_h(-jnp.exp(ssm_A_log[0].astype(F32))),
                  jnp.repeat(ssm_D[0].astype(F32), SSM_D_INNER // SSM_HEADS).reshape(1, -1),
                  ssm_gate_norm[0].reshape(1, -1), e32)

    zx_p, dt_p = _proj(y1p, ssm_norm[0], w_main, tm=1024, tn=1024, out_dtype=BF16, w_dt=w_dt)
    yg_p, conv_p, ssm_p = _ssd_prompt(zx_p, dt_p, b, ssm_params)
    y2p = _outproj(yg_p, w_out2, y1p, tm=1024)

    eg = (lax.broadcasted_iota(jnp.int32, (SSM_GROUPS * SSM_STATE, LANES), 0) // SSM_STATE
          == lax.broadcasted_iota(jnp.int32, (SSM_GROUPS * SSM_STATE, LANES), 1) // (SSM_HEADS // SSM_GROUPS))
    eg = (eg & (lax.broadcasted_iota(jnp.int32, eg.shape, 1) < SSM_HEADS)).astype(BF16)
    zx_s, dt_s = _proj(y1s, ssm_norm[0], w_main, tm=sb * st, tn=1024, out_dtype=F32, w_dt=w_dt)
    yg_s, conv_s, ssm_s = _ssd_sample(zx_s, dt_s, state_conv[0], state_ssm[0].reshape(sb, SSM_D_INNER, SSM_STATE),
                                      ssm_params + (eg,))
    y2s = _outproj(yg_s, w_out2, y1s, tm=sb * st)

    hp = SSM_D_INNER // SSM_HEADS
    return (y2p.reshape(b, seq, dm), y2s.reshape(sb, st, dm),
            kv_p[0], kv_p[1], kv_p[2], kv_s[0], kv_s[1], kv_s[2],
            conv_p[None], conv_s[None],
            ssm_p.reshape(1, b, SSM_HEADS, hp, SSM_STATE), ssm_s.reshape(1, sb, SSM_HEADS, hp, SSM_STATE))
```

```python
import functools

import jax
import jax.numpy as jnp
from jax import lax
from jax.experimental import pallas as pl
from jax.experimental.pallas import tpu as pltpu

F32 = jnp.float32
BF16 = jnp.bfloat16

D_MODEL = 1024
N_HEADS = 16
HEAD_DIM = 64
ATT_WIDTH = N_HEADS * HEAD_DIM
DILATIONS = (1, 4, 16)
WINDOW_STEPS = 128
ATT_IN = 10 * ATT_WIDTH
PERM = 16

SSM_D_INNER = 2048
SSM_HEADS = 32
SSM_STATE = 128
SSM_GROUPS = 4
SSM_GROUP_WIDTH = SSM_D_INNER // SSM_GROUPS
SSM_CONV = 4
SSM_CONV_DIM = SSM_D_INNER + 2 * SSM_GROUPS * SSM_STATE
SSM_MAIN = SSM_D_INNER + SSM_CONV_DIM
SSM_CHUNK = 128

NORM_EPS = 1e-6
GATE_NORM_EPS = 1e-5
MASKED = -1e30

LANES = 128
VMEM_LIMIT = 56 * 1024 * 1024

NT = (((1,), (1,)), ((), ()))
TN = (((0,), (0,)), ((), ()))


def _params(semantics):
    return pltpu.CompilerParams(dimension_semantics=semantics, vmem_limit_bytes=VMEM_LIMIT)


def _rms(x, w, eps):
    return x * lax.rsqrt(jnp.mean(x * x, axis=-1, keepdims=True) + eps) * w


def _sigmoid(x):
    return 1.0 / (1.0 + jnp.exp(-x))


def _split2(v):
    hi = v.astype(BF16)
    lo = (v - hi.astype(F32)).astype(BF16)
    return hi, lo


def _dot(a, b):
    return jnp.dot(a, b, preferred_element_type=F32)


def _dot_nt(a, b):
    return lax.dot_general(a, b, NT, preferred_element_type=F32)


def _dot2(v, e):
    hi, lo = _split2(v)
    return _dot(hi, e) + _dot(lo, e)


def _low_half():
    return lax.broadcasted_iota(jnp.int32, (1, LANES), 1) < HEAD_DIM


def _head_norm(x, eps):
    lo = _low_half()
    parts = []
    for j in range(x.shape[1] // LANES):
        t = x[:, j * LANES:(j + 1) * LANES]
        t2 = t * t
        s_lo = jnp.sum(jnp.where(lo, t2, 0.0), axis=-1, keepdims=True)
        s_hi = jnp.sum(jnp.where(lo, 0.0, t2), axis=-1, keepdims=True)
        r = jnp.where(lo, lax.rsqrt(s_lo * (1.0 / HEAD_DIM) + eps), lax.rsqrt(s_hi * (1.0 / HEAD_DIM) + eps))
        parts.append(t * r)
    return jnp.concatenate(parts, axis=-1)


def _split_heads(qp):
    lo = _low_half()
    zero = jnp.zeros_like(qp)
    return jnp.concatenate([jnp.where(lo, qp, zero), jnp.where(lo, zero, qp)], axis=0)


def _join_heads(x2):
    rows = x2.shape[0] // 2
    return jnp.where(_low_half(), x2[0:rows], x2[rows:])


def _proj_kernel(x_ref, nw_ref, w_ref, *rest, perm, with_dt, qk_norm):
    rest = list(rest)
    gain_ref = rest.pop(0) if qk_norm else None
    wdt_ref = rest.pop(0) if with_dt else None
    o_ref = rest.pop(0)
    dt_ref = rest.pop(0) if with_dt else None
    h_ref = rest.pop(0)
    slab_ref = rest.pop(0) if perm else None

    @pl.when(pl.program_id(1) == 0)
    def _():
        xn = _rms(x_ref[...], nw_ref[...], NORM_EPS)
        if perm:
            rows = x_ref.shape[0] // PERM
            for c in range(x_ref.shape[1] // LANES):
                cols = slice(c * LANES, (c + 1) * LANES)
                slab_ref[...] = xn[:, cols]
                for r in range(PERM):
                    h_ref[r * rows:(r + 1) * rows, cols] = slab_ref[pl.ds(r, rows, stride=PERM), :].astype(BF16)
        else:
            h_ref[...] = xn.astype(BF16)
        if with_dt:
            dt_ref[...] = _dot(h_ref[...], wdt_ref[...])

    if qk_norm:
        lead = (slice(None),) * (len(o_ref.shape) - 1)
        per_step = w_ref.shape[1] // ATT_WIDTH
        for c in range(per_step):
            cols = slice(c * ATT_WIDTH, (c + 1) * ATT_WIDTH)
            tile = pl.program_id(1) * per_step + c
            is_qk = (tile < 3 * len(DILATIONS)) & (tile % 3 < 2)

            def project(normed, c=c, cols=cols):
                res = _dot(h_ref[...], w_ref[:, cols])
                if normed:
                    res = _head_norm(res, NORM_EPS) * gain_ref[c]
                o_ref[lead + (cols,)] = res.astype(o_ref.dtype).reshape(o_ref.shape[:-1] + (ATT_WIDTH,))

            pl.when(is_qk)(functools.partial(project, True))
            pl.when(jnp.logical_not(is_qk))(functools.partial(project, False))
    else:
        o_ref[...] = _dot(h_ref[...], w_ref[...]).astype(o_ref.dtype).reshape(o_ref.shape)


def _proj(x, norm_w, w, *, tm, tn, out_dtype, n_out=None, perm_seq=None, w_dt=None, qk_gain=None):
    m, dm = x.shape
    n = w.shape[1] if n_out is None else n_out
    assert n % tn == 0 and m % tm == 0
    grid = (m // tm, n // tn)
    in_specs = [pl.BlockSpec((tm, dm), lambda i, j: (i, 0)),
                pl.BlockSpec((1, dm), lambda i, j: (0, 0)),
                pl.BlockSpec((dm, tn), lambda i, j: (0, j))]
    args = [x, norm_w.reshape(1, dm), w]
    if qk_gain is not None:
        in_specs.append(pl.BlockSpec((tn // ATT_WIDTH, 1, ATT_WIDTH), lambda i, j: (j, 0, 0)))
        args.append(qk_gain)
    if perm_seq is not None:
        per_b = perm_seq // tm
        rows = tm // PERM
        out_shape = [jax.ShapeDtypeStruct((m // perm_seq, PERM, perm_seq // PERM, n), out_dtype)]
        out_specs = [pl.BlockSpec((None, PERM, rows, tn), lambda i, j: (i // per_b, 0, i % per_b, j))]
    else:
        out_shape = [jax.ShapeDtypeStruct((m, n), out_dtype)]
        out_specs = [pl.BlockSpec((tm, tn), lambda i, j: (i, j))]
    if w_dt is not None:
        in_specs.append(pl.BlockSpec((dm, LANES), lambda i, j: (0, 0)))
        args.append(w_dt)
        out_shape.append(jax.ShapeDtypeStruct((m, LANES), F32))
        out_specs.append(pl.BlockSpec((tm, LANES), lambda i, j: (i, 0)))
    outs = pl.pallas_call(
        functools.partial(_proj_kernel, perm=perm_seq is not None, with_dt=w_dt is not None,
                          qk_norm=qk_gain is not None),
        grid=grid, in_specs=in_specs, out_specs=out_specs, out_shape=out_shape,
        scratch_shapes=[pltpu.VMEM((tm, dm), BF16)] + ([pltpu.VMEM((tm, LANES), F32)] if perm_seq else []),
        compiler_params=_params(("parallel", "arbitrary")),
        name="norm_proj",
    )(*args)
    return outs if w_dt is not None else outs[0]


def _outproj_kernel(a_ref, w_ref, x_ref, o_ref, *scratch, unperm):
    tm = o_ref.shape[0]
    a = a_ref[...].reshape(tm, a_ref.shape[-1]).astype(BF16)
    res = _dot(a, w_ref[...])
    if unperm:
        slab_ref, = scratch
        rows = tm // PERM
        for c in range(o_ref.shape[1] // LANES):
            cols = slice(c * LANES, (c + 1) * LANES)
            for r in range(PERM):
                slab_ref[pl.ds(r, rows, stride=PERM), :] = res[r * rows:(r + 1) * rows, cols]
            o_ref[:, cols] = x_ref[:, cols] + slab_ref[...]
    else:
        o_ref[...] = x_ref[...] + res


def _outproj(a, w, x, *, tm, perm_seq=None):
    m, n = x.shape
    k = w.shape[0]
    if perm_seq is not None:
        per_b = perm_seq // tm
        a_spec = pl.BlockSpec((None, PERM, tm // PERM, k), lambda i: (i // per_b, 0, i % per_b, 0))
    else:
        a_spec = pl.BlockSpec((tm, k), lambda i: (i, 0))
    return pl.pallas_call(
        functools.partial(_outproj_kernel, unperm=perm_seq is not None),
        grid=(m // tm,),
        in_specs=[a_spec, pl.BlockSpec((k, n), lambda i: (0, 0)), pl.BlockSpec((tm, n), lambda i: (i, 0))],
        out_specs=pl.BlockSpec((tm, n), lambda i: (i, 0)),
        out_shape=jax.ShapeDtypeStruct((m, n), F32),
        scratch_shapes=[pltpu.VMEM((tm, LANES), F32)] if perm_seq else [],
        compiler_params=_params(("parallel",)),
        name="out_proj",
    )(a, w, x)


ATT_SUB = 128


def _attn_group_kernel(*refs, n_chunks, n_sub, merge):
    it = iter(refs)
    q_ref, k_ref, v_ref = next(it), next(it), next(it)
    if merge:
        gate_ref = next(it)
        others = [(next(it), next(it)) for _ in range(len(DILATIONS) - 1)]
        o_out = next(it)
    else:
        o_out, lse_out = next(it), next(it)
    ks_ref, vs_ref = next(it), next(it)

    rc = q_ref.shape[1] // n_sub
    assert n_chunks * rc == ATT_SUB
    n = pl.program_id(2)
    narrow = rc % 16 != 0

    def subs(ref, cols=slice(None)):
        x = ref[:, :, cols]
        x = x.astype(F32) if narrow and x.dtype == BF16 else x
        return [x[:, u * rc:(u + 1) * rc, :].reshape(ATT_SUB, x.shape[-1]) for u in range(n_sub)]

    def unsubs(parts, dtype):
        parts = [p.reshape(n_chunks, rc, p.shape[-1]) for p in parts]
        if narrow:
            return jnp.concatenate(parts, axis=1).astype(dtype)
        return jnp.concatenate([p.astype(dtype) for p in parts], axis=1)

    @pl.when(n == 0)
    def _():
        ks_ref[0:ATT_SUB, :] = jnp.zeros((ATT_SUB, ATT_WIDTH), BF16)
        vs_ref[0:ATT_SUB, :] = jnp.zeros((ATT_SUB, ATT_WIDTH), BF16)

    q_sub = [x.astype(BF16) for x in subs(q_ref)]
    for u, (k, v) in enumerate(zip(subs(k_ref), subs(v_ref))):
        ks_ref[(u + 1) * ATT_SUB:(u + 2) * ATT_SUB, :] = k.astype(BF16)
        vs_ref[(u + 1) * ATT_SUB:(u + 2) * ATT_SUB, :] = v.astype(BF16)

    qi = lax.broadcasted_iota(jnp.int32, (ATT_SUB, 2 * ATT_SUB), 0)
    kj = lax.broadcasted_iota(jnp.int32, (ATT_SUB, 2 * ATT_SUB), 1)
    kc = jnp.where(kj >= ATT_SUB, kj - ATT_SUB, kj)
    tq = n_chunks * (qi % rc) + qi // rc
    tk = n_chunks * (kc % rc) + kc // rc + jnp.where(kj >= ATT_SUB, 0, -ATT_SUB)
    dist = tq - tk
    in_band = (dist >= 0) & (dist <= WINDOW_STEPS)
    band = jnp.where(in_band, 0.0, MASKED)
    band_first = jnp.where(in_band & ((kj >= ATT_SUB) | (n > 0)), 0.0, MASKED)
    bias = [jnp.concatenate([x, x], axis=0) for x in (band_first, band)]

    for j in range(N_HEADS // 2):
        cols = slice(j * LANES, (j + 1) * LANES)
        o_parts, lse_parts = [], []
        for u in range(n_sub):
            keys = slice(u * ATT_SUB, (u + 2) * ATT_SUB)
            s = _dot_nt(_split_heads(q_sub[u][:, cols]), ks_ref[keys, cols]) + bias[min(u, 1)]
            m = jnp.max(s, axis=-1, keepdims=True)
            p = jnp.exp2(s - m)
            l = jnp.sum(p, axis=-1, keepdims=True)
            pv = _dot(p.astype(BF16), vs_ref[keys, cols])
            o_parts.append(_join_heads(pv / l))
            lse_parts.append(_join_heads(jnp.broadcast_to(m + jnp.log2(l), (2 * ATT_SUB, LANES))))
        if merge:
            gate = subs(gate_ref, cols)
            o_g = [o_parts] + [subs(o_ref, cols) for o_ref, _ in others]
            lse_g = [lse_parts] + [subs(l_ref, cols) for _, l_ref in others]
            outs = []
            for u in range(n_sub):
                top = jnp.maximum(jnp.maximum(lse_g[0][u], lse_g[1][u]), lse_g[2][u])
                w = [jnp.exp2(lse_g[i][u] - top) for i in range(3)]
                o = sum(w[i] * o_g[i][u].astype(F32) for i in range(3)) / (w[0] + w[1] + w[2])
                g = gate[u].astype(F32)
                outs.append(o * (g * _sigmoid(g)))
            o_out[:, :, cols] = unsubs(outs, BF16)
        else:
            o_out[:, :, cols] = unsubs(o_parts, BF16)
            lse_out[:, :, cols] = unsubs(lse_parts, F32)
    ks_ref[0:ATT_SUB, :] = ks_ref[n_sub * ATT_SUB:(n_sub + 1) * ATT_SUB, :]
    vs_ref[0:ATT_SUB, :] = vs_ref[n_sub * ATT_SUB:(n_sub + 1) * ATT_SUB, :]


def _attn_group(qkvg, g, *, n_sub, others=None):
    b, _, t, _ = qkvg.shape
    d = DILATIONS[g]
    n_chunks = PERM // d
    rc = ATT_SUB // n_chunks * n_sub
    nb = t // rc
    merge = others is not None
    view = lambda a: a.reshape(b, n_chunks, d, t, a.shape[-1])

    def spec(col=0):
        return pl.BlockSpec((None, n_chunks, None, rc, ATT_WIDTH), lambda bb, r, n: (bb, 0, r, n, col))

    qkvg5 = view(qkvg)
    in_specs = [spec(3 * g), spec(3 * g + 1), spec(3 * g + 2)]
    args = [qkvg5, qkvg5, qkvg5]
    full = jax.ShapeDtypeStruct((b, n_chunks, d, t, ATT_WIDTH), BF16)
    if merge:
        in_specs.append(spec(ATT_IN // ATT_WIDTH - 1))
        args.append(qkvg5)
        for o, lse in others:
            in_specs += [spec(), spec()]
            args += [view(o), view(lse)]
        out_shape, out_specs = [full], [spec()]
    else:
        out_shape = [full, jax.ShapeDtypeStruct(full.shape, F32)]
        out_specs = [spec(), spec()]
    outs = pl.pallas_call(
        functools.partial(_attn_group_kernel, n_chunks=n_chunks, n_sub=n_sub, merge=merge),
        grid=(b, d, nb), in_specs=in_specs, out_specs=out_specs, out_shape=out_shape,
        scratch_shapes=[pltpu.VMEM(((n_sub + 1) * ATT_SUB, ATT_WIDTH), BF16)] * 2,
        compiler_params=_params(("parallel", "parallel", "arbitrary")),
        name=f"attn_group{g}",
    )(*args)
    outs = [a.reshape(b, PERM, t, ATT_WIDTH) for a in outs]
    return outs[0] if merge else tuple(outs)


def _kv_tail_kernel(x_ref, nw_ref, wk_ref, wv_ref, kg_ref, o_ref):
    h = _rms(x_ref[...], nw_ref[...], NORM_EPS).astype(BF16)
    o_ref[0:ATT_WIDTH, :] = (_head_norm(_dot(h, wk_ref[...]), NORM_EPS) * kg_ref[...]).T
    o_ref[ATT_WIDTH:2 * ATT_WIDTH, :] = _dot(h, wv_ref[...]).T


def _kv_tail(x, norm_w, w, g, k_gain, *, b, keep, tm):
    seq = x.shape[0] // b
    first = (seq - keep) // tm
    w_spec = lambda col: pl.BlockSpec((D_MODEL, ATT_WIDTH), lambda bb, i: (0, col))
    return pl.pallas_call(
        _kv_tail_kernel,
        grid=(b, keep // tm),
        in_specs=[pl.BlockSpec((tm, D_MODEL), lambda bb, i: (bb * (seq // tm) + first + i, 0)),
                  pl.BlockSpec((1, D_MODEL), lambda bb, i: (0, 0)),
                  w_spec(3 * g + 1), w_spec(3 * g + 2),
                  pl.BlockSpec((1, ATT_WIDTH), lambda bb, i: (0, 0))],
        out_specs=pl.BlockSpec((None, 2 * ATT_WIDTH, tm), lambda bb, i: (bb, 0, i)),
        out_shape=jax.ShapeDtypeStruct((b, 2 * ATT_WIDTH, keep), F32),
        compiler_params=_params(("parallel", "parallel")),
        name="kv_tail",
    )(x, norm_w.reshape(1, D_MODEL), w, w, k_gain)


SAMPLE_HEADS = 8


def _sample_attn_kernel(*refs):
    n_g = len(DILATIONS)
    qkv = [refs[3 * g:3 * g + 3] for g in range(n_g)]
    gate_ref = refs[3 * n_g]
    c_refs = refs[3 * n_g + 1:4 * n_g + 1]
    qg_ref, kg_ref = refs[4 * n_g + 1:4 * n_g + 3]
    o_ref = refs[4 * n_g + 3]
    kn_refs = refs[4 * n_g + 4:]
    t_new = o_ref.shape[0]
    t_row = lax.broadcasted_iota(jnp.int32, (2 * t_new, 1), 0) % t_new
    t_col = lax.broadcasted_iota(jnp.int32, (1, t_new), 1)

    o_cols = []
    kn_cols = [[] for _ in DILATIONS]
    for j in range(SAMPLE_HEADS // 2):
        cols = slice(j * LANES, (j + 1) * LANES)
        o_g, m_g, l_g = [], [], []
        for g, d in enumerate(DILATIONS):
            q_ref, k_ref, v_ref = qkv[g]
            qm = _split_heads(_head_norm(q_ref[:, cols], NORM_EPS) * qg_ref[g:g + 1, :]).astype(BF16)
            kn = _head_norm(k_ref[:, cols], NORM_EPS) * kg_ref[g:g + 1, :]
            kn_cols[g].append(kn)
            length = c_refs[g].shape[2]
            s_c = _dot(qm, c_refs[g][0, cols, :].astype(BF16))
            s_n = _dot_nt(qm, kn.astype(BF16))
            back = length + t_row - lax.broadcasted_iota(jnp.int32, (1, length), 1)
            s_c = jnp.where((back % d == 0) & (back <= WINDOW_STEPS * d), s_c, MASKED)
            back = t_row - t_col
            s_n = jnp.where((back >= 0) & (back % d == 0) & (back <= WINDOW_STEPS * d), s_n, MASKED)
            m = jnp.maximum(jnp.max(s_c, axis=-1, keepdims=True), jnp.max(s_n, axis=-1, keepdims=True))
            p_c = jnp.exp(s_c - m)
            p_n = jnp.exp(s_n - m)
            l_g.append(jnp.sum(p_c, axis=-1, keepdims=True) + jnp.sum(p_n, axis=-1, keepdims=True))
            o_g.append(_dot_nt(p_c.astype(BF16), c_refs[g][1, cols, :].astype(BF16))
                       + _dot(p_n.astype(BF16), v_ref[:, cols].astype(BF16)))
            m_g.append(m)
        top = jnp.maximum(jnp.maximum(m_g[0], m_g[1]), m_g[2])
        f = [jnp.exp(m - top) for m in m_g]
        den = f[0] * l_g[0] + f[1] * l_g[1] + f[2] * l_g[2]
        o = _join_heads((f[0] * o_g[0] + f[1] * o_g[1] + f[2] * o_g[2]) / den)
        gate = gate_ref[:, cols]
        o_cols.append(o * (gate * _sigmoid(gate)))
    o_ref[...] = jnp.concatenate(o_cols, axis=-1)
    for g in range(n_g):
        kn_refs[g][...] = jnp.concatenate(kn_cols[g], axis=-1)


def _sample_attn(proj, caches_t, q_gain, k_gain):
    b = caches_t[0].shape[0]
    t_new = proj.shape[0] // b
    width = SAMPLE_HEADS * HEAD_DIM
    per_chunk = ATT_WIDTH // width
    col = lambda c: pl.BlockSpec((t_new, width), lambda bb, hc: (bb, c * per_chunk + hc))
    in_specs, args = [], []
    for g in range(len(DILATIONS)):
        in_specs += [col(3 * g), col(3 * g + 1), col(3 * g + 2)]
        args += [proj, proj, proj]
    in_specs.append(col(ATT_IN // ATT_WIDTH - 1))
    args.append(proj)
    for c in caches_t:
        in_specs.append(pl.BlockSpec((None, 2, width, c.shape[3]), lambda bb, hc: (bb, 0, hc, 0)))
        args.append(c)
    gain_spec = pl.BlockSpec((len(DILATIONS), LANES), lambda bb, hc: (0, 0))
    out_spec = pl.BlockSpec((t_new, width), lambda bb, hc: (bb, hc))
    out_sds = jax.ShapeDtypeStruct((b * t_new, ATT_WIDTH), F32)
    return pl.pallas_call(
        _sample_attn_kernel,
        grid=(b, per_chunk),
        in_specs=in_specs + [gain_spec, gain_spec],
        out_specs=[out_spec] * 4,
        out_shape=[out_sds] * 4,
        compiler_params=_params(("parallel", "parallel")),
        name="sample_attn",
    )(*args, q_gain, k_gain)


def _softplus(x):
    return jnp.maximum(x, 0.0) + jnp.log(1.0 + jnp.exp(-jnp.abs(x)))


def _conv_silu(zx_ref, xpad_ref, cw_ref, cb_ref, lc):
    pad = 8
    xpad_ref[pad:pad + lc, :] = zx_ref[:, SSM_D_INNER:SSM_MAIN].astype(F32)
    conv = cb_ref[...] + xpad_ref[pad - 3:pad - 3 + lc, :] * cw_ref[0:1, :]
    for j in range(1, SSM_CONV):
        conv = conv + xpad_ref[pad - 3 + j:pad - 3 + j + lc, :] * cw_ref[j:j + 1, :]
    tail = xpad_ref[pad + lc - 3:pad + lc, :]
    xpad_ref[pad - 3:pad, :] = tail
    return conv * _sigmoid(conv), tail


def _gate_norm(y, z, gn):
    y = y * (z * _sigmoid(z))
    parts = []
    for g in range(SSM_GROUPS):
        yg = y[:, g * SSM_GROUP_WIDTH:(g + 1) * SSM_GROUP_WIDTH]
        parts.append(yg * lax.rsqrt(jnp.mean(yg * yg, axis=-1, keepdims=True) + GATE_NORM_EPS))
    return jnp.concatenate(parts, axis=-1) * gn


def _cumsum_rows(x):
    rows = x.shape[0]
    tril = (lax.broadcasted_iota(jnp.int32, (rows, rows), 0)
            >= lax.broadcasted_iota(jnp.int32, (rows, rows), 1)).astype(BF16)
    h1 = x.astype(BF16)
    r1 = x - h1.astype(F32)
    h2 = r1.astype(BF16)
    h3 = (r1 - h2.astype(F32)).astype(BF16)
    return _dot(tril, h1) + _dot(tril, h2) + _dot(tril, h3)


def _ssd_prompt_kernel(zx_ref, dt_ref, cw_ref, cb_ref, dtb_ref, a_ref, dskip_ref, gn_ref, e_ref,
                       y_ref, nconv_ref, nssm_ref, carry_ref, st_ref):
    c = pl.program_id(1)
    last = pl.num_programs(1) - 1

    @pl.when(c == 0)
    def _():
        carry_ref[...] = jnp.zeros(carry_ref.shape, BF16)
        st_ref[...] = jnp.zeros(st_ref.shape, F32)

    for u in range(zx_ref.shape[0] // SSM_CHUNK):
        rows = pl.ds(u * SSM_CHUNK, SSM_CHUNK)
        _ssd_chunk(zx_ref.at[rows], dt_ref.at[rows], cw_ref, cb_ref, dtb_ref, a_ref, dskip_ref, gn_ref, e_ref,
                   y_ref.at[rows], carry_ref, st_ref)

    @pl.when(c == last)
    def _():
        n_carry = carry_ref.shape[0]
        nconv_ref[...] = carry_ref[...].astype(F32)[n_carry - (SSM_CONV - 1):n_carry]
        nssm_ref[...] = st_ref[...].T


def _ssd_chunk(zx_ref, dt_ref, cw_ref, cb_ref, dtb_ref, a_ref, dskip_ref, gn_ref, e_ref, y_ref, carry_ref, st_ref):
    lc = zx_ref.shape[0]
    n_carry = carry_ref.shape[0]
    src = lax.broadcasted_iota(jnp.int32, (lc, n_carry + lc), 1) - n_carry
    dst = lax.broadcasted_iota(jnp.int32, (lc, n_carry + lc), 0)
    taps = SSM_CONV - 1
    shift = jnp.concatenate([(src == dst - (taps - j)).astype(BF16) for j in range(taps)], axis=0)

    def conv_silu(start, width):
        cols = slice(start, start + width)
        raw = zx_ref[:, SSM_D_INNER + start:SSM_D_INNER + start + width]
        shifted = _dot(shift, jnp.concatenate([carry_ref[:, cols], raw], axis=0))
        conv = cb_ref[:, cols] + shifted[0:lc] * cw_ref[0:1, cols]
        for j in range(1, taps):
            conv = conv + shifted[j * lc:(j + 1) * lc] * cw_ref[j:j + 1, cols]
        conv = conv + raw.astype(F32) * cw_ref[taps:taps + 1, cols]
        return conv * _sigmoid(conv)

    dt = _softplus(dt_ref[...] + dtb_ref[...])
    a_cs = _cumsum_rows(dt * a_ref[...])
    a_t = a_cs.T
    dt_t = dt.T
    a_last = a_cs[lc - 1:lc, :]
    to_end = (dt * jnp.exp(a_last - a_cs)).astype(BF16)
    from_start = jnp.exp(a_cs).astype(BF16)
    chunk_decay = jnp.broadcast_to(jnp.exp(a_last), (8, LANES))
    causal = (lax.broadcasted_iota(jnp.int32, (lc, lc), 0) >= lax.broadcasted_iota(jnp.int32, (lc, lc), 1))
    heads_per_group = SSM_HEADS // SSM_GROUPS
    head_w = SSM_D_INNER // SSM_HEADS
    quad = 4
    quad_w = quad * head_w
    lane_head = lax.broadcasted_iota(jnp.int32, (1, quad_w), 1) // head_w
    n_bc = SSM_GROUPS * SSM_STATE
    for g in range(SSM_GROUPS):
        gw = slice(g * SSM_GROUP_WIDTH, (g + 1) * SSM_GROUP_WIDTH)
        xs = conv_silu(g * SSM_GROUP_WIDTH, SSM_GROUP_WIDTH)
        bm = conv_silu(SSM_D_INNER + g * SSM_STATE, SSM_STATE).astype(BF16)
        cm = conv_silu(SSM_D_INNER + n_bc + g * SSM_STATE, SSM_STATE).astype(BF16)
        e = e_ref[:, gw]
        xs_b = xs.astype(BF16)
        xdte = (xs * _dot(to_end, e)).astype(BF16)
        s_prev = st_ref[:, gw]
        cb = _dot_nt(cm, bm)
        y = _dot(cm, s_prev.astype(BF16)) * _dot(from_start, e)
        st_ref[:, gw] = (s_prev * _dot2(chunk_decay, e)[0:1, :]
                         + lax.dot_general(bm, xdte, TN, preferred_element_type=F32))
        quad_out = []
        for qd in range(heads_per_group // quad):
            h0 = g * heads_per_group + quad * qd
            ms = []
            for h in range(h0, h0 + quad):
                seg = a_cs[:, h:h + 1] - a_t[h:h + 1, :]
                ms.append((cb * jnp.where(causal, jnp.exp(seg), 0.0) * dt_t[h:h + 1, :]).astype(BF16))
            xq = xs_b[:, qd * quad_w:(qd + 1) * quad_w]
            rhs = jnp.concatenate([jnp.where(lane_head == i, xq, jnp.zeros_like(xq)) for i in range(quad)], axis=0)
            quad_out.append(_dot(jnp.concatenate(ms, axis=1), rhs))
        y = y + jnp.concatenate(quad_out, axis=-1) + dskip_ref[:, gw] * xs
        z = zx_ref[:, gw].astype(F32)
        y = y * (z * _sigmoid(z))
        y = y * lax.rsqrt(jnp.mean(y * y, axis=-1, keepdims=True) + GATE_NORM_EPS) * gn_ref[:, gw]
        y_ref[:, gw] = y.astype(y_ref.dtype)

    carry_ref[...] = zx_ref[lc - n_carry:lc, SSM_D_INNER:SSM_MAIN]


def _ssd_prompt(zx, dt, b, params, *, chunks_per_step):
    m = zx.shape[0]
    rows = chunks_per_step * SSM_CHUNK
    nc = m // b // rows
    full = lambda x: pl.BlockSpec(x.shape, lambda bb, c: (0, 0))
    row = lambda w: pl.BlockSpec((rows, w), lambda bb, c: (bb * nc + c, 0))
    return pl.pallas_call(
        _ssd_prompt_kernel,
        grid=(b, nc),
        in_specs=[row(SSM_MAIN), row(LANES)] + [full(x) for x in params],
        out_specs=[row(SSM_D_INNER),
                   pl.BlockSpec((None, SSM_CONV - 1, SSM_CONV_DIM), lambda bb, c: (bb, 0, 0)),
                   pl.BlockSpec((None, SSM_D_INNER, SSM_STATE), lambda bb, c: (bb, 0, 0))],
        out_shape=[jax.ShapeDtypeStruct((m, SSM_D_INNER), BF16),
                   jax.ShapeDtypeStruct((b, SSM_CONV - 1, SSM_CONV_DIM), F32),
                   jax.ShapeDtypeStruct((b, SSM_D_INNER, SSM_STATE), F32)],
        scratch_shapes=[pltpu.VMEM((16, SSM_CONV_DIM), BF16),
                        pltpu.VMEM((SSM_STATE, SSM_D_INNER), F32)],
        compiler_params=_params(("parallel", "arbitrary")),
        name="ssd_prompt",
    )(zx, dt, *params)


def _ssd_sample_kernel(zx_ref, dt_ref, conv0_ref, s0_ref, cw_ref, cb_ref, dtb_ref, a_ref, dskip_ref, gn_ref,
                       e_ref, eg_ref, y_ref, nconv_ref, nssm_ref, xpad_ref):
    lc = zx_ref.shape[0]
    xpad_ref[0:5, :] = jnp.zeros((5, SSM_CONV_DIM), F32)
    xpad_ref[5:8, :] = conv0_ref[...]
    xbc, tail = _conv_silu(zx_ref, xpad_ref, cw_ref, cb_ref, lc)
    nconv_ref[...] = tail

    xs = xbc[:, 0:SSM_D_INNER]
    bm = xbc[:, SSM_D_INNER:SSM_D_INNER + SSM_GROUPS * SSM_STATE]
    cm = xbc[:, SSM_D_INNER + SSM_GROUPS * SSM_STATE:]
    e = e_ref[...]
    dt = _softplus(dt_ref[...] + dtb_ref[...])
    dta = dt * a_ref[...]
    rows = [dta[0:1, :]]
    for i in range(1, lc):
        rows.append(rows[-1] + dta[i:i + 1, :])
    a_cs = jnp.concatenate(rows, axis=0)
    a_last = rows[-1]
    xdt = xs * _dot2(dt, e)
    xdte = (xdt * _dot2(jnp.exp(a_last - a_cs), e)).astype(BF16)
    ea_e = _dot2(jnp.exp(a_cs), e)
    cd_e = _dot2(jnp.broadcast_to(jnp.exp(a_last), (8, LANES)), e)[0:1, :]

    a_l = jnp.concatenate([a_cs] * lc, axis=0)
    a_s = jnp.concatenate([jnp.broadcast_to(a_cs[s:s + 1, :], (lc, LANES)) for s in range(lc)], axis=0)
    l_idx = lax.broadcasted_iota(jnp.int32, (lc * lc, 1), 0) % lc
    s_idx = lax.broadcasted_iota(jnp.int32, (lc * lc, 1), 0) // lc
    decay = jnp.where(l_idx >= s_idx, jnp.exp(a_l - a_s), 0.0)
    cb_prod = jnp.concatenate([cm * bm[s:s + 1, :] for s in range(lc)], axis=0)
    mix = _dot2(_dot2(cb_prod, eg_ref[...]) * decay, e)
    y = ea_e * 0.0
    for s in range(lc):
        y = y + mix[s * lc:(s + 1) * lc, :] * xdt[s:s + 1, :]

    s_prev = s0_ref[...].T
    s_b = s_prev.astype(BF16)
    bm_b = bm.astype(BF16)
    cm_b = cm.astype(BF16)
    y_off, s_parts = [], []
    for g in range(SSM_GROUPS):
        gs = slice(g * SSM_STATE, (g + 1) * SSM_STATE)
        gw = slice(g * SSM_GROUP_WIDTH, (g + 1) * SSM_GROUP_WIDTH)
        y_off.append(_dot(cm_b[:, gs], s_b[:, gw]))
        s_parts.append(lax.dot_general(bm_b[:, gs], xdte[:, gw], TN, preferred_element_type=F32))
    y = y + jnp.concatenate(y_off, axis=-1) * ea_e + dskip_ref[...] * xs
    s_new = s_prev * cd_e + jnp.concatenate(s_parts, axis=-1)
    nssm_ref[...] = s_new.T
    z = zx_ref[:, 0:SSM_D_INNER]
    y_ref[...] = _gate_norm(y, z, gn_ref[...])


def _ssd_sample(zx, dt, conv0, s0, params):
    b = conv0.shape[0]
    t_new = zx.shape[0] // b
    full = lambda x: pl.BlockSpec(x.shape, lambda bb: (0, 0))
    row = lambda w: pl.BlockSpec((t_new, w), lambda bb: (bb, 0))
    conv_spec = pl.BlockSpec((None, SSM_CONV - 1, SSM_CONV_DIM), lambda bb: (bb, 0, 0))
    state_spec = pl.BlockSpec((None, SSM_D_INNER, SSM_STATE), lambda bb: (bb, 0, 0))
    return pl.pallas_call(
        _ssd_sample_kernel,
        grid=(b,),
        in_specs=[row(SSM_MAIN), row(LANES), conv_spec, state_spec] + [full(x) for x in params],
        out_specs=[row(SSM_D_INNER), conv_spec, state_spec],
        out_shape=[jax.ShapeDtypeStruct((b * t_new, SSM_D_INNER), F32),
                   jax.ShapeDtypeStruct((b, SSM_CONV - 1, SSM_CONV_DIM), F32),
                   jax.ShapeDtypeStruct((b, SSM_D_INNER, SSM_STATE), F32)],
        scratch_shapes=[pltpu.VMEM((8 + t_new, SSM_CONV_DIM), F32)],
        compiler_params=_params(("parallel",)),
        name="ssd_sample",
    )(zx, dt, conv0, s0, *params)


def _one_hot_expand(n_rows, n_cols, group):
    r = lax.broadcasted_iota(jnp.int32, (n_rows, n_cols), 0)
    c = lax.broadcasted_iota(jnp.int32, (n_rows, n_cols), 1)
    return (c // group == r).astype(BF16)


def _kv_rows(feature_major, b):
    tokens = feature_major.shape[-1]
    x = feature_major.reshape(1, b, 2, N_HEADS, HEAD_DIM, tokens)
    return jnp.transpose(x, (0, 1, 5, 2, 3, 4))


def kernel(x_prompt, x_sample, cache_kv_g0, cache_kv_g1, cache_kv_g2, state_conv, state_ssm, attn_norm, attn_w_in, attn_q_gain, attn_k_gain, attn_w_out, ssm_norm, ssm_w_in, ssm_conv_w, ssm_conv_b, ssm_dt_bias, ssm_A_log, ssm_D, ssm_gate_norm, ssm_w_out):
    b, seq, dm = x_prompt.shape
    sb, st, _ = x_sample.shape
    xp = x_prompt.reshape(b * seq, dm)
    xs = x_sample.reshape(sb * st, dm)

    w_in = attn_w_in[0].astype(BF16)
    w_out = attn_w_out[0].astype(BF16)
    scale = HEAD_DIM ** -0.5
    q_gain3 = jnp.tile(attn_q_gain[0], (1, N_HEADS)) * scale
    k_gain3 = jnp.tile(attn_k_gain[0], (1, N_HEADS))

    log2e = 1.4426950408889634
    tile_gain = jnp.ones((ATT_IN // ATT_WIDTH, 1, ATT_WIDTH), F32)
    for g in range(len(DILATIONS)):
        tile_gain = tile_gain.at[3 * g, 0].set(q_gain3[g] * log2e).at[3 * g + 1, 0].set(k_gain3[g])
    qkvg = _proj(xp, attn_norm[0], w_in, tm=1024, tn=2 * ATT_WIDTH, out_dtype=BF16, perm_seq=seq,
                 qk_gain=tile_gain)
    part2 = _attn_group(qkvg, 2, n_sub=2)
    part1 = _attn_group(qkvg, 1, n_sub=4)
    o_gated = _attn_group(qkvg, 0, n_sub=4, others=[part1, part2])
    y1p = _outproj(o_gated, w_out, xp, tm=1024, perm_seq=seq)
    kv_p = []
    for g, d in enumerate(DILATIONS):
        keep = min(d * WINDOW_STEPS, seq)
        kv_p.append(_kv_rows(_kv_tail(xp, attn_norm[0], w_in, g, k_gain3[g:g + 1], b=b, keep=keep,
                                      tm=min(keep, 512)), b))

    proj_s = _proj(xs, attn_norm[0], w_in, tm=sb * st, tn=1024, out_dtype=F32)
    caches_t = [jnp.transpose(c[0], (0, 2, 3, 4, 1)).reshape(sb, 2, ATT_WIDTH, c.shape[2])
                for c in (cache_kv_g0, cache_kv_g1, cache_kv_g2)]
    o_s, kn0, kn1, kn2 = _sample_attn(proj_s, caches_t, q_gain3[:, :LANES], k_gain3[:, :LANES])
    y1s = _outproj(o_s, w_out, xs, tm=sb * st)
    kv_s = []
    for g, kn in enumerate((kn0, kn1, kn2)):
        v = proj_s[:, (3 * g + 2) * ATT_WIDTH:(3 * g + 3) * ATT_WIDTH]
        kv_s.append(jnp.stack([kn, v], axis=1).reshape(1, sb, st, 2, N_HEADS, HEAD_DIM))

    w_in2 = ssm_w_in[0].astype(BF16)
    w_dt = jnp.pad(w_in2[:, SSM_MAIN:], ((0, 0), (0, LANES - SSM_HEADS)))
    w_out2 = ssm_w_out[0].astype(BF16)
    pad_h = lambda v: jnp.pad(v.astype(F32), (0, LANES - SSM_HEADS)).reshape(1, LANES)
    e32 = _one_hot_expand(LANES, SSM_D_INNER, SSM_D_INNER // SSM_HEADS)
    ssm_params = (ssm_conv_w[0], ssm_conv_b[0].reshape(1, -1), pad_h(ssm_dt_bias[0]),
                  pad_h(-jnp.exp(ssm_A_log[0].astype(F32))),
                  jnp.repeat(ssm_D[0].astype(F32), SSM_D_INNER // SSM_HEADS).reshape(1, -1),
                  ssm_gate_norm[0].reshape(1, -1), e32)

    zx_p, dt_p = _proj(y1p, ssm_norm[0], w_in2, tm=1024, tn=SSM_MAIN // 2, n_out=SSM_MAIN, out_dtype=BF16,
                       w_dt=w_dt)
    yg_p, conv_p, ssm_p = _ssd_prompt(zx_p, dt_p, b, ssm_params, chunks_per_step=2)
    y2p = _outproj(yg_p, w_out2, y1p, tm=1024)

    eg = (lax.broadcasted_iota(jnp.int32, (SSM_GROUPS * SSM_STATE, LANES), 0) // SSM_STATE
          == lax.broadcasted_iota(jnp.int32, (SSM_GROUPS * SSM_STATE, LANES), 1) // (SSM_HEADS // SSM_GROUPS))
    eg = (eg & (lax.broadcasted_iota(jnp.int32, eg.shape, 1) < SSM_HEADS)).astype(BF16)
    zx_s, dt_s = _proj(y1s, ssm_norm[0], w_in2, tm=sb * st, tn=SSM_MAIN // 2, n_out=SSM_MAIN, out_dtype=F32,
                       w_dt=w_dt)
    yg_s, conv_s, ssm_s = _ssd_sample(zx_s, dt_s, state_conv[0], state_ssm[0].reshape(sb, SSM_D_INNER, SSM_STATE),
                                      ssm_params + (eg,))
    y2s = _outproj(yg_s, w_out2, y1s, tm=sb * st)

    hp = SSM_D_INNER // SSM_HEADS
    return (y2p.reshape(b, seq, dm), y2s.reshape(sb, st, dm),
            kv_p[0], kv_p[1], kv_p[2], kv_s[0], kv_s[1], kv_s[2],
            conv_p[None], conv_s[None],
            ssm_p.reshape(1, b, SSM_HEADS, hp, SSM_STATE), ssm_s.reshape(1, sb, SSM_HEADS, hp, SSM_STATE))
```

```python
import functools

import jax
import jax.numpy as jnp
from jax import lax
from jax.experimental import pallas as pl
from jax.experimental.pallas import tpu as pltpu

F32 = jnp.float32
BF16 = jnp.bfloat16

D_MODEL = 1024
N_HEADS = 16
HEAD_DIM = 64
ATT_WIDTH = N_HEADS * HEAD_DIM
DILATIONS = (1, 4, 16)
WINDOW_STEPS = 128
ATT_IN = 10 * ATT_WIDTH
PERM = 16

SSM_D_INNER = 2048
SSM_HEADS = 32
SSM_STATE = 128
SSM_GROUPS = 4
SSM_GROUP_WIDTH = SSM_D_INNER // SSM_GROUPS
SSM_CONV = 4
SSM_CONV_DIM = SSM_D_INNER + 2 * SSM_GROUPS * SSM_STATE
SSM_MAIN = SSM_D_INNER + SSM_CONV_DIM
SSM_CHUNK = 128

NORM_EPS = 1e-6
GATE_NORM_EPS = 1e-5
MASKED = -1e30

LANES = 128
VMEM_LIMIT = 56 * 1024 * 1024

NT = (((1,), (1,)), ((), ()))
TN = (((0,), (0,)), ((), ()))


def _params(semantics):
    return pltpu.CompilerParams(dimension_semantics=semantics, vmem_limit_bytes=VMEM_LIMIT)


def _rms(x, w, eps):
    return x * lax.rsqrt(jnp.mean(x * x, axis=-1, keepdims=True) + eps) * w


def _sigmoid(x):
    return 1.0 / (1.0 + jnp.exp(-x))


def _split2(v):
    hi = v.astype(BF16)
    lo = (v - hi.astype(F32)).astype(BF16)
    return hi, lo


def _dot(a, b):
    return jnp.dot(a, b, preferred_element_type=F32)


def _dot_nt(a, b):
    return lax.dot_general(a, b, NT, preferred_element_type=F32)


def _dot2(v, e):
    hi, lo = _split2(v)
    return _dot(hi, e) + _dot(lo, e)


def _low_half():
    return lax.broadcasted_iota(jnp.int32, (1, LANES), 1) < HEAD_DIM


def _head_norm(x, eps):
    lo = _low_half()
    parts = []
    for j in range(x.shape[1] // LANES):
        t = x[:, j * LANES:(j + 1) * LANES]
        t2 = t * t
        s_lo = jnp.sum(jnp.where(lo, t2, 0.0), axis=-1, keepdims=True)
        s_hi = jnp.sum(jnp.where(lo, 0.0, t2), axis=-1, keepdims=True)
        r = jnp.where(lo, lax.rsqrt(s_lo * (1.0 / HEAD_DIM) + eps), lax.rsqrt(s_hi * (1.0 / HEAD_DIM) + eps))
        parts.append(t * r)
    return jnp.concatenate(parts, axis=-1)


def _split_heads(qp):
    lo = _low_half()
    zero = jnp.zeros_like(qp)
    return jnp.concatenate([jnp.where(lo, qp, zero), jnp.where(lo, zero, qp)], axis=0)


def _join_heads(x2):
    rows = x2.shape[0] // 2
    return jnp.where(_low_half(), x2[0:rows], x2[rows:])


def _proj_kernel(x_ref, nw_ref, w_ref, *rest, with_dt):
    rest = list(rest)
    wdt_ref = rest.pop(0) if with_dt else None
    o_ref = rest.pop(0)
    dt_ref = rest.pop(0) if with_dt else None
    h_ref = rest.pop(0)

    @pl.when(pl.program_id(1) == 0)
    def _():
        h_ref[...] = _rms(x_ref[...], nw_ref[...], NORM_EPS).astype(BF16)
        if with_dt:
            dt_ref[...] = _dot(h_ref[...], wdt_ref[...])

    o_ref[...] = _dot(h_ref[...], w_ref[...]).astype(o_ref.dtype)


def _proj(x, norm_w, w, *, tm, tn, out_dtype, n_out=None, w_dt=None):
    m, dm = x.shape
    n = w.shape[1] if n_out is None else n_out
    assert n % tn == 0 and m % tm == 0
    in_specs = [pl.BlockSpec((tm, dm), lambda i, j: (i, 0)),
                pl.BlockSpec((1, dm), lambda i, j: (0, 0)),
                pl.BlockSpec((dm, tn), lambda i, j: (0, j))]
    args = [x, norm_w.reshape(1, dm), w]
    out_shape = [jax.ShapeDtypeStruct((m, n), out_dtype)]
    out_specs = [pl.BlockSpec((tm, tn), lambda i, j: (i, j))]
    if w_dt is not None:
        in_specs.append(pl.BlockSpec((dm, LANES), lambda i, j: (0, 0)))
        args.append(w_dt)
        out_shape.append(jax.ShapeDtypeStruct((m, LANES), F32))
        out_specs.append(pl.BlockSpec((tm, LANES), lambda i, j: (i, 0)))
    outs = pl.pallas_call(
        functools.partial(_proj_kernel, with_dt=w_dt is not None),
        grid=(m // tm, n // tn), in_specs=in_specs, out_specs=out_specs, out_shape=out_shape,
        scratch_shapes=[pltpu.VMEM((tm, dm), BF16)],
        compiler_params=_params(("parallel", "arbitrary")),
        name="norm_proj",
    )(*args)
    return outs if w_dt is not None else outs[0]


N_TILES = ATT_IN // ATT_WIDTH
Q_TILES, K_TILES, V_TILES, GATE_TILE = (0, 3, 6), (1, 4, 7), (2, 5, 8), 9
STREAM_A = (0, 1, 3, 4, 6)
STREAM_B = (2, 5, 8, 9, 7)


def _lookup(j, table):
    out = jnp.int32(table[0])
    for t, v in enumerate(table[1:], 1):
        out = jnp.where(j == t, jnp.int32(v), out)
    return out


def _attn_proj_kernel(x_ref, nw_ref, wa_ref, wb_ref, ga_ref, gb_ref, oa_ref, ob_ref, h_ref, slab_ref):
    j = pl.program_id(1)

    @pl.when(j == 0)
    def _():
        xn = _rms(x_ref[...], nw_ref[...], NORM_EPS)
        rows = x_ref.shape[0] // PERM
        for c in range(x_ref.shape[1] // LANES):
            cols = slice(c * LANES, (c + 1) * LANES)
            slab_ref[...] = xn[:, cols]
            for r in range(PERM):
                h_ref[r * rows:(r + 1) * rows, cols] = slab_ref[pl.ds(r, rows, stride=PERM), :].astype(BF16)

    def tile(w_ref, gain_ref, o_ref, normed):
        res = _dot(h_ref[...], w_ref[...])
        if normed:
            res = _head_norm(res, NORM_EPS) * gain_ref[0]
        o_ref[...] = res.astype(o_ref.dtype).reshape(o_ref.shape)

    last = pl.num_programs(1) - 1

    @pl.when(j < last)
    def _():
        tile(wa_ref, ga_ref, oa_ref, True)
        tile(wb_ref, gb_ref, ob_ref, False)

    @pl.when(j == last)
    def _():
        tile(wa_ref, ga_ref, oa_ref, True)
        tile(wb_ref, gb_ref, ob_ref, True)


def _attn_proj(x, norm_w, w, tile_gain, *, tm, seq):
    m, dm = x.shape
    per_b = seq // tm
    rows = tm // PERM
    n_steps = len(STREAM_A)
    w_spec = lambda table: pl.BlockSpec((dm, ATT_WIDTH), lambda i, j: (0, _lookup(j, table)))
    g_spec = lambda table: pl.BlockSpec((1, 1, ATT_WIDTH), lambda i, j: (_lookup(j, table), 0, 0))
    out_spec = pl.BlockSpec((None, PERM, rows, ATT_WIDTH), lambda i, j: (i // per_b, 0, i % per_b, j))
    out_sds = jax.ShapeDtypeStruct((m // seq, PERM, seq // PERM, n_steps * ATT_WIDTH), BF16)
    return pl.pallas_call(
        _attn_proj_kernel,
        grid=(m // tm, n_steps),
        in_specs=[pl.BlockSpec((tm, dm), lambda i, j: (i, 0)), pl.BlockSpec((1, dm), lambda i, j: (0, 0)),
                  w_spec(STREAM_A), w_spec(STREAM_B), g_spec(STREAM_A), g_spec(STREAM_B)],
        out_specs=[out_spec, out_spec], out_shape=[out_sds, out_sds],
        scratch_shapes=[pltpu.VMEM((tm, dm), BF16), pltpu.VMEM((tm, LANES), F32)],
        compiler_params=_params(("parallel", "arbitrary")),
        name="attn_proj",
    )(x, norm_w.reshape(1, dm), w, w, tile_gain, tile_gain)


def _outproj_kernel(a_ref, w_ref, x_ref, o_ref, *scratch, unperm):
    tm = o_ref.shape[0]
    a = a_ref[...].reshape(tm, a_ref.shape[-1]).astype(BF16)
    res = _dot(a, w_ref[...])
    if unperm:
        slab_ref, = scratch
        rows = tm // PERM
        for c in range(o_ref.shape[1] // LANES):
            cols = slice(c * LANES, (c + 1) * LANES)
            for r in range(PERM):
                slab_ref[pl.ds(r, rows, stride=PERM), :] = res[r * rows:(r + 1) * rows, cols]
            o_ref[:, cols] = x_ref[:, cols] + slab_ref[...]
    else:
        o_ref[...] = x_ref[...] + res


def _outproj(a, w, x, *, tm, perm_seq=None):
    m, n = x.shape
    k = w.shape[0]
    if perm_seq is not None:
        per_b = perm_seq // tm
        a_spec = pl.BlockSpec((None, PERM, tm // PERM, k), lambda i: (i // per_b, 0, i % per_b, 0))
    else:
        a_spec = pl.BlockSpec((tm, k), lambda i: (i, 0))
    return pl.pallas_call(
        functools.partial(_outproj_kernel, unperm=perm_seq is not None),
        grid=(m // tm,),
        in_specs=[a_spec, pl.BlockSpec((k, n), lambda i: (0, 0)), pl.BlockSpec((tm, n), lambda i: (i, 0))],
        out_specs=pl.BlockSpec((tm, n), lambda i: (i, 0)),
        out_shape=jax.ShapeDtypeStruct((m, n), F32),
        scratch_shapes=[pltpu.VMEM((tm, LANES), F32)] if perm_seq else [],
        compiler_params=_params(("parallel",)),
        name="out_proj",
    )(a, w, x)


ATT_SUB = 128


def _attn_group_kernel(*refs, n_chunks, n_sub, merge):
    it = iter(refs)
    q_ref, k_ref, v_ref = next(it), next(it), next(it)
    if merge:
        gate_ref = next(it)
        others = [(next(it), next(it)) for _ in range(len(DILATIONS) - 1)]
        o_out = next(it)
    else:
        o_out, lse_out = next(it), next(it)
    ks_ref, vs_ref = next(it), next(it)

    rc = q_ref.shape[1] // n_sub
    assert n_chunks * rc == ATT_SUB
    n = pl.program_id(2)
    narrow = rc % 16 != 0

    def piece(ref, u, cols=slice(None)):
        if narrow and ref.dtype == BF16:
            pair = 2 * (u // 2) * rc
            x = ref[:, pair:pair + 2 * rc, cols].astype(F32)[:, (u % 2) * rc:(u % 2 + 1) * rc, :]
        else:
            x = ref[:, u * rc:(u + 1) * rc, cols]
        return x.reshape(ATT_SUB, x.shape[-1])

    def unsubs(parts, dtype):
        parts = [p.reshape(n_chunks, rc, p.shape[-1]) for p in parts]
        if narrow:
            return jnp.concatenate(parts, axis=1).astype(dtype)
        return jnp.concatenate([p.astype(dtype) for p in parts], axis=1)

    @pl.when(n == 0)
    def _():
        ks_ref[0:ATT_SUB, :] = jnp.zeros((ATT_SUB, ATT_WIDTH), BF16)
        vs_ref[0:ATT_SUB, :] = jnp.zeros((ATT_SUB, ATT_WIDTH), BF16)

    for u in range(n_sub):
        ks_ref[(u + 1) * ATT_SUB:(u + 2) * ATT_SUB, :] = piece(k_ref, u).astype(BF16)
        vs_ref[(u + 1) * ATT_SUB:(u + 2) * ATT_SUB, :] = piece(v_ref, u).astype(BF16)

    qi = lax.broadcasted_iota(jnp.int32, (ATT_SUB, 2 * ATT_SUB), 0)
    kj = lax.broadcasted_iota(jnp.int32, (ATT_SUB, 2 * ATT_SUB), 1)
    kc = jnp.where(kj >= ATT_SUB, kj - ATT_SUB, kj)
    tq = n_chunks * (qi % rc) + qi // rc
    tk = n_chunks * (kc % rc) + kc // rc + jnp.where(kj >= ATT_SUB, 0, -ATT_SUB)
    dist = tq - tk
    in_band = (dist >= 0) & (dist <= WINDOW_STEPS)
    band = jnp.where(in_band, 0.0, MASKED)
    band_first = jnp.where(in_band & ((kj >= ATT_SUB) | (n > 0)), 0.0, MASKED)
    bias = [jnp.concatenate([x, x], axis=0) for x in (band_first, band)]

    for j in range(N_HEADS // 2):
        cols = slice(j * LANES, (j + 1) * LANES)
        o_parts, lse_parts = [], []
        for u in range(n_sub):
            keys = slice(u * ATT_SUB, (u + 2) * ATT_SUB)
            q2 = _split_heads(piece(q_ref, u, cols).astype(BF16))
            s = _dot_nt(q2, ks_ref[keys, cols]) + bias[min(u, 1)]
            m = jnp.max(s, axis=-1, keepdims=True)
            p = jnp.exp2(s - m)
            l = jnp.sum(p, axis=-1, keepdims=True)
            o = _join_heads(_dot(p.astype(BF16), vs_ref[keys, cols]) / l)
            lse = _join_heads(jnp.broadcast_to(m + jnp.log2(l), (2 * ATT_SUB, LANES)))
            if merge:
                o_g = [o] + [piece(o_ref, u, cols).astype(F32) for o_ref, _ in others]
                lse_g = [lse] + [piece(l_ref, u, cols) for _, l_ref in others]
                top = jnp.maximum(jnp.maximum(lse_g[0], lse_g[1]), lse_g[2])
                w = [jnp.exp2(x - top) for x in lse_g]
                o = (w[0] * o_g[0] + w[1] * o_g[1] + w[2] * o_g[2]) / (w[0] + w[1] + w[2])
                g = piece(gate_ref, u, cols).astype(F32)
                o = o * (g * _sigmoid(g))
            o_parts.append(o)
            lse_parts.append(lse)
        o_out[:, :, cols] = unsubs(o_parts, BF16)
        if not merge:
            lse_out[:, :, cols] = unsubs(lse_parts, F32)
    ks_ref[0:ATT_SUB, :] = ks_ref[n_sub * ATT_SUB:(n_sub + 1) * ATT_SUB, :]
    vs_ref[0:ATT_SUB, :] = vs_ref[n_sub * ATT_SUB:(n_sub + 1) * ATT_SUB, :]


def _attn_group(qkvg, g, *, n_sub, others=None):
    b, _, t, _ = qkvg[0].shape
    d = DILATIONS[g]
    n_chunks = PERM // d
    rc = ATT_SUB // n_chunks * n_sub
    nb = t // rc
    merge = others is not None
    view = lambda a: a.reshape(b, n_chunks, d, t, a.shape[-1])

    def spec(col=0):
        return pl.BlockSpec((None, n_chunks, None, rc, ATT_WIDTH), lambda bb, r, n: (bb, 0, r, n, col))

    in_specs, args = [], []

    def add_tile(tile):
        stream, table = (0, STREAM_A) if tile in STREAM_A else (1, STREAM_B)
        in_specs.append(spec(table.index(tile)))
        args.append(view(qkvg[stream]))

    for tile in (Q_TILES[g], K_TILES[g], V_TILES[g]):
        add_tile(tile)
    full = jax.ShapeDtypeStruct((b, n_chunks, d, t, ATT_WIDTH), BF16)
    if merge:
        add_tile(GATE_TILE)
        for o, lse in others:
            in_specs += [spec(), spec()]
            args += [view(o), view(lse)]
        out_shape, out_specs = [full], [spec()]
    else:
        out_shape = [full, jax.ShapeDtypeStruct(full.shape, F32)]
        out_specs = [spec(), spec()]
    outs = pl.pallas_call(
        functools.partial(_attn_group_kernel, n_chunks=n_chunks, n_sub=n_sub, merge=merge),
        grid=(b, d, nb), in_specs=in_specs, out_specs=out_specs, out_shape=out_shape,
        scratch_shapes=[pltpu.VMEM(((n_sub + 1) * ATT_SUB, ATT_WIDTH), BF16)] * 2,
        compiler_params=_params(("parallel", "parallel", "arbitrary")),
        name=f"attn_group{g}",
    )(*args)
    outs = [a.reshape(b, PERM, t, ATT_WIDTH) for a in outs]
    return outs[0] if merge else tuple(outs)


def _kv_tail_kernel(x_ref, nw_ref, wk_ref, wv_ref, kg_ref, o_ref):
    h = _rms(x_ref[...], nw_ref[...], NORM_EPS).astype(BF16)
    o_ref[0:ATT_WIDTH, :] = (_head_norm(_dot(h, wk_ref[...]), NORM_EPS) * kg_ref[...]).T
    o_ref[ATT_WIDTH:2 * ATT_WIDTH, :] = _dot(h, wv_ref[...]).T


def _kv_tail(x, norm_w, w, g, k_gain, *, b, keep, tm):
    seq = x.shape[0] // b
    first = (seq - keep) // tm
    w_spec = lambda col: pl.BlockSpec((D_MODEL, ATT_WIDTH), lambda bb, i: (0, col))
    return pl.pallas_call(
        _kv_tail_kernel,
        grid=(b, keep // tm),
        in_specs=[pl.BlockSpec((tm, D_MODEL), lambda bb, i: (bb * (seq // tm) + first + i, 0)),
                  pl.BlockSpec((1, D_MODEL), lambda bb, i: (0, 0)),
                  w_spec(3 * g + 1), w_spec(3 * g + 2),
                  pl.BlockSpec((1, ATT_WIDTH), lambda bb, i: (0, 0))],
        out_specs=pl.BlockSpec((None, 2 * ATT_WIDTH, tm), lambda bb, i: (bb, 0, i)),
        out_shape=jax.ShapeDtypeStruct((b, 2 * ATT_WIDTH, keep), F32),
        compiler_params=_params(("parallel", "parallel")),
        name="kv_tail",
    )(x, norm_w.reshape(1, D_MODEL), w, w, k_gain)


SAMPLE_HEADS = 8


def _sample_attn_kernel(*refs):
    n_g = len(DILATIONS)
    qkv = [refs[3 * g:3 * g + 3] for g in range(n_g)]
    gate_ref = refs[3 * n_g]
    c_refs = refs[3 * n_g + 1:4 * n_g + 1]
    qg_ref, kg_ref = refs[4 * n_g + 1:4 * n_g + 3]
    o_ref = refs[4 * n_g + 3]
    kn_refs = refs[4 * n_g + 4:]
    t_new = o_ref.shape[0]
    t_row = lax.broadcasted_iota(jnp.int32, (2 * t_new, 1), 0) % t_new
    t_col = lax.broadcasted_iota(jnp.int32, (1, t_new), 1)

    o_cols = []
    kn_cols = [[] for _ in DILATIONS]
    for j in range(SAMPLE_HEADS // 2):
        cols = slice(j * LANES, (j + 1) * LANES)
        o_g, m_g, l_g = [], [], []
        for g, d in enumerate(DILATIONS):
            q_ref, k_ref, v_ref = qkv[g]
            qm = _split_heads(_head_norm(q_ref[:, cols], NORM_EPS) * qg_ref[g:g + 1, :]).astype(BF16)
            kn = _head_norm(k_ref[:, cols], NORM_EPS) * kg_ref[g:g + 1, :]
            kn_cols[g].append(kn)
            length = c_refs[g].shape[2]
            s_c = _dot(qm, c_refs[g][0, cols, :].astype(BF16))
            s_n = _dot_nt(qm, kn.astype(BF16))
            back = length + t_row - lax.broadcasted_iota(jnp.int32, (1, length), 1)
            s_c = jnp.where((back % d == 0) & (back <= WINDOW_STEPS * d), s_c, MASKED)
            back = t_row - t_col
            s_n = jnp.where((back >= 0) & (back % d == 0) & (back <= WINDOW_STEPS * d), s_n, MASKED)
            m = jnp.maximum(jnp.max(s_c, axis=-1, keepdims=True), jnp.max(s_n, axis=-1, keepdims=True))
            p_c = jnp.exp(s_c - m)
            p_n = jnp.exp(s_n - m)
            l_g.append(jnp.sum(p_c, axis=-1, keepdims=True) + jnp.sum(p_n, axis=-1, keepdims=True))
            o_g.append(_dot_nt(p_c.astype(BF16), c_refs[g][1, cols, :].astype(BF16))
                       + _dot(p_n.astype(BF16), v_ref[:, cols].astype(BF16)))
            m_g.append(m)
        top = jnp.maximum(jnp.maximum(m_g[0], m_g[1]), m_g[2])
        f = [jnp.exp(m - top) for m in m_g]
        den = f[0] * l_g[0] + f[1] * l_g[1] + f[2] * l_g[2]
        o = _join_heads((f[0] * o_g[0] + f[1] * o_g[1] + f[2] * o_g[2]) / den)
        gate = gate_ref[:, cols]
        o_cols.append(o * (gate * _sigmoid(gate)))
    o_ref[...] = jnp.concatenate(o_cols, axis=-1)
    for g in range(n_g):
        kn_refs[g][...] = jnp.concatenate(kn_cols[g], axis=-1)


def _sample_attn(proj, caches_t, q_gain, k_gain):
    b = caches_t[0].shape[0]
    t_new = proj.shape[0] // b
    width = SAMPLE_HEADS * HEAD_DIM
    per_chunk = ATT_WIDTH // width
    col = lambda c: pl.BlockSpec((t_new, width), lambda bb, hc: (bb, c * per_chunk + hc))
    in_specs, args = [], []
    for g in range(len(DILATIONS)):
        in_specs += [col(3 * g), col(3 * g + 1), col(3 * g + 2)]
        args += [proj, proj, proj]
    in_specs.append(col(ATT_IN // ATT_WIDTH - 1))
    args.append(proj)
    for c in caches_t:
        in_specs.append(pl.BlockSpec((None, 2, width, c.shape[3]), lambda bb, hc: (bb, 0, hc, 0)))
        args.append(c)
    gain_spec = pl.BlockSpec((len(DILATIONS), LANES), lambda bb, hc: (0, 0))
    out_spec = pl.BlockSpec((t_new, width), lambda bb, hc: (bb, hc))
    out_sds = jax.ShapeDtypeStruct((b * t_new, ATT_WIDTH), F32)
    return pl.pallas_call(
        _sample_attn_kernel,
        grid=(b, per_chunk),
        in_specs=in_specs + [gain_spec, gain_spec],
        out_specs=[out_spec] * 4,
        out_shape=[out_sds] * 4,
        compiler_params=_params(("parallel", "parallel")),
        name="sample_attn",
    )(*args, q_gain, k_gain)


def _softplus(x):
    return jnp.maximum(x, 0.0) + jnp.log(1.0 + jnp.exp(-jnp.abs(x)))


def _conv_silu(zx_ref, xpad_ref, cw_ref, cb_ref, lc):
    pad = 8
    xpad_ref[pad:pad + lc, :] = zx_ref[:, SSM_D_INNER:SSM_MAIN].astype(F32)
    conv = cb_ref[...] + xpad_ref[pad - 3:pad - 3 + lc, :] * cw_ref[0:1, :]
    for j in range(1, SSM_CONV):
        conv = conv + xpad_ref[pad - 3 + j:pad - 3 + j + lc, :] * cw_ref[j:j + 1, :]
    tail = xpad_ref[pad + lc - 3:pad + lc, :]
    xpad_ref[pad - 3:pad, :] = tail
    return conv * _sigmoid(conv), tail


def _gate_norm(y, z, gn):
    y = y * (z * _sigmoid(z))
    parts = []
    for g in range(SSM_GROUPS):
        yg = y[:, g * SSM_GROUP_WIDTH:(g + 1) * SSM_GROUP_WIDTH]
        parts.append(yg * lax.rsqrt(jnp.mean(yg * yg, axis=-1, keepdims=True) + GATE_NORM_EPS))
    return jnp.concatenate(parts, axis=-1) * gn


def _cumsum_rows(x):
    rows = x.shape[0]
    tril = (lax.broadcasted_iota(jnp.int32, (rows, rows), 0)
            >= lax.broadcasted_iota(jnp.int32, (rows, rows), 1)).astype(BF16)
    h1 = x.astype(BF16)
    r1 = x - h1.astype(F32)
    h2 = r1.astype(BF16)
    h3 = (r1 - h2.astype(F32)).astype(BF16)
    return _dot(tril, h1) + _dot(tril, h2) + _dot(tril, h3)


def _ssd_prompt_kernel(zx_ref, dt_ref, cw_ref, cb_ref, dtb_ref, a_ref, dskip_ref, gn_ref, e_ref,
                       y_ref, nconv_ref, nssm_ref, carry_ref, st_ref):
    c = pl.program_id(1)
    last = pl.num_programs(1) - 1

    @pl.when(c == 0)
    def _():
        carry_ref[...] = jnp.zeros(carry_ref.shape, BF16)
        st_ref[...] = jnp.zeros(st_ref.shape, F32)

    for u in range(zx_ref.shape[0] // SSM_CHUNK):
        rows = pl.ds(u * SSM_CHUNK, SSM_CHUNK)
        _ssd_chunk(zx_ref.at[rows], dt_ref.at[rows], cw_ref, cb_ref, dtb_ref, a_ref, dskip_ref, gn_ref, e_ref,
                   y_ref.at[rows], carry_ref, st_ref)

    @pl.when(c == last)
    def _():
        n_carry = carry_ref.shape[0]
        nconv_ref[...] = carry_ref[...].astype(F32)[n_carry - (SSM_CONV - 1):n_carry]
        nssm_ref[...] = st_ref[...].T


def _ssd_chunk(zx_ref, dt_ref, cw_ref, cb_ref, dtb_ref, a_ref, dskip_ref, gn_ref, e_ref, y_ref, carry_ref, st_ref):
    lc = zx_ref.shape[0]
    n_carry = carry_ref.shape[0]
    src = lax.broadcasted_iota(jnp.int32, (lc, n_carry + lc), 1) - n_carry
    dst = lax.broadcasted_iota(jnp.int32, (lc, n_carry + lc), 0)
    taps = SSM_CONV - 1
    shift = jnp.concatenate([(src == dst - (taps - j)).astype(BF16) for j in range(taps)], axis=0)

    def conv_silu(start, width):
        cols = slice(start, start + width)
        raw = zx_ref[:, SSM_D_INNER + start:SSM_D_INNER + start + width]
        shifted = _dot(shift, jnp.concatenate([carry_ref[:, cols], raw], axis=0))
        conv = cb_ref[:, cols] + shifted[0:lc] * cw_ref[0:1, cols]
        for j in range(1, taps):
            conv = conv + shifted[j * lc:(j + 1) * lc] * cw_ref[j:j + 1, cols]
        conv = conv + raw.astype(F32) * cw_ref[taps:taps + 1, cols]
        return conv * _sigmoid(conv)

    dt = _softplus(dt_ref[...] + dtb_ref[...])
    a_cs = _cumsum_rows(dt * a_ref[...])
    a_t = a_cs.T
    dt_t = dt.T
    a_last = a_cs[lc - 1:lc, :]
    to_end = (dt * jnp.exp(a_last - a_cs)).astype(BF16)
    from_start = jnp.exp(a_cs).astype(BF16)
    chunk_decay = jnp.broadcast_to(jnp.exp(a_last), (8, LANES))
    causal = (lax.broadcasted_iota(jnp.int32, (lc, lc), 0) >= lax.broadcasted_iota(jnp.int32, (lc, lc), 1))
    heads_per_group = SSM_HEADS // SSM_GROUPS
    head_w = SSM_D_INNER // SSM_HEADS
    quad = 4
    quad_w = quad * head_w
    lane_head = lax.broadcasted_iota(jnp.int32, (1, quad_w), 1) // head_w
    n_bc = SSM_GROUPS * SSM_STATE
    for g in range(SSM_GROUPS):
        gw = slice(g * SSM_GROUP_WIDTH, (g + 1) * SSM_GROUP_WIDTH)
        xs = conv_silu(g * SSM_GROUP_WIDTH, SSM_GROUP_WIDTH)
        bm = conv_silu(SSM_D_INNER + g * SSM_STATE, SSM_STATE).astype(BF16)
        cm = conv_silu(SSM_D_INNER + n_bc + g * SSM_STATE, SSM_STATE).astype(BF16)
        e = e_ref[:, gw]
        xs_b = xs.astype(BF16)
        xdte = (xs * _dot(to_end, e)).astype(BF16)
        s_prev = st_ref[:, gw]
        cb = _dot_nt(cm, bm)
        y = _dot(cm, s_prev.astype(BF16)) * _dot(from_start, e)
        st_ref[:, gw] = (s_prev * _dot2(chunk_decay, e)[0:1, :]
                         + lax.dot_general(bm, xdte, TN, preferred_element_type=F32))
        quad_out = []
        for qd in range(heads_per_group // quad):
            h0 = g * heads_per_group + quad * qd
            ms = []
            for h in range(h0, h0 + quad):
                seg = a_cs[:, h:h + 1] - a_t[h:h + 1, :]
                ms.append((cb * jnp.where(causal, jnp.exp(seg), 0.0) * dt_t[h:h + 1, :]).astype(BF16))
            xq = xs_b[:, qd * quad_w:(qd + 1) * quad_w]
            rhs = jnp.concatenate([jnp.where(lane_head == i, xq, jnp.zeros_like(xq)) for i in range(quad)], axis=0)
            quad_out.append(_dot(jnp.concatenate(ms, axis=1), rhs))
        y = y + jnp.concatenate(quad_out, axis=-1) + dskip_ref[:, gw] * xs
        z = zx_ref[:, gw].astype(F32)
        y = y * (z * _sigmoid(z))
        y = y * lax.rsqrt(jnp.mean(y * y, axis=-1, keepdims=True) + GATE_NORM_EPS) * gn_ref[:, gw]
        y_ref[:, gw] = y.astype(y_ref.dtype)

    carry_ref[...] = zx_ref[lc - n_carry:lc, SSM_D_INNER:SSM_MAIN]


def _ssd_prompt(zx, dt, b, params, *, chunks_per_step):
    m = zx.shape[0]
    rows = chunks_per_step * SSM_CHUNK
    nc = m // b // rows
    full = lambda x: pl.BlockSpec(x.shape, lambda bb, c: (0, 0))
    row = lambda w: pl.BlockSpec((rows, w), lambda bb, c: (bb * nc + c, 0))
    return pl.pallas_call(
        _ssd_prompt_kernel,
        grid=(b, nc),
        in_specs=[row(SSM_MAIN), row(LANES)] + [full(x) for x in params],
        out_specs=[row(SSM_D_INNER),
                   pl.BlockSpec((None, SSM_CONV - 1, SSM_CONV_DIM), lambda bb, c: (bb, 0, 0)),
                   pl.BlockSpec((None, SSM_D_INNER, SSM_STATE), lambda bb, c: (bb, 0, 0))],
        out_shape=[jax.ShapeDtypeStruct((m, SSM_D_INNER), BF16),
                   jax.ShapeDtypeStruct((b, SSM_CONV - 1, SSM_CONV_DIM), F32),
                   jax.ShapeDtypeStruct((b, SSM_D_INNER, SSM_STATE), F32)],
        scratch_shapes=[pltpu.VMEM((16, SSM_CONV_DIM), BF16),
                        pltpu.VMEM((SSM_STATE, SSM_D_INNER), F32)],
        compiler_params=_params(("parallel", "arbitrary")),
        name="ssd_prompt",
    )(zx, dt, *params)


def _ssd_sample_kernel(zx_ref, dt_ref, conv0_ref, s0_ref, cw_ref, cb_ref, dtb_ref, a_ref, dskip_ref, gn_ref,
                       e_ref, eg_ref, y_ref, nconv_ref, nssm_ref, xpad_ref):
    lc = zx_ref.shape[0]
    xpad_ref[0:5, :] = jnp.zeros((5, SSM_CONV_DIM), F32)
    xpad_ref[5:8, :] = conv0_ref[...]
    xbc, tail = _conv_silu(zx_ref, xpad_ref, cw_ref, cb_ref, lc)
    nconv_ref[...] = tail

    xs = xbc[:, 0:SSM_D_INNER]
    bm = xbc[:, SSM_D_INNER:SSM_D_INNER + SSM_GROUPS * SSM_STATE]
    cm = xbc[:, SSM_D_INNER + SSM_GROUPS * SSM_STATE:]
    e = e_ref[...]
    dt = _softplus(dt_ref[...] + dtb_ref[...])
    dta = dt * a_ref[...]
    rows = [dta[0:1, :]]
    for i in range(1, lc):
        rows.append(rows[-1] + dta[i:i + 1, :])
    a_cs = jnp.concatenate(rows, axis=0)
    a_last = rows[-1]
    xdt = xs * _dot2(dt, e)
    xdte = (xdt * _dot2(jnp.exp(a_last - a_cs), e)).astype(BF16)
    ea_e = _dot2(jnp.exp(a_cs), e)
    cd_e = _dot2(jnp.broadcast_to(jnp.exp(a_last), (8, LANES)), e)[0:1, :]

    a_l = jnp.concatenate([a_cs] * lc, axis=0)
    a_s = jnp.concatenate([jnp.broadcast_to(a_cs[s:s + 1, :], (lc, LANES)) for s in range(lc)], axis=0)
    l_idx = lax.broadcasted_iota(jnp.int32, (lc * lc, 1), 0) % lc
    s_idx = lax.broadcasted_iota(jnp.int32, (lc * lc, 1), 0) // lc
    decay = jnp.where(l_idx >= s_idx, jnp.exp(a_l - a_s), 0.0)
    cb_prod = jnp.concatenate([cm * bm[s:s + 1, :] for s in range(lc)], axis=0)
    mix = _dot2(_dot2(cb_prod, eg_ref[...]) * decay, e)
    y = ea_e * 0.0
    for s in range(lc):
        y = y + mix[s * lc:(s + 1) * lc, :] * xdt[s:s + 1, :]

    s_prev = s0_ref[...].T
    s_b = s_prev.astype(BF16)
    bm_b = bm.astype(BF16)
    cm_b = cm.astype(BF16)
    y_off, s_parts = [], []
    for g in range(SSM_GROUPS):
        gs = slice(g * SSM_STATE, (g + 1) * SSM_STATE)
        gw = slice(g * SSM_GROUP_WIDTH, (g + 1) * SSM_GROUP_WIDTH)
        y_off.append(_dot(cm_b[:, gs], s_b[:, gw]))
        s_parts.append(lax.dot_general(bm_b[:, gs], xdte[:, gw], TN, preferred_element_type=F32))
    y = y + jnp.concatenate(y_off, axis=-1) * ea_e + dskip_ref[...] * xs
    s_new = s_prev * cd_e + jnp.concatenate(s_parts, axis=-1)
    nssm_ref[...] = s_new.T
    z = zx_ref[:, 0:SSM_D_INNER]
    y_ref[...] = _gate_norm(y, z, gn_ref[...])


def _ssd_sample(zx, dt, conv0, s0, params):
    b = conv0.shape[0]
    t_new = zx.shape[0] // b
    full = lambda x: pl.BlockSpec(x.shape, lambda bb: (0, 0))
    row = lambda w: pl.BlockSpec((t_new, w), lambda bb: (bb, 0))
    conv_spec = pl.BlockSpec((None, SSM_CONV - 1, SSM_CONV_DIM), lambda bb: (bb, 0, 0))
    state_spec = pl.BlockSpec((None, SSM_D_INNER, SSM_STATE), lambda bb: (bb, 0, 0))
    return pl.pallas_call(
        _ssd_sample_kernel,
        grid=(b,),
        in_specs=[row(SSM_MAIN), row(LANES), conv_spec, state_spec] + [full(x) for x in params],
        out_specs=[row(SSM_D_INNER), conv_spec, state_spec],
        out_shape=[jax.ShapeDtypeStruct((b * t_new, SSM_D_INNER), F32),
                   jax.ShapeDtypeStruct((b, SSM_CONV - 1, SSM_CONV_DIM), F32),
                   jax.ShapeDtypeStruct((b, SSM_D_INNER, SSM_STATE), F32)],
        scratch_shapes=[pltpu.VMEM((8 + t_new, SSM_CONV_DIM), F32)],
        compiler_params=_params(("parallel",)),
        name="ssd_sample",
    )(zx, dt, conv0, s0, *params)


def _one_hot_expand(n_rows, n_cols, group):
    r = lax.broadcasted_iota(jnp.int32, (n_rows, n_cols), 0)
    c = lax.broadcasted_iota(jnp.int32, (n_rows, n_cols), 1)
    return (c // group == r).astype(BF16)


def _kv_rows(feature_major, b):
    tokens = feature_major.shape[-1]
    x = feature_major.reshape(1, b, 2, N_HEADS, HEAD_DIM, tokens)
    return jnp.transpose(x, (0, 1, 5, 2, 3, 4))


def kernel(x_prompt, x_sample, cache_kv_g0, cache_kv_g1, cache_kv_g2, state_conv, state_ssm, attn_norm, attn_w_in, attn_q_gain, attn_k_gain, attn_w_out, ssm_norm, ssm_w_in, ssm_conv_w, ssm_conv_b, ssm_dt_bias, ssm_A_log, ssm_D, ssm_gate_norm, ssm_w_out):
    b, seq, dm = x_prompt.shape
    sb, st, _ = x_sample.shape
    xp = x_prompt.reshape(b * seq, dm)
    xs = x_sample.reshape(sb * st, dm)

    w_in = attn_w_in[0].astype(BF16)
    w_out = attn_w_out[0].astype(BF16)
    scale = HEAD_DIM ** -0.5
    q_gain3 = jnp.tile(attn_q_gain[0], (1, N_HEADS)) * scale
    k_gain3 = jnp.tile(attn_k_gain[0], (1, N_HEADS))

    log2e = 1.4426950408889634
    tile_gain = jnp.ones((ATT_IN // ATT_WIDTH, 1, ATT_WIDTH), F32)
    for g in range(len(DILATIONS)):
        tile_gain = tile_gain.at[3 * g, 0].set(q_gain3[g] * log2e).at[3 * g + 1, 0].set(k_gain3[g])
    qkvg = _attn_proj(xp, attn_norm[0], w_in, tile_gain, tm=1024, seq=seq)
    part2 = _attn_group(qkvg, 2, n_sub=2)
    part1 = _attn_group(qkvg, 1, n_sub=4)
    o_gated = _attn_group(qkvg, 0, n_sub=4, others=[part1, part2])
    y1p = _outproj(o_gated, w_out, xp, tm=1024, perm_seq=seq)
    kv_p = []
    for g, d in enumerate(DILATIONS):
        keep = min(d * WINDOW_STEPS, seq)
        kv_p.append(_kv_rows(_kv_tail(xp, attn_norm[0], w_in, g, k_gain3[g:g + 1], b=b, keep=keep,
                                      tm=min(keep, 512)), b))

    proj_s = _proj(xs, attn_norm[0], w_in, tm=sb * st, tn=1024, out_dtype=F32)
    caches_t = [jnp.transpose(c[0], (0, 2, 3, 4, 1)).reshape(sb, 2, ATT_WIDTH, c.shape[2])
                for c in (cache_kv_g0, cache_kv_g1, cache_kv_g2)]
    o_s, kn0, kn1, kn2 = _sample_attn(proj_s, caches_t, q_gain3[:, :LANES], k_gain3[:, :LANES])
    y1s = _outproj(o_s, w_out, xs, tm=sb * st)
    kv_s = []
    for g, kn in enumerate((kn0, kn1, kn2)):
        v = proj_s[:, (3 * g + 2) * ATT_WIDTH:(3 * g + 3) * ATT_WIDTH]
        kv_s.append(jnp.stack([kn, v], axis=1).reshape(1, sb, st, 2, N_HEADS, HEAD_DIM))

    w_in2 = ssm_w_in[0].astype(BF16)
    w_dt = jnp.pad(w_in2[:, SSM_MAIN:], ((0, 0), (0, LANES - SSM_HEADS)))
    w_out2 = ssm_w_out[0].astype(BF16)
    pad_h = lambda v: jnp.pad(v.astype(F32), (0, LANES - SSM_HEADS)).reshape(1, LANES)
    e32 = _one_hot_expand(LANES, SSM_D_INNER, SSM_D_INNER // SSM_HEADS)
    ssm_params = (ssm_conv_w[0], ssm_conv_b[0].reshape(1, -1), pad_h(ssm_dt_bias[0]),
                  pad_h(-jnp.exp(ssm_A_log[0].astype(F32))),
                  jnp.repeat(ssm_D[0].astype(F32), SSM_D_INNER // SSM_HEADS).reshape(1, -1),
                  ssm_gate_norm[0].reshape(1, -1), e32)

    zx_p, dt_p = _proj(y1p, ssm_norm[0], w_in2, tm=1024, tn=SSM_MAIN // 2, n_out=SSM_MAIN, out_dtype=BF16,
                       w_dt=w_dt)
    yg_p, conv_p, ssm_p = _ssd_prompt(zx_p, dt_p, b, ssm_params, chunks_per_step=2)
    y2p = _outproj(yg_p, w_out2, y1p, tm=1024)

    eg = (lax.broadcasted_iota(jnp.int32, (SSM_GROUPS * SSM_STATE, LANES), 0) // SSM_STATE
          == lax.broadcasted_iota(jnp.int32, (SSM_GROUPS * SSM_STATE, LANES), 1) // (SSM_HEADS // SSM_GROUPS))
    eg = (eg & (lax.broadcasted_iota(jnp.int32, eg.shape, 1) < SSM_HEADS)).astype(BF16)
    zx_s, dt_s = _proj(y1s, ssm_norm[0], w_in2, tm=sb * st, tn=SSM_MAIN // 2, n_out=SSM_MAIN, out_dtype=F32,
                       w_dt=w_dt)
    yg_s, conv_s, ssm_s = _ssd_sample(zx_s, dt_s, state_conv[0], state_ssm[0].reshape(sb, SSM_D_INNER, SSM_STATE),
                                      ssm_params + (eg,))
    y2s = _outproj(yg_s, w_out2, y1s, tm=sb * st)

    hp = SSM_D_INNER // SSM_HEADS
    return (y2p.reshape(b, seq, dm), y2s.reshape(sb, st, dm),
            kv_p[0], kv_p[1], kv_p[2], kv_s[0], kv_s[1], kv_s[2],
            conv_p[None], conv_s[None],
            ssm_p.reshape(1, b, SSM_HEADS, hp, SSM_STATE), ssm_s.reshape(1, sb, SSM_HEADS, hp, SSM_STATE))
```

```python
import functools

import jax
import jax.numpy as jnp
from jax import lax
from jax.experimental import pallas as pl
from jax.experimental.pallas import tpu as pltpu

F32 = jnp.float32
BF16 = jnp.bfloat16

D_MODEL = 1024
N_HEADS = 16
HEAD_DIM = 64
ATT_WIDTH = N_HEADS * HEAD_DIM
DILATIONS = (1, 4, 16)
WINDOW_STEPS = 128
ATT_IN = 10 * ATT_WIDTH
PERM = 16

SSM_D_INNER = 2048
SSM_HEADS = 32
SSM_STATE = 128
SSM_GROUPS = 4
SSM_GROUP_WIDTH = SSM_D_INNER // SSM_GROUPS
SSM_CONV = 4
SSM_CONV_DIM = SSM_D_INNER + 2 * SSM_GROUPS * SSM_STATE
SSM_MAIN = SSM_D_INNER + SSM_CONV_DIM
SSM_CHUNK = 128

NORM_EPS = 1e-6
GATE_NORM_EPS = 1e-5
MASKED = -1e30
LOG2E = 1.4426950408889634

LANES = 128
VMEM_LIMIT = 56 * 1024 * 1024

NT = (((1,), (1,)), ((), ()))
TN = (((0,), (0,)), ((), ()))


def _params(semantics):
    return pltpu.CompilerParams(dimension_semantics=semantics, vmem_limit_bytes=VMEM_LIMIT)


def _rms(x, w, eps):
    return x * lax.rsqrt(jnp.mean(x * x, axis=-1, keepdims=True) + eps) * w


def _sigmoid(x):
    return 1.0 / (1.0 + jnp.exp(-x))


def _split2(v):
    hi = v.astype(BF16)
    lo = (v - hi.astype(F32)).astype(BF16)
    return hi, lo


def _dot(a, b):
    return jnp.dot(a, b, preferred_element_type=F32)


def _dot_nt(a, b):
    return lax.dot_general(a, b, NT, preferred_element_type=F32)


def _dot2(v, e):
    hi, lo = _split2(v)
    return _dot(hi, e) + _dot(lo, e)


def _low_half():
    return lax.broadcasted_iota(jnp.int32, (1, LANES), 1) < HEAD_DIM


def _head_norm(x, eps):
    lo = _low_half()
    parts = []
    for j in range(x.shape[1] // LANES):
        t = x[:, j * LANES:(j + 1) * LANES]
        t2 = t * t
        s_lo = jnp.sum(jnp.where(lo, t2, 0.0), axis=-1, keepdims=True)
        s_hi = jnp.sum(jnp.where(lo, 0.0, t2), axis=-1, keepdims=True)
        r = jnp.where(lo, lax.rsqrt(s_lo * (1.0 / HEAD_DIM) + eps), lax.rsqrt(s_hi * (1.0 / HEAD_DIM) + eps))
        parts.append(t * r)
    return jnp.concatenate(parts, axis=-1)


def _split_heads(qp):
    lo = _low_half()
    zero = jnp.zeros_like(qp)
    return jnp.concatenate([jnp.where(lo, qp, zero), jnp.where(lo, zero, qp)], axis=0)


def _join_heads(x2):
    rows = x2.shape[0] // 2
    return jnp.where(_low_half(), x2[0:rows], x2[rows:])


def _proj_kernel(x_ref, nw_ref, w_ref, *rest, with_dt):
    rest = list(rest)
    wdt_ref = rest.pop(0) if with_dt else None
    o_ref = rest.pop(0)
    dt_ref = rest.pop(0) if with_dt else None
    h_ref = rest.pop(0)

    @pl.when(pl.program_id(1) == 0)
    def _():
        h_ref[...] = _rms(x_ref[...], nw_ref[...], NORM_EPS).astype(BF16)
        if with_dt:
            dt_ref[...] = _dot(h_ref[...], wdt_ref[...])

    o_ref[...] = _dot(h_ref[...], w_ref[...]).astype(o_ref.dtype)


def _proj(x, norm_w, w, *, tm, tn, out_dtype, n_out=None, w_dt=None):
    m, dm = x.shape
    n = w.shape[1] if n_out is None else n_out
    assert n % tn == 0 and m % tm == 0
    in_specs = [pl.BlockSpec((tm, dm), lambda i, j: (i, 0)),
                pl.BlockSpec((1, dm), lambda i, j: (0, 0)),
                pl.BlockSpec((dm, tn), lambda i, j: (0, j))]
    args = [x, norm_w.reshape(1, dm), w]
    out_shape = [jax.ShapeDtypeStruct((m, n), out_dtype)]
    out_specs = [pl.BlockSpec((tm, tn), lambda i, j: (i, j))]
    if w_dt is not None:
        in_specs.append(pl.BlockSpec((dm, LANES), lambda i, j: (0, 0)))
        args.append(w_dt)
        out_shape.append(jax.ShapeDtypeStruct((m, LANES), F32))
        out_specs.append(pl.BlockSpec((tm, LANES), lambda i, j: (i, 0)))
    outs = pl.pallas_call(
        functools.partial(_proj_kernel, with_dt=w_dt is not None),
        grid=(m // tm, n // tn), in_specs=in_specs, out_specs=out_specs, out_shape=out_shape,
        scratch_shapes=[pltpu.VMEM((tm, dm), BF16)],
        compiler_params=_params(("parallel", "arbitrary")),
        name="norm_proj",
    )(*args)
    return outs if w_dt is not None else outs[0]


N_TILES = ATT_IN // ATT_WIDTH
Q_TILES, K_TILES, V_TILES, GATE_TILE = (0, 3, 6), (1, 4, 7), (2, 5, 8), 9
STREAM_A = (0, 1, 3, 4, 6)
STREAM_B = (2, 5, 8, 9, 7)


def _lookup(j, table):
    out = jnp.int32(table[0])
    for t, v in enumerate(table[1:], 1):
        out = jnp.where(j == t, jnp.int32(v), out)
    return out


def _attn_proj_kernel(x_ref, nw_ref, wa_ref, wb_ref, ga_ref, gb_ref, oa_ref, ob_ref, h_ref, slab_ref):
    j = pl.program_id(1)

    @pl.when(j == 0)
    def _():
        xn = _rms(x_ref[...], nw_ref[...], NORM_EPS)
        rows = x_ref.shape[0] // PERM
        for c in range(x_ref.shape[1] // LANES):
            cols = slice(c * LANES, (c + 1) * LANES)
            slab_ref[...] = xn[:, cols]
            for r in range(PERM):
                h_ref[r * rows:(r + 1) * rows, cols] = slab_ref[pl.ds(r, rows, stride=PERM), :].astype(BF16)

    def tile(w_ref, gain_ref, o_ref, normed):
        res = _dot(h_ref[...], w_ref[...])
        if normed:
            res = _head_norm(res, NORM_EPS) * gain_ref[0]
        o_ref[...] = res.astype(o_ref.dtype).reshape(o_ref.shape)

    last = pl.num_programs(1) - 1

    @pl.when(j < last)
    def _():
        tile(wa_ref, ga_ref, oa_ref, True)
        tile(wb_ref, gb_ref, ob_ref, False)

    @pl.when(j == last)
    def _():
        tile(wa_ref, ga_ref, oa_ref, True)
        tile(wb_ref, gb_ref, ob_ref, True)


def _attn_proj(x, norm_w, w, tile_gain, *, tm, seq):
    m, dm = x.shape
    per_b = seq // tm
    rows = tm // PERM
    n_steps = len(STREAM_A)
    w_spec = lambda table: pl.BlockSpec((dm, ATT_WIDTH), lambda i, j: (0, _lookup(j, table)))
    g_spec = lambda table: pl.BlockSpec((1, 1, ATT_WIDTH), lambda i, j: (_lookup(j, table), 0, 0))
    out_spec = pl.BlockSpec((None, PERM, rows, ATT_WIDTH), lambda i, j: (i // per_b, 0, i % per_b, j))
    out_sds = jax.ShapeDtypeStruct((m // seq, PERM, seq // PERM, n_steps * ATT_WIDTH), BF16)
    return pl.pallas_call(
        _attn_proj_kernel,
        grid=(m // tm, n_steps),
        in_specs=[pl.BlockSpec((tm, dm), lambda i, j: (i, 0)), pl.BlockSpec((1, dm), lambda i, j: (0, 0)),
                  w_spec(STREAM_A), w_spec(STREAM_B), g_spec(STREAM_A), g_spec(STREAM_B)],
        out_specs=[out_spec, out_spec], out_shape=[out_sds, out_sds],
        scratch_shapes=[pltpu.VMEM((tm, dm), BF16), pltpu.VMEM((tm, LANES), F32)],
        compiler_params=_params(("parallel", "arbitrary")),
        name="attn_proj",
    )(x, norm_w.reshape(1, dm), w, w, tile_gain, tile_gain)


def _outproj_kernel(a_ref, *rest, unperm, gated):
    rest = list(rest)
    z_ref, gn_ref = (rest.pop(0), rest.pop(0)) if gated else (None, None)
    w_ref, x_ref, o_ref = rest[:3]
    scratch = rest[3:]
    tm = o_ref.shape[0]
    if gated:
        chunk = 256
        parts = []
        for c in range(tm // chunk):
            rows = slice(c * chunk, (c + 1) * chunk)
            a = _gate_norm(a_ref[rows, :].astype(F32), z_ref[rows, :].astype(F32), gn_ref[...])
            parts.append(x_ref[rows, :] + _dot(a.astype(BF16), w_ref[...]))
        o_ref[...] = jnp.concatenate(parts, axis=0)
        return
    a = a_ref[...].reshape(tm, a_ref.shape[-1])
    res = _dot(a.astype(BF16), w_ref[...])
    if unperm:
        slab_ref, = scratch
        rows = tm // PERM
        for c in range(o_ref.shape[1] // LANES):
            cols = slice(c * LANES, (c + 1) * LANES)
            for r in range(PERM):
                slab_ref[pl.ds(r, rows, stride=PERM), :] = res[r * rows:(r + 1) * rows, cols]
            o_ref[:, cols] = x_ref[:, cols] + slab_ref[...]
    else:
        o_ref[...] = x_ref[...] + res


def _outproj(a, w, x, *, tm, perm_seq=None, gate=None):
    m, n = x.shape
    k = w.shape[0]
    if perm_seq is not None:
        per_b = perm_seq // tm
        a_spec = pl.BlockSpec((None, PERM, tm // PERM, k), lambda i: (i // per_b, 0, i % per_b, 0))
    else:
        a_spec = pl.BlockSpec((tm, k), lambda i: (i, 0))
    in_specs, args = [a_spec], [a]
    if gate is not None:
        in_specs += [pl.BlockSpec((tm, k), lambda i: (i, 0)), pl.BlockSpec((1, k), lambda i: (0, 0))]
        args += list(gate)
    in_specs += [pl.BlockSpec((k, n), lambda i: (0, 0)), pl.BlockSpec((tm, n), lambda i: (i, 0))]
    return pl.pallas_call(
        functools.partial(_outproj_kernel, unperm=perm_seq is not None, gated=gate is not None),
        grid=(m // tm,),
        in_specs=in_specs,
        out_specs=pl.BlockSpec((tm, n), lambda i: (i, 0)),
        out_shape=jax.ShapeDtypeStruct((m, n), F32),
        scratch_shapes=[pltpu.VMEM((tm, LANES), F32)] if perm_seq else [],
        compiler_params=_params(("parallel",)),
        name="out_proj",
    )(*args, w, x)


ATT_SUB = 128


def _attn_group_kernel(*refs, n_chunks, n_sub, merge):
    it = iter(refs)
    q_ref, k_ref, v_ref = next(it), next(it), next(it)
    if merge:
        gate_ref = next(it)
        others = [(next(it), next(it)) for _ in range(len(DILATIONS) - 1)]
        o_out = next(it)
    else:
        o_out, lse_out = next(it), next(it)
    ks_ref, vs_ref = next(it), next(it)

    rc = q_ref.shape[1] // n_sub
    assert n_chunks * rc == ATT_SUB
    n = pl.program_id(2)
    narrow = rc % 16 != 0

    def piece(ref, u, cols=slice(None)):
        if narrow and ref.dtype == BF16:
            pair = 2 * (u // 2) * rc
            x = ref[:, pair:pair + 2 * rc, cols].astype(F32)[:, (u % 2) * rc:(u % 2 + 1) * rc, :]
        else:
            x = ref[:, u * rc:(u + 1) * rc, cols]
        return x.reshape(ATT_SUB, x.shape[-1])

    def unsubs(parts, dtype):
        parts = [p.reshape(n_chunks, rc, p.shape[-1]) for p in parts]
        if narrow:
            return jnp.concatenate(parts, axis=1).astype(dtype)
        return jnp.concatenate([p.astype(dtype) for p in parts], axis=1)

    @pl.when(n == 0)
    def _():
        ks_ref[0:ATT_SUB, :] = jnp.zeros((ATT_SUB, ATT_WIDTH), BF16)
        vs_ref[0:ATT_SUB, :] = jnp.zeros((ATT_SUB, ATT_WIDTH), BF16)

    for u in range(n_sub):
        ks_ref[(u + 1) * ATT_SUB:(u + 2) * ATT_SUB, :] = piece(k_ref, u).astype(BF16)
        vs_ref[(u + 1) * ATT_SUB:(u + 2) * ATT_SUB, :] = piece(v_ref, u).astype(BF16)

    qi = lax.broadcasted_iota(jnp.int32, (ATT_SUB, 2 * ATT_SUB), 0)
    kj = lax.broadcasted_iota(jnp.int32, (ATT_SUB, 2 * ATT_SUB), 1)
    kc = jnp.where(kj >= ATT_SUB, kj - ATT_SUB, kj)
    tq = n_chunks * (qi % rc) + qi // rc
    tk = n_chunks * (kc % rc) + kc // rc + jnp.where(kj >= ATT_SUB, 0, -ATT_SUB)
    dist = tq - tk
    in_band = (dist >= 0) & (dist <= WINDOW_STEPS)
    band = jnp.where(in_band, 0.0, MASKED)
    band_first = jnp.where(in_band & ((kj >= ATT_SUB) | (n > 0)), 0.0, MASKED)
    bias = [jnp.concatenate([x, x], axis=0) for x in (band_first, band)]

    for j in range(N_HEADS // 2):
        cols = slice(j * LANES, (j + 1) * LANES)
        o_parts, lse_parts = [], []
        for u in range(n_sub):
            keys = slice(u * ATT_SUB, (u + 2) * ATT_SUB)
            q2 = _split_heads(piece(q_ref, u, cols).astype(BF16))
            s = _dot_nt(q2, ks_ref[keys, cols]) + bias[min(u, 1)]
            m = jnp.max(s, axis=-1, keepdims=True)
            p = jnp.exp2(s - m)
            l = jnp.sum(p, axis=-1, keepdims=True)
            o = _join_heads(_dot(p.astype(BF16), vs_ref[keys, cols]) / l)
            lse = _join_heads(jnp.broadcast_to(m + jnp.log2(l), (2 * ATT_SUB, LANES)))
            if merge:
                o_g = [o] + [piece(o_ref, u, cols).astype(F32) for o_ref, _ in others]
                lse_g = [lse] + [piece(l_ref, u, cols) for _, l_ref in others]
                top = jnp.maximum(jnp.maximum(lse_g[0], lse_g[1]), lse_g[2])
                w = [jnp.exp2(x - top) for x in lse_g]
                o = (w[0] * o_g[0] + w[1] * o_g[1] + w[2] * o_g[2]) / (w[0] + w[1] + w[2])
                g = piece(gate_ref, u, cols).astype(F32)
                o = o * (g * _sigmoid(g))
            o_parts.append(o)
            lse_parts.append(lse)
        o_out[:, :, cols] = unsubs(o_parts, BF16)
        if not merge:
            lse_out[:, :, cols] = unsubs(lse_parts, F32)
    ks_ref[0:ATT_SUB, :] = ks_ref[n_sub * ATT_SUB:(n_sub + 1) * ATT_SUB, :]
    vs_ref[0:ATT_SUB, :] = vs_ref[n_sub * ATT_SUB:(n_sub + 1) * ATT_SUB, :]


def _attn_group(qkvg, g, *, n_sub, others=None):
    b, _, t, _ = qkvg[0].shape
    d = DILATIONS[g]
    n_chunks = PERM // d
    rc = ATT_SUB // n_chunks * n_sub
    nb = t // rc
    merge = others is not None
    view = lambda a: a.reshape(b, n_chunks, d, t, a.shape[-1])

    def spec(col=0):
        return pl.BlockSpec((None, n_chunks, None, rc, ATT_WIDTH), lambda bb, r, n: (bb, 0, r, n, col))

    in_specs, args = [], []

    def add_tile(tile):
        stream, table = (0, STREAM_A) if tile in STREAM_A else (1, STREAM_B)
        in_specs.append(spec(table.index(tile)))
        args.append(view(qkvg[stream]))

    for tile in (Q_TILES[g], K_TILES[g], V_TILES[g]):
        add_tile(tile)
    full = jax.ShapeDtypeStruct((b, n_chunks, d, t, ATT_WIDTH), BF16)
    if merge:
        add_tile(GATE_TILE)
        for o, lse in others:
            in_specs += [spec(), spec()]
            args += [view(o), view(lse)]
        out_shape, out_specs = [full], [spec()]
    else:
        out_shape = [full, jax.ShapeDtypeStruct(full.shape, F32)]
        out_specs = [spec(), spec()]
    outs = pl.pallas_call(
        functools.partial(_attn_group_kernel, n_chunks=n_chunks, n_sub=n_sub, merge=merge),
        grid=(b, d, nb), in_specs=in_specs, out_specs=out_specs, out_shape=out_shape,
        scratch_shapes=[pltpu.VMEM(((n_sub + 1) * ATT_SUB, ATT_WIDTH), BF16)] * 2,
        compiler_params=_params(("parallel", "parallel", "arbitrary")),
        name=f"attn_group{g}",
    )(*args)
    outs = [a.reshape(b, PERM, t, ATT_WIDTH) for a in outs]
    return outs[0] if merge else tuple(outs)


def _kv_tail_kernel(x_ref, nw_ref, wk_ref, wv_ref, kg_ref, o_ref):
    h = _rms(x_ref[...], nw_ref[...], NORM_EPS).astype(BF16)
    o_ref[0:ATT_WIDTH, :] = (_head_norm(_dot(h, wk_ref[...]), NORM_EPS) * kg_ref[...]).T
    o_ref[ATT_WIDTH:2 * ATT_WIDTH, :] = _dot(h, wv_ref[...]).T


def _kv_tail(x, norm_w, w, g, k_gain, *, b, keep, tm):
    seq = x.shape[0] // b
    first = (seq - keep) // tm
    w_spec = lambda col: pl.BlockSpec((D_MODEL, ATT_WIDTH), lambda bb, i: (0, col))
    return pl.pallas_call(
        _kv_tail_kernel,
        grid=(b, keep // tm),
        in_specs=[pl.BlockSpec((tm, D_MODEL), lambda bb, i: (bb * (seq // tm) + first + i, 0)),
                  pl.BlockSpec((1, D_MODEL), lambda bb, i: (0, 0)),
                  w_spec(3 * g + 1), w_spec(3 * g + 2),
                  pl.BlockSpec((1, ATT_WIDTH), lambda bb, i: (0, 0))],
        out_specs=pl.BlockSpec((None, 2 * ATT_WIDTH, tm), lambda bb, i: (bb, 0, i)),
        out_shape=jax.ShapeDtypeStruct((b, 2 * ATT_WIDTH, keep), F32),
        compiler_params=_params(("parallel", "parallel")),
        name="kv_tail",
    )(x, norm_w.reshape(1, D_MODEL), w, w, k_gain)


SAMPLE_HEADS = 8


def _sample_attn_kernel(*refs):
    n_g = len(DILATIONS)
    qkv = [refs[3 * g:3 * g + 3] for g in range(n_g)]
    gate_ref = refs[3 * n_g]
    c_refs = refs[3 * n_g + 1:4 * n_g + 1]
    qg_ref, kg_ref = refs[4 * n_g + 1:4 * n_g + 3]
    o_ref = refs[4 * n_g + 3]
    kn_refs = refs[4 * n_g + 4:]
    t_new = o_ref.shape[0]
    t_row = lax.broadcasted_iota(jnp.int32, (2 * t_new, 1), 0) % t_new
    t_col = lax.broadcasted_iota(jnp.int32, (1, t_new), 1)

    o_cols = []
    kn_cols = [[] for _ in DILATIONS]
    for j in range(SAMPLE_HEADS // 2):
        cols = slice(j * LANES, (j + 1) * LANES)
        o_g, m_g, l_g = [], [], []
        for g, d in enumerate(DILATIONS):
            q_ref, k_ref, v_ref = qkv[g]
            qm = _split_heads(_head_norm(q_ref[:, cols], NORM_EPS) * qg_ref[g:g + 1, :]).astype(BF16)
            kn = _head_norm(k_ref[:, cols], NORM_EPS) * kg_ref[g:g + 1, :]
            kn_cols[g].append(kn)
            length = c_refs[g].shape[2]
            s_c = _dot(qm, c_refs[g][0, cols, :].astype(BF16))
            s_n = _dot_nt(qm, kn.astype(BF16))
            back = length + t_row - lax.broadcasted_iota(jnp.int32, (1, length), 1)
            s_c = jnp.where((back % d == 0) & (back <= WINDOW_STEPS * d), s_c, MASKED)
            back = t_row - t_col
            s_n = jnp.where((back >= 0) & (back % d == 0) & (back <= WINDOW_STEPS * d), s_n, MASKED)
            m = jnp.maximum(jnp.max(s_c, axis=-1, keepdims=True), jnp.max(s_n, axis=-1, keepdims=True))
            p_c = jnp.exp(s_c - m)
            p_n = jnp.exp(s_n - m)
            l_g.append(jnp.sum(p_c, axis=-1, keepdims=True) + jnp.sum(p_n, axis=-1, keepdims=True))
            o_g.append(_dot_nt(p_c.astype(BF16), c_refs[g][1, cols, :].astype(BF16))
                       + _dot(p_n.astype(BF16), v_ref[:, cols].astype(BF16)))
            m_g.append(m)
        top = jnp.maximum(jnp.maximum(m_g[0], m_g[1]), m_g[2])
        f = [jnp.exp(m - top) for m in m_g]
        den = f[0] * l_g[0] + f[1] * l_g[1] + f[2] * l_g[2]
        o = _join_heads((f[0] * o_g[0] + f[1] * o_g[1] + f[2] * o_g[2]) / den)
        gate = gate_ref[:, cols]
        o_cols.append(o * (gate * _sigmoid(gate)))
    o_ref[...] = jnp.concatenate(o_cols, axis=-1)
    for g in range(n_g):
        kn_refs[g][...] = jnp.concatenate(kn_cols[g], axis=-1)


def _sample_attn(proj, caches_t, q_gain, k_gain):
    b = caches_t[0].shape[0]
    t_new = proj.shape[0] // b
    width = SAMPLE_HEADS * HEAD_DIM
    per_chunk = ATT_WIDTH // width
    col = lambda c: pl.BlockSpec((t_new, width), lambda bb, hc: (bb, c * per_chunk + hc))
    in_specs, args = [], []
    for g in range(len(DILATIONS)):
        in_specs += [col(3 * g), col(3 * g + 1), col(3 * g + 2)]
        args += [proj, proj, proj]
    in_specs.append(col(ATT_IN // ATT_WIDTH - 1))
    args.append(proj)
    for c in caches_t:
        in_specs.append(pl.BlockSpec((None, 2, width, c.shape[3]), lambda bb, hc: (bb, 0, hc, 0)))
        args.append(c)
    gain_spec = pl.BlockSpec((len(DILATIONS), LANES), lambda bb, hc: (0, 0))
    out_spec = pl.BlockSpec((t_new, width), lambda bb, hc: (bb, hc))
    out_sds = jax.ShapeDtypeStruct((b * t_new, ATT_WIDTH), F32)
    return pl.pallas_call(
        _sample_attn_kernel,
        grid=(b, per_chunk),
        in_specs=in_specs + [gain_spec, gain_spec],
        out_specs=[out_spec] * 4,
        out_shape=[out_sds] * 4,
        compiler_params=_params(("parallel", "parallel")),
        name="sample_attn",
    )(*args, q_gain, k_gain)


def _softplus(x):
    return jnp.maximum(x, 0.0) + jnp.log(1.0 + jnp.exp(-jnp.abs(x)))


def _conv_silu(zx_ref, xpad_ref, cw_ref, cb_ref, lc):
    pad = 8
    xpad_ref[pad:pad + lc, :] = zx_ref[:, SSM_D_INNER:SSM_MAIN].astype(F32)
    conv = cb_ref[...] + xpad_ref[pad - 3:pad - 3 + lc, :] * cw_ref[0:1, :]
    for j in range(1, SSM_CONV):
        conv = conv + xpad_ref[pad - 3 + j:pad - 3 + j + lc, :] * cw_ref[j:j + 1, :]
    tail = xpad_ref[pad + lc - 3:pad + lc, :]
    xpad_ref[pad - 3:pad, :] = tail
    return conv * _sigmoid(conv), tail


def _gate_norm(y, z, gn):
    y = y * (z * _sigmoid(z))
    parts = []
    for g in range(SSM_GROUPS):
        yg = y[:, g * SSM_GROUP_WIDTH:(g + 1) * SSM_GROUP_WIDTH]
        parts.append(yg * lax.rsqrt(jnp.mean(yg * yg, axis=-1, keepdims=True) + GATE_NORM_EPS))
    return jnp.concatenate(parts, axis=-1) * gn


def _cumsum_rows(x):
    rows = x.shape[0]
    tril = (lax.broadcasted_iota(jnp.int32, (rows, rows), 0)
            >= lax.broadcasted_iota(jnp.int32, (rows, rows), 1)).astype(BF16)
    h1 = x.astype(BF16)
    r1 = x - h1.astype(F32)
    h2 = r1.astype(BF16)
    h3 = (r1 - h2.astype(F32)).astype(BF16)
    return _dot(tril, h1) + _dot(tril, h2) + _dot(tril, h3)


def _ssd_prompt_kernel(zx_ref, dt_ref, cw_ref, cb_ref, dtb_ref, a_ref, dskip_ref, e_ref,
                       y_ref, nconv_ref, nssm_ref, carry_ref, st_ref):
    c = pl.program_id(1)
    last = pl.num_programs(1) - 1

    @pl.when(c == 0)
    def _():
        carry_ref[...] = jnp.zeros(carry_ref.shape, BF16)
        st_ref[...] = jnp.zeros(st_ref.shape, F32)

    for u in range(zx_ref.shape[0] // SSM_CHUNK):
        rows = pl.ds(u * SSM_CHUNK, SSM_CHUNK)
        _ssd_chunk(zx_ref.at[rows], dt_ref.at[rows], cw_ref, cb_ref, dtb_ref, a_ref, dskip_ref, e_ref,
                   y_ref.at[rows], carry_ref, st_ref)

    @pl.when(c == last)
    def _():
        n_carry = carry_ref.shape[0]
        nconv_ref[...] = carry_ref[...].astype(F32)[n_carry - (SSM_CONV - 1):n_carry]
        nssm_ref[...] = st_ref[...].T


def _ssd_chunk(zx_ref, dt_ref, cw_ref, cb_ref, dtb_ref, a_ref, dskip_ref, e_ref, y_ref, carry_ref, st_ref):
    lc = zx_ref.shape[0]
    n_carry = carry_ref.shape[0]
    src = lax.broadcasted_iota(jnp.int32, (lc, n_carry + lc), 1) - n_carry
    dst = lax.broadcasted_iota(jnp.int32, (lc, n_carry + lc), 0)
    taps = SSM_CONV - 1
    shift = jnp.concatenate([(src == dst - (taps - j)).astype(BF16) for j in range(taps)], axis=0)

    def conv_silu(start, width):
        cols = slice(start, start + width)
        raw = zx_ref[:, SSM_D_INNER + start:SSM_D_INNER + start + width]
        shifted = _dot(shift, jnp.concatenate([carry_ref[:, cols], raw], axis=0))
        conv = cb_ref[:, cols] + shifted[0:lc] * cw_ref[0:1, cols]
        for j in range(1, taps):
            conv = conv + shifted[j * lc:(j + 1) * lc] * cw_ref[j:j + 1, cols]
        conv = conv + raw.astype(F32) * cw_ref[taps:taps + 1, cols]
        return conv * _sigmoid(conv)

    dt = _softplus(dt_ref[...] + dtb_ref[...])
    a_cs = _cumsum_rows(dt * a_ref[...])
    a_log2 = a_cs * LOG2E
    a_log2_t = a_log2.T
    dt_t = dt.T
    a_last = a_cs[lc - 1:lc, :]
    to_end = (dt * jnp.exp(a_last - a_cs)).astype(BF16)
    from_start = jnp.exp(a_cs).astype(BF16)
    chunk_decay = jnp.broadcast_to(jnp.exp(a_last), (8, LANES))
    causal = (lax.broadcasted_iota(jnp.int32, (lc, lc), 0) >= lax.broadcasted_iota(jnp.int32, (lc, lc), 1))
    heads_per_group = SSM_HEADS // SSM_GROUPS
    head_w = SSM_D_INNER // SSM_HEADS
    quad = 4
    quad_w = quad * head_w
    lane_head = lax.broadcasted_iota(jnp.int32, (1, quad_w), 1) // head_w
    n_bc = SSM_GROUPS * SSM_STATE
    for g in range(SSM_GROUPS):
        gw = slice(g * SSM_GROUP_WIDTH, (g + 1) * SSM_GROUP_WIDTH)
        xs = conv_silu(g * SSM_GROUP_WIDTH, SSM_GROUP_WIDTH)
        bm = conv_silu(SSM_D_INNER + g * SSM_STATE, SSM_STATE).astype(BF16)
        cm = conv_silu(SSM_D_INNER + n_bc + g * SSM_STATE, SSM_STATE).astype(BF16)
        e = e_ref[:, gw]
        xs_b = xs.astype(BF16)
        xdte = (xs * _dot(to_end, e)).astype(BF16)
        s_prev = st_ref[:, gw]
        cb = _dot_nt(cm, bm)
        y = _dot(cm, s_prev.astype(BF16)) * _dot(from_start, e)
        st_ref[:, gw] = (s_prev * _dot2(chunk_decay, e)[0:1, :]
                         + lax.dot_general(bm, xdte, TN, preferred_element_type=F32))
        quad_out = []
        for qd in range(heads_per_group // quad):
            h0 = g * heads_per_group + quad * qd
            ms = []
            for h in range(h0, h0 + quad):
                seg = a_log2[:, h:h + 1] - a_log2_t[h:h + 1, :]
                ms.append((cb * jnp.where(causal, jnp.exp2(seg), 0.0) * dt_t[h:h + 1, :]).astype(BF16))
            xq = xs_b[:, qd * quad_w:(qd + 1) * quad_w]
            rhs = jnp.concatenate([jnp.where(lane_head == i, xq, jnp.zeros_like(xq)) for i in range(quad)], axis=0)
            quad_out.append(_dot(jnp.concatenate(ms, axis=1), rhs))
        y = y + jnp.concatenate(quad_out, axis=-1) + dskip_ref[:, gw] * xs
        y_ref[:, gw] = y.astype(y_ref.dtype)

    carry_ref[...] = zx_ref[lc - n_carry:lc, SSM_D_INNER:SSM_MAIN]


def _ssd_prompt(zx, dt, b, params, *, chunks_per_step):
    m = zx.shape[0]
    rows = chunks_per_step * SSM_CHUNK
    nc = m // b // rows
    full = lambda x: pl.BlockSpec(x.shape, lambda bb, c: (0, 0))
    row = lambda w: pl.BlockSpec((rows, w), lambda bb, c: (bb * nc + c, 0))
    return pl.pallas_call(
        _ssd_prompt_kernel,
        grid=(b, nc),
        in_specs=[row(SSM_MAIN), row(LANES)] + [full(x) for x in params],
        out_specs=[row(SSM_D_INNER),
                   pl.BlockSpec((None, SSM_CONV - 1, SSM_CONV_DIM), lambda bb, c: (bb, 0, 0)),
                   pl.BlockSpec((None, SSM_D_INNER, SSM_STATE), lambda bb, c: (bb, 0, 0))],
        out_shape=[jax.ShapeDtypeStruct((m, SSM_D_INNER), BF16),
                   jax.ShapeDtypeStruct((b, SSM_CONV - 1, SSM_CONV_DIM), F32),
                   jax.ShapeDtypeStruct((b, SSM_D_INNER, SSM_STATE), F32)],
        scratch_shapes=[pltpu.VMEM((16, SSM_CONV_DIM), BF16),
                        pltpu.VMEM((SSM_STATE, SSM_D_INNER), F32)],
        compiler_params=_params(("parallel", "arbitrary")),
        name="ssd_prompt",
    )(zx, dt, *params)


def _ssd_sample_kernel(zx_ref, dt_ref, conv0_ref, s0_ref, cw_ref, cb_ref, dtb_ref, a_ref, dskip_ref, gn_ref,
                       e_ref, eg_ref, y_ref, nconv_ref, nssm_ref, xpad_ref):
    lc = zx_ref.shape[0]
    xpad_ref[0:5, :] = jnp.zeros((5, SSM_CONV_DIM), F32)
    xpad_ref[5:8, :] = conv0_ref[...]
    xbc, tail = _conv_silu(zx_ref, xpad_ref, cw_ref, cb_ref, lc)
    nconv_ref[...] = tail

    xs = xbc[:, 0:SSM_D_INNER]
    bm = xbc[:, SSM_D_INNER:SSM_D_INNER + SSM_GROUPS * SSM_STATE]
    cm = xbc[:, SSM_D_INNER + SSM_GROUPS * SSM_STATE:]
    e = e_ref[...]
    dt = _softplus(dt_ref[...] + dtb_ref[...])
    dta = dt * a_ref[...]
    rows = [dta[0:1, :]]
    for i in range(1, lc):
        rows.append(rows[-1] + dta[i:i + 1, :])
    a_cs = jnp.concatenate(rows, axis=0)
    a_last = rows[-1]
    xdt = xs * _dot2(dt, e)
    xdte = (xdt * _dot2(jnp.exp(a_last - a_cs), e)).astype(BF16)
    ea_e = _dot2(jnp.exp(a_cs), e)
    cd_e = _dot2(jnp.broadcast_to(jnp.exp(a_last), (8, LANES)), e)[0:1, :]

    a_l = jnp.concatenate([a_cs] * lc, axis=0)
    a_s = jnp.concatenate([jnp.broadcast_to(a_cs[s:s + 1, :], (lc, LANES)) for s in range(lc)], axis=0)
    l_idx = lax.broadcasted_iota(jnp.int32, (lc * lc, 1), 0) % lc
    s_idx = lax.broadcasted_iota(jnp.int32, (lc * lc, 1), 0) // lc
    decay = jnp.where(l_idx >= s_idx, jnp.exp(a_l - a_s), 0.0)
    cb_prod = jnp.concatenate([cm * bm[s:s + 1, :] for s in range(lc)], axis=0)
    mix = _dot2(_dot2(cb_prod, eg_ref[...]) * decay, e)
    y = ea_e * 0.0
    for s in range(lc):
        y = y + mix[s * lc:(s + 1) * lc, :] * xdt[s:s + 1, :]

    s_prev = s0_ref[...].T
    s_b = s_prev.astype(BF16)
    bm_b = bm.astype(BF16)
    cm_b = cm.astype(BF16)
    y_off, s_parts = [], []
    for g in range(SSM_GROUPS):
        gs = slice(g * SSM_STATE, (g + 1) * SSM_STATE)
        gw = slice(g * SSM_GROUP_WIDTH, (g + 1) * SSM_GROUP_WIDTH)
        y_off.append(_dot(cm_b[:, gs], s_b[:, gw]))
        s_parts.append(lax.dot_general(bm_b[:, gs], xdte[:, gw], TN, preferred_element_type=F32))
    y = y + jnp.concatenate(y_off, axis=-1) * ea_e + dskip_ref[...] * xs
    s_new = s_prev * cd_e + jnp.concatenate(s_parts, axis=-1)
    nssm_ref[...] = s_new.T
    z = zx_ref[:, 0:SSM_D_INNER]
    y_ref[...] = _gate_norm(y, z, gn_ref[...])


def _ssd_sample(zx, dt, conv0, s0, params):
    b = conv0.shape[0]
    t_new = zx.shape[0] // b
    full = lambda x: pl.BlockSpec(x.shape, lambda bb: (0, 0))
    row = lambda w: pl.BlockSpec((t_new, w), lambda bb: (bb, 0))
    conv_spec = pl.BlockSpec((None, SSM_CONV - 1, SSM_CONV_DIM), lambda bb: (bb, 0, 0))
    state_spec = pl.BlockSpec((None, SSM_D_INNER, SSM_STATE), lambda bb: (bb, 0, 0))
    return pl.pallas_call(
        _ssd_sample_kernel,
        grid=(b,),
        in_specs=[row(SSM_MAIN), row(LANES), conv_spec, state_spec] + [full(x) for x in params],
        out_specs=[row(SSM_D_INNER), conv_spec, state_spec],
        out_shape=[jax.ShapeDtypeStruct((b * t_new, SSM_D_INNER), F32),
                   jax.ShapeDtypeStruct((b, SSM_CONV - 1, SSM_CONV_DIM), F32),
                   jax.ShapeDtypeStruct((b, SSM_D_INNER, SSM_STATE), F32)],
        scratch_shapes=[pltpu.VMEM((8 + t_new, SSM_CONV_DIM), F32)],
        compiler_params=_params(("parallel",)),
        name="ssd_sample",
    )(zx, dt, conv0, s0, *params)


def _one_hot_expand(n_rows, n_cols, group):
    r = lax.broadcasted_iota(jnp.int32, (n_rows, n_cols), 0)
    c = lax.broadcasted_iota(jnp.int32, (n_rows, n_cols), 1)
    return (c // group == r).astype(BF16)


def _kv_rows(feature_major, b):
    tokens = feature_major.shape[-1]
    x = feature_major.reshape(1, b, 2, N_HEADS, HEAD_DIM, tokens)
    return jnp.transpose(x, (0, 1, 5, 2, 3, 4))


def kernel(x_prompt, x_sample, cache_kv_g0, cache_kv_g1, cache_kv_g2, state_conv, state_ssm, attn_norm, attn_w_in, attn_q_gain, attn_k_gain, attn_w_out, ssm_norm, ssm_w_in, ssm_conv_w, ssm_conv_b, ssm_dt_bias, ssm_A_log, ssm_D, ssm_gate_norm, ssm_w_out):
    b, seq, dm = x_prompt.shape
    sb, st, _ = x_sample.shape
    xp = x_prompt.reshape(b * seq, dm)
    xs = x_sample.reshape(sb * st, dm)

    w_in = attn_w_in[0].astype(BF16)
    w_out = attn_w_out[0].astype(BF16)
    scale = HEAD_DIM ** -0.5
    q_gain3 = jnp.tile(attn_q_gain[0], (1, N_HEADS)) * scale
    k_gain3 = jnp.tile(attn_k_gain[0], (1, N_HEADS))

    tile_gain = jnp.ones((N_TILES, 1, ATT_WIDTH), F32)
    for g in range(len(DILATIONS)):
        tile_gain = tile_gain.at[Q_TILES[g], 0].set(q_gain3[g] * LOG2E).at[K_TILES[g], 0].set(k_gain3[g])
    qkvg = _attn_proj(xp, attn_norm[0], w_in, tile_gain, tm=1024, seq=seq)
    part2 = _attn_group(qkvg, 2, n_sub=2)
    part1 = _attn_group(qkvg, 1, n_sub=4)
    o_gated = _attn_group(qkvg, 0, n_sub=4, others=[part1, part2])
    y1p = _outproj(o_gated, w_out, xp, tm=1024, perm_seq=seq)
    kv_p = []
    for g, d in enumerate(DILATIONS):
        keep = min(d * WINDOW_STEPS, seq)
        kv_p.append(_kv_rows(_kv_tail(xp, attn_norm[0], w_in, g, k_gain3[g:g + 1], b=b, keep=keep,
                                      tm=min(keep, 512)), b))

    proj_s = _proj(xs, attn_norm[0], w_in, tm=sb * st, tn=1024, out_dtype=F32)
    caches_t = [jnp.transpose(c[0], (0, 2, 3, 4, 1)).reshape(sb, 2, ATT_WIDTH, c.shape[2])
                for c in (cache_kv_g0, cache_kv_g1, cache_kv_g2)]
    o_s, kn0, kn1, kn2 = _sample_attn(proj_s, caches_t, q_gain3[:, :LANES], k_gain3[:, :LANES])
    y1s = _outproj(o_s, w_out, xs, tm=sb * st)
    kv_s = []
    for g, kn in enumerate((kn0, kn1, kn2)):
        v = proj_s[:, (3 * g + 2) * ATT_WIDTH:(3 * g + 3) * ATT_WIDTH]
        kv_s.append(jnp.stack([kn, v], axis=1).reshape(1, sb, st, 2, N_HEADS, HEAD_DIM))

    w_in2 = ssm_w_in[0].astype(BF16)
    w_dt = jnp.pad(w_in2[:, SSM_MAIN:], ((0, 0), (0, LANES - SSM_HEADS)))
    w_out2 = ssm_w_out[0].astype(BF16)
    pad_h = lambda v: jnp.pad(v.astype(F32), (0, LANES - SSM_HEADS)).reshape(1, LANES)
    e32 = _one_hot_expand(LANES, SSM_D_INNER, SSM_D_INNER // SSM_HEADS)
    gate_gain = ssm_gate_norm[0].reshape(1, -1)
    scan_params = (ssm_conv_w[0], ssm_conv_b[0].reshape(1, -1), pad_h(ssm_dt_bias[0]),
                   pad_h(-jnp.exp(ssm_A_log[0].astype(F32))),
                   jnp.repeat(ssm_D[0].astype(F32), SSM_D_INNER // SSM_HEADS).reshape(1, -1))

    zx_p, dt_p = _proj(y1p, ssm_norm[0], w_in2, tm=1024, tn=SSM_MAIN // 2, n_out=SSM_MAIN, out_dtype=BF16,
                       w_dt=w_dt)
    y_p, conv_p, ssm_p = _ssd_prompt(zx_p, dt_p, b, scan_params + (e32,), chunks_per_step=2)
    y2p = _outproj(y_p, w_out2, y1p, tm=1024, gate=(zx_p, gate_gain))

    eg = (lax.broadcasted_iota(jnp.int32, (SSM_GROUPS * SSM_STATE, LANES), 0) // SSM_STATE
          == lax.broadcasted_iota(jnp.int32, (SSM_GROUPS * SSM_STATE, LANES), 1) // (SSM_HEADS // SSM_GROUPS))
    eg = (eg & (lax.broadcasted_iota(jnp.int32, eg.shape, 1) < SSM_HEADS)).astype(BF16)
    zx_s, dt_s = _proj(y1s, ssm_norm[0], w_in2, tm=sb * st, tn=SSM_MAIN // 2, n_out=SSM_MAIN, out_dtype=F32,
                       w_dt=w_dt)
    yg_s, conv_s, ssm_s = _ssd_sample(zx_s, dt_s, state_conv[0], state_ssm[0].reshape(sb, SSM_D_INNER, SSM_STATE),
                                      scan_params + (gate_gain, e32, eg))
    y2s = _outproj(yg_s, w_out2, y1s, tm=sb * st)

    hp = SSM_D_INNER // SSM_HEADS
    return (y2p.reshape(b, seq, dm), y2s.reshape(sb, st, dm),
            kv_p[0], kv_p[1], kv_p[2], kv_s[0], kv_s[1], kv_s[2],
            conv_p[None], conv_s[None],
            ssm_p.reshape(1, b, SSM_HEADS, hp, SSM_STATE), ssm_s.reshape(1, sb, SSM_HEADS, hp, SSM_STATE))
```

```python
import functools

import jax
import jax.numpy as jnp
from jax import lax
from jax.experimental import pallas as pl
from jax.experimental.pallas import tpu as pltpu

F32 = jnp.float32
BF16 = jnp.bfloat16

D_MODEL = 1024
N_HEADS = 16
HEAD_DIM = 64
ATT_WIDTH = N_HEADS * HEAD_DIM
DILATIONS = (1, 4, 16)
WINDOW_STEPS = 128
ATT_IN = 10 * ATT_WIDTH
PERM = 16

SSM_D_INNER = 2048
SSM_HEADS = 32
SSM_STATE = 128
SSM_GROUPS = 4
SSM_GROUP_WIDTH = SSM_D_INNER // SSM_GROUPS
SSM_CONV = 4
SSM_CONV_DIM = SSM_D_INNER + 2 * SSM_GROUPS * SSM_STATE
SSM_MAIN = SSM_D_INNER + SSM_CONV_DIM
SSM_CHUNK = 128

NORM_EPS = 1e-6
GATE_NORM_EPS = 1e-5
MASKED = -1e30
LOG2E = 1.4426950408889634

LANES = 128
VMEM_LIMIT = 56 * 1024 * 1024

NT = (((1,), (1,)), ((), ()))
TN = (((0,), (0,)), ((), ()))


def _params(semantics):
    return pltpu.CompilerParams(dimension_semantics=semantics, vmem_limit_bytes=VMEM_LIMIT)


def _rms(x, w, eps):
    return x * lax.rsqrt(jnp.mean(x * x, axis=-1, keepdims=True) + eps) * w


def _sigmoid(x):
    return 1.0 / (1.0 + jnp.exp(-x))


def _split2(v):
    hi = v.astype(BF16)
    lo = (v - hi.astype(F32)).astype(BF16)
    return hi, lo


def _dot(a, b):
    return jnp.dot(a, b, preferred_element_type=F32)


def _dot_nt(a, b):
    return lax.dot_general(a, b, NT, preferred_element_type=F32)


def _dot2(v, e):
    hi, lo = _split2(v)
    return _dot(hi, e) + _dot(lo, e)


def _low_half():
    return lax.broadcasted_iota(jnp.int32, (1, LANES), 1) < HEAD_DIM


def _head_norm(x, eps):
    lo = _low_half()
    parts = []
    for j in range(x.shape[1] // LANES):
        t = x[:, j * LANES:(j + 1) * LANES]
        t2 = t * t
        s_lo = jnp.sum(jnp.where(lo, t2, 0.0), axis=-1, keepdims=True)
        s_hi = jnp.sum(jnp.where(lo, 0.0, t2), axis=-1, keepdims=True)
        r = jnp.where(lo, lax.rsqrt(s_lo * (1.0 / HEAD_DIM) + eps), lax.rsqrt(s_hi * (1.0 / HEAD_DIM) + eps))
        parts.append(t * r)
    return jnp.concatenate(parts, axis=-1)


def _split_heads(qp):
    lo = _low_half()
    zero = jnp.zeros_like(qp)
    return jnp.concatenate([jnp.where(lo, qp, zero), jnp.where(lo, zero, qp)], axis=0)


def _join_heads(x2):
    rows = x2.shape[0] // 2
    return jnp.where(_low_half(), x2[0:rows], x2[rows:])


def _proj_kernel(x_ref, nw_ref, w_ref, *rest, with_dt):
    rest = list(rest)
    wdt_ref = rest.pop(0) if with_dt else None
    o_ref = rest.pop(0)
    dt_ref = rest.pop(0) if with_dt else None
    h_ref = rest.pop(0)

    @pl.when(pl.program_id(1) == 0)
    def _():
        h_ref[...] = _rms(x_ref[...], nw_ref[...], NORM_EPS).astype(BF16)
        if with_dt:
            dt_ref[...] = _dot(h_ref[...], wdt_ref[...])

    o_ref[...] = _dot(h_ref[...], w_ref[...]).astype(o_ref.dtype)


def _proj(x, norm_w, w, *, tm, tn, out_dtype, n_out=None, w_dt=None):
    m, dm = x.shape
    n = w.shape[1] if n_out is None else n_out
    assert n % tn == 0 and m % tm == 0
    in_specs = [pl.BlockSpec((tm, dm), lambda i, j: (i, 0)),
                pl.BlockSpec((1, dm), lambda i, j: (0, 0)),
                pl.BlockSpec((dm, tn), lambda i, j: (0, j))]
    args = [x, norm_w.reshape(1, dm), w]
    out_shape = [jax.ShapeDtypeStruct((m, n), out_dtype)]
    out_specs = [pl.BlockSpec((tm, tn), lambda i, j: (i, j))]
    if w_dt is not None:
        in_specs.append(pl.BlockSpec((dm, LANES), lambda i, j: (0, 0)))
        args.append(w_dt)
        out_shape.append(jax.ShapeDtypeStruct((m, LANES), F32))
        out_specs.append(pl.BlockSpec((tm, LANES), lambda i, j: (i, 0)))
    outs = pl.pallas_call(
        functools.partial(_proj_kernel, with_dt=w_dt is not None),
        grid=(m // tm, n // tn), in_specs=in_specs, out_specs=out_specs, out_shape=out_shape,
        scratch_shapes=[pltpu.VMEM((tm, dm), BF16)],
        compiler_params=_params(("parallel", "arbitrary")),
        name="norm_proj",
    )(*args)
    return outs if w_dt is not None else outs[0]


N_TILES = ATT_IN // ATT_WIDTH
Q_TILES, K_TILES, V_TILES, GATE_TILE = (0, 3, 6), (1, 4, 7), (2, 5, 8), 9
STREAM_A = (0, 1, 3, 4, 6)
STREAM_B = (2, 5, 8, 9, 7)


def _lookup(j, table):
    out = jnp.int32(table[0])
    for t, v in enumerate(table[1:], 1):
        out = jnp.where(j == t, jnp.int32(v), out)
    return out


def _attn_proj_kernel(x0_ref, xnext_ref, nw_ref, wa_ref, wb_ref, ga_ref, gb_ref, oa_ref, ob_ref,
                      h0_ref, h1_ref, slab_ref):
    i = pl.program_id(0)
    j = pl.program_id(1)
    last = pl.num_programs(1) - 1
    tm = x0_ref.shape[0]
    run = tm // PERM

    def prepare(x, h_ref, first):
        n = x.shape[0]
        xn = _rms(x, nw_ref[...], NORM_EPS)
        for c in range(x.shape[1] // LANES):
            cols = slice(c * LANES, (c + 1) * LANES)
            slab_ref[0:n, :] = xn[:, cols]
            for r in range(PERM):
                start = r * run + first
                dst = pl.ds(start if isinstance(start, int) else pl.multiple_of(start, 16), n // PERM)
                h_ref[dst, cols] = slab_ref[pl.ds(r, n // PERM, stride=PERM), :].astype(BF16)

    @pl.when((i == 0) & (j == 0))
    def _():
        prepare(x0_ref[...], h0_ref, 0)

    def tile(h_ref, w_ref, gain_ref, o_ref, normed):
        res = _dot(h_ref[...], w_ref[...])
        if normed:
            res = _head_norm(res, NORM_EPS) * gain_ref[0]
        o_ref[...] = res.astype(o_ref.dtype).reshape(o_ref.shape)

    def step(h_cur, h_next):
        @pl.when(j < last)
        def _():
            tile(h_cur, wa_ref, ga_ref, oa_ref, True)
            tile(h_cur, wb_ref, gb_ref, ob_ref, False)
            piece = tm // last
            prepare(xnext_ref[pl.ds(pl.multiple_of(j * piece, piece), piece), :], h_next, j * (piece // PERM))

        @pl.when(j == last)
        def _():
            tile(h_cur, wa_ref, ga_ref, oa_ref, True)
            tile(h_cur, wb_ref, gb_ref, ob_ref, True)

    pl.when(i % 2 == 0)(functools.partial(step, h0_ref, h1_ref))
    pl.when(i % 2 == 1)(functools.partial(step, h1_ref, h0_ref))


def _attn_proj(x, norm_w, w, tile_gain, *, tm, seq):
    m, dm = x.shape
    per_b = seq // tm
    rows = tm // PERM
    n_steps = len(STREAM_A)
    w_spec = lambda table: pl.BlockSpec((dm, ATT_WIDTH), lambda i, j: (0, _lookup(j, table)))
    g_spec = lambda table: pl.BlockSpec((1, 1, ATT_WIDTH), lambda i, j: (_lookup(j, table), 0, 0))
    out_spec = pl.BlockSpec((None, PERM, rows, ATT_WIDTH), lambda i, j: (i // per_b, 0, i % per_b, j))
    out_sds = jax.ShapeDtypeStruct((m // seq, PERM, seq // PERM, n_steps * ATT_WIDTH), BF16)
    n_tiles = m // tm
    assert tm % (n_steps - 1) == 0 and (tm // (n_steps - 1)) % (PERM * 16) == 0
    return pl.pallas_call(
        _attn_proj_kernel,
        grid=(n_tiles, n_steps),
        in_specs=[pl.BlockSpec((tm, dm), lambda i, j: (0, 0)),
                  pl.BlockSpec((tm, dm), lambda i, j: (jnp.minimum(i + 1, n_tiles - 1), 0)),
                  pl.BlockSpec((1, dm), lambda i, j: (0, 0)),
                  w_spec(STREAM_A), w_spec(STREAM_B), g_spec(STREAM_A), g_spec(STREAM_B)],
        out_specs=[out_spec, out_spec], out_shape=[out_sds, out_sds],
        scratch_shapes=[pltpu.VMEM((tm, dm), BF16), pltpu.VMEM((tm, dm), BF16), pltpu.VMEM((tm, LANES), F32)],
        compiler_params=_params(("arbitrary", "arbitrary")),
        name="attn_proj",
    )(x, x, norm_w.reshape(1, dm), w, w, tile_gain, tile_gain)


def _outproj_kernel(a_ref, w_ref, x_ref, o_ref, *scratch, unperm):
    tm = o_ref.shape[0]
    a = a_ref[...].reshape(tm, a_ref.shape[-1]).astype(BF16)
    res = _dot(a, w_ref[...])
    if unperm:
        slab_ref, = scratch
        rows = tm // PERM
        for c in range(o_ref.shape[1] // LANES):
            cols = slice(c * LANES, (c + 1) * LANES)
            for r in range(PERM):
                slab_ref[pl.ds(r, rows, stride=PERM), :] = res[r * rows:(r + 1) * rows, cols]
            o_ref[:, cols] = x_ref[:, cols] + slab_ref[...]
    else:
        o_ref[...] = x_ref[...] + res


def _outproj(a, w, x, *, tm, perm_seq=None):
    m, n = x.shape
    k = w.shape[0]
    if perm_seq is not None:
        per_b = perm_seq // tm
        a_spec = pl.BlockSpec((None, PERM, tm // PERM, k), lambda i: (i // per_b, 0, i % per_b, 0))
    else:
        a_spec = pl.BlockSpec((tm, k), lambda i: (i, 0))
    return pl.pallas_call(
        functools.partial(_outproj_kernel, unperm=perm_seq is not None),
        grid=(m // tm,),
        in_specs=[a_spec, pl.BlockSpec((k, n), lambda i: (0, 0)), pl.BlockSpec((tm, n), lambda i: (i, 0))],
        out_specs=pl.BlockSpec((tm, n), lambda i: (i, 0)),
        out_shape=jax.ShapeDtypeStruct((m, n), F32),
        scratch_shapes=[pltpu.VMEM((tm, LANES), F32)] if perm_seq else [],
        compiler_params=_params(("parallel",)),
        name="out_proj",
    )(a, w, x)


ATT_SUB = 128


def _attn_group_kernel(*refs, n_chunks, n_sub, merge):
    it = iter(refs)
    q_ref, k_ref, v_ref = next(it), next(it), next(it)
    if merge:
        gate_ref = next(it)
        others = [(next(it), next(it)) for _ in range(len(DILATIONS) - 1)]
        o_out = next(it)
    else:
        o_out, lse_out = next(it), next(it)
    ks_ref, vs_ref = next(it), next(it)

    rc = q_ref.shape[1] // n_sub
    assert n_chunks * rc == ATT_SUB
    n = pl.program_id(2)
    narrow = rc % 16 != 0

    def piece(ref, u, cols=slice(None)):
        if narrow and ref.dtype == BF16:
            pair = 2 * (u // 2) * rc
            x = ref[:, pair:pair + 2 * rc, cols].astype(F32)[:, (u % 2) * rc:(u % 2 + 1) * rc, :]
        else:
            x = ref[:, u * rc:(u + 1) * rc, cols]
        return x.reshape(ATT_SUB, x.shape[-1])

    def unsubs(parts, dtype):
        parts = [p.reshape(n_chunks, rc, p.shape[-1]) for p in parts]
        if narrow:
            return jnp.concatenate(parts, axis=1).astype(dtype)
        return jnp.concatenate([p.astype(dtype) for p in parts], axis=1)

    @pl.when(n == 0)
    def _():
        ks_ref[0:ATT_SUB, :] = jnp.zeros((ATT_SUB, ATT_WIDTH), BF16)
        vs_ref[0:ATT_SUB, :] = jnp.zeros((ATT_SUB, ATT_WIDTH), BF16)

    for u in range(n_sub):
        ks_ref[(u + 1) * ATT_SUB:(u + 2) * ATT_SUB, :] = piece(k_ref, u).astype(BF16)
        vs_ref[(u + 1) * ATT_SUB:(u + 2) * ATT_SUB, :] = piece(v_ref, u).astype(BF16)

    qi = lax.broadcasted_iota(jnp.int32, (ATT_SUB, 2 * ATT_SUB), 0)
    kj = lax.broadcasted_iota(jnp.int32, (ATT_SUB, 2 * ATT_SUB), 1)
    kc = jnp.where(kj >= ATT_SUB, kj - ATT_SUB, kj)
    tq = n_chunks * (qi % rc) + qi // rc
    tk = n_chunks * (kc % rc) + kc // rc + jnp.where(kj >= ATT_SUB, 0, -ATT_SUB)
    dist = tq - tk
    in_band = (dist >= 0) & (dist <= WINDOW_STEPS)
    band = jnp.where(in_band, 0.0, MASKED)
    band_first = jnp.where(in_band & ((kj >= ATT_SUB) | (n > 0)), 0.0, MASKED)
    bias = [jnp.concatenate([x, x], axis=0) for x in (band_first, band)]

    for j in range(N_HEADS // 2):
        cols = slice(j * LANES, (j + 1) * LANES)
        o_parts, lse_parts = [], []
        for u in range(n_sub):
            keys = slice(u * ATT_SUB, (u + 2) * ATT_SUB)
            q2 = _split_heads(piece(q_ref, u, cols).astype(BF16))
            s = _dot_nt(q2, ks_ref[keys, cols]) + bias[min(u, 1)]
            m = jnp.max(s, axis=-1, keepdims=True)
            p = jnp.exp2(s - m)
            l = jnp.sum(p, axis=-1, keepdims=True)
            o = _join_heads(_dot(p.astype(BF16), vs_ref[keys, cols]) / l)
            lse = _join_heads(jnp.broadcast_to(m + jnp.log2(l), (2 * ATT_SUB, LANES)))
            if merge:
                o_g = [o] + [piece(o_ref, u, cols).astype(F32) for o_ref, _ in others]
                lse_g = [lse] + [piece(l_ref, u, cols) for _, l_ref in others]
                top = jnp.maximum(jnp.maximum(lse_g[0], lse_g[1]), lse_g[2])
                w = [jnp.exp2(x - top) for x in lse_g]
                o = (w[0] * o_g[0] + w[1] * o_g[1] + w[2] * o_g[2]) / (w[0] + w[1] + w[2])
                g = piece(gate_ref, u, cols).astype(F32)
                o = o * (g * _sigmoid(g))
            o_parts.append(o)
            lse_parts.append(lse)
        o_out[:, :, cols] = unsubs(o_parts, BF16)
        if not merge:
            lse_out[:, :, cols] = unsubs(lse_parts, F32)
    ks_ref[0:ATT_SUB, :] = ks_ref[n_sub * ATT_SUB:(n_sub + 1) * ATT_SUB, :]
    vs_ref[0:ATT_SUB, :] = vs_ref[n_sub * ATT_SUB:(n_sub + 1) * ATT_SUB, :]


def _attn_group(qkvg, g, *, n_sub, others=None):
    b, _, t, _ = qkvg[0].shape
    d = DILATIONS[g]
    n_chunks = PERM // d
    rc = ATT_SUB // n_chunks * n_sub
    nb = t // rc
    merge = others is not None
    view = lambda a: a.reshape(b, n_chunks, d, t, a.shape[-1])

    def spec(col=0):
        return pl.BlockSpec((None, n_chunks, None, rc, ATT_WIDTH), lambda bb, r, n: (bb, 0, r, n, col))

    in_specs, args = [], []

    def add_tile(tile):
        stream, table = (0, STREAM_A) if tile in STREAM_A else (1, STREAM_B)
        in_specs.append(spec(table.index(tile)))
        args.append(view(qkvg[stream]))

    for tile in (Q_TILES[g], K_TILES[g], V_TILES[g]):
        add_tile(tile)
    full = jax.ShapeDtypeStruct((b, n_chunks, d, t, ATT_WIDTH), BF16)
    if merge:
        add_tile(GATE_TILE)
        for o, lse in others:
            in_specs += [spec(), spec()]
            args += [view(o), view(lse)]
        out_shape, out_specs = [full], [spec()]
    else:
        out_shape = [full, jax.ShapeDtypeStruct(full.shape, F32)]
        out_specs = [spec(), spec()]
    outs = pl.pallas_call(
        functools.partial(_attn_group_kernel, n_chunks=n_chunks, n_sub=n_sub, merge=merge),
        grid=(b, d, nb), in_specs=in_specs, out_specs=out_specs, out_shape=out_shape,
        scratch_shapes=[pltpu.VMEM(((n_sub + 1) * ATT_SUB, ATT_WIDTH), BF16)] * 2,
        compiler_params=_params(("parallel", "parallel", "arbitrary")),
        name=f"attn_group{g}",
    )(*args)
    outs = [a.reshape(b, PERM, t, ATT_WIDTH) for a in outs]
    return outs[0] if merge else tuple(outs)


def _kv_tail_kernel(x_ref, nw_ref, wk_ref, wv_ref, kg_ref, o_ref):
    h = _rms(x_ref[...], nw_ref[...], NORM_EPS).astype(BF16)
    o_ref[0:ATT_WIDTH, :] = (_head_norm(_dot(h, wk_ref[...]), NORM_EPS) * kg_ref[...]).T
    o_ref[ATT_WIDTH:2 * ATT_WIDTH, :] = _dot(h, wv_ref[...]).T


def _kv_tail(x, norm_w, w, g, k_gain, *, b, keep, tm):
    seq = x.shape[0] // b
    first = (seq - keep) // tm
    w_spec = lambda col: pl.BlockSpec((D_MODEL, ATT_WIDTH), lambda bb, i: (0, col))
    return pl.pallas_call(
        _kv_tail_kernel,
        grid=(b, keep // tm),
        in_specs=[pl.BlockSpec((tm, D_MODEL), lambda bb, i: (bb * (seq // tm) + first + i, 0)),
                  pl.BlockSpec((1, D_MODEL), lambda bb, i: (0, 0)),
                  w_spec(3 * g + 1), w_spec(3 * g + 2),
                  pl.BlockSpec((1, ATT_WIDTH), lambda bb, i: (0, 0))],
        out_specs=pl.BlockSpec((None, 2 * ATT_WIDTH, tm), lambda bb, i: (bb, 0, i)),
        out_shape=jax.ShapeDtypeStruct((b, 2 * ATT_WIDTH, keep), F32),
        compiler_params=_params(("parallel", "parallel")),
        name="kv_tail",
    )(x, norm_w.reshape(1, D_MODEL), w, w, k_gain)


SAMPLE_HEADS = 8


def _sample_attn_kernel(*refs):
    n_g = len(DILATIONS)
    qkv = [refs[3 * g:3 * g + 3] for g in range(n_g)]
    gate_ref = refs[3 * n_g]
    c_refs = refs[3 * n_g + 1:4 * n_g + 1]
    qg_ref, kg_ref = refs[4 * n_g + 1:4 * n_g + 3]
    o_ref = refs[4 * n_g + 3]
    kn_refs = refs[4 * n_g + 4:]
    t_new = o_ref.shape[0]
    t_row = lax.broadcasted_iota(jnp.int32, (2 * t_new, 1), 0) % t_new
    t_col = lax.broadcasted_iota(jnp.int32, (1, t_new), 1)

    o_cols = []
    kn_cols = [[] for _ in DILATIONS]
    for j in range(SAMPLE_HEADS // 2):
        cols = slice(j * LANES, (j + 1) * LANES)
        o_g, m_g, l_g = [], [], []
        for g, d in enumerate(DILATIONS):
            q_ref, k_ref, v_ref = qkv[g]
            qm = _split_heads(_head_norm(q_ref[:, cols], NORM_EPS) * qg_ref[g:g + 1, :]).astype(BF16)
            kn = _head_norm(k_ref[:, cols], NORM_EPS) * kg_ref[g:g + 1, :]
            kn_cols[g].append(kn)
            length = c_refs[g].shape[2]
            s_c = _dot(qm, c_refs[g][0, cols, :].astype(BF16))
            s_n = _dot_nt(qm, kn.astype(BF16))
            back = length + t_row - lax.broadcasted_iota(jnp.int32, (1, length), 1)
            s_c = jnp.where((back % d == 0) & (back <= WINDOW_STEPS * d), s_c, MASKED)
            back = t_row - t_col
            s_n = jnp.where((back >= 0) & (back % d == 0) & (back <= WINDOW_STEPS * d), s_n, MASKED)
            m = jnp.maximum(jnp.max(s_c, axis=-1, keepdims=True), jnp.max(s_n, axis=-1, keepdims=True))
            p_c = jnp.exp(s_c - m)
            p_n = jnp.exp(s_n - m)
            l_g.append(jnp.sum(p_c, axis=-1, keepdims=True) + jnp.sum(p_n, axis=-1, keepdims=True))
            o_g.append(_dot_nt(p_c.astype(BF16), c_refs[g][1, cols, :].astype(BF16))
                       + _dot(p_n.astype(BF16), v_ref[:, cols].astype(BF16)))
            m_g.append(m)
        top = jnp.maximum(jnp.maximum(m_g[0], m_g[1]), m_g[2])
        f = [jnp.exp(m - top) for m in m_g]
        den = f[0] * l_g[0] + f[1] * l_g[1] + f[2] * l_g[2]
        o = _join_heads((f[0] * o_g[0] + f[1] * o_g[1] + f[2] * o_g[2]) / den)
        gate = gate_ref[:, cols]
        o_cols.append(o * (gate * _sigmoid(gate)))
    o_ref[...] = jnp.concatenate(o_cols, axis=-1)
    for g in range(n_g):
        kn_refs[g][...] = jnp.concatenate(kn_cols[g], axis=-1)


def _sample_attn(proj, caches_t, q_gain, k_gain):
    b = caches_t[0].shape[0]
    t_new = proj.shape[0] // b
    width = SAMPLE_HEADS * HEAD_DIM
    per_chunk = ATT_WIDTH // width
    col = lambda c: pl.BlockSpec((t_new, width), lambda bb, hc: (bb, c * per_chunk + hc))
    in_specs, args = [], []
    for g in range(len(DILATIONS)):
        in_specs += [col(3 * g), col(3 * g + 1), col(3 * g + 2)]
        args += [proj, proj, proj]
    in_specs.append(col(ATT_IN // ATT_WIDTH - 1))
    args.append(proj)
    for c in caches_t:
        in_specs.append(pl.BlockSpec((None, 2, width, c.shape[3]), lambda bb, hc: (bb, 0, hc, 0)))
        args.append(c)
    gain_spec = pl.BlockSpec((len(DILATIONS), LANES), lambda bb, hc: (0, 0))
    out_spec = pl.BlockSpec((t_new, width), lambda bb, hc: (bb, hc))
    out_sds = jax.ShapeDtypeStruct((b * t_new, ATT_WIDTH), F32)
    return pl.pallas_call(
        _sample_attn_kernel,
        grid=(b, per_chunk),
        in_specs=in_specs + [gain_spec, gain_spec],
        out_specs=[out_spec] * 4,
        out_shape=[out_sds] * 4,
        compiler_params=_params(("parallel", "parallel")),
        name="sample_attn",
    )(*args, q_gain, k_gain)


def _softplus(x):
    return jnp.maximum(x, 0.0) + jnp.log(1.0 + jnp.exp(-jnp.abs(x)))


def _conv_silu(zx_ref, xpad_ref, cw_ref, cb_ref, lc):
    pad = 8
    xpad_ref[pad:pad + lc, :] = zx_ref[:, SSM_D_INNER:SSM_MAIN].astype(F32)
    conv = cb_ref[...] + xpad_ref[pad - 3:pad - 3 + lc, :] * cw_ref[0:1, :]
    for j in range(1, SSM_CONV):
        conv = conv + xpad_ref[pad - 3 + j:pad - 3 + j + lc, :] * cw_ref[j:j + 1, :]
    tail = xpad_ref[pad + lc - 3:pad + lc, :]
    xpad_ref[pad - 3:pad, :] = tail
    return conv * _sigmoid(conv), tail


def _gate_norm(y, z, gn):
    y = y * (z * _sigmoid(z))
    parts = []
    for g in range(SSM_GROUPS):
        yg = y[:, g * SSM_GROUP_WIDTH:(g + 1) * SSM_GROUP_WIDTH]
        parts.append(yg * lax.rsqrt(jnp.mean(yg * yg, axis=-1, keepdims=True) + GATE_NORM_EPS))
    return jnp.concatenate(parts, axis=-1) * gn


def _cumsum_rows(x):
    rows = x.shape[0]
    tril = (lax.broadcasted_iota(jnp.int32, (rows, rows), 0)
            >= lax.broadcasted_iota(jnp.int32, (rows, rows), 1)).astype(BF16)
    h1 = x.astype(BF16)
    r1 = x - h1.astype(F32)
    h2 = r1.astype(BF16)
    h3 = (r1 - h2.astype(F32)).astype(BF16)
    return _dot(tril, h1) + _dot(tril, h2) + _dot(tril, h3)


def _ssd_prompt_kernel(zx_ref, dt_ref, cw_ref, cb_ref, dtb_ref, a_ref, dskip_ref, gn_ref, e_ref,
                       y_ref, nconv_ref, nssm_ref, carry_ref, st_ref):
    c = pl.program_id(1)
    last = pl.num_programs(1) - 1

    @pl.when(c == 0)
    def _():
        carry_ref[...] = jnp.zeros(carry_ref.shape, BF16)
        st_ref[...] = jnp.zeros(st_ref.shape, F32)

    for u in range(zx_ref.shape[0] // SSM_CHUNK):
        rows = pl.ds(u * SSM_CHUNK, SSM_CHUNK)
        _ssd_chunk(zx_ref.at[rows], dt_ref.at[rows], cw_ref, cb_ref, dtb_ref, a_ref, dskip_ref, gn_ref, e_ref,
                   y_ref.at[rows], carry_ref, st_ref)

    @pl.when(c == last)
    def _():
        n_carry = carry_ref.shape[0]
        nconv_ref[...] = carry_ref[...].astype(F32)[n_carry - (SSM_CONV - 1):n_carry]
        nssm_ref[...] = st_ref[...].T


def _ssd_chunk(zx_ref, dt_ref, cw_ref, cb_ref, dtb_ref, a_ref, dskip_ref, gn_ref, e_ref, y_ref, carry_ref, st_ref):
    lc = zx_ref.shape[0]
    n_carry = carry_ref.shape[0]
    src = lax.broadcasted_iota(jnp.int32, (lc, n_carry + lc), 1) - n_carry
    dst = lax.broadcasted_iota(jnp.int32, (lc, n_carry + lc), 0)
    taps = SSM_CONV - 1
    shift = jnp.concatenate([(src == dst - (taps - j)).astype(BF16) for j in range(taps)], axis=0)

    def conv_silu(start, width):
        cols = slice(start, start + width)
        raw = zx_ref[:, SSM_D_INNER + start:SSM_D_INNER + start + width]
        shifted = _dot(shift, jnp.concatenate([carry_ref[:, cols], raw], axis=0))
        conv = cb_ref[:, cols] + shifted[0:lc] * cw_ref[0:1, cols]
        for j in range(1, taps):
            conv = conv + shifted[j * lc:(j + 1) * lc] * cw_ref[j:j + 1, cols]
        conv = conv + raw.astype(F32) * cw_ref[taps:taps + 1, cols]
        return conv * _sigmoid(conv)

    dt = _softplus(dt_ref[...] + dtb_ref[...])
    a_cs = _cumsum_rows(dt * a_ref[...])
    a_log2 = a_cs * LOG2E
    a_log2_t = a_log2.T
    dt_t = dt.T
    a_last = a_cs[lc - 1:lc, :]
    to_end = (dt * jnp.exp(a_last - a_cs)).astype(BF16)
    from_start = jnp.exp(a_cs).astype(BF16)
    chunk_decay = jnp.broadcast_to(jnp.exp(a_last), (8, LANES))
    causal = (lax.broadcasted_iota(jnp.int32, (lc, lc), 0) >= lax.broadcasted_iota(jnp.int32, (lc, lc), 1))
    heads_per_group = SSM_HEADS // SSM_GROUPS
    head_w = SSM_D_INNER // SSM_HEADS
    quad = 4
    quad_w = quad * head_w
    lane_head = lax.broadcasted_iota(jnp.int32, (1, quad_w), 1) // head_w
    n_bc = SSM_GROUPS * SSM_STATE
    for g in range(SSM_GROUPS):
        gw = slice(g * SSM_GROUP_WIDTH, (g + 1) * SSM_GROUP_WIDTH)
        xs = conv_silu(g * SSM_GROUP_WIDTH, SSM_GROUP_WIDTH)
        bm = conv_silu(SSM_D_INNER + g * SSM_STATE, SSM_STATE).astype(BF16)
        cm = conv_silu(SSM_D_INNER + n_bc + g * SSM_STATE, SSM_STATE).astype(BF16)
        e = e_ref[:, gw]
        xs_b = xs.astype(BF16)
        xdte = (xs * _dot(to_end, e)).astype(BF16)
        s_prev = st_ref[:, gw]
        cb = _dot_nt(cm, bm)
        y = _dot(cm, s_prev.astype(BF16)) * _dot(from_start, e)
        st_ref[:, gw] = (s_prev * _dot2(chunk_decay, e)[0:1, :]
                         + lax.dot_general(bm, xdte, TN, preferred_element_type=F32))
        quad_out = []
        for qd in range(heads_per_group // quad):
            h0 = g * heads_per_group + quad * qd
            ms = []
            for h in range(h0, h0 + quad):
                seg = a_log2[:, h:h + 1] - a_log2_t[h:h + 1, :]
                ms.append((cb * jnp.where(causal, jnp.exp2(seg), 0.0) * dt_t[h:h + 1, :]).astype(BF16))
            xq = xs_b[:, qd * quad_w:(qd + 1) * quad_w]
            rhs = jnp.concatenate([jnp.where(lane_head == i, xq, jnp.zeros_like(xq)) for i in range(quad)], axis=0)
            quad_out.append(_dot(jnp.concatenate(ms, axis=1), rhs))
        y = y + jnp.concatenate(quad_out, axis=-1) + dskip_ref[:, gw] * xs
        z = zx_ref[:, gw].astype(F32)
        y = y * (z * _sigmoid(z))
        y = y * lax.rsqrt(jnp.mean(y * y, axis=-1, keepdims=True) + GATE_NORM_EPS) * gn_ref[:, gw]
        y_ref[:, gw] = y.astype(y_ref.dtype)

    carry_ref[...] = zx_ref[lc - n_carry:lc, SSM_D_INNER:SSM_MAIN]


def _ssd_prompt(zx, dt, b, params, *, chunks_per_step):
    m = zx.shape[0]
    rows = chunks_per_step * SSM_CHUNK
    nc = m // b // rows
    full = lambda x: pl.BlockSpec(x.shape, lambda bb, c: (0, 0))
    row = lambda w: pl.BlockSpec((rows, w), lambda bb, c: (bb * nc + c, 0))
    return pl.pallas_call(
        _ssd_prompt_kernel,
        grid=(b, nc),
        in_specs=[row(SSM_MAIN), row(LANES)] + [full(x) for x in params],
        out_specs=[row(SSM_D_INNER),
                   pl.BlockSpec((None, SSM_CONV - 1, SSM_CONV_DIM), lambda bb, c: (bb, 0, 0)),
                   pl.BlockSpec((None, SSM_D_INNER, SSM_STATE), lambda bb, c: (bb, 0, 0))],
        out_shape=[jax.ShapeDtypeStruct((m, SSM_D_INNER), BF16),
                   jax.ShapeDtypeStruct((b, SSM_CONV - 1, SSM_CONV_DIM), F32),
                   jax.ShapeDtypeStruct((b, SSM_D_INNER, SSM_STATE), F32)],
        scratch_shapes=[pltpu.VMEM((16, SSM_CONV_DIM), BF16),
                        pltpu.VMEM((SSM_STATE, SSM_D_INNER), F32)],
        compiler_params=_params(("parallel", "arbitrary")),
        name="ssd_prompt",
    )(zx, dt, *params)


def _ssd_sample_kernel(zx_ref, dt_ref, conv0_ref, s0_ref, cw_ref, cb_ref, dtb_ref, a_ref, dskip_ref, gn_ref,
                       e_ref, eg_ref, y_ref, nconv_ref, nssm_ref, xpad_ref):
    lc = zx_ref.shape[0]
    xpad_ref[0:5, :] = jnp.zeros((5, SSM_CONV_DIM), F32)
    xpad_ref[5:8, :] = conv0_ref[...]
    xbc, tail = _conv_silu(zx_ref, xpad_ref, cw_ref, cb_ref, lc)
    nconv_ref[...] = tail

    xs = xbc[:, 0:SSM_D_INNER]
    bm = xbc[:, SSM_D_INNER:SSM_D_INNER + SSM_GROUPS * SSM_STATE]
    cm = xbc[:, SSM_D_INNER + SSM_GROUPS * SSM_STATE:]
    e = e_ref[...]
    dt = _softplus(dt_ref[...] + dtb_ref[...])
    dta = dt * a_ref[...]
    rows = [dta[0:1, :]]
    for i in range(1, lc):
        rows.append(rows[-1] + dta[i:i + 1, :])
    a_cs = jnp.concatenate(rows, axis=0)
    a_last = rows[-1]
    xdt = xs * _dot2(dt, e)
    xdte = (xdt * _dot2(jnp.exp(a_last - a_cs), e)).astype(BF16)
    ea_e = _dot2(jnp.exp(a_cs), e)
    cd_e = _dot2(jnp.broadcast_to(jnp.exp(a_last), (8, LANES)), e)[0:1, :]

    a_l = jnp.concatenate([a_cs] * lc, axis=0)
    a_s = jnp.concatenate([jnp.broadcast_to(a_cs[s:s + 1, :], (lc, LANES)) for s in range(lc)], axis=0)
    l_idx = lax.broadcasted_iota(jnp.int32, (lc * lc, 1), 0) % lc
    s_idx = lax.broadcasted_iota(jnp.int32, (lc * lc, 1), 0) // lc
    decay = jnp.where(l_idx >= s_idx, jnp.exp(a_l - a_s), 0.0)
    cb_prod = jnp.concatenate([cm * bm[s:s + 1, :] for s in range(lc)], axis=0)
    mix = _dot2(_dot2(cb_prod, eg_ref[...]) * decay, e)
    y = ea_e * 0.0
    for s in range(lc):
        y = y + mix[s * lc:(s + 1) * lc, :] * xdt[s:s + 1, :]

    s_prev = s0_ref[...].T
    s_b = s_prev.astype(BF16)
    bm_b = bm.astype(BF16)
    cm_b = cm.astype(BF16)
    y_off, s_parts = [], []
    for g in range(SSM_GROUPS):
        gs = slice(g * SSM_STATE, (g + 1) * SSM_STATE)
        gw = slice(g * SSM_GROUP_WIDTH, (g + 1) * SSM_GROUP_WIDTH)
        y_off.append(_dot(cm_b[:, gs], s_b[:, gw]))
        s_parts.append(lax.dot_general(bm_b[:, gs], xdte[:, gw], TN, preferred_element_type=F32))
    y = y + jnp.concatenate(y_off, axis=-1) * ea_e + dskip_ref[...] * xs
    s_new = s_prev * cd_e + jnp.concatenate(s_parts, axis=-1)
    nssm_ref[...] = s_new.T
    z = zx_ref[:, 0:SSM_D_INNER]
    y_ref[...] = _gate_norm(y, z, gn_ref[...])


def _ssd_sample(zx, dt, conv0, s0, params):
    b = conv0.shape[0]
    t_new = zx.shape[0] // b
    full = lambda x: pl.BlockSpec(x.shape, lambda bb: (0, 0))
    row = lambda w: pl.BlockSpec((t_new, w), lambda bb: (bb, 0))
    conv_spec = pl.BlockSpec((None, SSM_CONV - 1, SSM_CONV_DIM), lambda bb: (bb, 0, 0))
    state_spec = pl.BlockSpec((None, SSM_D_INNER, SSM_STATE), lambda bb: (bb, 0, 0))
    return pl.pallas_call(
        _ssd_sample_kernel,
        grid=(b,),
        in_specs=[row(SSM_MAIN), row(LANES), conv_spec, state_spec] + [full(x) for x in params],
        out_specs=[row(SSM_D_INNER), conv_spec, state_spec],
        out_shape=[jax.ShapeDtypeStruct((b * t_new, SSM_D_INNER), F32),
                   jax.ShapeDtypeStruct((b, SSM_CONV - 1, SSM_CONV_DIM), F32),
                   jax.ShapeDtypeStruct((b, SSM_D_INNER, SSM_STATE), F32)],
        scratch_shapes=[pltpu.VMEM((8 + t_new, SSM_CONV_DIM), F32)],
        compiler_params=_params(("parallel",)),
        name="ssd_sample",
    )(zx, dt, conv0, s0, *params)


def _one_hot_expand(n_rows, n_cols, group):
    r = lax.broadcasted_iota(jnp.int32, (n_rows, n_cols), 0)
    c = lax.broadcasted_iota(jnp.int32, (n_rows, n_cols), 1)
    return (c // group == r).astype(BF16)


def _kv_rows(feature_major, b):
    tokens = feature_major.shape[-1]
    x = feature_major.reshape(1, b, 2, N_HEADS, HEAD_DIM, tokens)
    return jnp.transpose(x, (0, 1, 5, 2, 3, 4))


def kernel(x_prompt, x_sample, cache_kv_g0, cache_kv_g1, cache_kv_g2, state_conv, state_ssm, attn_norm, attn_w_in, attn_q_gain, attn_k_gain, attn_w_out, ssm_norm, ssm_w_in, ssm_conv_w, ssm_conv_b, ssm_dt_bias, ssm_A_log, ssm_D, ssm_gate_norm, ssm_w_out):
    b, seq, dm = x_prompt.shape
    sb, st, _ = x_sample.shape
    xp = x_prompt.reshape(b * seq, dm)
    xs = x_sample.reshape(sb * st, dm)

    w_in = attn_w_in[0].astype(BF16)
    w_out = attn_w_out[0].astype(BF16)
    scale = HEAD_DIM ** -0.5
    q_gain3 = jnp.tile(attn_q_gain[0], (1, N_HEADS)) * scale
    k_gain3 = jnp.tile(attn_k_gain[0], (1, N_HEADS))

    tile_gain = jnp.ones((N_TILES, 1, ATT_WIDTH), F32)
    for g in range(len(DILATIONS)):
        tile_gain = tile_gain.at[Q_TILES[g], 0].set(q_gain3[g] * LOG2E).at[K_TILES[g], 0].set(k_gain3[g])
    qkvg = _attn_proj(xp, attn_norm[0], w_in, tile_gain, tm=1024, seq=seq)
    part2 = _attn_group(qkvg, 2, n_sub=2)
    part1 = _attn_group(qkvg, 1, n_sub=4)
    o_gated = _attn_group(qkvg, 0, n_sub=4, others=[part1, part2])
    y1p = _outproj(o_gated, w_out, xp, tm=1024, perm_seq=seq)
    kv_p = []
    for g, d in enumerate(DILATIONS):
        keep = min(d * WINDOW_STEPS, seq)
        kv_p.append(_kv_rows(_kv_tail(xp, attn_norm[0], w_in, g, k_gain3[g:g + 1], b=b, keep=keep,
                                      tm=min(keep, 512)), b))

    proj_s = _proj(xs, attn_norm[0], w_in, tm=sb * st, tn=1024, out_dtype=F32)
    caches_t = [jnp.transpose(c[0], (0, 2, 3, 4, 1)).reshape(sb, 2, ATT_WIDTH, c.shape[2])
                for c in (cache_kv_g0, cache_kv_g1, cache_kv_g2)]
    o_s, kn0, kn1, kn2 = _sample_attn(proj_s, caches_t, q_gain3[:, :LANES], k_gain3[:, :LANES])
    y1s = _outproj(o_s, w_out, xs, tm=sb * st)
    kv_s = []
    for g, kn in enumerate((kn0, kn1, kn2)):
        v = proj_s[:, (3 * g + 2) * ATT_WIDTH:(3 * g + 3) * ATT_WIDTH]
        kv_s.append(jnp.stack([kn, v], axis=1).reshape(1, sb, st, 2, N_HEADS, HEAD_DIM))

    w_in2 = ssm_w_in[0].astype(BF16)
    w_dt = jnp.pad(w_in2[:, SSM_MAIN:], ((0, 0), (0, LANES - SSM_HEADS)))
    w_out2 = ssm_w_out[0].astype(BF16)
    pad_h = lambda v: jnp.pad(v.astype(F32), (0, LANES - SSM_HEADS)).reshape(1, LANES)
    e32 = _one_hot_expand(LANES, SSM_D_INNER, SSM_D_INNER // SSM_HEADS)
    gate_gain = ssm_gate_norm[0].reshape(1, -1)
    scan_params = (ssm_conv_w[0], ssm_conv_b[0].reshape(1, -1), pad_h(ssm_dt_bias[0]),
                   pad_h(-jnp.exp(ssm_A_log[0].astype(F32))),
                   jnp.repeat(ssm_D[0].astype(F32), SSM_D_INNER // SSM_HEADS).reshape(1, -1))

    zx_p, dt_p = _proj(y1p, ssm_norm[0], w_in2, tm=1024, tn=SSM_MAIN // 2, n_out=SSM_MAIN, out_dtype=BF16,
                       w_dt=w_dt)
    yg_p, conv_p, ssm_p = _ssd_prompt(zx_p, dt_p, b, scan_params + (gate_gain, e32), chunks_per_step=2)
    y2p = _outproj(yg_p, w_out2, y1p, tm=1024)

    eg = (lax.broadcasted_iota(jnp.int32, (SSM_GROUPS * SSM_STATE, LANES), 0) // SSM_STATE
          == lax.broadcasted_iota(jnp.int32, (SSM_GROUPS * SSM_STATE, LANES), 1) // (SSM_HEADS // SSM_GROUPS))
    eg = (eg & (lax.broadcasted_iota(jnp.int32, eg.shape, 1) < SSM_HEADS)).astype(BF16)
    zx_s, dt_s = _proj(y1s, ssm_norm[0], w_in2, tm=sb * st, tn=SSM_MAIN // 2, n_out=SSM_MAIN, out_dtype=F32,
                       w_dt=w_dt)
    yg_s, conv_s, ssm_s = _ssd_sample(zx_s, dt_s, state_conv[0], state_ssm[0].reshape(sb, SSM_D_INNER, SSM_STATE),
                                      scan_params + (gate_gain, e32, eg))
    y2s = _outproj(yg_s, w_out2, y1s, tm=sb * st)

    hp = SSM_D_INNER // SSM_HEADS
    return (y2p.reshape(b, seq, dm), y2s.reshape(sb, st, dm),
            kv_p[0], kv_p[1], kv_p[2], kv_s[0], kv_s[1], kv_s[2],
            conv_p[None], conv_s[None],
            ssm_p.reshape(1, b, SSM_HEADS, hp, SSM_STATE), ssm_s.reshape(1, sb, SSM_HEADS, hp, SSM_STATE))
```

```python
import functools

import jax
import jax.numpy as jnp
from jax import lax
from jax.experimental import pallas as pl
from jax.experimental.pallas import tpu as pltpu

F32 = jnp.float32
BF16 = jnp.bfloat16

D_MODEL = 1024
N_HEADS = 16
HEAD_DIM = 64
ATT_WIDTH = N_HEADS * HEAD_DIM
DILATIONS = (1, 4, 16)
WINDOW_STEPS = 128
ATT_IN = 10 * ATT_WIDTH
PERM = 16

SSM_D_INNER = 2048
SSM_HEADS = 32
SSM_STATE = 128
SSM_GROUPS = 4
SSM_GROUP_WIDTH = SSM_D_INNER // SSM_GROUPS
SSM_CONV = 4
SSM_CONV_DIM = SSM_D_INNER + 2 * SSM_GROUPS * SSM_STATE
SSM_MAIN = SSM_D_INNER + SSM_CONV_DIM
SSM_CHUNK = 128

NORM_EPS = 1e-6
GATE_NORM_EPS = 1e-5
MASKED = -1e30
LOG2E = 1.4426950408889634

LANES = 128
VMEM_LIMIT = 56 * 1024 * 1024

NT = (((1,), (1,)), ((), ()))
TN = (((0,), (0,)), ((), ()))


def _params(semantics):
    return pltpu.CompilerParams(dimension_semantics=semantics, vmem_limit_bytes=VMEM_LIMIT)


def _rms(x, w, eps):
    return x * lax.rsqrt(jnp.mean(x * x, axis=-1, keepdims=True) + eps) * w


def _sigmoid(x):
    return 1.0 / (1.0 + jnp.exp(-x))


def _split2(v):
    hi = v.astype(BF16)
    lo = (v - hi.astype(F32)).astype(BF16)
    return hi, lo


def _dot(a, b):
    return jnp.dot(a, b, preferred_element_type=F32)


def _dot_nt(a, b):
    return lax.dot_general(a, b, NT, preferred_element_type=F32)


def _dot2(v, e):
    hi, lo = _split2(v)
    return _dot(hi, e) + _dot(lo, e)


def _low_half():
    return lax.broadcasted_iota(jnp.int32, (1, LANES), 1) < HEAD_DIM


def _head_norm(x, eps):
    lo = _low_half()
    parts = []
    for j in range(x.shape[1] // LANES):
        t = x[:, j * LANES:(j + 1) * LANES]
        t2 = t * t
        s_lo = jnp.sum(jnp.where(lo, t2, 0.0), axis=-1, keepdims=True)
        s_hi = jnp.sum(jnp.where(lo, 0.0, t2), axis=-1, keepdims=True)
        r = jnp.where(lo, lax.rsqrt(s_lo * (1.0 / HEAD_DIM) + eps), lax.rsqrt(s_hi * (1.0 / HEAD_DIM) + eps))
        parts.append(t * r)
    return jnp.concatenate(parts, axis=-1)


def _split_heads(qp):
    lo = _low_half()
    zero = jnp.zeros_like(qp)
    return jnp.concatenate([jnp.where(lo, qp, zero), jnp.where(lo, zero, qp)], axis=0)


def _join_heads(x2):
    rows = x2.shape[0] // 2
    return jnp.where(_low_half(), x2[0:rows], x2[rows:])


def _proj_kernel(x_ref, nw_ref, w_ref, *rest, with_dt):
    rest = list(rest)
    wdt_ref = rest.pop(0) if with_dt else None
    o_ref = rest.pop(0)
    dt_ref = rest.pop(0) if with_dt else None
    h_ref = rest.pop(0)

    @pl.when(pl.program_id(1) == 0)
    def _():
        h_ref[...] = _rms(x_ref[...], nw_ref[...], NORM_EPS).astype(BF16)
        if with_dt:
            dt_ref[...] = _dot(h_ref[...], wdt_ref[...])

    o_ref[...] = _dot(h_ref[...], w_ref[...]).astype(o_ref.dtype)


def _proj(x, norm_w, w, *, tm, tn, out_dtype, n_out=None, w_dt=None):
    m, dm = x.shape
    n = w.shape[1] if n_out is None else n_out
    assert n % tn == 0 and m % tm == 0
    in_specs = [pl.BlockSpec((tm, dm), lambda i, j: (i, 0)),
                pl.BlockSpec((1, dm), lambda i, j: (0, 0)),
                pl.BlockSpec((dm, tn), lambda i, j: (0, j))]
    args = [x, norm_w.reshape(1, dm), w]
    out_shape = [jax.ShapeDtypeStruct((m, n), out_dtype)]
    out_specs = [pl.BlockSpec((tm, tn), lambda i, j: (i, j))]
    if w_dt is not None:
        in_specs.append(pl.BlockSpec((dm, LANES), lambda i, j: (0, 0)))
        args.append(w_dt)
        out_shape.append(jax.ShapeDtypeStruct((m, LANES), F32))
        out_specs.append(pl.BlockSpec((tm, LANES), lambda i, j: (i, 0)))
    outs = pl.pallas_call(
        functools.partial(_proj_kernel, with_dt=w_dt is not None),
        grid=(m // tm, n // tn), in_specs=in_specs, out_specs=out_specs, out_shape=out_shape,
        scratch_shapes=[pltpu.VMEM((tm, dm), BF16)],
        compiler_params=_params(("parallel", "arbitrary")),
        name="norm_proj",
    )(*args)
    return outs if w_dt is not None else outs[0]


N_TILES = ATT_IN // ATT_WIDTH
Q_TILES, K_TILES, V_TILES, GATE_TILE = (0, 3, 6), (1, 4, 7), (2, 5, 8), 9
STREAM_A = (0, 1, 3, 4, 6)
STREAM_B = (2, 5, 8, 9, 7)


def _lookup(j, table):
    out = jnp.int32(table[0])
    for t, v in enumerate(table[1:], 1):
        out = jnp.where(j == t, jnp.int32(v), out)
    return out


def _attn_proj_kernel(x_ref, nw_ref, wa_ref, wb_ref, ga_ref, gb_ref, oa_ref, ob_ref, h_ref, slab_ref):
    j = pl.program_id(1)

    @pl.when(j == 0)
    def _():
        xn = _rms(x_ref[...], nw_ref[...], NORM_EPS)
        rows = x_ref.shape[0] // PERM
        for c in range(x_ref.shape[1] // LANES):
            cols = slice(c * LANES, (c + 1) * LANES)
            slab_ref[...] = xn[:, cols]
            for r in range(PERM):
                h_ref[r * rows:(r + 1) * rows, cols] = slab_ref[pl.ds(r, rows, stride=PERM), :].astype(BF16)

    def tile(w_ref, gain_ref, o_ref, normed):
        res = _dot(h_ref[...], w_ref[...])
        if normed:
            res = _head_norm(res, NORM_EPS) * gain_ref[0]
        o_ref[...] = res.astype(o_ref.dtype).reshape(o_ref.shape)

    last = pl.num_programs(1) - 1

    @pl.when(j < last)
    def _():
        tile(wa_ref, ga_ref, oa_ref, True)
        tile(wb_ref, gb_ref, ob_ref, False)

    @pl.when(j == last)
    def _():
        tile(wa_ref, ga_ref, oa_ref, True)
        tile(wb_ref, gb_ref, ob_ref, True)


def _attn_proj(x, norm_w, w, tile_gain, *, tm, seq):
    m, dm = x.shape
    per_b = seq // tm
    rows = tm // PERM
    n_steps = len(STREAM_A)
    w_spec = lambda table: pl.BlockSpec((dm, ATT_WIDTH), lambda i, j: (0, _lookup(j, table)))
    g_spec = lambda table: pl.BlockSpec((1, 1, ATT_WIDTH), lambda i, j: (_lookup(j, table), 0, 0))
    out_spec = pl.BlockSpec((None, PERM, rows, ATT_WIDTH), lambda i, j: (i // per_b, 0, i % per_b, j))
    out_sds = jax.ShapeDtypeStruct((m // seq, PERM, seq // PERM, n_steps * ATT_WIDTH), BF16)
    return pl.pallas_call(
        _attn_proj_kernel,
        grid=(m // tm, n_steps),
        in_specs=[pl.BlockSpec((tm, dm), lambda i, j: (i, 0)), pl.BlockSpec((1, dm), lambda i, j: (0, 0)),
                  w_spec(STREAM_A), w_spec(STREAM_B), g_spec(STREAM_A), g_spec(STREAM_B)],
        out_specs=[out_spec, out_spec], out_shape=[out_sds, out_sds],
        scratch_shapes=[pltpu.VMEM((tm, dm), BF16), pltpu.VMEM((tm, LANES), F32)],
        compiler_params=_params(("parallel", "arbitrary")),
        name="attn_proj",
    )(x, norm_w.reshape(1, dm), w, w, tile_gain, tile_gain)


def _outproj_kernel(a_ref, w_ref, x_ref, o_ref, *scratch, unperm):
    tm = o_ref.shape[0]
    a = a_ref[...].reshape(tm, a_ref.shape[-1]).astype(BF16)
    res = _dot(a, w_ref[...])
    if unperm:
        slab_ref, = scratch
        rows = tm // PERM
        for c in range(o_ref.shape[1] // LANES):
            cols = slice(c * LANES, (c + 1) * LANES)
            for r in range(PERM):
                slab_ref[pl.ds(r, rows, stride=PERM), :] = res[r * rows:(r + 1) * rows, cols]
            o_ref[:, cols] = x_ref[:, cols] + slab_ref[...]
    else:
        o_ref[...] = x_ref[...] + res


def _outproj(a, w, x, *, tm, perm_seq=None):
    m, n = x.shape
    k = w.shape[0]
    if perm_seq is not None:
        per_b = perm_seq // tm
        a_spec = pl.BlockSpec((None, PERM, tm // PERM, k), lambda i: (i // per_b, 0, i % per_b, 0))
    else:
        a_spec = pl.BlockSpec((tm, k), lambda i: (i, 0))
    return pl.pallas_call(
        functools.partial(_outproj_kernel, unperm=perm_seq is not None),
        grid=(m // tm,),
        in_specs=[a_spec, pl.BlockSpec((k, n), lambda i: (0, 0)), pl.BlockSpec((tm, n), lambda i: (i, 0))],
        out_specs=pl.BlockSpec((tm, n), lambda i: (i, 0)),
        out_shape=jax.ShapeDtypeStruct((m, n), F32),
        scratch_shapes=[pltpu.VMEM((tm, LANES), F32)] if perm_seq else [],
        compiler_params=_params(("parallel",)),
        name="out_proj",
    )(a, w, x)


ATT_SUB = 128


def _attn_group_kernel(*refs, n_chunks, n_sub, merge):
    it = iter(refs)
    q_ref, k_ref, v_ref = next(it), next(it), next(it)
    if merge:
        gate_ref = next(it)
        others = [(next(it), next(it)) for _ in range(len(DILATIONS) - 1)]
        o_out = next(it)
    else:
        o_out, lse_out = next(it), next(it)
    ks_ref, vs_ref = next(it), next(it)

    rc = q_ref.shape[1] // n_sub
    assert n_chunks * rc == ATT_SUB
    n = pl.program_id(2)
    narrow = rc % 16 != 0

    def piece(ref, u, cols=slice(None)):
        if narrow and ref.dtype == BF16:
            pair = 2 * (u // 2) * rc
            x = ref[:, pair:pair + 2 * rc, cols].astype(F32)[:, (u % 2) * rc:(u % 2 + 1) * rc, :]
        else:
            x = ref[:, u * rc:(u + 1) * rc, cols]
        return x.reshape(ATT_SUB, x.shape[-1])

    def unsubs(parts, dtype):
        parts = [p.reshape(n_chunks, rc, p.shape[-1]) for p in parts]
        if narrow:
            return jnp.concatenate(parts, axis=1).astype(dtype)
        return jnp.concatenate([p.astype(dtype) for p in parts], axis=1)

    @pl.when(n == 0)
    def _():
        ks_ref[0:ATT_SUB, :] = jnp.zeros((ATT_SUB, ATT_WIDTH), BF16)
        vs_ref[0:ATT_SUB, :] = jnp.zeros((ATT_SUB, ATT_WIDTH), BF16)

    for u in range(n_sub):
        ks_ref[(u + 1) * ATT_SUB:(u + 2) * ATT_SUB, :] = piece(k_ref, u).astype(BF16)
        vs_ref[(u + 1) * ATT_SUB:(u + 2) * ATT_SUB, :] = piece(v_ref, u).astype(BF16)

    qi = lax.broadcasted_iota(jnp.int32, (ATT_SUB, 2 * ATT_SUB), 0)
    kj = lax.broadcasted_iota(jnp.int32, (ATT_SUB, 2 * ATT_SUB), 1)
    kc = jnp.where(kj >= ATT_SUB, kj - ATT_SUB, kj)
    tq = n_chunks * (qi % rc) + qi // rc
    tk = n_chunks * (kc % rc) + kc // rc + jnp.where(kj >= ATT_SUB, 0, -ATT_SUB)
    dist = tq - tk
    in_band = (dist >= 0) & (dist <= WINDOW_STEPS)
    band = jnp.where(in_band, 0.0, MASKED)
    band_first = jnp.where(in_band & ((kj >= ATT_SUB) | (n > 0)), 0.0, MASKED)
    bias = [jnp.concatenate([x, x], axis=0) for x in (band_first, band)]
    ones = jnp.ones((2 * ATT_SUB, LANES), BF16)

    for j in range(N_HEADS // 2):
        cols = slice(j * LANES, (j + 1) * LANES)
        o_parts, lse_parts = [], []
        for u in range(n_sub):
            keys = slice(u * ATT_SUB, (u + 2) * ATT_SUB)
            q2 = _split_heads(piece(q_ref, u, cols).astype(BF16))
            s = _dot_nt(q2, ks_ref[keys, cols]) + bias[min(u, 1)]
            m = jnp.max(s, axis=-1, keepdims=True)
            p = jnp.exp2(s - m).astype(BF16)
            pv = _dot(p, jnp.concatenate([vs_ref[keys, cols], ones], axis=1))
            l = pv[:, LANES:]
            o = _join_heads(pv[:, :LANES] / l)
            lse = _join_heads(m + jnp.log2(l))
            if merge:
                o_g = [o] + [piece(o_ref, u, cols).astype(F32) for o_ref, _ in others]
                lse_g = [lse] + [piece(l_ref, u, cols) for _, l_ref in others]
                top = jnp.maximum(jnp.maximum(lse_g[0], lse_g[1]), lse_g[2])
                w = [jnp.exp2(x - top) for x in lse_g]
                o = (w[0] * o_g[0] + w[1] * o_g[1] + w[2] * o_g[2]) / (w[0] + w[1] + w[2])
                g = piece(gate_ref, u, cols).astype(F32)
                o = o * (g * _sigmoid(g))
            o_parts.append(o)
            lse_parts.append(lse)
        o_out[:, :, cols] = unsubs(o_parts, BF16)
        if not merge:
            lse_out[:, :, cols] = unsubs(lse_parts, F32)
    ks_ref[0:ATT_SUB, :] = ks_ref[n_sub * ATT_SUB:(n_sub + 1) * ATT_SUB, :]
    vs_ref[0:ATT_SUB, :] = vs_ref[n_sub * ATT_SUB:(n_sub + 1) * ATT_SUB, :]


def _attn_group(qkvg, g, *, n_sub, others=None):
    b, _, t, _ = qkvg[0].shape
    d = DILATIONS[g]
    n_chunks = PERM // d
    rc = ATT_SUB // n_chunks * n_sub
    nb = t // rc
    merge = others is not None
    view = lambda a: a.reshape(b, n_chunks, d, t, a.shape[-1])

    def spec(col=0):
        return pl.BlockSpec((None, n_chunks, None, rc, ATT_WIDTH), lambda bb, r, n: (bb, 0, r, n, col))

    in_specs, args = [], []

    def add_tile(tile):
        stream, table = (0, STREAM_A) if tile in STREAM_A else (1, STREAM_B)
        in_specs.append(spec(table.index(tile)))
        args.append(view(qkvg[stream]))

    for tile in (Q_TILES[g], K_TILES[g], V_TILES[g]):
        add_tile(tile)
    full = jax.ShapeDtypeStruct((b, n_chunks, d, t, ATT_WIDTH), BF16)
    if merge:
        add_tile(GATE_TILE)
        for o, lse in others:
            in_specs += [spec(), spec()]
            args += [view(o), view(lse)]
        out_shape, out_specs = [full], [spec()]
    else:
        out_shape = [full, jax.ShapeDtypeStruct(full.shape, F32)]
        out_specs = [spec(), spec()]
    outs = pl.pallas_call(
        functools.partial(_attn_group_kernel, n_chunks=n_chunks, n_sub=n_sub, merge=merge),
        grid=(b, d, nb), in_specs=in_specs, out_specs=out_specs, out_shape=out_shape,
        scratch_shapes=[pltpu.VMEM(((n_sub + 1) * ATT_SUB, ATT_WIDTH), BF16)] * 2,
        compiler_params=_params(("parallel", "parallel", "arbitrary")),
        name=f"attn_group{g}",
    )(*args)
    outs = [a.reshape(b, PERM, t, ATT_WIDTH) for a in outs]
    return outs[0] if merge else tuple(outs)


def _kv_tail_kernel(x_ref, nw_ref, wk_ref, wv_ref, kg_ref, o_ref):
    h = _rms(x_ref[...], nw_ref[...], NORM_EPS).astype(BF16)
    o_ref[0:ATT_WIDTH, :] = (_head_norm(_dot(h, wk_ref[...]), NORM_EPS) * kg_ref[...]).T
    o_ref[ATT_WIDTH:2 * ATT_WIDTH, :] = _dot(h, wv_ref[...]).T


def _kv_tail(x, norm_w, w, g, k_gain, *, b, keep, tm):
    seq = x.shape[0] // b
    first = (seq - keep) // tm
    w_spec = lambda col: pl.BlockSpec((D_MODEL, ATT_WIDTH), lambda bb, i: (0, col))
    return pl.pallas_call(
        _kv_tail_kernel,
        grid=(b, keep // tm),
        in_specs=[pl.BlockSpec((tm, D_MODEL), lambda bb, i: (bb * (seq // tm) + first + i, 0)),
                  pl.BlockSpec((1, D_MODEL), lambda bb, i: (0, 0)),
                  w_spec(3 * g + 1), w_spec(3 * g + 2),
                  pl.BlockSpec((1, ATT_WIDTH), lambda bb, i: (0, 0))],
        out_specs=pl.BlockSpec((None, 2 * ATT_WIDTH, tm), lambda bb, i: (bb, 0, i)),
        out_shape=jax.ShapeDtypeStruct((b, 2 * ATT_WIDTH, keep), F32),
        compiler_params=_params(("parallel", "parallel")),
        name="kv_tail",
    )(x, norm_w.reshape(1, D_MODEL), w, w, k_gain)


SAMPLE_HEADS = 8


def _sample_attn_kernel(*refs):
    n_g = len(DILATIONS)
    qkv = [refs[3 * g:3 * g + 3] for g in range(n_g)]
    gate_ref = refs[3 * n_g]
    c_refs = refs[3 * n_g + 1:4 * n_g + 1]
    qg_ref, kg_ref = refs[4 * n_g + 1:4 * n_g + 3]
    o_ref = refs[4 * n_g + 3]
    kn_refs = refs[4 * n_g + 4:]
    t_new = o_ref.shape[0]
    t_row = lax.broadcasted_iota(jnp.int32, (2 * t_new, 1), 0) % t_new
    t_col = lax.broadcasted_iota(jnp.int32, (1, t_new), 1)

    o_cols = []
    kn_cols = [[] for _ in DILATIONS]
    for j in range(SAMPLE_HEADS // 2):
        cols = slice(j * LANES, (j + 1) * LANES)
        o_g, m_g, l_g = [], [], []
        for g, d in enumerate(DILATIONS):
            q_ref, k_ref, v_ref = qkv[g]
            qm = _split_heads(_head_norm(q_ref[:, cols], NORM_EPS) * qg_ref[g:g + 1, :]).astype(BF16)
            kn = _head_norm(k_ref[:, cols], NORM_EPS) * kg_ref[g:g + 1, :]
            kn_cols[g].append(kn)
            length = c_refs[g].shape[2]
            s_c = _dot(qm, c_refs[g][0, cols, :].astype(BF16))
            s_n = _dot_nt(qm, kn.astype(BF16))
            back = length + t_row - lax.broadcasted_iota(jnp.int32, (1, length), 1)
            s_c = jnp.where((back % d == 0) & (back <= WINDOW_STEPS * d), s_c, MASKED)
            back = t_row - t_col
            s_n = jnp.where((back >= 0) & (back % d == 0) & (back <= WINDOW_STEPS * d), s_n, MASKED)
            m = jnp.maximum(jnp.max(s_c, axis=-1, keepdims=True), jnp.max(s_n, axis=-1, keepdims=True))
            p_c = jnp.exp(s_c - m)
            p_n = jnp.exp(s_n - m)
            l_g.append(jnp.sum(p_c, axis=-1, keepdims=True) + jnp.sum(p_n, axis=-1, keepdims=True))
            o_g.append(_dot_nt(p_c.astype(BF16), c_refs[g][1, cols, :].astype(BF16))
                       + _dot(p_n.astype(BF16), v_ref[:, cols].astype(BF16)))
            m_g.append(m)
        top = jnp.maximum(jnp.maximum(m_g[0], m_g[1]), m_g[2])
        f = [jnp.exp(m - top) for m in m_g]
        den = f[0] * l_g[0] + f[1] * l_g[1] + f[2] * l_g[2]
        o = _join_heads((f[0] * o_g[0] + f[1] * o_g[1] + f[2] * o_g[2]) / den)
        gate = gate_ref[:, cols]
        o_cols.append(o * (gate * _sigmoid(gate)))
    o_ref[...] = jnp.concatenate(o_cols, axis=-1)
    for g in range(n_g):
        kn_refs[g][...] = jnp.concatenate(kn_cols[g], axis=-1)


def _sample_attn(proj, caches_t, q_gain, k_gain):
    b = caches_t[0].shape[0]
    t_new = proj.shape[0] // b
    width = SAMPLE_HEADS * HEAD_DIM
    per_chunk = ATT_WIDTH // width
    col = lambda c: pl.BlockSpec((t_new, width), lambda bb, hc: (bb, c * per_chunk + hc))
    in_specs, args = [], []
    for g in range(len(DILATIONS)):
        in_specs += [col(3 * g), col(3 * g + 1), col(3 * g + 2)]
        args += [proj, proj, proj]
    in_specs.append(col(ATT_IN // ATT_WIDTH - 1))
    args.append(proj)
    for c in caches_t:
        in_specs.append(pl.BlockSpec((None, 2, width, c.shape[3]), lambda bb, hc: (bb, 0, hc, 0)))
        args.append(c)
    gain_spec = pl.BlockSpec((len(DILATIONS), LANES), lambda bb, hc: (0, 0))
    out_spec = pl.BlockSpec((t_new, width), lambda bb, hc: (bb, hc))
    out_sds = jax.ShapeDtypeStruct((b * t_new, ATT_WIDTH), F32)
    return pl.pallas_call(
        _sample_attn_kernel,
        grid=(b, per_chunk),
        in_specs=in_specs + [gain_spec, gain_spec],
        out_specs=[out_spec] * 4,
        out_shape=[out_sds] * 4,
        compiler_params=_params(("parallel", "parallel")),
        name="sample_attn",
    )(*args, q_gain, k_gain)


def _softplus(x):
    return jnp.maximum(x, 0.0) + jnp.log(1.0 + jnp.exp(-jnp.abs(x)))


def _conv_silu(zx_ref, xpad_ref, cw_ref, cb_ref, lc):
    pad = 8
    xpad_ref[pad:pad + lc, :] = zx_ref[:, SSM_D_INNER:SSM_MAIN].astype(F32)
    conv = cb_ref[...] + xpad_ref[pad - 3:pad - 3 + lc, :] * cw_ref[0:1, :]
    for j in range(1, SSM_CONV):
        conv = conv + xpad_ref[pad - 3 + j:pad - 3 + j + lc, :] * cw_ref[j:j + 1, :]
    tail = xpad_ref[pad + lc - 3:pad + lc, :]
    xpad_ref[pad - 3:pad, :] = tail
    return conv * _sigmoid(conv), tail


def _gate_norm(y, z, gn):
    y = y * (z * _sigmoid(z))
    parts = []
    for g in range(SSM_GROUPS):
        yg = y[:, g * SSM_GROUP_WIDTH:(g + 1) * SSM_GROUP_WIDTH]
        parts.append(yg * lax.rsqrt(jnp.mean(yg * yg, axis=-1, keepdims=True) + GATE_NORM_EPS))
    return jnp.concatenate(parts, axis=-1) * gn


def _cumsum_rows(x):
    rows = x.shape[0]
    tril = (lax.broadcasted_iota(jnp.int32, (rows, rows), 0)
            >= lax.broadcasted_iota(jnp.int32, (rows, rows), 1)).astype(BF16)
    h1 = x.astype(BF16)
    r1 = x - h1.astype(F32)
    h2 = r1.astype(BF16)
    h3 = (r1 - h2.astype(F32)).astype(BF16)
    return _dot(tril, h1) + _dot(tril, h2) + _dot(tril, h3)


def _ssd_prompt_kernel(zx_ref, dt_ref, cw_ref, cb_ref, dtb_ref, a_ref, dskip_ref, gn_ref, e_ref,
                       y_ref, nconv_ref, nssm_ref, carry_ref, st_ref):
    c = pl.program_id(1)
    last = pl.num_programs(1) - 1

    @pl.when(c == 0)
    def _():
        carry_ref[...] = jnp.zeros(carry_ref.shape, BF16)
        st_ref[...] = jnp.zeros(st_ref.shape, F32)

    for u in range(zx_ref.shape[0] // SSM_CHUNK):
        rows = pl.ds(u * SSM_CHUNK, SSM_CHUNK)
        _ssd_chunk(zx_ref.at[rows], dt_ref.at[rows], cw_ref, cb_ref, dtb_ref, a_ref, dskip_ref, gn_ref, e_ref,
                   y_ref.at[rows], carry_ref, st_ref)

    @pl.when(c == last)
    def _():
        n_carry = carry_ref.shape[0]
        nconv_ref[...] = carry_ref[...].astype(F32)[n_carry - (SSM_CONV - 1):n_carry]
        nssm_ref[...] = st_ref[...].T


def _ssd_chunk(zx_ref, dt_ref, cw_ref, cb_ref, dtb_ref, a_ref, dskip_ref, gn_ref, e_ref, y_ref, carry_ref, st_ref):
    lc = zx_ref.shape[0]
    n_carry = carry_ref.shape[0]
    src = lax.broadcasted_iota(jnp.int32, (lc, n_carry + lc), 1) - n_carry
    dst = lax.broadcasted_iota(jnp.int32, (lc, n_carry + lc), 0)
    taps = SSM_CONV - 1
    shift = jnp.concatenate([(src == dst - (taps - j)).astype(BF16) for j in range(taps)], axis=0)

    def conv_silu(start, width):
        cols = slice(start, start + width)
        raw = zx_ref[:, SSM_D_INNER + start:SSM_D_INNER + start + width]
        shifted = _dot(shift, jnp.concatenate([carry_ref[:, cols], raw], axis=0))
        conv = cb_ref[:, cols] + shifted[0:lc] * cw_ref[0:1, cols]
        for j in range(1, taps):
            conv = conv + shifted[j * lc:(j + 1) * lc] * cw_ref[j:j + 1, cols]
        conv = conv + raw.astype(F32) * cw_ref[taps:taps + 1, cols]
        return conv * _sigmoid(conv)

    dt = _softplus(dt_ref[...] + dtb_ref[...])
    a_cs = _cumsum_rows(dt * a_ref[...])
    a_log2 = a_cs * LOG2E
    a_log2_t = a_log2.T
    dt_t = dt.T
    a_last = a_cs[lc - 1:lc, :]
    to_end = (dt * jnp.exp(a_last - a_cs)).astype(BF16)
    from_start = jnp.exp(a_cs).astype(BF16)
    chunk_decay = jnp.broadcast_to(jnp.exp(a_last), (8, LANES))
    causal = (lax.broadcasted_iota(jnp.int32, (lc, lc), 0) >= lax.broadcasted_iota(jnp.int32, (lc, lc), 1))
    heads_per_group = SSM_HEADS // SSM_GROUPS
    head_w = SSM_D_INNER // SSM_HEADS
    quad = 4
    quad_w = quad * head_w
    lane_head = lax.broadcasted_iota(jnp.int32, (1, quad_w), 1) // head_w
    n_bc = SSM_GROUPS * SSM_STATE
    for g in range(SSM_GROUPS):
        gw = slice(g * SSM_GROUP_WIDTH, (g + 1) * SSM_GROUP_WIDTH)
        xs = conv_silu(g * SSM_GROUP_WIDTH, SSM_GROUP_WIDTH)
        bm = conv_silu(SSM_D_INNER + g * SSM_STATE, SSM_STATE).astype(BF16)
        cm = conv_silu(SSM_D_INNER + n_bc + g * SSM_STATE, SSM_STATE).astype(BF16)
        e = e_ref[:, gw]
        xs_b = xs.astype(BF16)
        xdte = (xs * _dot(to_end, e)).astype(BF16)
        s_prev = st_ref[:, gw]
        cb = _dot_nt(cm, bm)
        y = _dot(cm, s_prev.astype(BF16)) * _dot(from_start, e)
        st_ref[:, gw] = (s_prev * _dot2(chunk_decay, e)[0:1, :]
                         + lax.dot_general(bm, xdte, TN, preferred_element_type=F32))
        quad_out = []
        for qd in range(heads_per_group // quad):
            h0 = g * heads_per_group + quad * qd
            ms = []
            for h in range(h0, h0 + quad):
                seg = a_log2[:, h:h + 1] - a_log2_t[h:h + 1, :]
                ms.append((cb * jnp.where(causal, jnp.exp2(seg), 0.0) * dt_t[h:h + 1, :]).astype(BF16))
            xq = xs_b[:, qd * quad_w:(qd + 1) * quad_w]
            rhs = jnp.concatenate([jnp.where(lane_head == i, xq, jnp.zeros_like(xq)) for i in range(quad)], axis=0)
            quad_out.append(_dot(jnp.concatenate(ms, axis=1), rhs))
        y = y + jnp.concatenate(quad_out, axis=-1) + dskip_ref[:, gw] * xs
        z = zx_ref[:, gw].astype(F32)
        y = y * (z * _sigmoid(z))
        y = y * lax.rsqrt(jnp.mean(y * y, axis=-1, keepdims=True) + GATE_NORM_EPS) * gn_ref[:, gw]
        y_ref[:, gw] = y.astype(y_ref.dtype)

    carry_ref[...] = zx_ref[lc - n_carry:lc, SSM_D_INNER:SSM_MAIN]


def _ssd_prompt(zx, dt, b, params, *, chunks_per_step):
    m = zx.shape[0]
    rows = chunks_per_step * SSM_CHUNK
    nc = m // b // rows
    full = lambda x: pl.BlockSpec(x.shape, lambda bb, c: (0, 0))
    row = lambda w: pl.BlockSpec((rows, w), lambda bb, c: (bb * nc + c, 0))
    return pl.pallas_call(
        _ssd_prompt_kernel,
        grid=(b, nc),
        in_specs=[row(SSM_MAIN), row(LANES)] + [full(x) for x in params],
        out_specs=[row(SSM_D_INNER),
                   pl.BlockSpec((None, SSM_CONV - 1, SSM_CONV_DIM), lambda bb, c: (bb, 0, 0)),
                   pl.BlockSpec((None, SSM_D_INNER, SSM_STATE), lambda bb, c: (bb, 0, 0))],
        out_shape=[jax.ShapeDtypeStruct((m, SSM_D_INNER), BF16),
                   jax.ShapeDtypeStruct((b, SSM_CONV - 1, SSM_CONV_DIM), F32),
                   jax.ShapeDtypeStruct((b, SSM_D_INNER, SSM_STATE), F32)],
        scratch_shapes=[pltpu.VMEM((16, SSM_CONV_DIM), BF16),
                        pltpu.VMEM((SSM_STATE, SSM_D_INNER), F32)],
        compiler_params=_params(("parallel", "arbitrary")),
        name="ssd_prompt",
    )(zx, dt, *params)


def _ssd_sample_kernel(zx_ref, dt_ref, conv0_ref, s0_ref, cw_ref, cb_ref, dtb_ref, a_ref, dskip_ref, gn_ref,
                       e_ref, eg_ref, y_ref, nconv_ref, nssm_ref, xpad_ref):
    lc = zx_ref.shape[0]
    xpad_ref[0:5, :] = jnp.zeros((5, SSM_CONV_DIM), F32)
    xpad_ref[5:8, :] = conv0_ref[...]
    xbc, tail = _conv_silu(zx_ref, xpad_ref, cw_ref, cb_ref, lc)
    nconv_ref[...] = tail

    xs = xbc[:, 0:SSM_D_INNER]
    bm = xbc[:, SSM_D_INNER:SSM_D_INNER + SSM_GROUPS * SSM_STATE]
    cm = xbc[:, SSM_D_INNER + SSM_GROUPS * SSM_STATE:]
    e = e_ref[...]
    dt = _softplus(dt_ref[...] + dtb_ref[...])
    dta = dt * a_ref[...]
    rows = [dta[0:1, :]]
    for i in range(1, lc):
        rows.append(rows[-1] + dta[i:i + 1, :])
    a_cs = jnp.concatenate(rows, axis=0)
    a_last = rows[-1]
    xdt = xs * _dot2(dt, e)
    xdte = (xdt * _dot2(jnp.exp(a_last - a_cs), e)).astype(BF16)
    ea_e = _dot2(jnp.exp(a_cs), e)
    cd_e = _dot2(jnp.broadcast_to(jnp.exp(a_last), (8, LANES)), e)[0:1, :]

    a_l = jnp.concatenate([a_cs] * lc, axis=0)
    a_s = jnp.concatenate([jnp.broadcast_to(a_cs[s:s + 1, :], (lc, LANES)) for s in range(lc)], axis=0)
    l_idx = lax.broadcasted_iota(jnp.int32, (lc * lc, 1), 0) % lc
    s_idx = lax.broadcasted_iota(jnp.int32, (lc * lc, 1), 0) // lc
    decay = jnp.where(l_idx >= s_idx, jnp.exp(a_l - a_s), 0.0)
    cb_prod = jnp.concatenate([cm * bm[s:s + 1, :] for s in range(lc)], axis=0)
    mix = _dot2(_dot2(cb_prod, eg_ref[...]) * decay, e)
    y = ea_e * 0.0
    for s in range(lc):
        y = y + mix[s * lc:(s + 1) * lc, :] * xdt[s:s + 1, :]

    s_prev = s0_ref[...].T
    s_b = s_prev.astype(BF16)
    bm_b = bm.astype(BF16)
    cm_b = cm.astype(BF16)
    y_off, s_parts = [], []
    for g in range(SSM_GROUPS):
        gs = slice(g * SSM_STATE, (g + 1) * SSM_STATE)
        gw = slice(g * SSM_GROUP_WIDTH, (g + 1) * SSM_GROUP_WIDTH)
        y_off.append(_dot(cm_b[:, gs], s_b[:, gw]))
        s_parts.append(lax.dot_general(bm_b[:, gs], xdte[:, gw], TN, preferred_element_type=F32))
    y = y + jnp.concatenate(y_off, axis=-1) * ea_e + dskip_ref[...] * xs
    s_new = s_prev * cd_e + jnp.concatenate(s_parts, axis=-1)
    nssm_ref[...] = s_new.T
    z = zx_ref[:, 0:SSM_D_INNER]
    y_ref[...] = _gate_norm(y, z, gn_ref[...])


def _ssd_sample(zx, dt, conv0, s0, params):
    b = conv0.shape[0]
    t_new = zx.shape[0] // b
    full = lambda x: pl.BlockSpec(x.shape, lambda bb: (0, 0))
    row = lambda w: pl.BlockSpec((t_new, w), lambda bb: (bb, 0))
    conv_spec = pl.BlockSpec((None, SSM_CONV - 1, SSM_CONV_DIM), lambda bb: (bb, 0, 0))
    state_spec = pl.BlockSpec((None, SSM_D_INNER, SSM_STATE), lambda bb: (bb, 0, 0))
    return pl.pallas_call(
        _ssd_sample_kernel,
        grid=(b,),
        in_specs=[row(SSM_MAIN), row(LANES), conv_spec, state_spec] + [full(x) for x in params],
        out_specs=[row(SSM_D_INNER), conv_spec, state_spec],
        out_shape=[jax.ShapeDtypeStruct((b * t_new, SSM_D_INNER), F32),
                   jax.ShapeDtypeStruct((b, SSM_CONV - 1, SSM_CONV_DIM), F32),
                   jax.ShapeDtypeStruct((b, SSM_D_INNER, SSM_STATE), F32)],
        scratch_shapes=[pltpu.VMEM((8 + t_new, SSM_CONV_DIM), F32)],
        compiler_params=_params(("parallel",)),
        name="ssd_sample",
    )(zx, dt, conv0, s0, *params)


def _one_hot_expand(n_rows, n_cols, group):
    r = lax.broadcasted_iota(jnp.int32, (n_rows, n_cols), 0)
    c = lax.broadcasted_iota(jnp.int32, (n_rows, n_cols), 1)
    return (c // group == r).astype(BF16)


def _kv_rows(feature_major, b):
    tokens = feature_major.shape[-1]
    x = feature_major.reshape(1, b, 2, N_HEADS, HEAD_DIM, tokens)
    return jnp.transpose(x, (0, 1, 5, 2, 3, 4))


def kernel(x_prompt, x_sample, cache_kv_g0, cache_kv_g1, cache_kv_g2, state_conv, state_ssm, attn_norm, attn_w_in, attn_q_gain, attn_k_gain, attn_w_out, ssm_norm, ssm_w_in, ssm_conv_w, ssm_conv_b, ssm_dt_bias, ssm_A_log, ssm_D, ssm_gate_norm, ssm_w_out):
    b, seq, dm = x_prompt.shape
    sb, st, _ = x_sample.shape
    xp = x_prompt.reshape(b * seq, dm)
    xs = x_sample.reshape(sb * st, dm)

    w_in = attn_w_in[0].astype(BF16)
    w_out = attn_w_out[0].astype(BF16)
    scale = HEAD_DIM ** -0.5
    q_gain3 = jnp.tile(attn_q_gain[0], (1, N_HEADS)) * scale
    k_gain3 = jnp.tile(attn_k_gain[0], (1, N_HEADS))

    tile_gain = jnp.ones((N_TILES, 1, ATT_WIDTH), F32)
    for g in range(len(DILATIONS)):
        tile_gain = tile_gain.at[Q_TILES[g], 0].set(q_gain3[g] * LOG2E).at[K_TILES[g], 0].set(k_gain3[g])
    qkvg = _attn_proj(xp, attn_norm[0], w_in, tile_gain, tm=1024, seq=seq)
    part2 = _attn_group(qkvg, 2, n_sub=2)
    part1 = _attn_group(qkvg, 1, n_sub=4)
    o_gated = _attn_group(qkvg, 0, n_sub=4, others=[part1, part2])
    y1p = _outproj(o_gated, w_out, xp, tm=1024, perm_seq=seq)
    kv_p = []
    for g, d in enumerate(DILATIONS):
        keep = min(d * WINDOW_STEPS, seq)
        kv_p.append(_kv_rows(_kv_tail(xp, attn_norm[0], w_in, g, k_gain3[g:g + 1], b=b, keep=keep,
                                      tm=min(keep, 512)), b))

    proj_s = _proj(xs, attn_norm[0], w_in, tm=sb * st, tn=1024, out_dtype=F32)
    caches_t = [jnp.transpose(c[0], (0, 2, 3, 4, 1)).reshape(sb, 2, ATT_WIDTH, c.shape[2])
                for c in (cache_kv_g0, cache_kv_g1, cache_kv_g2)]
    o_s, kn0, kn1, kn2 = _sample_attn(proj_s, caches_t, q_gain3[:, :LANES], k_gain3[:, :LANES])
    y1s = _outproj(o_s, w_out, xs, tm=sb * st)
    kv_s = []
    for g, kn in enumerate((kn0, kn1, kn2)):
        v = proj_s[:, (3 * g + 2) * ATT_WIDTH:(3 * g + 3) * ATT_WIDTH]
        kv_s.append(jnp.stack([kn, v], axis=1).reshape(1, sb, st, 2, N_HEADS, HEAD_DIM))

    w_in2 = ssm_w_in[0].astype(BF16)
    w_dt = jnp.pad(w_in2[:, SSM_MAIN:], ((0, 0), (0, LANES - SSM_HEADS)))
    w_out2 = ssm_w_out[0].astype(BF16)
    pad_h = lambda v: jnp.pad(v.astype(F32), (0, LANES - SSM_HEADS)).reshape(1, LANES)
    e32 = _one_hot_expand(LANES, SSM_D_INNER, SSM_D_INNER // SSM_HEADS)
    gate_gain = ssm_gate_norm[0].reshape(1, -1)
    scan_params = (ssm_conv_w[0], ssm_conv_b[0].reshape(1, -1), pad_h(ssm_dt_bias[0]),
                   pad_h(-jnp.exp(ssm_A_log[0].astype(F32))),
                   jnp.repeat(ssm_D[0].astype(F32), SSM_D_INNER // SSM_HEADS).reshape(1, -1))

    zx_p, dt_p = _proj(y1p, ssm_norm[0], w_in2, tm=1024, tn=SSM_MAIN // 2, n_out=SSM_MAIN, out_dtype=BF16,
                       w_dt=w_dt)
    yg_p, conv_p, ssm_p = _ssd_prompt(zx_p, dt_p, b, scan_params + (gate_gain, e32), chunks_per_step=2)
    y2p = _outproj(yg_p, w_out2, y1p, tm=1024)

    eg = (lax.broadcasted_iota(jnp.int32, (SSM_GROUPS * SSM_STATE, LANES), 0) // SSM_STATE
          == lax.broadcasted_iota(jnp.int32, (SSM_GROUPS * SSM_STATE, LANES), 1) // (SSM_HEADS // SSM_GROUPS))
    eg = (eg & (lax.broadcasted_iota(jnp.int32, eg.shape, 1) < SSM_HEADS)).astype(BF16)
    zx_s, dt_s = _proj(y1s, ssm_norm[0], w_in2, tm=sb * st, tn=SSM_MAIN // 2, n_out=SSM_MAIN, out_dtype=F32,
                       w_dt=w_dt)
    yg_s, conv_s, ssm_s = _ssd_sample(zx_s, dt_s, state_conv[0], state_ssm[0].reshape(sb, SSM_D_INNER, SSM_STATE),
                                      scan_params + (gate_gain, e32, eg))
    y2s = _outproj(yg_s, w_out2, y1s, tm=sb * st)

    hp = SSM_D_INNER // SSM_HEADS
    return (y2p.reshape(b, seq, dm), y2s.reshape(sb, st, dm),
            kv_p[0], kv_p[1], kv_p[2], kv_s[0], kv_s[1], kv_s[2],
            conv_p[None], conv_s[None],
            ssm_p.reshape(1, b, SSM_HEADS, hp, SSM_STATE), ssm_s.reshape(1, sb, SSM_HEADS, hp, SSM_STATE))
```

```python
import functools

import jax
import jax.numpy as jnp
from jax import lax
from jax.experimental import pallas as pl
from jax.experimental.pallas import tpu as pltpu

F32 = jnp.float32
BF16 = jnp.bfloat16

D_MODEL = 1024
N_HEADS = 16
HEAD_DIM = 64
ATT_WIDTH = N_HEADS * HEAD_DIM
DILATIONS = (1, 4, 16)
WINDOW_STEPS = 128
ATT_IN = 10 * ATT_WIDTH
PERM = 16

SSM_D_INNER = 2048
SSM_HEADS = 32
SSM_STATE = 128
SSM_GROUPS = 4
SSM_GROUP_WIDTH = SSM_D_INNER // SSM_GROUPS
SSM_CONV = 4
SSM_CONV_DIM = SSM_D_INNER + 2 * SSM_GROUPS * SSM_STATE
SSM_MAIN = SSM_D_INNER + SSM_CONV_DIM
SSM_CHUNK = 128

NORM_EPS = 1e-6
GATE_NORM_EPS = 1e-5
MASKED = -1e30
LOG2E = 1.4426950408889634

LANES = 128
VMEM_LIMIT = 56 * 1024 * 1024

NT = (((1,), (1,)), ((), ()))
TN = (((0,), (0,)), ((), ()))


def _params(semantics):
    return pltpu.CompilerParams(dimension_semantics=semantics, vmem_limit_bytes=VMEM_LIMIT)


def _rms(x, w, eps):
    return x * lax.rsqrt(jnp.mean(x * x, axis=-1, keepdims=True) + eps) * w


def _sigmoid(x):
    return 1.0 / (1.0 + jnp.exp(-x))


def _split2(v):
    hi = v.astype(BF16)
    lo = (v - hi.astype(F32)).astype(BF16)
    return hi, lo


def _dot(a, b):
    return jnp.dot(a, b, preferred_element_type=F32)


def _dot_nt(a, b):
    return lax.dot_general(a, b, NT, preferred_element_type=F32)


def _dot2(v, e):
    hi, lo = _split2(v)
    return _dot(hi, e) + _dot(lo, e)


def _low_half():
    return lax.broadcasted_iota(jnp.int32, (1, LANES), 1) < HEAD_DIM


def _head_norm(x, eps):
    lo = _low_half()
    parts = []
    for j in range(x.shape[1] // LANES):
        t = x[:, j * LANES:(j + 1) * LANES]
        t2 = t * t
        s_lo = jnp.sum(jnp.where(lo, t2, 0.0), axis=-1, keepdims=True)
        s_hi = jnp.sum(jnp.where(lo, 0.0, t2), axis=-1, keepdims=True)
        r = jnp.where(lo, lax.rsqrt(s_lo * (1.0 / HEAD_DIM) + eps), lax.rsqrt(s_hi * (1.0 / HEAD_DIM) + eps))
        parts.append(t * r)
    return jnp.concatenate(parts, axis=-1)


def _split_heads(qp):
    lo = _low_half()
    zero = jnp.zeros_like(qp)
    return jnp.concatenate([jnp.where(lo, qp, zero), jnp.where(lo, zero, qp)], axis=0)


def _join_heads(x2):
    rows = x2.shape[0] // 2
    return jnp.where(_low_half(), x2[0:rows], x2[rows:])


def _proj_kernel(x_ref, nw_ref, w_ref, *rest, with_dt):
    rest = list(rest)
    wdt_ref = rest.pop(0) if with_dt else None
    o_ref = rest.pop(0)
    dt_ref = rest.pop(0) if with_dt else None
    h_ref = rest.pop(0)

    @pl.when(pl.program_id(1) == 0)
    def _():
        h_ref[...] = _rms(x_ref[...], nw_ref[...], NORM_EPS).astype(BF16)
        if with_dt:
            dt_ref[...] = _dot(h_ref[...], wdt_ref[...])

    o_ref[...] = _dot(h_ref[...], w_ref[...]).astype(o_ref.dtype)


def _proj(x, norm_w, w, *, tm, tn, out_dtype, n_out=None, w_dt=None):
    m, dm = x.shape
    n = w.shape[1] if n_out is None else n_out
    assert n % tn == 0 and m % tm == 0
    in_specs = [pl.BlockSpec((tm, dm), lambda i, j: (i, 0)),
                pl.BlockSpec((1, dm), lambda i, j: (0, 0)),
                pl.BlockSpec((dm, tn), lambda i, j: (0, j))]
    args = [x, norm_w.reshape(1, dm), w]
    out_shape = [jax.ShapeDtypeStruct((m, n), out_dtype)]
    out_specs = [pl.BlockSpec((tm, tn), lambda i, j: (i, j))]
    if w_dt is not None:
        in_specs.append(pl.BlockSpec((dm, LANES), lambda i, j: (0, 0)))
        args.append(w_dt)
        out_shape.append(jax.ShapeDtypeStruct((m, LANES), F32))
        out_specs.append(pl.BlockSpec((tm, LANES), lambda i, j: (i, 0)))
    outs = pl.pallas_call(
        functools.partial(_proj_kernel, with_dt=w_dt is not None),
        grid=(m // tm, n // tn), in_specs=in_specs, out_specs=out_specs, out_shape=out_shape,
        scratch_shapes=[pltpu.VMEM((tm, dm), BF16)],
        compiler_params=_params(("parallel", "arbitrary")),
        name="norm_proj",
    )(*args)
    return outs if w_dt is not None else outs[0]


N_TILES = ATT_IN // ATT_WIDTH
Q_TILES, K_TILES, V_TILES, GATE_TILE = (0, 3, 6), (1, 4, 7), (2, 5, 8), 9
STREAM_A = (0, 1, 3, 4, 6)
STREAM_B = (2, 5, 8, 9, 7)


def _lookup(j, table):
    out = jnp.int32(table[0])
    for t, v in enumerate(table[1:], 1):
        out = jnp.where(j == t, jnp.int32(v), out)
    return out


def _attn_proj_kernel(x_ref, nw_ref, wa_ref, wb_ref, ga_ref, gb_ref, oa_ref, ob_ref, h_ref, slab_ref):
    j = pl.program_id(1)

    @pl.when(j == 0)
    def _():
        xn = _rms(x_ref[...], nw_ref[...], NORM_EPS)
        rows = x_ref.shape[0] // PERM
        for c in range(x_ref.shape[1] // LANES):
            cols = slice(c * LANES, (c + 1) * LANES)
            slab_ref[...] = xn[:, cols]
            for r in range(PERM):
                h_ref[r * rows:(r + 1) * rows, cols] = slab_ref[pl.ds(r, rows, stride=PERM), :].astype(BF16)

    def tile(w_ref, gain_ref, o_ref, normed):
        res = _dot(h_ref[...], w_ref[...])
        if normed:
            res = _head_norm(res, NORM_EPS) * gain_ref[0]
        o_ref[...] = res.astype(o_ref.dtype).reshape(o_ref.shape)

    last = pl.num_programs(1) - 1

    @pl.when(j < last)
    def _():
        tile(wa_ref, ga_ref, oa_ref, True)
        tile(wb_ref, gb_ref, ob_ref, False)

    @pl.when(j == last)
    def _():
        tile(wa_ref, ga_ref, oa_ref, True)
        tile(wb_ref, gb_ref, ob_ref, True)


def _attn_proj(x, norm_w, w, tile_gain, *, tm, seq):
    m, dm = x.shape
    per_b = seq // tm
    rows = tm // PERM
    n_steps = len(STREAM_A)
    w_spec = lambda table: pl.BlockSpec((dm, ATT_WIDTH), lambda i, j: (0, _lookup(j, table)))
    g_spec = lambda table: pl.BlockSpec((1, 1, ATT_WIDTH), lambda i, j: (_lookup(j, table), 0, 0))
    out_spec = pl.BlockSpec((None, PERM, rows, ATT_WIDTH), lambda i, j: (i // per_b, 0, i % per_b, j))
    out_sds = jax.ShapeDtypeStruct((m // seq, PERM, seq // PERM, n_steps * ATT_WIDTH), BF16)
    return pl.pallas_call(
        _attn_proj_kernel,
        grid=(m // tm, n_steps),
        in_specs=[pl.BlockSpec((tm, dm), lambda i, j: (i, 0)), pl.BlockSpec((1, dm), lambda i, j: (0, 0)),
                  w_spec(STREAM_A), w_spec(STREAM_B), g_spec(STREAM_A), g_spec(STREAM_B)],
        out_specs=[out_spec, out_spec], out_shape=[out_sds, out_sds],
        scratch_shapes=[pltpu.VMEM((tm, dm), BF16), pltpu.VMEM((tm, LANES), F32)],
        compiler_params=_params(("parallel", "arbitrary")),
        name="attn_proj",
    )(x, norm_w.reshape(1, dm), w, w, tile_gain, tile_gain)


def _outproj_kernel(a_ref, w_ref, x_ref, o_ref, *scratch, unperm):
    tm = o_ref.shape[0]
    a = a_ref[...].reshape(tm, a_ref.shape[-1]).astype(BF16)
    res = _dot(a, w_ref[...])
    if unperm:
        slab_ref, = scratch
        rows = tm // PERM
        for c in range(o_ref.shape[1] // LANES):
            cols = slice(c * LANES, (c + 1) * LANES)
            for r in range(PERM):
                slab_ref[pl.ds(r, rows, stride=PERM), :] = res[r * rows:(r + 1) * rows, cols]
            o_ref[:, cols] = x_ref[:, cols] + slab_ref[...]
    else:
        o_ref[...] = x_ref[...] + res


def _outproj(a, w, x, *, tm, perm_seq=None):
    m, n = x.shape
    k = w.shape[0]
    if perm_seq is not None:
        per_b = perm_seq // tm
        a_spec = pl.BlockSpec((None, PERM, tm // PERM, k), lambda i: (i // per_b, 0, i % per_b, 0))
    else:
        a_spec = pl.BlockSpec((tm, k), lambda i: (i, 0))
    return pl.pallas_call(
        functools.partial(_outproj_kernel, unperm=perm_seq is not None),
        grid=(m // tm,),
        in_specs=[a_spec, pl.BlockSpec((k, n), lambda i: (0, 0)), pl.BlockSpec((tm, n), lambda i: (i, 0))],
        out_specs=pl.BlockSpec((tm, n), lambda i: (i, 0)),
        out_shape=jax.ShapeDtypeStruct((m, n), F32),
        scratch_shapes=[pltpu.VMEM((tm, LANES), F32)] if perm_seq else [],
        compiler_params=_params(("parallel",)),
        name="out_proj",
    )(a, w, x)


ATT_SUB = 128


def _attn_group_kernel(*refs, n_chunks, n_sub, merge):
    it = iter(refs)
    q_ref, k_ref, v_ref = next(it), next(it), next(it)
    if merge:
        gate_ref = next(it)
        others = [(next(it), next(it)) for _ in range(len(DILATIONS) - 1)]
        o_out = next(it)
    else:
        o_out, lse_out = next(it), next(it)
    ks_ref, vs_ref = next(it), next(it)

    rc = q_ref.shape[1] // n_sub
    assert n_chunks * rc == ATT_SUB
    n = pl.program_id(2)
    narrow = rc % 16 != 0

    def piece(ref, u, cols=slice(None)):
        if narrow and ref.dtype == BF16:
            pair = 2 * (u // 2) * rc
            x = ref[:, pair:pair + 2 * rc, cols].astype(F32)[:, (u % 2) * rc:(u % 2 + 1) * rc, :]
        else:
            x = ref[:, u * rc:(u + 1) * rc, cols]
        return x.reshape(ATT_SUB, x.shape[-1])

    def unsubs(parts, dtype):
        parts = [p.reshape(n_chunks, rc, p.shape[-1]) for p in parts]
        if narrow:
            return jnp.concatenate(parts, axis=1).astype(dtype)
        return jnp.concatenate([p.astype(dtype) for p in parts], axis=1)

    @pl.when(n == 0)
    def _():
        ks_ref[0:ATT_SUB, :] = jnp.zeros((ATT_SUB, ATT_WIDTH), BF16)
        vs_ref[0:ATT_SUB, :] = jnp.zeros((ATT_SUB, ATT_WIDTH), BF16)

    for u in range(n_sub):
        ks_ref[(u + 1) * ATT_SUB:(u + 2) * ATT_SUB, :] = piece(k_ref, u).astype(BF16)
        vs_ref[(u + 1) * ATT_SUB:(u + 2) * ATT_SUB, :] = piece(v_ref, u).astype(BF16)

    qi = lax.broadcasted_iota(jnp.int32, (ATT_SUB, 2 * ATT_SUB), 0)
    kj = lax.broadcasted_iota(jnp.int32, (ATT_SUB, 2 * ATT_SUB), 1)
    kc = jnp.where(kj >= ATT_SUB, kj - ATT_SUB, kj)
    tq = n_chunks * (qi % rc) + qi // rc
    tk = n_chunks * (kc % rc) + kc // rc + jnp.where(kj >= ATT_SUB, 0, -ATT_SUB)
    dist = tq - tk
    in_band = (dist >= 0) & (dist <= WINDOW_STEPS)
    band = jnp.where(in_band, 0.0, MASKED)
    band_first = jnp.where(in_band & ((kj >= ATT_SUB) | (n > 0)), 0.0, MASKED)
    bias = [jnp.concatenate([x, x], axis=0) for x in (band_first, band)]
    ones = jnp.ones((2 * ATT_SUB, LANES), BF16)

    for j in range(N_HEADS // 2):
        cols = slice(j * LANES, (j + 1) * LANES)
        o_parts, lse_parts = [], []
        for u in range(n_sub):
            keys = slice(u * ATT_SUB, (u + 2) * ATT_SUB)
            q2 = _split_heads(piece(q_ref, u, cols).astype(BF16))
            s = _dot_nt(q2, ks_ref[keys, cols]) + bias[min(u, 1)]
            m = jnp.max(s, axis=-1, keepdims=True)
            p = jnp.exp2(s - m).astype(BF16)
            pv = _dot(p, jnp.concatenate([vs_ref[keys, cols], ones], axis=1))
            l = pv[:, LANES:]
            o = _join_heads(pv[:, :LANES] / l)
            lse = _join_heads(m + jnp.log2(l))
            if merge:
                o_g = [o] + [piece(o_ref, u, cols).astype(F32) for o_ref, _ in others]
                lse_g = [lse] + [piece(l_ref, u, cols) for _, l_ref in others]
                top = jnp.maximum(jnp.maximum(lse_g[0], lse_g[1]), lse_g[2])
                w = [jnp.exp2(x - top) for x in lse_g]
                o = (w[0] * o_g[0] + w[1] * o_g[1] + w[2] * o_g[2]) / (w[0] + w[1] + w[2])
                g = piece(gate_ref, u, cols).astype(F32)
                o = o * (g * _sigmoid(g))
            o_parts.append(o)
            lse_parts.append(lse)
        o_out[:, :, cols] = unsubs(o_parts, BF16)
        if not merge:
            lse_out[:, :, cols] = unsubs(lse_parts, F32)
    ks_ref[0:ATT_SUB, :] = ks_ref[n_sub * ATT_SUB:(n_sub + 1) * ATT_SUB, :]
    vs_ref[0:ATT_SUB, :] = vs_ref[n_sub * ATT_SUB:(n_sub + 1) * ATT_SUB, :]


def _attn_group(qkvg, g, *, n_sub, others=None):
    b, _, t, _ = qkvg[0].shape
    d = DILATIONS[g]
    n_chunks = PERM // d
    rc = ATT_SUB // n_chunks * n_sub
    nb = t // rc
    merge = others is not None
    view = lambda a: a.reshape(b, n_chunks, d, t, a.shape[-1])

    def spec(col=0):
        return pl.BlockSpec((None, n_chunks, None, rc, ATT_WIDTH), lambda bb, r, n: (bb, 0, r, n, col))

    in_specs, args = [], []

    def add_tile(tile):
        stream, table = (0, STREAM_A) if tile in STREAM_A else (1, STREAM_B)
        in_specs.append(spec(table.index(tile)))
        args.append(view(qkvg[stream]))

    for tile in (Q_TILES[g], K_TILES[g], V_TILES[g]):
        add_tile(tile)
    full = jax.ShapeDtypeStruct((b, n_chunks, d, t, ATT_WIDTH), BF16)
    if merge:
        add_tile(GATE_TILE)
        for o, lse in others:
            in_specs += [spec(), spec()]
            args += [view(o), view(lse)]
        out_shape, out_specs = [full], [spec()]
    else:
        out_shape = [full, jax.ShapeDtypeStruct(full.shape, F32)]
        out_specs = [spec(), spec()]
    outs = pl.pallas_call(
        functools.partial(_attn_group_kernel, n_chunks=n_chunks, n_sub=n_sub, merge=merge),
        grid=(b, d, nb), in_specs=in_specs, out_specs=out_specs, out_shape=out_shape,
        scratch_shapes=[pltpu.VMEM(((n_sub + 1) * ATT_SUB, ATT_WIDTH), BF16)] * 2,
        compiler_params=_params(("parallel", "parallel", "arbitrary")),
        name=f"attn_group{g}",
    )(*args)
    outs = [a.reshape(b, PERM, t, ATT_WIDTH) for a in outs]
    return outs[0] if merge else tuple(outs)


def _kv_tail_kernel(x_ref, nw_ref, wk_ref, wv_ref, kg_ref, o_ref):
    h = _rms(x_ref[...], nw_ref[...], NORM_EPS).astype(BF16)
    o_ref[0:ATT_WIDTH, :] = (_head_norm(_dot(h, wk_ref[...]), NORM_EPS) * kg_ref[...]).T
    o_ref[ATT_WIDTH:2 * ATT_WIDTH, :] = _dot(h, wv_ref[...]).T


def _kv_tail(x, norm_w, w, g, k_gain, *, b, keep, tm):
    seq = x.shape[0] // b
    first = (seq - keep) // tm
    w_spec = lambda col: pl.BlockSpec((D_MODEL, ATT_WIDTH), lambda bb, i: (0, col))
    return pl.pallas_call(
        _kv_tail_kernel,
        grid=(b, keep // tm),
        in_specs=[pl.BlockSpec((tm, D_MODEL), lambda bb, i: (bb * (seq // tm) + first + i, 0)),
                  pl.BlockSpec((1, D_MODEL), lambda bb, i: (0, 0)),
                  w_spec(3 * g + 1), w_spec(3 * g + 2),
                  pl.BlockSpec((1, ATT_WIDTH), lambda bb, i: (0, 0))],
        out_specs=pl.BlockSpec((None, 2 * ATT_WIDTH, tm), lambda bb, i: (bb, 0, i)),
        out_shape=jax.ShapeDtypeStruct((b, 2 * ATT_WIDTH, keep), F32),
        compiler_params=_params(("parallel", "parallel")),
        name="kv_tail",
    )(x, norm_w.reshape(1, D_MODEL), w, w, k_gain)


SAMPLE_HEADS = 8


def _sample_attn_kernel(*refs):
    n_g = len(DILATIONS)
    qkv = [refs[3 * g:3 * g + 3] for g in range(n_g)]
    gate_ref = refs[3 * n_g]
    c_refs = refs[3 * n_g + 1:4 * n_g + 1]
    qg_ref, kg_ref = refs[4 * n_g + 1:4 * n_g + 3]
    o_ref = refs[4 * n_g + 3]
    kn_refs = refs[4 * n_g + 4:]
    t_new = o_ref.shape[0]
    t_row = lax.broadcasted_iota(jnp.int32, (2 * t_new, 1), 0) % t_new
    t_col = lax.broadcasted_iota(jnp.int32, (1, t_new), 1)

    valid_c, valid_n = [], []
    for g, d in enumerate(DILATIONS):
        assert d & (d - 1) == 0
        length = c_refs[g].shape[2]
        back = length + t_row - lax.broadcasted_iota(jnp.int32, (1, length), 1)
        valid_c.append(((back & (d - 1)) == 0) & (back <= WINDOW_STEPS * d))
        back = t_row - t_col
        valid_n.append((back >= 0) & ((back & (d - 1)) == 0) & (back <= WINDOW_STEPS * d))
    ones_rows = jnp.ones((LANES, c_refs[-1].shape[2]), BF16)
    ones_new = jnp.ones((t_new, LANES), BF16)

    o_cols = []
    kn_cols = [[] for _ in DILATIONS]
    for j in range(SAMPLE_HEADS // 2):
        cols = slice(j * LANES, (j + 1) * LANES)
        o_g, m_g, l_g = [], [], []
        for g, d in enumerate(DILATIONS):
            q_ref, k_ref, v_ref = qkv[g]
            qm = _split_heads(_head_norm(q_ref[:, cols], NORM_EPS) * qg_ref[g:g + 1, :]).astype(BF16)
            kn = _head_norm(k_ref[:, cols], NORM_EPS) * kg_ref[g:g + 1, :]
            kn_cols[g].append(kn)
            length = c_refs[g].shape[2]
            s_c = jnp.where(valid_c[g], _dot(qm, c_refs[g][0, cols, :].astype(BF16)), MASKED)
            s_n = jnp.where(valid_n[g], _dot_nt(qm, kn.astype(BF16)), MASKED)
            m = jnp.maximum(jnp.max(s_c, axis=-1, keepdims=True), jnp.max(s_n, axis=-1, keepdims=True))
            vt = jnp.concatenate([c_refs[g][1, cols, :].astype(BF16), ones_rows[:, 0:length]], axis=0)
            vn = jnp.concatenate([v_ref[:, cols].astype(BF16), ones_new], axis=1)
            pv = (_dot_nt(jnp.exp(s_c - m).astype(BF16), vt)
                  + _dot(jnp.exp(s_n - m).astype(BF16), vn))
            o_g.append(pv[:, :LANES])
            l_g.append(pv[:, LANES:])
            m_g.append(m)
        top = jnp.maximum(jnp.maximum(m_g[0], m_g[1]), m_g[2])
        f = [jnp.exp(m - top) for m in m_g]
        den = f[0] * l_g[0] + f[1] * l_g[1] + f[2] * l_g[2]
        o = _join_heads((f[0] * o_g[0] + f[1] * o_g[1] + f[2] * o_g[2]) / den)
        gate = gate_ref[:, cols]
        o_cols.append(o * (gate * _sigmoid(gate)))
    o_ref[...] = jnp.concatenate(o_cols, axis=-1)
    for g in range(n_g):
        kn_refs[g][...] = jnp.concatenate(kn_cols[g], axis=-1)


def _sample_attn(proj, caches_t, q_gain, k_gain):
    b = caches_t[0].shape[0]
    t_new = proj.shape[0] // b
    width = SAMPLE_HEADS * HEAD_DIM
    per_chunk = ATT_WIDTH // width
    col = lambda c: pl.BlockSpec((t_new, width), lambda bb, hc: (bb, c * per_chunk + hc))
    in_specs, args = [], []
    for g in range(len(DILATIONS)):
        in_specs += [col(3 * g), col(3 * g + 1), col(3 * g + 2)]
        args += [proj, proj, proj]
    in_specs.append(col(ATT_IN // ATT_WIDTH - 1))
    args.append(proj)
    for c in caches_t:
        in_specs.append(pl.BlockSpec((None, 2, width, c.shape[3]), lambda bb, hc: (bb, 0, hc, 0)))
        args.append(c)
    gain_spec = pl.BlockSpec((len(DILATIONS), LANES), lambda bb, hc: (0, 0))
    out_spec = pl.BlockSpec((t_new, width), lambda bb, hc: (bb, hc))
    out_sds = jax.ShapeDtypeStruct((b * t_new, ATT_WIDTH), F32)
    return pl.pallas_call(
        _sample_attn_kernel,
        grid=(b, per_chunk),
        in_specs=in_specs + [gain_spec, gain_spec],
        out_specs=[out_spec] * 4,
        out_shape=[out_sds] * 4,
        compiler_params=_params(("parallel", "parallel")),
        name="sample_attn",
    )(*args, q_gain, k_gain)


def _softplus(x):
    return jnp.maximum(x, 0.0) + jnp.log(1.0 + jnp.exp(-jnp.abs(x)))


def _conv_silu(zx_ref, xpad_ref, cw_ref, cb_ref, lc):
    pad = 8
    xpad_ref[pad:pad + lc, :] = zx_ref[:, SSM_D_INNER:SSM_MAIN].astype(F32)
    conv = cb_ref[...] + xpad_ref[pad - 3:pad - 3 + lc, :] * cw_ref[0:1, :]
    for j in range(1, SSM_CONV):
        conv = conv + xpad_ref[pad - 3 + j:pad - 3 + j + lc, :] * cw_ref[j:j + 1, :]
    tail = xpad_ref[pad + lc - 3:pad + lc, :]
    xpad_ref[pad - 3:pad, :] = tail
    return conv * _sigmoid(conv), tail


def _gate_norm(y, z, gn):
    y = y * (z * _sigmoid(z))
    parts = []
    for g in range(SSM_GROUPS):
        yg = y[:, g * SSM_GROUP_WIDTH:(g + 1) * SSM_GROUP_WIDTH]
        parts.append(yg * lax.rsqrt(jnp.mean(yg * yg, axis=-1, keepdims=True) + GATE_NORM_EPS))
    return jnp.concatenate(parts, axis=-1) * gn


def _cumsum_rows(x):
    rows = x.shape[0]
    tril = (lax.broadcasted_iota(jnp.int32, (rows, rows), 0)
            >= lax.broadcasted_iota(jnp.int32, (rows, rows), 1)).astype(BF16)
    h1 = x.astype(BF16)
    r1 = x - h1.astype(F32)
    h2 = r1.astype(BF16)
    h3 = (r1 - h2.astype(F32)).astype(BF16)
    return _dot(tril, h1) + _dot(tril, h2) + _dot(tril, h3)


def _ssd_prompt_kernel(zx_ref, dt_ref, cw_ref, cb_ref, dtb_ref, a_ref, dskip_ref, gn_ref, e_ref,
                       y_ref, nconv_ref, nssm_ref, carry_ref, st_ref):
    c = pl.program_id(1)
    last = pl.num_programs(1) - 1

    @pl.when(c == 0)
    def _():
        carry_ref[...] = jnp.zeros(carry_ref.shape, BF16)
        st_ref[...] = jnp.zeros(st_ref.shape, F32)

    for u in range(zx_ref.shape[0] // SSM_CHUNK):
        rows = pl.ds(u * SSM_CHUNK, SSM_CHUNK)
        _ssd_chunk(zx_ref.at[rows], dt_ref.at[rows], cw_ref, cb_ref, dtb_ref, a_ref, dskip_ref, gn_ref, e_ref,
                   y_ref.at[rows], carry_ref, st_ref)

    @pl.when(c == last)
    def _():
        n_carry = carry_ref.shape[0]
        nconv_ref[...] = carry_ref[...].astype(F32)[n_carry - (SSM_CONV - 1):n_carry]
        nssm_ref[...] = st_ref[...].T


def _ssd_chunk(zx_ref, dt_ref, cw_ref, cb_ref, dtb_ref, a_ref, dskip_ref, gn_ref, e_ref, y_ref, carry_ref, st_ref):
    lc = zx_ref.shape[0]
    n_carry = carry_ref.shape[0]
    src = lax.broadcasted_iota(jnp.int32, (lc, n_carry + lc), 1) - n_carry
    dst = lax.broadcasted_iota(jnp.int32, (lc, n_carry + lc), 0)
    taps = SSM_CONV - 1
    shift = jnp.concatenate([(src == dst - (taps - j)).astype(BF16) for j in range(taps)], axis=0)

    def conv_silu(start, width):
        cols = slice(start, start + width)
        raw = zx_ref[:, SSM_D_INNER + start:SSM_D_INNER + start + width]
        shifted = _dot(shift, jnp.concatenate([carry_ref[:, cols], raw], axis=0))
        conv = cb_ref[:, cols] + shifted[0:lc] * cw_ref[0:1, cols]
        for j in range(1, taps):
            conv = conv + shifted[j * lc:(j + 1) * lc] * cw_ref[j:j + 1, cols]
        conv = conv + raw.astype(F32) * cw_ref[taps:taps + 1, cols]
        return conv * _sigmoid(conv)

    dt = _softplus(dt_ref[...] + dtb_ref[...])
    a_cs = _cumsum_rows(dt * a_ref[...])
    a_log2 = a_cs * LOG2E
    a_log2_t = a_log2.T
    dt_t = dt.T
    a_last = a_cs[lc - 1:lc, :]
    to_end = (dt * jnp.exp(a_last - a_cs)).astype(BF16)
    from_start = jnp.exp(a_cs).astype(BF16)
    chunk_decay = jnp.broadcast_to(jnp.exp(a_last), (8, LANES))
    causal = (lax.broadcasted_iota(jnp.int32, (lc, lc), 0) >= lax.broadcasted_iota(jnp.int32, (lc, lc), 1))
    heads_per_group = SSM_HEADS // SSM_GROUPS
    head_w = SSM_D_INNER // SSM_HEADS
    quad = 4
    quad_w = quad * head_w
    lane_head = lax.broadcasted_iota(jnp.int32, (1, quad_w), 1) // head_w
    n_bc = SSM_GROUPS * SSM_STATE
    for g in range(SSM_GROUPS):
        gw = slice(g * SSM_GROUP_WIDTH, (g + 1) * SSM_GROUP_WIDTH)
        xs = conv_silu(g * SSM_GROUP_WIDTH, SSM_GROUP_WIDTH)
        bm = conv_silu(SSM_D_INNER + g * SSM_STATE, SSM_STATE).astype(BF16)
        cm = conv_silu(SSM_D_INNER + n_bc + g * SSM_STATE, SSM_STATE).astype(BF16)
        e = e_ref[:, gw]
        xs_b = xs.astype(BF16)
        xdte = (xs * _dot(to_end, e)).astype(BF16)
        s_prev = st_ref[:, gw]
        cb = _dot_nt(cm, bm)
        y = _dot(cm, s_prev.astype(BF16)) * _dot(from_start, e)
        st_ref[:, gw] = (s_prev * _dot2(chunk_decay, e)[0:1, :]
                         + lax.dot_general(bm, xdte, TN, preferred_element_type=F32))
        quad_out = []
        for qd in range(heads_per_group // quad):
            h0 = g * heads_per_group + quad * qd
            ms = []
            for h in range(h0, h0 + quad):
                seg = a_log2[:, h:h + 1] - a_log2_t[h:h + 1, :]
                ms.append((cb * jnp.where(causal, jnp.exp2(seg), 0.0) * dt_t[h:h + 1, :]).astype(BF16))
            xq = xs_b[:, qd * quad_w:(qd + 1) * quad_w]
            rhs = jnp.concatenate([jnp.where(lane_head == i, xq, jnp.zeros_like(xq)) for i in range(quad)], axis=0)
            quad_out.append(_dot(jnp.concatenate(ms, axis=1), rhs))
        y = y + jnp.concatenate(quad_out, axis=-1) + dskip_ref[:, gw] * xs
        z = zx_ref[:, gw].astype(F32)
        y = y * (z * _sigmoid(z))
        y = y * lax.rsqrt(jnp.mean(y * y, axis=-1, keepdims=True) + GATE_NORM_EPS) * gn_ref[:, gw]
        y_ref[:, gw] = y.astype(y_ref.dtype)

    carry_ref[...] = zx_ref[lc - n_carry:lc, SSM_D_INNER:SSM_MAIN]


def _ssd_prompt(zx, dt, b, params, *, chunks_per_step):
    m = zx.shape[0]
    rows = chunks_per_step * SSM_CHUNK
    nc = m // b // rows
    full = lambda x: pl.BlockSpec(x.shape, lambda bb, c: (0, 0))
    row = lambda w: pl.BlockSpec((rows, w), lambda bb, c: (bb * nc + c, 0))
    return pl.pallas_call(
        _ssd_prompt_kernel,
        grid=(b, nc),
        in_specs=[row(SSM_MAIN), row(LANES)] + [full(x) for x in params],
        out_specs=[row(SSM_D_INNER),
                   pl.BlockSpec((None, SSM_CONV - 1, SSM_CONV_DIM), lambda bb, c: (bb, 0, 0)),
                   pl.BlockSpec((None, SSM_D_INNER, SSM_STATE), lambda bb, c: (bb, 0, 0))],
        out_shape=[jax.ShapeDtypeStruct((m, SSM_D_INNER), BF16),
                   jax.ShapeDtypeStruct((b, SSM_CONV - 1, SSM_CONV_DIM), F32),
                   jax.ShapeDtypeStruct((b, SSM_D_INNER, SSM_STATE), F32)],
        scratch_shapes=[pltpu.VMEM((16, SSM_CONV_DIM), BF16),
                        pltpu.VMEM((SSM_STATE, SSM_D_INNER), F32)],
        compiler_params=_params(("parallel", "arbitrary")),
        name="ssd_prompt",
    )(zx, dt, *params)


def _ssd_sample_kernel(zx_ref, dt_ref, conv0_ref, s0_ref, cw_ref, cb_ref, dtb_ref, a_ref, dskip_ref, gn_ref,
                       e_ref, eg_ref, y_ref, nconv_ref, nssm_ref, xpad_ref):
    lc = zx_ref.shape[0]
    xpad_ref[0:5, :] = jnp.zeros((5, SSM_CONV_DIM), F32)
    xpad_ref[5:8, :] = conv0_ref[...]
    xbc, tail = _conv_silu(zx_ref, xpad_ref, cw_ref, cb_ref, lc)
    nconv_ref[...] = tail

    xs = xbc[:, 0:SSM_D_INNER]
    bm = xbc[:, SSM_D_INNER:SSM_D_INNER + SSM_GROUPS * SSM_STATE]
    cm = xbc[:, SSM_D_INNER + SSM_GROUPS * SSM_STATE:]
    e = e_ref[...]
    dt = _softplus(dt_ref[...] + dtb_ref[...])
    dta = dt * a_ref[...]
    rows = [dta[0:1, :]]
    for i in range(1, lc):
        rows.append(rows[-1] + dta[i:i + 1, :])
    a_cs = jnp.concatenate(rows, axis=0)
    a_last = rows[-1]
    xdt = xs * _dot2(dt, e)
    xdte = (xdt * _dot2(jnp.exp(a_last - a_cs), e)).astype(BF16)
    ea_e = _dot2(jnp.exp(a_cs), e)
    cd_e = _dot2(jnp.broadcast_to(jnp.exp(a_last), (8, LANES)), e)[0:1, :]

    a_l = jnp.concatenate([a_cs] * lc, axis=0)
    a_s = jnp.concatenate([jnp.broadcast_to(a_cs[s:s + 1, :], (lc, LANES)) for s in range(lc)], axis=0)
    l_idx = lax.broadcasted_iota(jnp.int32, (lc * lc, 1), 0) % lc
    s_idx = lax.broadcasted_iota(jnp.int32, (lc * lc, 1), 0) // lc
    decay = jnp.where(l_idx >= s_idx, jnp.exp(a_l - a_s), 0.0)
    cb_prod = jnp.concatenate([cm * bm[s:s + 1, :] for s in range(lc)], axis=0)
    mix = _dot2(_dot2(cb_prod, eg_ref[...]) * decay, e)
    y = ea_e * 0.0
    for s in range(lc):
        y = y + mix[s * lc:(s + 1) * lc, :] * xdt[s:s + 1, :]

    s_prev = s0_ref[...].T
    s_b = s_prev.astype(BF16)
    bm_b = bm.astype(BF16)
    cm_b = cm.astype(BF16)
    y_off, s_parts = [], []
    for g in range(SSM_GROUPS):
        gs = slice(g * SSM_STATE, (g + 1) * SSM_STATE)
        gw = slice(g * SSM_GROUP_WIDTH, (g + 1) * SSM_GROUP_WIDTH)
        y_off.append(_dot(cm_b[:, gs], s_b[:, gw]))
        s_parts.append(lax.dot_general(bm_b[:, gs], xdte[:, gw], TN, preferred_element_type=F32))
    y = y + jnp.concatenate(y_off, axis=-1) * ea_e + dskip_ref[...] * xs
    s_new = s_prev * cd_e + jnp.concatenate(s_parts, axis=-1)
    nssm_ref[...] = s_new.T
    z = zx_ref[:, 0:SSM_D_INNER]
    y_ref[...] = _gate_norm(y, z, gn_ref[...])


def _ssd_sample(zx, dt, conv0, s0, params):
    b = conv0.shape[0]
    t_new = zx.shape[0] // b
    full = lambda x: pl.BlockSpec(x.shape, lambda bb: (0, 0))
    row = lambda w: pl.BlockSpec((t_new, w), lambda bb: (bb, 0))
    conv_spec = pl.BlockSpec((None, SSM_CONV - 1, SSM_CONV_DIM), lambda bb: (bb, 0, 0))
    state_spec = pl.BlockSpec((None, SSM_D_INNER, SSM_STATE), lambda bb: (bb, 0, 0))
    return pl.pallas_call(
        _ssd_sample_kernel,
        grid=(b,),
        in_specs=[row(SSM_MAIN), row(LANES), conv_spec, state_spec] + [full(x) for x in params],
        out_specs=[row(SSM_D_INNER), conv_spec, state_spec],
        out_shape=[jax.ShapeDtypeStruct((b * t_new, SSM_D_INNER), F32),
                   jax.ShapeDtypeStruct((b, SSM_CONV - 1, SSM_CONV_DIM), F32),
                   jax.ShapeDtypeStruct((b, SSM_D_INNER, SSM_STATE), F32)],
        scratch_shapes=[pltpu.VMEM((8 + t_new, SSM_CONV_DIM), F32)],
        compiler_params=_params(("parallel",)),
        name="ssd_sample",
    )(zx, dt, conv0, s0, *params)


def _one_hot_expand(n_rows, n_cols, group):
    r = lax.broadcasted_iota(jnp.int32, (n_rows, n_cols), 0)
    c = lax.broadcasted_iota(jnp.int32, (n_rows, n_cols), 1)
    return (c // group == r).astype(BF16)


def _kv_rows(feature_major, b):
    tokens = feature_major.shape[-1]
    x = feature_major.reshape(1, b, 2, N_HEADS, HEAD_DIM, tokens)
    return jnp.transpose(x, (0, 1, 5, 2, 3, 4))


def kernel(x_prompt, x_sample, cache_kv_g0, cache_kv_g1, cache_kv_g2, state_conv, state_ssm, attn_norm, attn_w_in, attn_q_gain, attn_k_gain, attn_w_out, ssm_norm, ssm_w_in, ssm_conv_w, ssm_conv_b, ssm_dt_bias, ssm_A_log, ssm_D, ssm_gate_norm, ssm_w_out):
    b, seq, dm = x_prompt.shape
    sb, st, _ = x_sample.shape
    xp = x_prompt.reshape(b * seq, dm)
    xs = x_sample.reshape(sb * st, dm)

    w_in = attn_w_in[0].astype(BF16)
    w_out = attn_w_out[0].astype(BF16)
    scale = HEAD_DIM ** -0.5
    q_gain3 = jnp.tile(attn_q_gain[0], (1, N_HEADS)) * scale
    k_gain3 = jnp.tile(attn_k_gain[0], (1, N_HEADS))

    tile_gain = jnp.ones((N_TILES, 1, ATT_WIDTH), F32)
    for g in range(len(DILATIONS)):
        tile_gain = tile_gain.at[Q_TILES[g], 0].set(q_gain3[g] * LOG2E).at[K_TILES[g], 0].set(k_gain3[g])
    qkvg = _attn_proj(xp, attn_norm[0], w_in, tile_gain, tm=2048, seq=seq)
    part2 = _attn_group(qkvg, 2, n_sub=2)
    part1 = _attn_group(qkvg, 1, n_sub=8)
    o_gated = _attn_group(qkvg, 0, n_sub=4, others=[part1, part2])
    y1p = _outproj(o_gated, w_out, xp, tm=1024, perm_seq=seq)
    kv_p = []
    for g, d in enumerate(DILATIONS):
        keep = min(d * WINDOW_STEPS, seq)
        kv_p.append(_kv_rows(_kv_tail(xp, attn_norm[0], w_in, g, k_gain3[g:g + 1], b=b, keep=keep,
                                      tm=min(keep, 1024)), b))

    proj_s = _proj(xs, attn_norm[0], w_in, tm=sb * st, tn=1024, out_dtype=F32)
    caches_t = [jnp.transpose(c[0], (0, 2, 3, 4, 1)).reshape(sb, 2, ATT_WIDTH, c.shape[2])
                for c in (cache_kv_g0, cache_kv_g1, cache_kv_g2)]
    o_s, kn0, kn1, kn2 = _sample_attn(proj_s, caches_t, q_gain3[:, :LANES], k_gain3[:, :LANES])
    y1s = _outproj(o_s, w_out, xs, tm=sb * st)
    kv_s = []
    for g, kn in enumerate((kn0, kn1, kn2)):
        v = proj_s[:, (3 * g + 2) * ATT_WIDTH:(3 * g + 3) * ATT_WIDTH]
        kv_s.append(jnp.stack([kn, v], axis=1).reshape(1, sb, st, 2, N_HEADS, HEAD_DIM))

    w_in2 = ssm_w_in[0].astype(BF16)
    w_dt = jnp.pad(w_in2[:, SSM_MAIN:], ((0, 0), (0, LANES - SSM_HEADS)))
    w_out2 = ssm_w_out[0].astype(BF16)
    pad_h = lambda v: jnp.pad(v.astype(F32), (0, LANES - SSM_HEADS)).reshape(1, LANES)
    e32 = _one_hot_expand(LANES, SSM_D_INNER, SSM_D_INNER // SSM_HEADS)
    gate_gain = ssm_gate_norm[0].reshape(1, -1)
    scan_params = (ssm_conv_w[0], ssm_conv_b[0].reshape(1, -1), pad_h(ssm_dt_bias[0]),
                   pad_h(-jnp.exp(ssm_A_log[0].astype(F32))),
                   jnp.repeat(ssm_D[0].astype(F32), SSM_D_INNER // SSM_HEADS).reshape(1, -1))

    zx_p, dt_p = _proj(y1p, ssm_norm[0], w_in2, tm=1024, tn=SSM_MAIN // 2, n_out=SSM_MAIN, out_dtype=BF16,
                       w_dt=w_dt)
    yg_p, conv_p, ssm_p = _ssd_prompt(zx_p, dt_p, b, scan_params + (gate_gain, e32), chunks_per_step=2)
    y2p = _outproj(yg_p, w_out2, y1p, tm=1024)

    eg = (lax.broadcasted_iota(jnp.int32, (SSM_GROUPS * SSM_STATE, LANES), 0) // SSM_STATE
          == lax.broadcasted_iota(jnp.int32, (SSM_GROUPS * SSM_STATE, LANES), 1) // (SSM_HEADS // SSM_GROUPS))
    eg = (eg & (lax.broadcasted_iota(jnp.int32, eg.shape, 1) < SSM_HEADS)).astype(BF16)
    zx_s, dt_s = _proj(y1s, ssm_norm[0], w_in2, tm=sb * st, tn=SSM_MAIN // 2, n_out=SSM_MAIN, out_dtype=F32,
                       w_dt=w_dt)
    yg_s, conv_s, ssm_s = _ssd_sample(zx_s, dt_s, state_conv[0], state_ssm[0].reshape(sb, SSM_D_INNER, SSM_STATE),
                                      scan_params + (gate_gain, e32, eg))
    y2s = _outproj(yg_s, w_out2, y1s, tm=sb * st)

    hp = SSM_D_INNER // SSM_HEADS
    return (y2p.reshape(b, seq, dm), y2s.reshape(sb, st, dm),
            kv_p[0], kv_p[1], kv_p[2], kv_s[0], kv_s[1], kv_s[2],
            conv_p[None], conv_s[None],
            ssm_p.reshape(1, b, SSM_HEADS, hp, SSM_STATE), ssm_s.reshape(1, sb, SSM_HEADS, hp, SSM_STATE))
```

```python
import functools

import jax
import jax.numpy as jnp
from jax import lax
from jax.experimental import pallas as pl
from jax.experimental.pallas import tpu as pltpu

F32 = jnp.float32
BF16 = jnp.bfloat16

D_MODEL = 1024
N_HEADS = 16
HEAD_DIM = 64
ATT_WIDTH = N_HEADS * HEAD_DIM
DILATIONS = (1, 4, 16)
WINDOW_STEPS = 128
ATT_IN = 10 * ATT_WIDTH
PERM = 16

SSM_D_INNER = 2048
SSM_HEADS = 32
SSM_STATE = 128
SSM_GROUPS = 4
SSM_GROUP_WIDTH = SSM_D_INNER // SSM_GROUPS
SSM_CONV = 4
SSM_CONV_DIM = SSM_D_INNER + 2 * SSM_GROUPS * SSM_STATE
SSM_MAIN = SSM_D_INNER + SSM_CONV_DIM
SSM_CHUNK = 128

NORM_EPS = 1e-6
GATE_NORM_EPS = 1e-5
MASKED = -1e30
LOG2E = 1.4426950408889634

LANES = 128
VMEM_LIMIT = 56 * 1024 * 1024

NT = (((1,), (1,)), ((), ()))
TN = (((0,), (0,)), ((), ()))


def _params(semantics):
    return pltpu.CompilerParams(dimension_semantics=semantics, vmem_limit_bytes=VMEM_LIMIT)


def _rms(x, w, eps):
    return x * lax.rsqrt(jnp.mean(x * x, axis=-1, keepdims=True) + eps) * w


def _sigmoid(x):
    return 1.0 / (1.0 + jnp.exp(-x))


def _split2(v):
    hi = v.astype(BF16)
    lo = (v - hi.astype(F32)).astype(BF16)
    return hi, lo


def _dot(a, b):
    return jnp.dot(a, b, preferred_element_type=F32)


def _dot_nt(a, b):
    return lax.dot_general(a, b, NT, preferred_element_type=F32)


def _dot2(v, e):
    hi, lo = _split2(v)
    return _dot(hi, e) + _dot(lo, e)


def _low_half():
    return lax.broadcasted_iota(jnp.int32, (1, LANES), 1) < HEAD_DIM


def _head_norm(x, eps):
    lo = _low_half()
    parts = []
    for j in range(x.shape[1] // LANES):
        t = x[:, j * LANES:(j + 1) * LANES]
        t2 = t * t
        s_lo = jnp.sum(jnp.where(lo, t2, 0.0), axis=-1, keepdims=True)
        s_hi = jnp.sum(jnp.where(lo, 0.0, t2), axis=-1, keepdims=True)
        r = jnp.where(lo, lax.rsqrt(s_lo * (1.0 / HEAD_DIM) + eps), lax.rsqrt(s_hi * (1.0 / HEAD_DIM) + eps))
        parts.append(t * r)
    return jnp.concatenate(parts, axis=-1)


def _split_heads(qp):
    lo = _low_half()
    zero = jnp.zeros_like(qp)
    return jnp.concatenate([jnp.where(lo, qp, zero), jnp.where(lo, zero, qp)], axis=0)


def _join_heads(x2):
    rows = x2.shape[0] // 2
    return jnp.where(_low_half(), x2[0:rows], x2[rows:])


def _proj_kernel(x_ref, nw_ref, w_ref, *rest, with_dt):
    rest = list(rest)
    wdt_ref = rest.pop(0) if with_dt else None
    o_ref = rest.pop(0)
    dt_ref = rest.pop(0) if with_dt else None
    h_ref = rest.pop(0)

    @pl.when(pl.program_id(1) == 0)
    def _():
        h_ref[...] = _rms(x_ref[...], nw_ref[...], NORM_EPS).astype(BF16)
        if with_dt:
            dt_ref[...] = _dot(h_ref[...], wdt_ref[...])

    o_ref[...] = _dot(h_ref[...], w_ref[...].astype(BF16)).astype(o_ref.dtype)


def _proj(x, norm_w, w, *, tm, tn, out_dtype, n_out=None, w_dt=None):
    m, dm = x.shape
    n = w.shape[1] if n_out is None else n_out
    assert n % tn == 0 and m % tm == 0
    in_specs = [pl.BlockSpec((tm, dm), lambda i, j: (i, 0)),
                pl.BlockSpec((1, dm), lambda i, j: (0, 0)),
                pl.BlockSpec((dm, tn), lambda i, j: (0, j))]
    args = [x, norm_w.reshape(1, dm), w]
    out_shape = [jax.ShapeDtypeStruct((m, n), out_dtype)]
    out_specs = [pl.BlockSpec((tm, tn), lambda i, j: (i, j))]
    if w_dt is not None:
        in_specs.append(pl.BlockSpec((dm, LANES), lambda i, j: (0, 0)))
        args.append(w_dt)
        out_shape.append(jax.ShapeDtypeStruct((m, LANES), F32))
        out_specs.append(pl.BlockSpec((tm, LANES), lambda i, j: (i, 0)))
    outs = pl.pallas_call(
        functools.partial(_proj_kernel, with_dt=w_dt is not None),
        grid=(m // tm, n // tn), in_specs=in_specs, out_specs=out_specs, out_shape=out_shape,
        scratch_shapes=[pltpu.VMEM((tm, dm), BF16)],
        compiler_params=_params(("parallel", "arbitrary")),
        name="norm_proj",
    )(*args)
    return outs if w_dt is not None else outs[0]


N_TILES = ATT_IN // ATT_WIDTH
Q_TILES, K_TILES, V_TILES, GATE_TILE = (0, 3, 6), (1, 4, 7), (2, 5, 8), 9
STREAM_A = (0, 1, 3, 4, 6)
STREAM_B = (2, 5, 8, 9, 7)


def _lookup(j, table):
    out = jnp.int32(table[0])
    for t, v in enumerate(table[1:], 1):
        out = jnp.where(j == t, jnp.int32(v), out)
    return out


def _attn_proj_kernel(x_ref, nw_ref, wa_ref, wb_ref, ga_ref, gb_ref, oa_ref, ob_ref, h_ref, slab_ref):
    j = pl.program_id(1)

    @pl.when(j == 0)
    def _():
        xn = _rms(x_ref[...], nw_ref[...], NORM_EPS)
        rows = x_ref.shape[0] // PERM
        for c in range(x_ref.shape[1] // LANES):
            cols = slice(c * LANES, (c + 1) * LANES)
            slab_ref[...] = xn[:, cols]
            for r in range(PERM):
                h_ref[r * rows:(r + 1) * rows, cols] = slab_ref[pl.ds(r, rows, stride=PERM), :].astype(BF16)

    def tile(w_ref, gain_ref, o_ref, normed):
        res = _dot(h_ref[...], w_ref[...].astype(BF16))
        if normed:
            res = _head_norm(res, NORM_EPS) * gain_ref[0]
        o_ref[...] = res.astype(o_ref.dtype).reshape(o_ref.shape)

    last = pl.num_programs(1) - 1

    @pl.when(j < last)
    def _():
        tile(wa_ref, ga_ref, oa_ref, True)
        tile(wb_ref, gb_ref, ob_ref, False)

    @pl.when(j == last)
    def _():
        tile(wa_ref, ga_ref, oa_ref, True)
        tile(wb_ref, gb_ref, ob_ref, True)


def _attn_proj(x, norm_w, w, tile_gain, *, tm, seq):
    m, dm = x.shape
    per_b = seq // tm
    rows = tm // PERM
    n_steps = len(STREAM_A)
    w_spec = lambda table: pl.BlockSpec((dm, ATT_WIDTH), lambda i, j: (0, _lookup(j, table)))
    g_spec = lambda table: pl.BlockSpec((1, 1, ATT_WIDTH), lambda i, j: (_lookup(j, table), 0, 0))
    out_spec = pl.BlockSpec((None, PERM, rows, ATT_WIDTH), lambda i, j: (i // per_b, 0, i % per_b, j))
    out_sds = jax.ShapeDtypeStruct((m // seq, PERM, seq // PERM, n_steps * ATT_WIDTH), BF16)
    return pl.pallas_call(
        _attn_proj_kernel,
        grid=(m // tm, n_steps),
        in_specs=[pl.BlockSpec((tm, dm), lambda i, j: (i, 0)), pl.BlockSpec((1, dm), lambda i, j: (0, 0)),
                  w_spec(STREAM_A), w_spec(STREAM_B), g_spec(STREAM_A), g_spec(STREAM_B)],
        out_specs=[out_spec, out_spec], out_shape=[out_sds, out_sds],
        scratch_shapes=[pltpu.VMEM((tm, dm), BF16), pltpu.VMEM((tm, LANES), F32)],
        compiler_params=_params(("parallel", "arbitrary")),
        name="attn_proj",
    )(x, norm_w.reshape(1, dm), w, w, tile_gain, tile_gain)


def _outproj_kernel(a_ref, w_ref, x_ref, o_ref, *scratch, unperm):
    tm = o_ref.shape[0]
    a = a_ref[...].reshape(tm, a_ref.shape[-1]).astype(BF16)
    res = _dot(a, w_ref[...])
    if unperm:
        slab_ref, = scratch
        rows = tm // PERM
        for c in range(o_ref.shape[1] // LANES):
            cols = slice(c * LANES, (c + 1) * LANES)
            for r in range(PERM):
                slab_ref[pl.ds(r, rows, stride=PERM), :] = res[r * rows:(r + 1) * rows, cols]
            o_ref[:, cols] = x_ref[:, cols] + slab_ref[...]
    else:
        o_ref[...] = x_ref[...] + res


def _outproj(a, w, x, *, tm, perm_seq=None):
    m, n = x.shape
    k = w.shape[0]
    if perm_seq is not None:
        per_b = perm_seq // tm
        a_spec = pl.BlockSpec((None, PERM, tm // PERM, k), lambda i: (i // per_b, 0, i % per_b, 0))
    else:
        a_spec = pl.BlockSpec((tm, k), lambda i: (i, 0))
    return pl.pallas_call(
        functools.partial(_outproj_kernel, unperm=perm_seq is not None),
        grid=(m // tm,),
        in_specs=[a_spec, pl.BlockSpec((k, n), lambda i: (0, 0)), pl.BlockSpec((tm, n), lambda i: (i, 0))],
        out_specs=pl.BlockSpec((tm, n), lambda i: (i, 0)),
        out_shape=jax.ShapeDtypeStruct((m, n), F32),
        scratch_shapes=[pltpu.VMEM((tm, LANES), F32)] if perm_seq else [],
        compiler_params=_params(("parallel",)),
        name="out_proj",
    )(a, w, x)


ATT_SUB = 128


def _attn_group_kernel(*refs, n_chunks, n_sub, merge):
    it = iter(refs)
    q_ref, k_ref, v_ref = next(it), next(it), next(it)
    if merge:
        gate_ref = next(it)
        others = [(next(it), next(it)) for _ in range(len(DILATIONS) - 1)]
        o_out = next(it)
    else:
        o_out, lse_out = next(it), next(it)
    ks_ref, vs_ref = next(it), next(it)

    rc = q_ref.shape[1] // n_sub
    assert n_chunks * rc == ATT_SUB
    n = pl.program_id(2)
    narrow = rc % 16 != 0

    def piece(ref, u, cols=slice(None)):
        if narrow and ref.dtype == BF16:
            pair = 2 * (u // 2) * rc
            x = ref[:, pair:pair + 2 * rc, cols].astype(F32)[:, (u % 2) * rc:(u % 2 + 1) * rc, :]
        else:
            x = ref[:, u * rc:(u + 1) * rc, cols]
        return x.reshape(ATT_SUB, x.shape[-1])

    def unsubs(parts, dtype):
        parts = [p.reshape(n_chunks, rc, p.shape[-1]) for p in parts]
        if narrow:
            return jnp.concatenate(parts, axis=1).astype(dtype)
        return jnp.concatenate([p.astype(dtype) for p in parts], axis=1)

    @pl.when(n == 0)
    def _():
        ks_ref[0:ATT_SUB, :] = jnp.zeros((ATT_SUB, ATT_WIDTH), BF16)
        vs_ref[0:ATT_SUB, :] = jnp.zeros((ATT_SUB, ATT_WIDTH), BF16)

    for u in range(n_sub):
        ks_ref[(u + 1) * ATT_SUB:(u + 2) * ATT_SUB, :] = piece(k_ref, u).astype(BF16)
        vs_ref[(u + 1) * ATT_SUB:(u + 2) * ATT_SUB, :] = piece(v_ref, u).astype(BF16)

    qi = lax.broadcasted_iota(jnp.int32, (ATT_SUB, 2 * ATT_SUB), 0)
    kj = lax.broadcasted_iota(jnp.int32, (ATT_SUB, 2 * ATT_SUB), 1)
    kc = jnp.where(kj >= ATT_SUB, kj - ATT_SUB, kj)
    tq = n_chunks * (qi % rc) + qi // rc
    tk = n_chunks * (kc % rc) + kc // rc + jnp.where(kj >= ATT_SUB, 0, -ATT_SUB)
    dist = tq - tk
    in_band = (dist >= 0) & (dist <= WINDOW_STEPS)
    band = jnp.where(in_band, 0.0, MASKED)
    band_first = jnp.where(in_band & ((kj >= ATT_SUB) | (n > 0)), 0.0, MASKED)
    bias = [jnp.concatenate([x, x], axis=0) for x in (band_first, band)]
    ones = jnp.ones((2 * ATT_SUB, LANES), BF16)

    for j in range(N_HEADS // 2):
        cols = slice(j * LANES, (j + 1) * LANES)
        o_parts, lse_parts = [], []
        for u in range(n_sub):
            keys = slice(u * ATT_SUB, (u + 2) * ATT_SUB)
            q2 = _split_heads(piece(q_ref, u, cols).astype(BF16))
            s = _dot_nt(q2, ks_ref[keys, cols]) + bias[min(u, 1)]
            m = jnp.max(s, axis=-1, keepdims=True)
            p = jnp.exp2(s - m).astype(BF16)
            pv = _dot(p, jnp.concatenate([vs_ref[keys, cols], ones], axis=1))
            l = pv[:, LANES:]
            o = _join_heads(pv[:, :LANES] / l)
            lse = _join_heads(m + jnp.log2(l))
            if merge:
                o_g = [o] + [piece(o_ref, u, cols).astype(F32) for o_ref, _ in others]
                lse_g = [lse] + [piece(l_ref, u, cols) for _, l_ref in others]
                top = jnp.maximum(jnp.maximum(lse_g[0], lse_g[1]), lse_g[2])
                w = [jnp.exp2(x - top) for x in lse_g]
                o = (w[0] * o_g[0] + w[1] * o_g[1] + w[2] * o_g[2]) / (w[0] + w[1] + w[2])
                g = piece(gate_ref, u, cols).astype(F32)
                o = o * (g * _sigmoid(g))
            o_parts.append(o)
            lse_parts.append(lse)
        o_out[:, :, cols] = unsubs(o_parts, BF16)
        if not merge:
            lse_out[:, :, cols] = unsubs(lse_parts, F32)
    ks_ref[0:ATT_SUB, :] = ks_ref[n_sub * ATT_SUB:(n_sub + 1) * ATT_SUB, :]
    vs_ref[0:ATT_SUB, :] = vs_ref[n_sub * ATT_SUB:(n_sub + 1) * ATT_SUB, :]


def _attn_group(qkvg, g, *, n_sub, others=None):
    b, _, t, _ = qkvg[0].shape
    d = DILATIONS[g]
    n_chunks = PERM // d
    rc = ATT_SUB // n_chunks * n_sub
    nb = t // rc
    merge = others is not None
    view = lambda a: a.reshape(b, n_chunks, d, t, a.shape[-1])

    def spec(col=0):
        return pl.BlockSpec((None, n_chunks, None, rc, ATT_WIDTH), lambda bb, r, n: (bb, 0, r, n, col))

    in_specs, args = [], []

    def add_tile(tile):
        stream, table = (0, STREAM_A) if tile in STREAM_A else (1, STREAM_B)
        in_specs.append(spec(table.index(tile)))
        args.append(view(qkvg[stream]))

    for tile in (Q_TILES[g], K_TILES[g], V_TILES[g]):
        add_tile(tile)
    full = jax.ShapeDtypeStruct((b, n_chunks, d, t, ATT_WIDTH), BF16)
    if merge:
        add_tile(GATE_TILE)
        for o, lse in others:
            in_specs += [spec(), spec()]
            args += [view(o), view(lse)]
        out_shape, out_specs = [full], [spec()]
    else:
        out_shape = [full, jax.ShapeDtypeStruct(full.shape, F32)]
        out_specs = [spec(), spec()]
    outs = pl.pallas_call(
        functools.partial(_attn_group_kernel, n_chunks=n_chunks, n_sub=n_sub, merge=merge),
        grid=(b, d, nb), in_specs=in_specs, out_specs=out_specs, out_shape=out_shape,
        scratch_shapes=[pltpu.VMEM(((n_sub + 1) * ATT_SUB, ATT_WIDTH), BF16)] * 2,
        compiler_params=_params(("parallel", "parallel", "arbitrary")),
        name=f"attn_group{g}",
    )(*args)
    outs = [a.reshape(b, PERM, t, ATT_WIDTH) for a in outs]
    return outs[0] if merge else tuple(outs)


def _kv_tail_kernel(x_ref, nw_ref, wk_ref, wv_ref, kg_ref, o_ref):
    h = _rms(x_ref[...], nw_ref[...], NORM_EPS).astype(BF16)
    o_ref[0:ATT_WIDTH, :] = (_head_norm(_dot(h, wk_ref[...].astype(BF16)), NORM_EPS) * kg_ref[...]).T
    o_ref[ATT_WIDTH:2 * ATT_WIDTH, :] = _dot(h, wv_ref[...].astype(BF16)).T


def _kv_tail(x, norm_w, w, g, k_gain, *, b, keep, tm):
    seq = x.shape[0] // b
    first = (seq - keep) // tm
    w_spec = lambda col: pl.BlockSpec((D_MODEL, ATT_WIDTH), lambda bb, i: (0, col))
    return pl.pallas_call(
        _kv_tail_kernel,
        grid=(b, keep // tm),
        in_specs=[pl.BlockSpec((tm, D_MODEL), lambda bb, i: (bb * (seq // tm) + first + i, 0)),
                  pl.BlockSpec((1, D_MODEL), lambda bb, i: (0, 0)),
                  w_spec(3 * g + 1), w_spec(3 * g + 2),
                  pl.BlockSpec((1, ATT_WIDTH), lambda bb, i: (0, 0))],
        out_specs=pl.BlockSpec((None, 2 * ATT_WIDTH, tm), lambda bb, i: (bb, 0, i)),
        out_shape=jax.ShapeDtypeStruct((b, 2 * ATT_WIDTH, keep), F32),
        compiler_params=_params(("parallel", "parallel")),
        name="kv_tail",
    )(x, norm_w.reshape(1, D_MODEL), w, w, k_gain)


SAMPLE_HEADS = 8


def _sample_attn_kernel(*refs):
    n_g = len(DILATIONS)
    qkv = [refs[3 * g:3 * g + 3] for g in range(n_g)]
    gate_ref = refs[3 * n_g]
    c_refs = refs[3 * n_g + 1:4 * n_g + 1]
    qg_ref, kg_ref = refs[4 * n_g + 1:4 * n_g + 3]
    o_ref = refs[4 * n_g + 3]
    kn_refs = refs[4 * n_g + 4:]
    t_new = o_ref.shape[0]
    t_row = lax.broadcasted_iota(jnp.int32, (2 * t_new, 1), 0) % t_new
    t_col = lax.broadcasted_iota(jnp.int32, (1, t_new), 1)

    valid_c, valid_n = [], []
    for g, d in enumerate(DILATIONS):
        assert d & (d - 1) == 0
        length = c_refs[g].shape[2]
        back = length + t_row - lax.broadcasted_iota(jnp.int32, (1, length), 1)
        valid_c.append(((back & (d - 1)) == 0) & (back <= WINDOW_STEPS * d))
        back = t_row - t_col
        valid_n.append((back >= 0) & ((back & (d - 1)) == 0) & (back <= WINDOW_STEPS * d))
    ones_rows = jnp.ones((LANES, c_refs[-1].shape[2]), BF16)
    ones_new = jnp.ones((t_new, LANES), BF16)

    o_cols = []
    kn_cols = [[] for _ in DILATIONS]
    for j in range(SAMPLE_HEADS // 2):
        cols = slice(j * LANES, (j + 1) * LANES)
        o_g, m_g, l_g = [], [], []
        for g, d in enumerate(DILATIONS):
            q_ref, k_ref, v_ref = qkv[g]
            qm = _split_heads(_head_norm(q_ref[:, cols], NORM_EPS) * qg_ref[g:g + 1, :]).astype(BF16)
            kn = _head_norm(k_ref[:, cols], NORM_EPS) * kg_ref[g:g + 1, :]
            kn_cols[g].append(kn)
            length = c_refs[g].shape[2]
            s_c = jnp.where(valid_c[g], _dot(qm, c_refs[g][0, cols, :].astype(BF16)), MASKED)
            s_n = jnp.where(valid_n[g], _dot_nt(qm, kn.astype(BF16)), MASKED)
            m = jnp.maximum(jnp.max(s_c, axis=-1, keepdims=True), jnp.max(s_n, axis=-1, keepdims=True))
            vt = jnp.concatenate([c_refs[g][1, cols, :].astype(BF16), ones_rows[:, 0:length]], axis=0)
            vn = jnp.concatenate([v_ref[:, cols].astype(BF16), ones_new], axis=1)
            pv = (_dot_nt(jnp.exp(s_c - m).astype(BF16), vt)
                  + _dot(jnp.exp(s_n - m).astype(BF16), vn))
            o_g.append(pv[:, :LANES])
            l_g.append(pv[:, LANES:])
            m_g.append(m)
        top = jnp.maximum(jnp.maximum(m_g[0], m_g[1]), m_g[2])
        f = [jnp.exp(m - top) for m in m_g]
        den = f[0] * l_g[0] + f[1] * l_g[1] + f[2] * l_g[2]
        o = _join_heads((f[0] * o_g[0] + f[1] * o_g[1] + f[2] * o_g[2]) / den)
        gate = gate_ref[:, cols]
        o_cols.append(o * (gate * _sigmoid(gate)))
    o_ref[...] = jnp.concatenate(o_cols, axis=-1)
    for g in range(n_g):
        kn_refs[g][...] = jnp.concatenate(kn_cols[g], axis=-1)


def _sample_attn(proj, caches_t, q_gain, k_gain):
    b = caches_t[0].shape[0]
    t_new = proj.shape[0] // b
    width = SAMPLE_HEADS * HEAD_DIM
    per_chunk = ATT_WIDTH // width
    col = lambda c: pl.BlockSpec((t_new, width), lambda bb, hc: (bb, c * per_chunk + hc))
    in_specs, args = [], []
    for g in range(len(DILATIONS)):
        in_specs += [col(3 * g), col(3 * g + 1), col(3 * g + 2)]
        args += [proj, proj, proj]
    in_specs.append(col(ATT_IN // ATT_WIDTH - 1))
    args.append(proj)
    for c in caches_t:
        in_specs.append(pl.BlockSpec((None, 2, width, c.shape[3]), lambda bb, hc: (bb, 0, hc, 0)))
        args.append(c)
    gain_spec = pl.BlockSpec((len(DILATIONS), LANES), lambda bb, hc: (0, 0))
    out_spec = pl.BlockSpec((t_new, width), lambda bb, hc: (bb, hc))
    out_sds = jax.ShapeDtypeStruct((b * t_new, ATT_WIDTH), F32)
    return pl.pallas_call(
        _sample_attn_kernel,
        grid=(b, per_chunk),
        in_specs=in_specs + [gain_spec, gain_spec],
        out_specs=[out_spec] * 4,
        out_shape=[out_sds] * 4,
        compiler_params=_params(("parallel", "parallel")),
        name="sample_attn",
    )(*args, q_gain, k_gain)


def _softplus(x):
    return jnp.maximum(x, 0.0) + jnp.log(1.0 + jnp.exp(-jnp.abs(x)))


def _conv_silu(zx_ref, xpad_ref, cw_ref, cb_ref, lc):
    pad = 8
    xpad_ref[pad:pad + lc, :] = zx_ref[:, SSM_D_INNER:SSM_MAIN].astype(F32)
    conv = cb_ref[...] + xpad_ref[pad - 3:pad - 3 + lc, :] * cw_ref[0:1, :]
    for j in range(1, SSM_CONV):
        conv = conv + xpad_ref[pad - 3 + j:pad - 3 + j + lc, :] * cw_ref[j:j + 1, :]
    tail = xpad_ref[pad + lc - 3:pad + lc, :]
    xpad_ref[pad - 3:pad, :] = tail
    return conv * _sigmoid(conv), tail


def _gate_norm(y, z, gn):
    y = y * (z * _sigmoid(z))
    parts = []
    for g in range(SSM_GROUPS):
        yg = y[:, g * SSM_GROUP_WIDTH:(g + 1) * SSM_GROUP_WIDTH]
        parts.append(yg * lax.rsqrt(jnp.mean(yg * yg, axis=-1, keepdims=True) + GATE_NORM_EPS))
    return jnp.concatenate(parts, axis=-1) * gn


def _cumsum_rows(x):
    rows = x.shape[0]
    tril = (lax.broadcasted_iota(jnp.int32, (rows, rows), 0)
            >= lax.broadcasted_iota(jnp.int32, (rows, rows), 1)).astype(BF16)
    h1 = x.astype(BF16)
    r1 = x - h1.astype(F32)
    h2 = r1.astype(BF16)
    h3 = (r1 - h2.astype(F32)).astype(BF16)
    return _dot(tril, h1) + _dot(tril, h2) + _dot(tril, h3)


def _ssd_prompt_kernel(zx_ref, dt_ref, cw_ref, cb_ref, dtb_ref, a_ref, dskip_ref, gn_ref, e_ref,
                       y_ref, nconv_ref, nssm_ref, carry_ref, st_ref):
    c = pl.program_id(1)
    last = pl.num_programs(1) - 1

    @pl.when(c == 0)
    def _():
        carry_ref[...] = jnp.zeros(carry_ref.shape, BF16)
        st_ref[...] = jnp.zeros(st_ref.shape, F32)

    for u in range(zx_ref.shape[0] // SSM_CHUNK):
        rows = pl.ds(u * SSM_CHUNK, SSM_CHUNK)
        _ssd_chunk(zx_ref.at[rows], dt_ref.at[rows], cw_ref, cb_ref, dtb_ref, a_ref, dskip_ref, gn_ref, e_ref,
                   y_ref.at[rows], carry_ref, st_ref)

    @pl.when(c == last)
    def _():
        n_carry = carry_ref.shape[0]
        nconv_ref[...] = carry_ref[...].astype(F32)[n_carry - (SSM_CONV - 1):n_carry]
        nssm_ref[...] = st_ref[...].T


def _ssd_chunk(zx_ref, dt_ref, cw_ref, cb_ref, dtb_ref, a_ref, dskip_ref, gn_ref, e_ref, y_ref, carry_ref, st_ref):
    lc = zx_ref.shape[0]
    n_carry = carry_ref.shape[0]
    src = lax.broadcasted_iota(jnp.int32, (lc, n_carry + lc), 1) - n_carry
    dst = lax.broadcasted_iota(jnp.int32, (lc, n_carry + lc), 0)
    taps = SSM_CONV - 1
    shift = jnp.concatenate([(src == dst - (taps - j)).astype(BF16) for j in range(taps)], axis=0)

    def conv_silu(start, width):
        cols = slice(start, start + width)
        raw = zx_ref[:, SSM_D_INNER + start:SSM_D_INNER + start + width]
        shifted = _dot(shift, jnp.concatenate([carry_ref[:, cols], raw], axis=0))
        conv = cb_ref[:, cols] + shifted[0:lc] * cw_ref[0:1, cols]
        for j in range(1, taps):
            conv = conv + shifted[j * lc:(j + 1) * lc] * cw_ref[j:j + 1, cols]
        conv = conv + raw.astype(F32) * cw_ref[taps:taps + 1, cols]
        return conv * _sigmoid(conv)

    dt = _softplus(dt_ref[...] + dtb_ref[...])
    a_cs = _cumsum_rows(dt * a_ref[...])
    a_log2 = a_cs * LOG2E
    a_log2_t = a_log2.T
    dt_t = dt.T
    a_last = a_cs[lc - 1:lc, :]
    to_end = (dt * jnp.exp(a_last - a_cs)).astype(BF16)
    from_start = jnp.exp(a_cs).astype(BF16)
    chunk_decay = jnp.broadcast_to(jnp.exp(a_last), (8, LANES))
    causal = (lax.broadcasted_iota(jnp.int32, (lc, lc), 0) >= lax.broadcasted_iota(jnp.int32, (lc, lc), 1))
    heads_per_group = SSM_HEADS // SSM_GROUPS
    head_w = SSM_D_INNER // SSM_HEADS
    quad = 4
    quad_w = quad * head_w
    lane_head = lax.broadcasted_iota(jnp.int32, (1, quad_w), 1) // head_w
    n_bc = SSM_GROUPS * SSM_STATE
    for g in range(SSM_GROUPS):
        gw = slice(g * SSM_GROUP_WIDTH, (g + 1) * SSM_GROUP_WIDTH)
        xs = conv_silu(g * SSM_GROUP_WIDTH, SSM_GROUP_WIDTH)
        bm = conv_silu(SSM_D_INNER + g * SSM_STATE, SSM_STATE).astype(BF16)
        cm = conv_silu(SSM_D_INNER + n_bc + g * SSM_STATE, SSM_STATE).astype(BF16)
        e = e_ref[:, gw]
        xs_b = xs.astype(BF16)
        xdte = (xs * _dot(to_end, e)).astype(BF16)
        s_prev = st_ref[:, gw]
        cb = _dot_nt(cm, bm)
        y = _dot(cm, s_prev.astype(BF16)) * _dot(from_start, e)
        st_ref[:, gw] = (s_prev * _dot2(chunk_decay, e)[0:1, :]
                         + lax.dot_general(bm, xdte, TN, preferred_element_type=F32))
        quad_out = []
        for qd in range(heads_per_group // quad):
            h0 = g * heads_per_group + quad * qd
            ms = []
            for h in range(h0, h0 + quad):
                seg = a_log2[:, h:h + 1] - a_log2_t[h:h + 1, :]
                ms.append((cb * jnp.where(causal, jnp.exp2(seg), 0.0) * dt_t[h:h + 1, :]).astype(BF16))
            xq = xs_b[:, qd * quad_w:(qd + 1) * quad_w]
            rhs = jnp.concatenate([jnp.where(lane_head == i, xq, jnp.zeros_like(xq)) for i in range(quad)], axis=0)
            quad_out.append(_dot(jnp.concatenate(ms, axis=1), rhs))
        y = y + jnp.concatenate(quad_out, axis=-1) + dskip_ref[:, gw] * xs
        z = zx_ref[:, gw].astype(F32)
        y = y * (z * _sigmoid(z))
        y = y * lax.rsqrt(jnp.mean(y * y, axis=-1, keepdims=True) + GATE_NORM_EPS) * gn_ref[:, gw]
        y_ref[:, gw] = y.astype(y_ref.dtype)

    carry_ref[...] = zx_ref[lc - n_carry:lc, SSM_D_INNER:SSM_MAIN]


def _ssd_prompt(zx, dt, b, params, *, chunks_per_step):
    m = zx.shape[0]
    rows = chunks_per_step * SSM_CHUNK
    nc = m // b // rows
    full = lambda x: pl.BlockSpec(x.shape, lambda bb, c: (0, 0))
    row = lambda w: pl.BlockSpec((rows, w), lambda bb, c: (bb * nc + c, 0))
    return pl.pallas_call(
        _ssd_prompt_kernel,
        grid=(b, nc),
        in_specs=[row(SSM_MAIN), row(LANES)] + [full(x) for x in params],
        out_specs=[row(SSM_D_INNER),
                   pl.BlockSpec((None, SSM_CONV - 1, SSM_CONV_DIM), lambda bb, c: (bb, 0, 0)),
                   pl.BlockSpec((None, SSM_D_INNER, SSM_STATE), lambda bb, c: (bb, 0, 0))],
        out_shape=[jax.ShapeDtypeStruct((m, SSM_D_INNER), BF16),
                   jax.ShapeDtypeStruct((b, SSM_CONV - 1, SSM_CONV_DIM), F32),
                   jax.ShapeDtypeStruct((b, SSM_D_INNER, SSM_STATE), F32)],
        scratch_shapes=[pltpu.VMEM((16, SSM_CONV_DIM), BF16),
                        pltpu.VMEM((SSM_STATE, SSM_D_INNER), F32)],
        compiler_params=_params(("parallel", "arbitrary")),
        name="ssd_prompt",
    )(zx, dt, *params)


def _ssd_sample_kernel(zx_ref, dt_ref, conv0_ref, s0_ref, cw_ref, cb_ref, dtb_ref, a_ref, dskip_ref, gn_ref,
                       e_ref, eg_ref, y_ref, nconv_ref, nssm_ref, xpad_ref):
    lc = zx_ref.shape[0]
    xpad_ref[0:5, :] = jnp.zeros((5, SSM_CONV_DIM), F32)
    xpad_ref[5:8, :] = conv0_ref[...]
    xbc, tail = _conv_silu(zx_ref, xpad_ref, cw_ref, cb_ref, lc)
    nconv_ref[...] = tail

    xs = xbc[:, 0:SSM_D_INNER]
    bm = xbc[:, SSM_D_INNER:SSM_D_INNER + SSM_GROUPS * SSM_STATE]
    cm = xbc[:, SSM_D_INNER + SSM_GROUPS * SSM_STATE:]
    e = e_ref[...]
    dt = _softplus(dt_ref[...] + dtb_ref[...])
    dta = dt * a_ref[...]
    rows = [dta[0:1, :]]
    for i in range(1, lc):
        rows.append(rows[-1] + dta[i:i + 1, :])
    a_cs = jnp.concatenate(rows, axis=0)
    a_last = rows[-1]
    xdt = xs * _dot2(dt, e)
    xdte = (xdt * _dot2(jnp.exp(a_last - a_cs), e)).astype(BF16)
    ea_e = _dot2(jnp.exp(a_cs), e)
    cd_e = _dot2(jnp.broadcast_to(jnp.exp(a_last), (8, LANES)), e)[0:1, :]

    a_l = jnp.concatenate([a_cs] * lc, axis=0)
    a_s = jnp.concatenate([jnp.broadcast_to(a_cs[s:s + 1, :], (lc, LANES)) for s in range(lc)], axis=0)
    l_idx = lax.broadcasted_iota(jnp.int32, (lc * lc, 1), 0) % lc
    s_idx = lax.broadcasted_iota(jnp.int32, (lc * lc, 1), 0) // lc
    decay = jnp.where(l_idx >= s_idx, jnp.exp(a_l - a_s), 0.0)
    cb_prod = jnp.concatenate([cm * bm[s:s + 1, :] for s in range(lc)], axis=0)
    mix = _dot2(_dot2(cb_prod, eg_ref[...]) * decay, e)
    y = ea_e * 0.0
    for s in range(lc):
        y = y + mix[s * lc:(s + 1) * lc, :] * xdt[s:s + 1, :]

    s_prev = s0_ref[...].T
    s_b = s_prev.astype(BF16)
    bm_b = bm.astype(BF16)
    cm_b = cm.astype(BF16)
    y_off, s_parts = [], []
    for g in range(SSM_GROUPS):
        gs = slice(g * SSM_STATE, (g + 1) * SSM_STATE)
        gw = slice(g * SSM_GROUP_WIDTH, (g + 1) * SSM_GROUP_WIDTH)
        y_off.append(_dot(cm_b[:, gs], s_b[:, gw]))
        s_parts.append(lax.dot_general(bm_b[:, gs], xdte[:, gw], TN, preferred_element_type=F32))
    y = y + jnp.concatenate(y_off, axis=-1) * ea_e + dskip_ref[...] * xs
    s_new = s_prev * cd_e + jnp.concatenate(s_parts, axis=-1)
    nssm_ref[...] = s_new.T
    z = zx_ref[:, 0:SSM_D_INNER]
    y_ref[...] = _gate_norm(y, z, gn_ref[...])


def _ssd_sample(zx, dt, conv0, s0, params):
    b = conv0.shape[0]
    t_new = zx.shape[0] // b
    full = lambda x: pl.BlockSpec(x.shape, lambda bb: (0, 0))
    row = lambda w: pl.BlockSpec((t_new, w), lambda bb: (bb, 0))
    conv_spec = pl.BlockSpec((None, SSM_CONV - 1, SSM_CONV_DIM), lambda bb: (bb, 0, 0))
    state_spec = pl.BlockSpec((None, SSM_D_INNER, SSM_STATE), lambda bb: (bb, 0, 0))
    return pl.pallas_call(
        _ssd_sample_kernel,
        grid=(b,),
        in_specs=[row(SSM_MAIN), row(LANES), conv_spec, state_spec] + [full(x) for x in params],
        out_specs=[row(SSM_D_INNER), conv_spec, state_spec],
        out_shape=[jax.ShapeDtypeStruct((b * t_new, SSM_D_INNER), F32),
                   jax.ShapeDtypeStruct((b, SSM_CONV - 1, SSM_CONV_DIM), F32),
                   jax.ShapeDtypeStruct((b, SSM_D_INNER, SSM_STATE), F32)],
        scratch_shapes=[pltpu.VMEM((8 + t_new, SSM_CONV_DIM), F32)],
        compiler_params=_params(("parallel",)),
        name="ssd_sample",
    )(zx, dt, conv0, s0, *params)


def _one_hot_expand(n_rows, n_cols, group):
    r = lax.broadcasted_iota(jnp.int32, (n_rows, n_cols), 0)
    c = lax.broadcasted_iota(jnp.int32, (n_rows, n_cols), 1)
    return (c // group == r).astype(BF16)


def _kv_rows(feature_major, b):
    tokens = feature_major.shape[-1]
    x = feature_major.reshape(1, b, 2, N_HEADS, HEAD_DIM, tokens)
    return jnp.transpose(x, (0, 1, 5, 2, 3, 4))


def kernel(x_prompt, x_sample, cache_kv_g0, cache_kv_g1, cache_kv_g2, state_conv, state_ssm, attn_norm, attn_w_in, attn_q_gain, attn_k_gain, attn_w_out, ssm_norm, ssm_w_in, ssm_conv_w, ssm_conv_b, ssm_dt_bias, ssm_A_log, ssm_D, ssm_gate_norm, ssm_w_out):
    b, seq, dm = x_prompt.shape
    sb, st, _ = x_sample.shape
    xp = x_prompt.reshape(b * seq, dm)
    xs = x_sample.reshape(sb * st, dm)

    w_in = attn_w_in[0]
    w_out = attn_w_out[0].astype(BF16)
    scale = HEAD_DIM ** -0.5
    q_gain3 = jnp.tile(attn_q_gain[0], (1, N_HEADS)) * scale
    k_gain3 = jnp.tile(attn_k_gain[0], (1, N_HEADS))

    tile_gain = jnp.ones((N_TILES, 1, ATT_WIDTH), F32)
    for g in range(len(DILATIONS)):
        tile_gain = tile_gain.at[Q_TILES[g], 0].set(q_gain3[g] * LOG2E).at[K_TILES[g], 0].set(k_gain3[g])
    qkvg = _attn_proj(xp, attn_norm[0], w_in, tile_gain, tm=1024, seq=seq)
    part2 = _attn_group(qkvg, 2, n_sub=2)
    part1 = _attn_group(qkvg, 1, n_sub=8)
    o_gated = _attn_group(qkvg, 0, n_sub=4, others=[part1, part2])
    y1p = _outproj(o_gated, w_out, xp, tm=1024, perm_seq=seq)
    kv_p = []
    for g, d in enumerate(DILATIONS):
        keep = min(d * WINDOW_STEPS, seq)
        kv_p.append(_kv_rows(_kv_tail(xp, attn_norm[0], w_in, g, k_gain3[g:g + 1], b=b, keep=keep,
                                      tm=min(keep, 1024)), b))

    proj_s = _proj(xs, attn_norm[0], w_in, tm=sb * st, tn=1024, out_dtype=F32)
    caches_t = [jnp.transpose(c[0], (0, 2, 3, 4, 1)).reshape(sb, 2, ATT_WIDTH, c.shape[2])
                for c in (cache_kv_g0, cache_kv_g1, cache_kv_g2)]
    o_s, kn0, kn1, kn2 = _sample_attn(proj_s, caches_t, q_gain3[:, :LANES], k_gain3[:, :LANES])
    y1s = _outproj(o_s, w_out, xs, tm=sb * st)
    kv_s = []
    for g, kn in enumerate((kn0, kn1, kn2)):
        v = proj_s[:, (3 * g + 2) * ATT_WIDTH:(3 * g + 3) * ATT_WIDTH]
        kv_s.append(jnp.stack([kn, v], axis=1).reshape(1, sb, st, 2, N_HEADS, HEAD_DIM))

    w_in2 = ssm_w_in[0].astype(BF16)
    w_dt = jnp.pad(w_in2[:, SSM_MAIN:], ((0, 0), (0, LANES - SSM_HEADS)))
    w_out2 = ssm_w_out[0].astype(BF16)
    pad_h = lambda v: jnp.pad(v.astype(F32), (0, LANES - SSM_HEADS)).reshape(1, LANES)
    e32 = _one_hot_expand(LANES, SSM_D_INNER, SSM_D_INNER // SSM_HEADS)
    gate_gain = ssm_gate_norm[0].reshape(1, -1)
    scan_params = (ssm_conv_w[0], ssm_conv_b[0].reshape(1, -1), pad_h(ssm_dt_bias[0]),
                   pad_h(-jnp.exp(ssm_A_log[0].astype(F32))),
                   jnp.repeat(ssm_D[0].astype(F32), SSM_D_INNER // SSM_HEADS).reshape(1, -1))

    zx_p, dt_p = _proj(y1p, ssm_norm[0], w_in2, tm=1024, tn=SSM_MAIN // 2, n_out=SSM_MAIN, out_dtype=BF16,
                       w_dt=w_dt)
    yg_p, conv_p, ssm_p = _ssd_prompt(zx_p, dt_p, b, scan_params + (gate_gain, e32), chunks_per_step=4)
    y2p = _outproj(yg_p, w_out2, y1p, tm=1024)

    eg = (lax.broadcasted_iota(jnp.int32, (SSM_GROUPS * SSM_STATE, LANES), 0) // SSM_STATE
          == lax.broadcasted_iota(jnp.int32, (SSM_GROUPS * SSM_STATE, LANES), 1) // (SSM_HEADS // SSM_GROUPS))
    eg = (eg & (lax.broadcasted_iota(jnp.int32, eg.shape, 1) < SSM_HEADS)).astype(BF16)
    zx_s, dt_s = _proj(y1s, ssm_norm[0], w_in2, tm=sb * st, tn=SSM_MAIN // 2, n_out=SSM_MAIN, out_dtype=F32,
                       w_dt=w_dt)
    yg_s, conv_s, ssm_s = _ssd_sample(zx_s, dt_s, state_conv[0], state_ssm[0].reshape(sb, SSM_D_INNER, SSM_STATE),
                                      scan_params + (gate_gain, e32, eg))
    y2s = _outproj(yg_s, w_out2, y1s, tm=sb * st)

    hp = SSM_D_INNER // SSM_HEADS
    return (y2p.reshape(b, seq, dm), y2s.reshape(sb, st, dm),
            kv_p[0], kv_p[1], kv_p[2], kv_s[0], kv_s[1], kv_s[2],
            conv_p[None], conv_s[None],
            ssm_p.reshape(1, b, SSM_HEADS, hp, SSM_STATE), ssm_s.reshape(1, sb, SSM_HEADS, hp, SSM_STATE))
```

```python
import functools

import jax
import jax.numpy as jnp
from jax import lax
from jax.experimental import pallas as pl
from jax.experimental.pallas import tpu as pltpu

F32 = jnp.float32
BF16 = jnp.bfloat16

D_MODEL = 1024
N_HEADS = 16
HEAD_DIM = 64
ATT_WIDTH = N_HEADS * HEAD_DIM
DILATIONS = (1, 4, 16)
WINDOW_STEPS = 128
ATT_IN = 10 * ATT_WIDTH
PERM = 16

SSM_D_INNER = 2048
SSM_HEADS = 32
SSM_STATE = 128
SSM_GROUPS = 4
SSM_GROUP_WIDTH = SSM_D_INNER // SSM_GROUPS
SSM_CONV = 4
SSM_CONV_DIM = SSM_D_INNER + 2 * SSM_GROUPS * SSM_STATE
SSM_MAIN = SSM_D_INNER + SSM_CONV_DIM
SSM_CHUNK = 128

NORM_EPS = 1e-6
GATE_NORM_EPS = 1e-5
MASKED = -1e30
LOG2E = 1.4426950408889634

LANES = 128
VMEM_LIMIT = 56 * 1024 * 1024

NT = (((1,), (1,)), ((), ()))
TN = (((0,), (0,)), ((), ()))


def _params(semantics):
    return pltpu.CompilerParams(dimension_semantics=semantics, vmem_limit_bytes=VMEM_LIMIT)


def _rms(x, w, eps):
    return x * lax.rsqrt(jnp.mean(x * x, axis=-1, keepdims=True) + eps) * w


def _sigmoid(x):
    return 1.0 / (1.0 + jnp.exp(-x))


def _split2(v):
    hi = v.astype(BF16)
    lo = (v - hi.astype(F32)).astype(BF16)
    return hi, lo


def _dot(a, b):
    return jnp.dot(a, b, preferred_element_type=F32)


def _dot_nt(a, b):
    return lax.dot_general(a, b, NT, preferred_element_type=F32)


def _dot2(v, e):
    hi, lo = _split2(v)
    return _dot(hi, e) + _dot(lo, e)


def _low_half():
    return lax.broadcasted_iota(jnp.int32, (1, LANES), 1) < HEAD_DIM


def _head_norm(x, eps):
    lo = _low_half()
    parts = []
    for j in range(x.shape[1] // LANES):
        t = x[:, j * LANES:(j + 1) * LANES]
        t2 = t * t
        s_lo = jnp.sum(jnp.where(lo, t2, 0.0), axis=-1, keepdims=True)
        s_hi = jnp.sum(jnp.where(lo, 0.0, t2), axis=-1, keepdims=True)
        r = jnp.where(lo, lax.rsqrt(s_lo * (1.0 / HEAD_DIM) + eps), lax.rsqrt(s_hi * (1.0 / HEAD_DIM) + eps))
        parts.append(t * r)
    return jnp.concatenate(parts, axis=-1)


def _split_heads(qp):
    lo = _low_half()
    zero = jnp.zeros_like(qp)
    return jnp.concatenate([jnp.where(lo, qp, zero), jnp.where(lo, zero, qp)], axis=0)


def _join_heads(x2):
    rows = x2.shape[0] // 2
    return jnp.where(_low_half(), x2[0:rows], x2[rows:])


def _proj_kernel(x_ref, nw_ref, w_ref, *rest, with_dt):
    rest = list(rest)
    wdt_ref = rest.pop(0) if with_dt else None
    o_ref = rest.pop(0)
    dt_ref = rest.pop(0) if with_dt else None
    h_ref = rest.pop(0)

    @pl.when(pl.program_id(1) == 0)
    def _():
        h_ref[...] = _rms(x_ref[...], nw_ref[...], NORM_EPS).astype(BF16)
        if with_dt:
            dt_ref[...] = _dot(h_ref[...], wdt_ref[...])

    o_ref[...] = _dot(h_ref[...], w_ref[...]).astype(o_ref.dtype)


def _proj(x, norm_w, w, *, tm, tn, out_dtype, n_out=None, w_dt=None):
    m, dm = x.shape
    n = w.shape[1] if n_out is None else n_out
    assert n % tn == 0 and m % tm == 0
    in_specs = [pl.BlockSpec((tm, dm), lambda i, j: (i, 0)),
                pl.BlockSpec((1, dm), lambda i, j: (0, 0)),
                pl.BlockSpec((dm, tn), lambda i, j: (0, j))]
    args = [x, norm_w.reshape(1, dm), w]
    out_shape = [jax.ShapeDtypeStruct((m, n), out_dtype)]
    out_specs = [pl.BlockSpec((tm, tn), lambda i, j: (i, j))]
    if w_dt is not None:
        in_specs.append(pl.BlockSpec((dm, LANES), lambda i, j: (0, 0)))
        args.append(w_dt)
        out_shape.append(jax.ShapeDtypeStruct((m, LANES), F32))
        out_specs.append(pl.BlockSpec((tm, LANES), lambda i, j: (i, 0)))
    outs = pl.pallas_call(
        functools.partial(_proj_kernel, with_dt=w_dt is not None),
        grid=(m // tm, n // tn), in_specs=in_specs, out_specs=out_specs, out_shape=out_shape,
        scratch_shapes=[pltpu.VMEM((tm, dm), BF16)],
        compiler_params=_params(("parallel", "arbitrary")),
        name="norm_proj",
    )(*args)
    return outs if w_dt is not None else outs[0]


N_TILES = ATT_IN // ATT_WIDTH
Q_TILES, K_TILES, V_TILES, GATE_TILE = (0, 3, 6), (1, 4, 7), (2, 5, 8), 9
STREAM_A = (0, 1, 3, 4, 6)
STREAM_B = (2, 5, 8, 9, 7)


def _lookup(j, table):
    out = jnp.int32(table[0])
    for t, v in enumerate(table[1:], 1):
        out = jnp.where(j == t, jnp.int32(v), out)
    return out


def _attn_proj_kernel(x_ref, nw_ref, wa_ref, wb_ref, ga_ref, gb_ref, oa_ref, ob_ref, h_ref, slab_ref):
    j = pl.program_id(1)

    @pl.when(j == 0)
    def _():
        xn = _rms(x_ref[...], nw_ref[...], NORM_EPS)
        tm = x_ref.shape[0]
        rows = tm // PERM
        step = 4
        assert PERM == step * step
        for c in range(x_ref.shape[1] // LANES):
            cols = slice(c * LANES, (c + 1) * LANES)
            slab_ref[0] = xn[:, cols]
            for ra in range(step):
                slab_ref[1, ra * (tm // step):(ra + 1) * (tm // step), :] = slab_ref[0, pl.ds(ra, tm // step, stride=step), :]
            for ra in range(step):
                for rb in range(step):
                    r = ra + step * rb
                    piece = slab_ref[1, pl.ds(ra * (tm // step) + rb, rows, stride=step), :]
                    h_ref[r * rows:(r + 1) * rows, cols] = piece.astype(BF16)

    def tile(w_ref, gain_ref, o_ref, normed):
        res = _dot(h_ref[...], w_ref[...])
        if normed:
            res = _head_norm(res, NORM_EPS) * gain_ref[0]
        o_ref[...] = res.astype(o_ref.dtype).reshape(o_ref.shape)

    last = pl.num_programs(1) - 1

    @pl.when(j < last)
    def _():
        tile(wa_ref, ga_ref, oa_ref, True)
        tile(wb_ref, gb_ref, ob_ref, False)

    @pl.when(j == last)
    def _():
        tile(wa_ref, ga_ref, oa_ref, True)
        tile(wb_ref, gb_ref, ob_ref, True)


def _attn_proj(x, norm_w, w, tile_gain, *, tm, seq):
    m, dm = x.shape
    per_b = seq // tm
    rows = tm // PERM
    n_steps = len(STREAM_A)
    w_spec = lambda table: pl.BlockSpec((dm, ATT_WIDTH), lambda i, j: (0, _lookup(j, table)))
    g_spec = lambda table: pl.BlockSpec((1, 1, ATT_WIDTH), lambda i, j: (_lookup(j, table), 0, 0))
    out_spec = pl.BlockSpec((None, PERM, rows, ATT_WIDTH), lambda i, j: (i // per_b, 0, i % per_b, j))
    out_sds = jax.ShapeDtypeStruct((m // seq, PERM, seq // PERM, n_steps * ATT_WIDTH), BF16)
    return pl.pallas_call(
        _attn_proj_kernel,
        grid=(m // tm, n_steps),
        in_specs=[pl.BlockSpec((tm, dm), lambda i, j: (i, 0)), pl.BlockSpec((1, dm), lambda i, j: (0, 0)),
                  w_spec(STREAM_A), w_spec(STREAM_B), g_spec(STREAM_A), g_spec(STREAM_B)],
        out_specs=[out_spec, out_spec], out_shape=[out_sds, out_sds],
        scratch_shapes=[pltpu.VMEM((tm, dm), BF16), pltpu.VMEM((2, tm, LANES), F32)],
        compiler_params=_params(("parallel", "arbitrary")),
        name="attn_proj",
    )(x, norm_w.reshape(1, dm), w, w, tile_gain, tile_gain)


def _outproj_kernel(a_ref, w_ref, x_ref, o_ref, *scratch, unperm):
    tm = o_ref.shape[0]
    a = a_ref[...].reshape(tm, a_ref.shape[-1]).astype(BF16)
    res = _dot(a, w_ref[...])
    if unperm:
        slab_ref, = scratch
        rows = tm // PERM
        step = 4
        assert PERM == step * step
        for c in range(o_ref.shape[1] // LANES):
            cols = slice(c * LANES, (c + 1) * LANES)
            for ra in range(step):
                for rb in range(step):
                    r = ra + step * rb
                    dst = pl.ds(ra * (tm // step) + rb, rows, stride=step)
                    slab_ref[0, dst, :] = res[r * rows:(r + 1) * rows, cols]
            for ra in range(step):
                slab_ref[1, pl.ds(ra, tm // step, stride=step), :] = slab_ref[0, ra * (tm // step):(ra + 1) * (tm // step), :]
            o_ref[:, cols] = x_ref[:, cols] + slab_ref[1]
    else:
        o_ref[...] = x_ref[...] + res


def _outproj(a, w, x, *, tm, perm_seq=None):
    m, n = x.shape
    k = w.shape[0]
    if perm_seq is not None:
        per_b = perm_seq // tm
        a_spec = pl.BlockSpec((None, PERM, tm // PERM, k), lambda i: (i // per_b, 0, i % per_b, 0))
    else:
        a_spec = pl.BlockSpec((tm, k), lambda i: (i, 0))
    return pl.pallas_call(
        functools.partial(_outproj_kernel, unperm=perm_seq is not None),
        grid=(m // tm,),
        in_specs=[a_spec, pl.BlockSpec((k, n), lambda i: (0, 0)), pl.BlockSpec((tm, n), lambda i: (i, 0))],
        out_specs=pl.BlockSpec((tm, n), lambda i: (i, 0)),
        out_shape=jax.ShapeDtypeStruct((m, n), F32),
        scratch_shapes=[pltpu.VMEM((2, tm, LANES), F32)] if perm_seq else [],
        compiler_params=_params(("parallel",)),
        name="out_proj",
    )(a, w, x)


ATT_SUB = 128


def _attn_group_kernel(*refs, n_chunks, n_sub, merge):
    it = iter(refs)
    q_ref, k_ref, v_ref = next(it), next(it), next(it)
    if merge:
        gate_ref = next(it)
        others = [(next(it), next(it)) for _ in range(len(DILATIONS) - 1)]
        o_out = next(it)
    else:
        o_out, lse_out = next(it), next(it)
    ks_ref, vs_ref = next(it), next(it)

    rc = q_ref.shape[1] // n_sub
    assert n_chunks * rc == ATT_SUB
    n = pl.program_id(2)
    narrow = rc % 16 != 0

    def piece(ref, u, cols=slice(None)):
        if narrow and ref.dtype == BF16:
            pair = 2 * (u // 2) * rc
            x = ref[:, pair:pair + 2 * rc, cols].astype(F32)[:, (u % 2) * rc:(u % 2 + 1) * rc, :]
        else:
            x = ref[:, u * rc:(u + 1) * rc, cols]
        return x.reshape(ATT_SUB, x.shape[-1])

    def unsubs(parts, dtype):
        parts = [p.reshape(n_chunks, rc, p.shape[-1]) for p in parts]
        if narrow:
            return jnp.concatenate(parts, axis=1).astype(dtype)
        return jnp.concatenate([p.astype(dtype) for p in parts], axis=1)

    @pl.when(n == 0)
    def _():
        ks_ref[0:ATT_SUB, :] = jnp.zeros((ATT_SUB, ATT_WIDTH), BF16)
        vs_ref[0:ATT_SUB, :] = jnp.zeros((ATT_SUB, ATT_WIDTH), BF16)

    for u in range(n_sub):
        ks_ref[(u + 1) * ATT_SUB:(u + 2) * ATT_SUB, :] = piece(k_ref, u).astype(BF16)
        vs_ref[(u + 1) * ATT_SUB:(u + 2) * ATT_SUB, :] = piece(v_ref, u).astype(BF16)

    qi = lax.broadcasted_iota(jnp.int32, (ATT_SUB, 2 * ATT_SUB), 0)
    kj = lax.broadcasted_iota(jnp.int32, (ATT_SUB, 2 * ATT_SUB), 1)
    kc = jnp.where(kj >= ATT_SUB, kj - ATT_SUB, kj)
    tq = n_chunks * (qi % rc) + qi // rc
    tk = n_chunks * (kc % rc) + kc // rc + jnp.where(kj >= ATT_SUB, 0, -ATT_SUB)
    dist = tq - tk
    in_band = (dist >= 0) & (dist <= WINDOW_STEPS)
    band = jnp.where(in_band, 0.0, MASKED)
    band_first = jnp.where(in_band & ((kj >= ATT_SUB) | (n > 0)), 0.0, MASKED)
    bias = [jnp.concatenate([x, x], axis=0) for x in (band_first, band)]
    ones = jnp.ones((2 * ATT_SUB, LANES), BF16)

    for j in range(N_HEADS // 2):
        cols = slice(j * LANES, (j + 1) * LANES)
        o_parts, lse_parts = [], []
        for u in range(n_sub):
            keys = slice(u * ATT_SUB, (u + 2) * ATT_SUB)
            q2 = _split_heads(piece(q_ref, u, cols).astype(BF16))
            s = _dot_nt(q2, ks_ref[keys, cols]) + bias[min(u, 1)]
            m = jnp.max(s, axis=-1, keepdims=True)
            p = jnp.exp2(s - m).astype(BF16)
            pv = _dot(p, jnp.concatenate([vs_ref[keys, cols], ones], axis=1))
            l = pv[:, LANES:]
            o = _join_heads(pv[:, :LANES] / l)
            lse = _join_heads(m + jnp.log2(l))
            if merge:
                o_g = [o] + [piece(o_ref, u, cols).astype(F32) for o_ref, _ in others]
                lse_g = [lse] + [piece(l_ref, u, cols) for _, l_ref in others]
                top = jnp.maximum(jnp.maximum(lse_g[0], lse_g[1]), lse_g[2])
                w = [jnp.exp2(x - top) for x in lse_g]
                o = (w[0] * o_g[0] + w[1] * o_g[1] + w[2] * o_g[2]) / (w[0] + w[1] + w[2])
                g = piece(gate_ref, u, cols).astype(F32)
                o = o * (g * _sigmoid(g))
            o_parts.append(o)
            lse_parts.append(lse)
        o_out[:, :, cols] = unsubs(o_parts, BF16)
        if not merge:
            lse_out[:, :, cols] = unsubs(lse_parts, F32)
    ks_ref[0:ATT_SUB, :] = ks_ref[n_sub * ATT_SUB:(n_sub + 1) * ATT_SUB, :]
    vs_ref[0:ATT_SUB, :] = vs_ref[n_sub * ATT_SUB:(n_sub + 1) * ATT_SUB, :]


def _attn_group(qkvg, g, *, n_sub, others=None):
    b, _, t, _ = qkvg[0].shape
    d = DILATIONS[g]
    n_chunks = PERM // d
    rc = ATT_SUB // n_chunks * n_sub
    nb = t // rc
    merge = others is not None
    view = lambda a: a.reshape(b, n_chunks, d, t, a.shape[-1])

    def spec(col=0):
        return pl.BlockSpec((None, n_chunks, None, rc, ATT_WIDTH), lambda bb, r, n: (bb, 0, r, n, col))

    in_specs, args = [], []

    def add_tile(tile):
        stream, table = (0, STREAM_A) if tile in STREAM_A else (1, STREAM_B)
        in_specs.append(spec(table.index(tile)))
        args.append(view(qkvg[stream]))

    for tile in (Q_TILES[g], K_TILES[g], V_TILES[g]):
        add_tile(tile)
    full = jax.ShapeDtypeStruct((b, n_chunks, d, t, ATT_WIDTH), BF16)
    if merge:
        add_tile(GATE_TILE)
        for o, lse in others:
            in_specs += [spec(), spec()]
            args += [view(o), view(lse)]
        out_shape, out_specs = [full], [spec()]
    else:
        out_shape = [full, jax.ShapeDtypeStruct(full.shape, F32)]
        out_specs = [spec(), spec()]
    outs = pl.pallas_call(
        functools.partial(_attn_group_kernel, n_chunks=n_chunks, n_sub=n_sub, merge=merge),
        grid=(b, d, nb), in_specs=in_specs, out_specs=out_specs, out_shape=out_shape,
        scratch_shapes=[pltpu.VMEM(((n_sub + 1) * ATT_SUB, ATT_WIDTH), BF16)] * 2,
        compiler_params=_params(("parallel", "parallel", "arbitrary")),
        name=f"attn_group{g}",
    )(*args)
    outs = [a.reshape(b, PERM, t, ATT_WIDTH) for a in outs]
    return outs[0] if merge else tuple(outs)


def _kv_tail_kernel(x_ref, nw_ref, wk_ref, wv_ref, kg_ref, o_ref):
    h = _rms(x_ref[...], nw_ref[...], NORM_EPS).astype(BF16)
    o_ref[0:ATT_WIDTH, :] = (_head_norm(_dot(h, wk_ref[...]), NORM_EPS) * kg_ref[...]).T
    o_ref[ATT_WIDTH:2 * ATT_WIDTH, :] = _dot(h, wv_ref[...]).T


def _kv_tail(x, norm_w, w, g, k_gain, *, b, keep, tm):
    seq = x.shape[0] // b
    first = (seq - keep) // tm
    w_spec = lambda col: pl.BlockSpec((D_MODEL, ATT_WIDTH), lambda bb, i: (0, col))
    return pl.pallas_call(
        _kv_tail_kernel,
        grid=(b, keep // tm),
        in_specs=[pl.BlockSpec((tm, D_MODEL), lambda bb, i: (bb * (seq // tm) + first + i, 0)),
                  pl.BlockSpec((1, D_MODEL), lambda bb, i: (0, 0)),
                  w_spec(3 * g + 1), w_spec(3 * g + 2),
                  pl.BlockSpec((1, ATT_WIDTH), lambda bb, i: (0, 0))],
        out_specs=pl.BlockSpec((None, 2 * ATT_WIDTH, tm), lambda bb, i: (bb, 0, i)),
        out_shape=jax.ShapeDtypeStruct((b, 2 * ATT_WIDTH, keep), F32),
        compiler_params=_params(("parallel", "parallel")),
        name="kv_tail",
    )(x, norm_w.reshape(1, D_MODEL), w, w, k_gain)


SAMPLE_HEADS = 8


def _sample_attn_kernel(*refs):
    n_g = len(DILATIONS)
    qkv = [refs[3 * g:3 * g + 3] for g in range(n_g)]
    gate_ref = refs[3 * n_g]
    c_refs = refs[3 * n_g + 1:4 * n_g + 1]
    qg_ref, kg_ref = refs[4 * n_g + 1:4 * n_g + 3]
    o_ref = refs[4 * n_g + 3]
    kn_refs = refs[4 * n_g + 4:]
    t_new = o_ref.shape[0]
    t_row = lax.broadcasted_iota(jnp.int32, (2 * t_new, 1), 0) % t_new
    t_col = lax.broadcasted_iota(jnp.int32, (1, t_new), 1)

    valid_c, valid_n = [], []
    for g, d in enumerate(DILATIONS):
        assert d & (d - 1) == 0
        length = c_refs[g].shape[2]
        back = length + t_row - lax.broadcasted_iota(jnp.int32, (1, length), 1)
        valid_c.append(((back & (d - 1)) == 0) & (back <= WINDOW_STEPS * d))
        back = t_row - t_col
        valid_n.append((back >= 0) & ((back & (d - 1)) == 0) & (back <= WINDOW_STEPS * d))
    ones_rows = jnp.ones((LANES, c_refs[-1].shape[2]), BF16)
    ones_new = jnp.ones((t_new, LANES), BF16)

    o_cols = []
    kn_cols = [[] for _ in DILATIONS]
    for j in range(SAMPLE_HEADS // 2):
        cols = slice(j * LANES, (j + 1) * LANES)
        o_g, m_g, l_g = [], [], []
        for g, d in enumerate(DILATIONS):
            q_ref, k_ref, v_ref = qkv[g]
            qm = _split_heads(_head_norm(q_ref[:, cols], NORM_EPS) * qg_ref[g:g + 1, :]).astype(BF16)
            kn = _head_norm(k_ref[:, cols], NORM_EPS) * kg_ref[g:g + 1, :]
            kn_cols[g].append(kn)
            length = c_refs[g].shape[2]
            s_c = jnp.where(valid_c[g], _dot(qm, c_refs[g][0, cols, :].astype(BF16)), MASKED)
            s_n = jnp.where(valid_n[g], _dot_nt(qm, kn.astype(BF16)), MASKED)
            m = jnp.maximum(jnp.max(s_c, axis=-1, keepdims=True), jnp.max(s_n, axis=-1, keepdims=True))
            vt = jnp.concatenate([c_refs[g][1, cols, :].astype(BF16), ones_rows[:, 0:length]], axis=0)
            vn = jnp.concatenate([v_ref[:, cols].astype(BF16), ones_new], axis=1)
            pv = (_dot_nt(jnp.exp(s_c - m).astype(BF16), vt)
                  + _dot(jnp.exp(s_n - m).astype(BF16), vn))
            o_g.append(pv[:, :LANES])
            l_g.append(pv[:, LANES:])
            m_g.append(m)
        top = jnp.maximum(jnp.maximum(m_g[0], m_g[1]), m_g[2])
        f = [jnp.exp(m - top) for m in m_g]
        den = f[0] * l_g[0] + f[1] * l_g[1] + f[2] * l_g[2]
        o = _join_heads((f[0] * o_g[0] + f[1] * o_g[1] + f[2] * o_g[2]) / den)
        gate = gate_ref[:, cols]
        o_cols.append(o * (gate * _sigmoid(gate)))
    o_ref[...] = jnp.concatenate(o_cols, axis=-1)
    for g in range(n_g):
        kn_refs[g][...] = jnp.concatenate(kn_cols[g], axis=-1)


def _sample_attn(proj, caches_t, q_gain, k_gain):
    b = caches_t[0].shape[0]
    t_new = proj.shape[0] // b
    width = SAMPLE_HEADS * HEAD_DIM
    per_chunk = ATT_WIDTH // width
    col = lambda c: pl.BlockSpec((t_new, width), lambda bb, hc: (bb, c * per_chunk + hc))
    in_specs, args = [], []
    for g in range(len(DILATIONS)):
        in_specs += [col(3 * g), col(3 * g + 1), col(3 * g + 2)]
        args += [proj, proj, proj]
    in_specs.append(col(ATT_IN // ATT_WIDTH - 1))
    args.append(proj)
    for c in caches_t:
        in_specs.append(pl.BlockSpec((None, 2, width, c.shape[3]), lambda bb, hc: (bb, 0, hc, 0)))
        args.append(c)
    gain_spec = pl.BlockSpec((len(DILATIONS), LANES), lambda bb, hc: (0, 0))
    out_spec = pl.BlockSpec((t_new, width), lambda bb, hc: (bb, hc))
    out_sds = jax.ShapeDtypeStruct((b * t_new, ATT_WIDTH), F32)
    return pl.pallas_call(
        _sample_attn_kernel,
        grid=(b, per_chunk),
        in_specs=in_specs + [gain_spec, gain_spec],
        out_specs=[out_spec] * 4,
        out_shape=[out_sds] * 4,
        compiler_params=_params(("parallel", "parallel")),
        name="sample_attn",
    )(*args, q_gain, k_gain)


def _softplus(x):
    return jnp.maximum(x, 0.0) + jnp.log(1.0 + jnp.exp(-jnp.abs(x)))


def _conv_silu(zx_ref, xpad_ref, cw_ref, cb_ref, lc):
    pad = 8
    xpad_ref[pad:pad + lc, :] = zx_ref[:, SSM_D_INNER:SSM_MAIN].astype(F32)
    conv = cb_ref[...] + xpad_ref[pad - 3:pad - 3 + lc, :] * cw_ref[0:1, :]
    for j in range(1, SSM_CONV):
        conv = conv + xpad_ref[pad - 3 + j:pad - 3 + j + lc, :] * cw_ref[j:j + 1, :]
    tail = xpad_ref[pad + lc - 3:pad + lc, :]
    xpad_ref[pad - 3:pad, :] = tail
    return conv * _sigmoid(conv), tail


def _gate_norm(y, z, gn):
    y = y * (z * _sigmoid(z))
    parts = []
    for g in range(SSM_GROUPS):
        yg = y[:, g * SSM_GROUP_WIDTH:(g + 1) * SSM_GROUP_WIDTH]
        parts.append(yg * lax.rsqrt(jnp.mean(yg * yg, axis=-1, keepdims=True) + GATE_NORM_EPS))
    return jnp.concatenate(parts, axis=-1) * gn


def _cumsum_rows(x):
    rows = x.shape[0]
    tril = (lax.broadcasted_iota(jnp.int32, (rows, rows), 0)
            >= lax.broadcasted_iota(jnp.int32, (rows, rows), 1)).astype(BF16)
    h1 = x.astype(BF16)
    r1 = x - h1.astype(F32)
    h2 = r1.astype(BF16)
    h3 = (r1 - h2.astype(F32)).astype(BF16)
    return _dot(tril, h1) + _dot(tril, h2) + _dot(tril, h3)


def _ssd_prompt_kernel(zx_ref, dt_ref, cw_ref, cb_ref, dtb_ref, a_ref, dskip_ref, gn_ref, e_ref,
                       y_ref, nconv_ref, nssm_ref, carry_ref, st_ref):
    c = pl.program_id(1)
    last = pl.num_programs(1) - 1

    @pl.when(c == 0)
    def _():
        carry_ref[...] = jnp.zeros(carry_ref.shape, BF16)
        st_ref[...] = jnp.zeros(st_ref.shape, F32)

    for u in range(zx_ref.shape[0] // SSM_CHUNK):
        rows = pl.ds(u * SSM_CHUNK, SSM_CHUNK)
        _ssd_chunk(zx_ref.at[rows], dt_ref.at[rows], cw_ref, cb_ref, dtb_ref, a_ref, dskip_ref, gn_ref, e_ref,
                   y_ref.at[rows], carry_ref, st_ref)

    @pl.when(c == last)
    def _():
        n_carry = carry_ref.shape[0]
        nconv_ref[...] = carry_ref[...].astype(F32)[n_carry - (SSM_CONV - 1):n_carry]
        nssm_ref[...] = st_ref[...].T


def _ssd_chunk(zx_ref, dt_ref, cw_ref, cb_ref, dtb_ref, a_ref, dskip_ref, gn_ref, e_ref, y_ref, carry_ref, st_ref):
    lc = zx_ref.shape[0]
    n_carry = carry_ref.shape[0]
    src = lax.broadcasted_iota(jnp.int32, (lc, n_carry + lc), 1) - n_carry
    dst = lax.broadcasted_iota(jnp.int32, (lc, n_carry + lc), 0)
    taps = SSM_CONV - 1
    shift = jnp.concatenate([(src == dst - (taps - j)).astype(BF16) for j in range(taps)], axis=0)

    def conv_silu(start, width):
        cols = slice(start, start + width)
        raw = zx_ref[:, SSM_D_INNER + start:SSM_D_INNER + start + width]
        shifted = _dot(shift, jnp.concatenate([carry_ref[:, cols], raw], axis=0))
        conv = cb_ref[:, cols] + shifted[0:lc] * cw_ref[0:1, cols]
        for j in range(1, taps):
            conv = conv + shifted[j * lc:(j + 1) * lc] * cw_ref[j:j + 1, cols]
        conv = conv + raw.astype(F32) * cw_ref[taps:taps + 1, cols]
        return conv * _sigmoid(conv)

    dt = _softplus(dt_ref[...] + dtb_ref[...])
    a_cs = _cumsum_rows(dt * a_ref[...])
    a_log2 = a_cs * LOG2E
    a_log2_t = a_log2.T
    dt_t = dt.T
    a_last = a_cs[lc - 1:lc, :]
    to_end = (dt * jnp.exp(a_last - a_cs)).astype(BF16)
    from_start = jnp.exp(a_cs).astype(BF16)
    chunk_decay = jnp.broadcast_to(jnp.exp(a_last), (8, LANES))
    causal = (lax.broadcasted_iota(jnp.int32, (lc, lc), 0) >= lax.broadcasted_iota(jnp.int32, (lc, lc), 1))
    heads_per_group = SSM_HEADS // SSM_GROUPS
    head_w = SSM_D_INNER // SSM_HEADS
    quad = 4
    quad_w = quad * head_w
    lane_head = lax.broadcasted_iota(jnp.int32, (1, quad_w), 1) // head_w
    n_bc = SSM_GROUPS * SSM_STATE
    for g in range(SSM_GROUPS):
        gw = slice(g * SSM_GROUP_WIDTH, (g + 1) * SSM_GROUP_WIDTH)
        xs = conv_silu(g * SSM_GROUP_WIDTH, SSM_GROUP_WIDTH)
        bm = conv_silu(SSM_D_INNER + g * SSM_STATE, SSM_STATE).astype(BF16)
        cm = conv_silu(SSM_D_INNER + n_bc + g * SSM_STATE, SSM_STATE).astype(BF16)
        e = e_ref[:, gw]
        xs_b = xs.astype(BF16)
        xdte = (xs * _dot(to_end, e)).astype(BF16)
        s_prev = st_ref[:, gw]
        cb = _dot_nt(cm, bm)
        y = _dot(cm, s_prev.astype(BF16)) * _dot(from_start, e)
        st_ref[:, gw] = (s_prev * _dot2(chunk_decay, e)[0:1, :]
                         + lax.dot_general(bm, xdte, TN, preferred_element_type=F32))
        quad_out = []
        for qd in range(heads_per_group // quad):
            h0 = g * heads_per_group + quad * qd
            ms = []
            for h in range(h0, h0 + quad):
                seg = a_log2[:, h:h + 1] - a_log2_t[h:h + 1, :]
                ms.append((cb * jnp.where(causal, jnp.exp2(seg), 0.0) * dt_t[h:h + 1, :]).astype(BF16))
            xq = xs_b[:, qd * quad_w:(qd + 1) * quad_w]
            rhs = jnp.concatenate([jnp.where(lane_head == i, xq, jnp.zeros_like(xq)) for i in range(quad)], axis=0)
            quad_out.append(_dot(jnp.concatenate(ms, axis=1), rhs))
        y = y + jnp.concatenate(quad_out, axis=-1) + dskip_ref[:, gw] * xs
        z = zx_ref[:, gw].astype(F32)
        y = y * (z * _sigmoid(z))
        y = y * lax.rsqrt(jnp.mean(y * y, axis=-1, keepdims=True) + GATE_NORM_EPS) * gn_ref[:, gw]
        y_ref[:, gw] = y.astype(y_ref.dtype)

    carry_ref[...] = zx_ref[lc - n_carry:lc, SSM_D_INNER:SSM_MAIN]


def _ssd_prompt(zx, dt, b, params, *, chunks_per_step):
    m = zx.shape[0]
    rows = chunks_per_step * SSM_CHUNK
    nc = m // b // rows
    full = lambda x: pl.BlockSpec(x.shape, lambda bb, c: (0, 0))
    row = lambda w: pl.BlockSpec((rows, w), lambda bb, c: (bb * nc + c, 0))
    return pl.pallas_call(
        _ssd_prompt_kernel,
        grid=(b, nc),
        in_specs=[row(SSM_MAIN), row(LANES)] + [full(x) for x in params],
        out_specs=[row(SSM_D_INNER),
                   pl.BlockSpec((None, SSM_CONV - 1, SSM_CONV_DIM), lambda bb, c: (bb, 0, 0)),
                   pl.BlockSpec((None, SSM_D_INNER, SSM_STATE), lambda bb, c: (bb, 0, 0))],
        out_shape=[jax.ShapeDtypeStruct((m, SSM_D_INNER), BF16),
                   jax.ShapeDtypeStruct((b, SSM_CONV - 1, SSM_CONV_DIM), F32),
                   jax.ShapeDtypeStruct((b, SSM_D_INNER, SSM_STATE), F32)],
        scratch_shapes=[pltpu.VMEM((16, SSM_CONV_DIM), BF16),
                        pltpu.VMEM((SSM_STATE, SSM_D_INNER), F32)],
        compiler_params=_params(("parallel", "arbitrary")),
        name="ssd_prompt",
    )(zx, dt, *params)


def _ssd_sample_kernel(zx_ref, dt_ref, conv0_ref, s0_ref, cw_ref, cb_ref, dtb_ref, a_ref, dskip_ref, gn_ref,
                       e_ref, eg_ref, y_ref, nconv_ref, nssm_ref, xpad_ref):
    lc = zx_ref.shape[0]
    xpad_ref[0:5, :] = jnp.zeros((5, SSM_CONV_DIM), F32)
    xpad_ref[5:8, :] = conv0_ref[...]
    xbc, tail = _conv_silu(zx_ref, xpad_ref, cw_ref, cb_ref, lc)
    nconv_ref[...] = tail

    xs = xbc[:, 0:SSM_D_INNER]
    bm = xbc[:, SSM_D_INNER:SSM_D_INNER + SSM_GROUPS * SSM_STATE]
    cm = xbc[:, SSM_D_INNER + SSM_GROUPS * SSM_STATE:]
    e = e_ref[...]
    dt = _softplus(dt_ref[...] + dtb_ref[...])
    dta = dt * a_ref[...]
    rows = [dta[0:1, :]]
    for i in range(1, lc):
        rows.append(rows[-1] + dta[i:i + 1, :])
    a_cs = jnp.concatenate(rows, axis=0)
    a_last = rows[-1]
    xdt = xs * _dot2(dt, e)
    xdte = (xdt * _dot2(jnp.exp(a_last - a_cs), e)).astype(BF16)
    ea_e = _dot2(jnp.exp(a_cs), e)
    cd_e = _dot2(jnp.broadcast_to(jnp.exp(a_last), (8, LANES)), e)[0:1, :]

    a_l = jnp.concatenate([a_cs] * lc, axis=0)
    a_s = jnp.concatenate([jnp.broadcast_to(a_cs[s:s + 1, :], (lc, LANES)) for s in range(lc)], axis=0)
    l_idx = lax.broadcasted_iota(jnp.int32, (lc * lc, 1), 0) % lc
    s_idx = lax.broadcasted_iota(jnp.int32, (lc * lc, 1), 0) // lc
    decay = jnp.where(l_idx >= s_idx, jnp.exp(a_l - a_s), 0.0)
    cb_prod = jnp.concatenate([cm * bm[s:s + 1, :] for s in range(lc)], axis=0)
    mix = _dot2(_dot2(cb_prod, eg_ref[...]) * decay, e)
    y = ea_e * 0.0
    for s in range(lc):
        y = y + mix[s * lc:(s + 1) * lc, :] * xdt[s:s + 1, :]

    s_prev = s0_ref[...].T
    s_b = s_prev.astype(BF16)
    bm_b = bm.astype(BF16)
    cm_b = cm.astype(BF16)
    y_off, s_parts = [], []
    for g in range(SSM_GROUPS):
        gs = slice(g * SSM_STATE, (g + 1) * SSM_STATE)
        gw = slice(g * SSM_GROUP_WIDTH, (g + 1) * SSM_GROUP_WIDTH)
        y_off.append(_dot(cm_b[:, gs], s_b[:, gw]))
        s_parts.append(lax.dot_general(bm_b[:, gs], xdte[:, gw], TN, preferred_element_type=F32))
    y = y + jnp.concatenate(y_off, axis=-1) * ea_e + dskip_ref[...] * xs
    s_new = s_prev * cd_e + jnp.concatenate(s_parts, axis=-1)
    nssm_ref[...] = s_new.T
    z = zx_ref[:, 0:SSM_D_INNER]
    y_ref[...] = _gate_norm(y, z, gn_ref[...])


def _ssd_sample(zx, dt, conv0, s0, params):
    b = conv0.shape[0]
    t_new = zx.shape[0] // b
    full = lambda x: pl.BlockSpec(x.shape, lambda bb: (0, 0))
    row = lambda w: pl.BlockSpec((t_new, w), lambda bb: (bb, 0))
    conv_spec = pl.BlockSpec((None, SSM_CONV - 1, SSM_CONV_DIM), lambda bb: (bb, 0, 0))
    state_spec = pl.BlockSpec((None, SSM_D_INNER, SSM_STATE), lambda bb: (bb, 0, 0))
    return pl.pallas_call(
        _ssd_sample_kernel,
        grid=(b,),
        in_specs=[row(SSM_MAIN), row(LANES), conv_spec, state_spec] + [full(x) for x in params],
        out_specs=[row(SSM_D_INNER), conv_spec, state_spec],
        out_shape=[jax.ShapeDtypeStruct((b * t_new, SSM_D_INNER), F32),
                   jax.ShapeDtypeStruct((b, SSM_CONV - 1, SSM_CONV_DIM), F32),
                   jax.ShapeDtypeStruct((b, SSM_D_INNER, SSM_STATE), F32)],
        scratch_shapes=[pltpu.VMEM((8 + t_new, SSM_CONV_DIM), F32)],
        compiler_params=_params(("parallel",)),
        name="ssd_sample",
    )(zx, dt, conv0, s0, *params)


def _one_hot_expand(n_rows, n_cols, group):
    r = lax.broadcasted_iota(jnp.int32, (n_rows, n_cols), 0)
    c = lax.broadcasted_iota(jnp.int32, (n_rows, n_cols), 1)
    return (c // group == r).astype(BF16)


def _kv_rows(feature_major, b):
    tokens = feature_major.shape[-1]
    x = feature_major.reshape(1, b, 2, N_HEADS, HEAD_DIM, tokens)
    return jnp.transpose(x, (0, 1, 5, 2, 3, 4))


def kernel(x_prompt, x_sample, cache_kv_g0, cache_kv_g1, cache_kv_g2, state_conv, state_ssm, attn_norm, attn_w_in, attn_q_gain, attn_k_gain, attn_w_out, ssm_norm, ssm_w_in, ssm_conv_w, ssm_conv_b, ssm_dt_bias, ssm_A_log, ssm_D, ssm_gate_norm, ssm_w_out):
    b, seq, dm = x_prompt.shape
    sb, st, _ = x_sample.shape
    xp = x_prompt.reshape(b * seq, dm)
    xs = x_sample.reshape(sb * st, dm)

    w_in = attn_w_in[0].astype(BF16)
    w_out = attn_w_out[0].astype(BF16)
    scale = HEAD_DIM ** -0.5
    q_gain3 = jnp.tile(attn_q_gain[0], (1, N_HEADS)) * scale
    k_gain3 = jnp.tile(attn_k_gain[0], (1, N_HEADS))

    tile_gain = jnp.ones((N_TILES, 1, ATT_WIDTH), F32)
    for g in range(len(DILATIONS)):
        tile_gain = tile_gain.at[Q_TILES[g], 0].set(q_gain3[g] * LOG2E).at[K_TILES[g], 0].set(k_gain3[g])
    qkvg = _attn_proj(xp, attn_norm[0], w_in, tile_gain, tm=1024, seq=seq)
    part2 = _attn_group(qkvg, 2, n_sub=2)
    part1 = _attn_group(qkvg, 1, n_sub=8)
    o_gated = _attn_group(qkvg, 0, n_sub=4, others=[part1, part2])
    y1p = _outproj(o_gated, w_out, xp, tm=1024, perm_seq=seq)
    kv_p = []
    for g, d in enumerate(DILATIONS):
        keep = min(d * WINDOW_STEPS, seq)
        kv_p.append(_kv_rows(_kv_tail(xp, attn_norm[0], w_in, g, k_gain3[g:g + 1], b=b, keep=keep,
                                      tm=min(keep, 1024)), b))

    proj_s = _proj(xs, attn_norm[0], w_in, tm=sb * st, tn=1024, out_dtype=F32)
    caches_t = [jnp.transpose(c[0], (0, 2, 3, 4, 1)).reshape(sb, 2, ATT_WIDTH, c.shape[2])
                for c in (cache_kv_g0, cache_kv_g1, cache_kv_g2)]
    o_s, kn0, kn1, kn2 = _sample_attn(proj_s, caches_t, q_gain3[:, :LANES], k_gain3[:, :LANES])
    y1s = _outproj(o_s, w_out, xs, tm=sb * st)
    kv_s = []
    for g, kn in enumerate((kn0, kn1, kn2)):
        v = proj_s[:, (3 * g + 2) * ATT_WIDTH:(3 * g + 3) * ATT_WIDTH]
        kv_s.append(jnp.stack([kn, v], axis=1).reshape(1, sb, st, 2, N_HEADS, HEAD_DIM))

    w_in2 = ssm_w_in[0].astype(BF16)
    w_dt = jnp.pad(w_in2[:, SSM_MAIN:], ((0, 0), (0, LANES - SSM_HEADS)))
    w_out2 = ssm_w_out[0].astype(BF16)
    pad_h = lambda v: jnp.pad(v.astype(F32), (0, LANES - SSM_HEADS)).reshape(1, LANES)
    e32 = _one_hot_expand(LANES, SSM_D_INNER, SSM_D_INNER // SSM_HEADS)
    gate_gain = ssm_gate_norm[0].reshape(1, -1)
    scan_params = (ssm_conv_w[0], ssm_conv_b[0].reshape(1, -1), pad_h(ssm_dt_bias[0]),
                   pad_h(-jnp.exp(ssm_A_log[0].astype(F32))),
                   jnp.repeat(ssm_D[0].astype(F32), SSM_D_INNER // SSM_HEADS).reshape(1, -1))

    zx_p, dt_p = _proj(y1p, ssm_norm[0], w_in2, tm=1024, tn=SSM_MAIN // 2, n_out=SSM_MAIN, out_dtype=BF16,
                       w_dt=w_dt)
    yg_p, conv_p, ssm_p = _ssd_prompt(zx_p, dt_p, b, scan_params + (gate_gain, e32), chunks_per_step=4)
    y2p = _outproj(yg_p, w_out2, y1p, tm=1024)

    eg = (lax.broadcasted_iota(jnp.int32, (SSM_GROUPS * SSM_STATE, LANES), 0) // SSM_STATE
          == lax.broadcasted_iota(jnp.int32, (SSM_GROUPS * SSM_STATE, LANES), 1) // (SSM_HEADS // SSM_GROUPS))
    eg = (eg & (lax.broadcasted_iota(jnp.int32, eg.shape, 1) < SSM_HEADS)).astype(BF16)
    zx_s, dt_s = _proj(y1s, ssm_norm[0], w_in2, tm=sb * st, tn=SSM_MAIN // 2, n_out=SSM_MAIN, out_dtype=F32,
                       w_dt=w_dt)
    yg_s, conv_s, ssm_s = _ssd_sample(zx_s, dt_s, state_conv[0], state_ssm[0].reshape(sb, SSM_D_INNER, SSM_STATE),
                                      scan_params + (gate_gain, e32, eg))
    y2s = _outproj(yg_s, w_out2, y1s, tm=sb * st)

    hp = SSM_D_INNER // SSM_HEADS
    return (y2p.reshape(b, seq, dm), y2s.reshape(sb, st, dm),
            kv_p[0], kv_p[1], kv_p[2], kv_s[0], kv_s[1], kv_s[2],
            conv_p[None], conv_s[None],
            ssm_p.reshape(1, b, SSM_HEADS, hp, SSM_STATE), ssm_s.reshape(1, sb, SSM_HEADS, hp, SSM_STATE))
```

```python
import functools

import jax
import jax.numpy as jnp
from jax import lax
from jax.experimental import pallas as pl
from jax.experimental.pallas import tpu as pltpu

F32 = jnp.float32
BF16 = jnp.bfloat16

D_MODEL = 1024
N_HEADS = 16
HEAD_DIM = 64
ATT_WIDTH = N_HEADS * HEAD_DIM
DILATIONS = (1, 4, 16)
WINDOW_STEPS = 128
ATT_IN = 10 * ATT_WIDTH
PERM = 16

SSM_D_INNER = 2048
SSM_HEADS = 32
SSM_STATE = 128
SSM_GROUPS = 4
SSM_GROUP_WIDTH = SSM_D_INNER // SSM_GROUPS
SSM_CONV = 4
SSM_CONV_DIM = SSM_D_INNER + 2 * SSM_GROUPS * SSM_STATE
SSM_MAIN = SSM_D_INNER + SSM_CONV_DIM
SSM_CHUNK = 128

NORM_EPS = 1e-6
GATE_NORM_EPS = 1e-5
MASKED = -1e30
LOG2E = 1.4426950408889634

LANES = 128
VMEM_LIMIT = 56 * 1024 * 1024

NT = (((1,), (1,)), ((), ()))
TN = (((0,), (0,)), ((), ()))


def _params(semantics):
    return pltpu.CompilerParams(dimension_semantics=semantics, vmem_limit_bytes=VMEM_LIMIT)


def _rms(x, w, eps):
    return x * lax.rsqrt(jnp.mean(x * x, axis=-1, keepdims=True) + eps) * w


def _sigmoid(x):
    return 1.0 / (1.0 + jnp.exp(-x))


def _split2(v):
    hi = v.astype(BF16)
    lo = (v - hi.astype(F32)).astype(BF16)
    return hi, lo


def _dot(a, b):
    return jnp.dot(a, b, preferred_element_type=F32)


def _dot_nt(a, b):
    return lax.dot_general(a, b, NT, preferred_element_type=F32)


def _dot2(v, e):
    hi, lo = _split2(v)
    return _dot(hi, e) + _dot(lo, e)


def _low_half():
    return lax.broadcasted_iota(jnp.int32, (1, LANES), 1) < HEAD_DIM


def _head_norm(x, eps):
    lo = _low_half()
    parts = []
    for j in range(x.shape[1] // LANES):
        t = x[:, j * LANES:(j + 1) * LANES]
        t2 = t * t
        s_lo = jnp.sum(jnp.where(lo, t2, 0.0), axis=-1, keepdims=True)
        s_hi = jnp.sum(jnp.where(lo, 0.0, t2), axis=-1, keepdims=True)
        r = jnp.where(lo, lax.rsqrt(s_lo * (1.0 / HEAD_DIM) + eps), lax.rsqrt(s_hi * (1.0 / HEAD_DIM) + eps))
        parts.append(t * r)
    return jnp.concatenate(parts, axis=-1)


def _split_heads(qp):
    lo = _low_half()
    zero = jnp.zeros_like(qp)
    return jnp.concatenate([jnp.where(lo, qp, zero), jnp.where(lo, zero, qp)], axis=0)


def _join_heads(x2):
    rows = x2.shape[0] // 2
    return jnp.where(_low_half(), x2[0:rows], x2[rows:])


def _proj_kernel(x_ref, nw_ref, w_ref, *rest, with_dt):
    rest = list(rest)
    wdt_ref = rest.pop(0) if with_dt else None
    o_ref = rest.pop(0)
    dt_ref = rest.pop(0) if with_dt else None
    h_ref = rest.pop(0)

    @pl.when(pl.program_id(1) == 0)
    def _():
        h_ref[...] = _rms(x_ref[...], nw_ref[...], NORM_EPS).astype(BF16)
        if with_dt:
            dt_ref[...] = _dot(h_ref[...], wdt_ref[...])

    o_ref[...] = _dot(h_ref[...], w_ref[...]).astype(o_ref.dtype)


def _proj(x, norm_w, w, *, tm, tn, out_dtype, n_out=None, w_dt=None):
    m, dm = x.shape
    n = w.shape[1] if n_out is None else n_out
    assert n % tn == 0 and m % tm == 0
    in_specs = [pl.BlockSpec((tm, dm), lambda i, j: (i, 0)),
                pl.BlockSpec((1, dm), lambda i, j: (0, 0)),
                pl.BlockSpec((dm, tn), lambda i, j: (0, j))]
    args = [x, norm_w.reshape(1, dm), w]
    out_shape = [jax.ShapeDtypeStruct((m, n), out_dtype)]
    out_specs = [pl.BlockSpec((tm, tn), lambda i, j: (i, j))]
    if w_dt is not None:
        in_specs.append(pl.BlockSpec((dm, LANES), lambda i, j: (0, 0)))
        args.append(w_dt)
        out_shape.append(jax.ShapeDtypeStruct((m, LANES), F32))
        out_specs.append(pl.BlockSpec((tm, LANES), lambda i, j: (i, 0)))
    outs = pl.pallas_call(
        functools.partial(_proj_kernel, with_dt=w_dt is not None),
        grid=(m // tm, n // tn), in_specs=in_specs, out_specs=out_specs, out_shape=out_shape,
        scratch_shapes=[pltpu.VMEM((tm, dm), BF16)],
        compiler_params=_params(("parallel", "arbitrary")),
        name="norm_proj",
    )(*args)
    return outs if w_dt is not None else outs[0]


N_TILES = ATT_IN // ATT_WIDTH
Q_TILES, K_TILES, V_TILES, GATE_TILE = (0, 3, 6), (1, 4, 7), (2, 5, 8), 9
STREAM_A = (0, 1, 3, 4, 6)
STREAM_B = (2, 5, 8, 9, 7)


def _lookup(j, table):
    out = jnp.int32(table[0])
    for t, v in enumerate(table[1:], 1):
        out = jnp.where(j == t, jnp.int32(v), out)
    return out


def _attn_proj_kernel(x_ref, nw_ref, wa_ref, wb_ref, ga_ref, gb_ref, oa_ref, ob_ref, h_ref, slab_ref):
    j = pl.program_id(1)

    @pl.when(j == 0)
    def _():
        xn = _rms(x_ref[...], nw_ref[...], NORM_EPS)
        tm = x_ref.shape[0]
        rows = tm // PERM
        step = 4
        assert PERM == step * step
        for c in range(x_ref.shape[1] // LANES):
            cols = slice(c * LANES, (c + 1) * LANES)
            slab_ref[0] = xn[:, cols]
            for ra in range(step):
                slab_ref[1, ra * (tm // step):(ra + 1) * (tm // step), :] = slab_ref[0, pl.ds(ra, tm // step, stride=step), :]
            for ra in range(step):
                for rb in range(step):
                    r = ra + step * rb
                    piece = slab_ref[1, pl.ds(ra * (tm // step) + rb, rows, stride=step), :]
                    h_ref[r * rows:(r + 1) * rows, cols] = piece.astype(BF16)

    def tile(w_ref, gain_ref, o_ref, normed):
        res = _dot(h_ref[...], w_ref[...])
        if normed:
            res = _head_norm(res, NORM_EPS) * gain_ref[0]
        o_ref[...] = res.astype(o_ref.dtype).reshape(o_ref.shape)

    last = pl.num_programs(1) - 1

    @pl.when(j < last)
    def _():
        tile(wa_ref, ga_ref, oa_ref, True)
        tile(wb_ref, gb_ref, ob_ref, False)

    @pl.when(j == last)
    def _():
        tile(wa_ref, ga_ref, oa_ref, True)
        tile(wb_ref, gb_ref, ob_ref, True)


def _attn_proj(x, norm_w, w, tile_gain, *, tm, seq):
    m, dm = x.shape
    per_b = seq // tm
    rows = tm // PERM
    n_steps = len(STREAM_A)
    w_spec = lambda table: pl.BlockSpec((dm, ATT_WIDTH), lambda i, j: (0, _lookup(j, table)))
    g_spec = lambda table: pl.BlockSpec((1, 1, ATT_WIDTH), lambda i, j: (_lookup(j, table), 0, 0))
    out_spec = pl.BlockSpec((None, PERM, rows, ATT_WIDTH), lambda i, j: (i // per_b, 0, i % per_b, j))
    out_sds = jax.ShapeDtypeStruct((m // seq, PERM, seq // PERM, n_steps * ATT_WIDTH), BF16)
    return pl.pallas_call(
        _attn_proj_kernel,
        grid=(m // tm, n_steps),
        in_specs=[pl.BlockSpec((tm, dm), lambda i, j: (i, 0)), pl.BlockSpec((1, dm), lambda i, j: (0, 0)),
                  w_spec(STREAM_A), w_spec(STREAM_B), g_spec(STREAM_A), g_spec(STREAM_B)],
        out_specs=[out_spec, out_spec], out_shape=[out_sds, out_sds],
        scratch_shapes=[pltpu.VMEM((tm, dm), BF16), pltpu.VMEM((2, tm, LANES), F32)],
        compiler_params=_params(("parallel", "arbitrary")),
        name="attn_proj",
    )(x, norm_w.reshape(1, dm), w, w, tile_gain, tile_gain)


def _rows_in_order(slab_ref, x):
    n = x.shape[0]
    run = n // PERM
    step = 4
    assert PERM == step * step
    quarter = n // step
    for ra in range(step):
        for rb in range(step):
            r = ra + step * rb
            slab_ref[0, pl.ds(ra * quarter + rb, run, stride=step), :] = x[r * run:(r + 1) * run]
    for ra in range(step):
        slab_ref[1, pl.ds(ra, quarter, stride=step), :] = slab_ref[0, ra * quarter:(ra + 1) * quarter, :]
    return slab_ref[1, 0:n, :]


def _outproj_kernel(a_ref, w_ref, x_ref, o_ref, *scratch, unperm):
    tm = o_ref.shape[0]
    a = a_ref[...].reshape(tm, a_ref.shape[-1]).astype(BF16)
    res = _dot(a, w_ref[...])
    if unperm:
        slab_ref, = scratch
        for c in range(o_ref.shape[1] // LANES):
            cols = slice(c * LANES, (c + 1) * LANES)
            o_ref[:, cols] = x_ref[:, cols] + _rows_in_order(slab_ref, res[:, cols])
    else:
        o_ref[...] = x_ref[...] + res


def _outproj(a, w, x, *, tm, perm_seq=None):
    m, n = x.shape
    k = w.shape[0]
    if perm_seq is not None:
        per_b = perm_seq // tm
        a_spec = pl.BlockSpec((None, PERM, tm // PERM, k), lambda i: (i // per_b, 0, i % per_b, 0))
    else:
        a_spec = pl.BlockSpec((tm, k), lambda i: (i, 0))
    return pl.pallas_call(
        functools.partial(_outproj_kernel, unperm=perm_seq is not None),
        grid=(m // tm,),
        in_specs=[a_spec, pl.BlockSpec((k, n), lambda i: (0, 0)), pl.BlockSpec((tm, n), lambda i: (i, 0))],
        out_specs=pl.BlockSpec((tm, n), lambda i: (i, 0)),
        out_shape=jax.ShapeDtypeStruct((m, n), F32),
        scratch_shapes=[pltpu.VMEM((2, tm, LANES), F32)] if perm_seq else [],
        compiler_params=_params(("parallel",)),
        name="out_proj",
    )(a, w, x)


ATT_SUB = 128


def _attn_group_kernel(*refs, n_chunks, n_sub, merge):
    it = iter(refs)
    q_ref, k_ref, v_ref = next(it), next(it), next(it)
    if merge:
        gate_ref = next(it)
        others = [(next(it), next(it)) for _ in range(len(DILATIONS) - 1)]
        o_out = next(it)
    else:
        o_out, lse_out = next(it), next(it)
    ks_ref, vs_ref = next(it), next(it)

    rc = q_ref.shape[1] // n_sub
    assert n_chunks * rc == ATT_SUB
    n = pl.program_id(2)
    narrow = rc % 16 != 0

    def piece(ref, u, cols=slice(None)):
        if narrow and ref.dtype == BF16:
            pair = 2 * (u // 2) * rc
            x = ref[:, pair:pair + 2 * rc, cols].astype(F32)[:, (u % 2) * rc:(u % 2 + 1) * rc, :]
        else:
            x = ref[:, u * rc:(u + 1) * rc, cols]
        return x.reshape(ATT_SUB, x.shape[-1])

    def unsubs(parts, dtype):
        parts = [p.reshape(n_chunks, rc, p.shape[-1]) for p in parts]
        if narrow:
            return jnp.concatenate(parts, axis=1).astype(dtype)
        return jnp.concatenate([p.astype(dtype) for p in parts], axis=1)

    @pl.when(n == 0)
    def _():
        ks_ref[0:ATT_SUB, :] = jnp.zeros((ATT_SUB, ATT_WIDTH), BF16)
        vs_ref[0:ATT_SUB, :] = jnp.zeros((ATT_SUB, ATT_WIDTH), BF16)

    for u in range(n_sub):
        ks_ref[(u + 1) * ATT_SUB:(u + 2) * ATT_SUB, :] = piece(k_ref, u).astype(BF16)
        vs_ref[(u + 1) * ATT_SUB:(u + 2) * ATT_SUB, :] = piece(v_ref, u).astype(BF16)

    qi = lax.broadcasted_iota(jnp.int32, (ATT_SUB, 2 * ATT_SUB), 0)
    kj = lax.broadcasted_iota(jnp.int32, (ATT_SUB, 2 * ATT_SUB), 1)
    kc = jnp.where(kj >= ATT_SUB, kj - ATT_SUB, kj)
    tq = n_chunks * (qi % rc) + qi // rc
    tk = n_chunks * (kc % rc) + kc // rc + jnp.where(kj >= ATT_SUB, 0, -ATT_SUB)
    dist = tq - tk
    in_band = (dist >= 0) & (dist <= WINDOW_STEPS)
    band = jnp.where(in_band, 0.0, MASKED)
    band_first = jnp.where(in_band & ((kj >= ATT_SUB) | (n > 0)), 0.0, MASKED)
    bias = [jnp.concatenate([x, x], axis=0) for x in (band_first, band)]
    ones = jnp.ones((2 * ATT_SUB, LANES), BF16)

    for j in range(N_HEADS // 2):
        cols = slice(j * LANES, (j + 1) * LANES)
        o_parts, lse_parts = [], []
        for u in range(n_sub):
            keys = slice(u * ATT_SUB, (u + 2) * ATT_SUB)
            q2 = _split_heads(piece(q_ref, u, cols).astype(BF16))
            s = _dot_nt(q2, ks_ref[keys, cols]) + bias[min(u, 1)]
            m = jnp.max(s, axis=-1, keepdims=True)
            p = jnp.exp2(s - m).astype(BF16)
            pv = _dot(p, jnp.concatenate([vs_ref[keys, cols], ones], axis=1))
            l = pv[:, LANES:]
            o = _join_heads(pv[:, :LANES] / l)
            lse = _join_heads(m + jnp.log2(l))
            if merge:
                o_g = [o] + [piece(o_ref, u, cols).astype(F32) for o_ref, _ in others]
                lse_g = [lse] + [piece(l_ref, u, cols) for _, l_ref in others]
                top = jnp.maximum(jnp.maximum(lse_g[0], lse_g[1]), lse_g[2])
                w = [jnp.exp2(x - top) for x in lse_g]
                o = (w[0] * o_g[0] + w[1] * o_g[1] + w[2] * o_g[2]) / (w[0] + w[1] + w[2])
                g = piece(gate_ref, u, cols).astype(F32)
                o = o * (g * _sigmoid(g))
            o_parts.append(o)
            lse_parts.append(lse)
        o_out[:, :, cols] = unsubs(o_parts, BF16)
        if not merge:
            lse_out[:, :, cols] = unsubs(lse_parts, F32)
    ks_ref[0:ATT_SUB, :] = ks_ref[n_sub * ATT_SUB:(n_sub + 1) * ATT_SUB, :]
    vs_ref[0:ATT_SUB, :] = vs_ref[n_sub * ATT_SUB:(n_sub + 1) * ATT_SUB, :]


def _attn_group(qkvg, g, *, n_sub, others=None):
    b, _, t, _ = qkvg[0].shape
    d = DILATIONS[g]
    n_chunks = PERM // d
    rc = ATT_SUB // n_chunks * n_sub
    nb = t // rc
    merge = others is not None
    view = lambda a: a.reshape(b, n_chunks, d, t, a.shape[-1])

    def spec(col=0):
        return pl.BlockSpec((None, n_chunks, None, rc, ATT_WIDTH), lambda bb, r, n: (bb, 0, r, n, col))

    in_specs, args = [], []

    def add_tile(tile):
        stream, table = (0, STREAM_A) if tile in STREAM_A else (1, STREAM_B)
        in_specs.append(spec(table.index(tile)))
        args.append(view(qkvg[stream]))

    for tile in (Q_TILES[g], K_TILES[g], V_TILES[g]):
        add_tile(tile)
    full = jax.ShapeDtypeStruct((b, n_chunks, d, t, ATT_WIDTH), BF16)
    if merge:
        add_tile(GATE_TILE)
        for o, lse in others:
            in_specs += [spec(), spec()]
            args += [view(o), view(lse)]
        out_shape, out_specs = [full], [spec()]
    else:
        out_shape = [full, jax.ShapeDtypeStruct(full.shape, F32)]
        out_specs = [spec(), spec()]
    outs = pl.pallas_call(
        functools.partial(_attn_group_kernel, n_chunks=n_chunks, n_sub=n_sub, merge=merge),
        grid=(b, d, nb), in_specs=in_specs, out_specs=out_specs, out_shape=out_shape,
        scratch_shapes=[pltpu.VMEM(((n_sub + 1) * ATT_SUB, ATT_WIDTH), BF16)] * 2,
        compiler_params=_params(("parallel", "parallel", "arbitrary")),
        name=f"attn_group{g}",
    )(*args)
    outs = [a.reshape(b, PERM, t, ATT_WIDTH) for a in outs]
    return outs[0] if merge else tuple(outs)


def _kv_tail_kernel(k_ref, v_ref, o_ref, slab_ref):
    n = PERM * k_ref.shape[1]
    t_out = o_ref.shape[1]
    for half, ref in enumerate((k_ref, v_ref)):
        for c in range(ATT_WIDTH // LANES):
            x = ref[:, :, c * LANES:(c + 1) * LANES].astype(F32).reshape(n, LANES)
            first = half * ATT_WIDTH + c * LANES
            o_ref[first:first + LANES, :] = _rows_in_order(slab_ref, x).T[:, n - t_out:]


def _kv_tail(qkvg, g, *, keep, block):
    b, _, t, _ = qkvg[0].shape
    run = block // PERM
    t_out = min(block, keep)
    first = (t * PERM - max(keep, block)) // block

    def tile(idx):
        stream, table = (0, STREAM_A) if idx in STREAM_A else (1, STREAM_B)
        col = table.index(idx)
        return qkvg[stream], pl.BlockSpec((None, PERM, run, ATT_WIDTH), lambda bb, i: (bb, 0, first + i, col))

    (k_arr, k_spec), (v_arr, v_spec) = tile(K_TILES[g]), tile(V_TILES[g])
    return pl.pallas_call(
        _kv_tail_kernel,
        grid=(b, keep // t_out),
        in_specs=[k_spec, v_spec],
        out_specs=pl.BlockSpec((None, 2 * ATT_WIDTH, t_out), lambda bb, i: (bb, 0, i)),
        out_shape=jax.ShapeDtypeStruct((b, 2 * ATT_WIDTH, keep), F32),
        scratch_shapes=[pltpu.VMEM((2, block, LANES), F32)],
        compiler_params=_params(("parallel", "parallel")),
        name="kv_tail",
    )(k_arr, v_arr)


SAMPLE_HEADS = 8


def _sample_attn_kernel(*refs):
    n_g = len(DILATIONS)
    qkv = [refs[3 * g:3 * g + 3] for g in range(n_g)]
    gate_ref = refs[3 * n_g]
    c_refs = refs[3 * n_g + 1:4 * n_g + 1]
    qg_ref, kg_ref = refs[4 * n_g + 1:4 * n_g + 3]
    o_ref = refs[4 * n_g + 3]
    kn_refs = refs[4 * n_g + 4:]
    t_new = o_ref.shape[0]
    t_row = lax.broadcasted_iota(jnp.int32, (2 * t_new, 1), 0) % t_new
    t_col = lax.broadcasted_iota(jnp.int32, (1, t_new), 1)

    valid_c, valid_n = [], []
    for g, d in enumerate(DILATIONS):
        assert d & (d - 1) == 0
        length = c_refs[g].shape[2]
        back = length + t_row - lax.broadcasted_iota(jnp.int32, (1, length), 1)
        valid_c.append(((back & (d - 1)) == 0) & (back <= WINDOW_STEPS * d))
        back = t_row - t_col
        valid_n.append((back >= 0) & ((back & (d - 1)) == 0) & (back <= WINDOW_STEPS * d))
    ones_rows = jnp.ones((LANES, c_refs[-1].shape[2]), BF16)
    ones_new = jnp.ones((t_new, LANES), BF16)

    o_cols = []
    kn_cols = [[] for _ in DILATIONS]
    for j in range(SAMPLE_HEADS // 2):
        cols = slice(j * LANES, (j + 1) * LANES)
        o_g, m_g, l_g = [], [], []
        for g, d in enumerate(DILATIONS):
            q_ref, k_ref, v_ref = qkv[g]
            qm = _split_heads(_head_norm(q_ref[:, cols], NORM_EPS) * qg_ref[g:g + 1, :]).astype(BF16)
            kn = _head_norm(k_ref[:, cols], NORM_EPS) * kg_ref[g:g + 1, :]
            kn_cols[g].append(kn)
            length = c_refs[g].shape[2]
            s_c = jnp.where(valid_c[g], _dot(qm, c_refs[g][0, cols, :].astype(BF16)), MASKED)
            s_n = jnp.where(valid_n[g], _dot_nt(qm, kn.astype(BF16)), MASKED)
            m = jnp.maximum(jnp.max(s_c, axis=-1, keepdims=True), jnp.max(s_n, axis=-1, keepdims=True))
            vt = jnp.concatenate([c_refs[g][1, cols, :].astype(BF16), ones_rows[:, 0:length]], axis=0)
            vn = jnp.concatenate([v_ref[:, cols].astype(BF16), ones_new], axis=1)
            pv = (_dot_nt(jnp.exp(s_c - m).astype(BF16), vt)
                  + _dot(jnp.exp(s_n - m).astype(BF16), vn))
            o_g.append(pv[:, :LANES])
            l_g.append(pv[:, LANES:])
            m_g.append(m)
        top = jnp.maximum(jnp.maximum(m_g[0], m_g[1]), m_g[2])
        f = [jnp.exp(m - top) for m in m_g]
        den = f[0] * l_g[0] + f[1] * l_g[1] + f[2] * l_g[2]
        o = _join_heads((f[0] * o_g[0] + f[1] * o_g[1] + f[2] * o_g[2]) / den)
        gate = gate_ref[:, cols]
        o_cols.append(o * (gate * _sigmoid(gate)))
    o_ref[...] = jnp.concatenate(o_cols, axis=-1)
    for g in range(n_g):
        kn_refs[g][...] = jnp.concatenate(kn_cols[g], axis=-1)


def _sample_attn(proj, caches_t, q_gain, k_gain):
    b = caches_t[0].shape[0]
    t_new = proj.shape[0] // b
    width = SAMPLE_HEADS * HEAD_DIM
    per_chunk = ATT_WIDTH // width
    col = lambda c: pl.BlockSpec((t_new, width), lambda bb, hc: (bb, c * per_chunk + hc))
    in_specs, args = [], []
    for g in range(len(DILATIONS)):
        in_specs += [col(3 * g), col(3 * g + 1), col(3 * g + 2)]
        args += [proj, proj, proj]
    in_specs.append(col(ATT_IN // ATT_WIDTH - 1))
    args.append(proj)
    for c in caches_t:
        in_specs.append(pl.BlockSpec((None, 2, width, c.shape[3]), lambda bb, hc: (bb, 0, hc, 0)))
        args.append(c)
    gain_spec = pl.BlockSpec((len(DILATIONS), LANES), lambda bb, hc: (0, 0))
    out_spec = pl.BlockSpec((t_new, width), lambda bb, hc: (bb, hc))
    out_sds = jax.ShapeDtypeStruct((b * t_new, ATT_WIDTH), F32)
    return pl.pallas_call(
        _sample_attn_kernel,
        grid=(b, per_chunk),
        in_specs=in_specs + [gain_spec, gain_spec],
        out_specs=[out_spec] * 4,
        out_shape=[out_sds] * 4,
        compiler_params=_params(("parallel", "parallel")),
        name="sample_attn",
    )(*args, q_gain, k_gain)


def _softplus(x):
    return jnp.maximum(x, 0.0) + jnp.log(1.0 + jnp.exp(-jnp.abs(x)))


def _conv_silu(zx_ref, xpad_ref, cw_ref, cb_ref, lc):
    pad = 8
    xpad_ref[pad:pad + lc, :] = zx_ref[:, SSM_D_INNER:SSM_MAIN].astype(F32)
    conv = cb_ref[...] + xpad_ref[pad - 3:pad - 3 + lc, :] * cw_ref[0:1, :]
    for j in range(1, SSM_CONV):
        conv = conv + xpad_ref[pad - 3 + j:pad - 3 + j + lc, :] * cw_ref[j:j + 1, :]
    tail = xpad_ref[pad + lc - 3:pad + lc, :]
    xpad_ref[pad - 3:pad, :] = tail
    return conv * _sigmoid(conv), tail


def _gate_norm(y, z, gn):
    y = y * (z * _sigmoid(z))
    parts = []
    for g in range(SSM_GROUPS):
        yg = y[:, g * SSM_GROUP_WIDTH:(g + 1) * SSM_GROUP_WIDTH]
        parts.append(yg * lax.rsqrt(jnp.mean(yg * yg, axis=-1, keepdims=True) + GATE_NORM_EPS))
    return jnp.concatenate(parts, axis=-1) * gn


def _cumsum_rows(x):
    rows = x.shape[0]
    tril = (lax.broadcasted_iota(jnp.int32, (rows, rows), 0)
            >= lax.broadcasted_iota(jnp.int32, (rows, rows), 1)).astype(BF16)
    h1 = x.astype(BF16)
    r1 = x - h1.astype(F32)
    h2 = r1.astype(BF16)
    h3 = (r1 - h2.astype(F32)).astype(BF16)
    return _dot(tril, h1) + _dot(tril, h2) + _dot(tril, h3)


def _ssd_prompt_kernel(zx_ref, dt_ref, cw_ref, cb_ref, dtb_ref, a_ref, dskip_ref, gn_ref, e_ref,
                       y_ref, nconv_ref, nssm_ref, carry_ref, st_ref):
    c = pl.program_id(1)
    last = pl.num_programs(1) - 1

    @pl.when(c == 0)
    def _():
        carry_ref[...] = jnp.zeros(carry_ref.shape, BF16)
        st_ref[...] = jnp.zeros(st_ref.shape, F32)

    for u in range(zx_ref.shape[0] // SSM_CHUNK):
        rows = pl.ds(u * SSM_CHUNK, SSM_CHUNK)
        _ssd_chunk(zx_ref.at[rows], dt_ref.at[rows], cw_ref, cb_ref, dtb_ref, a_ref, dskip_ref, gn_ref, e_ref,
                   y_ref.at[rows], carry_ref, st_ref)

    @pl.when(c == last)
    def _():
        n_carry = carry_ref.shape[0]
        nconv_ref[...] = carry_ref[...].astype(F32)[n_carry - (SSM_CONV - 1):n_carry]
        nssm_ref[...] = st_ref[...].T


def _ssd_chunk(zx_ref, dt_ref, cw_ref, cb_ref, dtb_ref, a_ref, dskip_ref, gn_ref, e_ref, y_ref, carry_ref, st_ref):
    lc = zx_ref.shape[0]
    n_carry = carry_ref.shape[0]
    src = lax.broadcasted_iota(jnp.int32, (lc, n_carry + lc), 1) - n_carry
    dst = lax.broadcasted_iota(jnp.int32, (lc, n_carry + lc), 0)
    taps = SSM_CONV - 1
    shift = jnp.concatenate([(src == dst - (taps - j)).astype(BF16) for j in range(taps)], axis=0)

    def conv_silu(start, width):
        cols = slice(start, start + width)
        raw = zx_ref[:, SSM_D_INNER + start:SSM_D_INNER + start + width]
        shifted = _dot(shift, jnp.concatenate([carry_ref[:, cols], raw], axis=0))
        conv = cb_ref[:, cols] + shifted[0:lc] * cw_ref[0:1, cols]
        for j in range(1, taps):
            conv = conv + shifted[j * lc:(j + 1) * lc] * cw_ref[j:j + 1, cols]
        conv = conv + raw.astype(F32) * cw_ref[taps:taps + 1, cols]
        return conv * _sigmoid(conv)

    dt = _softplus(dt_ref[...] + dtb_ref[...])
    a_cs = _cumsum_rows(dt * a_ref[...])
    a_log2 = a_cs * LOG2E
    a_log2_t = a_log2.T
    dt_t = dt.T
    a_last = a_cs[lc - 1:lc, :]
    to_end = (dt * jnp.exp(a_last - a_cs)).astype(BF16)
    from_start = jnp.exp(a_cs).astype(BF16)
    chunk_decay = jnp.broadcast_to(jnp.exp(a_last), (8, LANES))
    causal = (lax.broadcasted_iota(jnp.int32, (lc, lc), 0) >= lax.broadcasted_iota(jnp.int32, (lc, lc), 1))
    heads_per_group = SSM_HEADS // SSM_GROUPS
    head_w = SSM_D_INNER // SSM_HEADS
    quad = 4
    quad_w = quad * head_w
    lane_head = lax.broadcasted_iota(jnp.int32, (1, quad_w), 1) // head_w
    n_bc = SSM_GROUPS * SSM_STATE
    for g in range(SSM_GROUPS):
        gw = slice(g * SSM_GROUP_WIDTH, (g + 1) * SSM_GROUP_WIDTH)
        xs = conv_silu(g * SSM_GROUP_WIDTH, SSM_GROUP_WIDTH)
        bm = conv_silu(SSM_D_INNER + g * SSM_STATE, SSM_STATE).astype(BF16)
        cm = conv_silu(SSM_D_INNER + n_bc + g * SSM_STATE, SSM_STATE).astype(BF16)
        e = e_ref[:, gw]
        xs_b = xs.astype(BF16)
        xdte = (xs * _dot(to_end, e)).astype(BF16)
        s_prev = st_ref[:, gw]
        cb = _dot_nt(cm, bm)
        y = _dot(cm, s_prev.astype(BF16)) * _dot(from_start, e)
        st_ref[:, gw] = (s_prev * _dot2(chunk_decay, e)[0:1, :]
                         + lax.dot_general(bm, xdte, TN, preferred_element_type=F32))
        quad_out = []
        for qd in range(heads_per_group // quad):
            h0 = g * heads_per_group + quad * qd
            ms = []
            for h in range(h0, h0 + quad):
                seg = a_log2[:, h:h + 1] - a_log2_t[h:h + 1, :]
                ms.append((cb * jnp.where(causal, jnp.exp2(seg), 0.0) * dt_t[h:h + 1, :]).astype(BF16))
            xq = xs_b[:, qd * quad_w:(qd + 1) * quad_w]
            rhs = jnp.concatenate([jnp.where(lane_head == i, xq, jnp.zeros_like(xq)) for i in range(quad)], axis=0)
            quad_out.append(_dot(jnp.concatenate(ms, axis=1), rhs))
        y = y + jnp.concatenate(quad_out, axis=-1) + dskip_ref[:, gw] * xs
        z = zx_ref[:, gw].astype(F32)
        y = y * (z * _sigmoid(z))
        y = y * lax.rsqrt(jnp.mean(y * y, axis=-1, keepdims=True) + GATE_NORM_EPS) * gn_ref[:, gw]
        y_ref[:, gw] = y.astype(y_ref.dtype)

    carry_ref[...] = zx_ref[lc - n_carry:lc, SSM_D_INNER:SSM_MAIN]


def _ssd_prompt(zx, dt, b, params, *, chunks_per_step):
    m = zx.shape[0]
    rows = chunks_per_step * SSM_CHUNK
    nc = m // b // rows
    full = lambda x: pl.BlockSpec(x.shape, lambda bb, c: (0, 0))
    row = lambda w: pl.BlockSpec((rows, w), lambda bb, c: (bb * nc + c, 0))
    return pl.pallas_call(
        _ssd_prompt_kernel,
        grid=(b, nc),
        in_specs=[row(SSM_MAIN), row(LANES)] + [full(x) for x in params],
        out_specs=[row(SSM_D_INNER),
                   pl.BlockSpec((None, SSM_CONV - 1, SSM_CONV_DIM), lambda bb, c: (bb, 0, 0)),
                   pl.BlockSpec((None, SSM_D_INNER, SSM_STATE), lambda bb, c: (bb, 0, 0))],
        out_shape=[jax.ShapeDtypeStruct((m, SSM_D_INNER), BF16),
                   jax.ShapeDtypeStruct((b, SSM_CONV - 1, SSM_CONV_DIM), F32),
                   jax.ShapeDtypeStruct((b, SSM_D_INNER, SSM_STATE), F32)],
        scratch_shapes=[pltpu.VMEM((16, SSM_CONV_DIM), BF16),
                        pltpu.VMEM((SSM_STATE, SSM_D_INNER), F32)],
        compiler_params=_params(("parallel", "arbitrary")),
        name="ssd_prompt",
    )(zx, dt, *params)


def _ssd_sample_kernel(zx_ref, dt_ref, conv0_ref, s0_ref, cw_ref, cb_ref, dtb_ref, a_ref, dskip_ref, gn_ref,
                       e_ref, eg_ref, y_ref, nconv_ref, nssm_ref, xpad_ref):
    lc = zx_ref.shape[0]
    xpad_ref[0:5, :] = jnp.zeros((5, SSM_CONV_DIM), F32)
    xpad_ref[5:8, :] = conv0_ref[...]
    xbc, tail = _conv_silu(zx_ref, xpad_ref, cw_ref, cb_ref, lc)
    nconv_ref[...] = tail

    xs = xbc[:, 0:SSM_D_INNER]
    bm = xbc[:, SSM_D_INNER:SSM_D_INNER + SSM_GROUPS * SSM_STATE]
    cm = xbc[:, SSM_D_INNER + SSM_GROUPS * SSM_STATE:]
    e = e_ref[...]
    dt = _softplus(dt_ref[...] + dtb_ref[...])
    dta = dt * a_ref[...]
    rows = [dta[0:1, :]]
    for i in range(1, lc):
        rows.append(rows[-1] + dta[i:i + 1, :])
    a_cs = jnp.concatenate(rows, axis=0)
    a_last = rows[-1]
    xdt = xs * _dot2(dt, e)
    xdte = (xdt * _dot2(jnp.exp(a_last - a_cs), e)).astype(BF16)
    ea_e = _dot2(jnp.exp(a_cs), e)
    cd_e = _dot2(jnp.broadcast_to(jnp.exp(a_last), (8, LANES)), e)[0:1, :]

    a_l = jnp.concatenate([a_cs] * lc, axis=0)
    a_s = jnp.concatenate([jnp.broadcast_to(a_cs[s:s + 1, :], (lc, LANES)) for s in range(lc)], axis=0)
    l_idx = lax.broadcasted_iota(jnp.int32, (lc * lc, 1), 0) % lc
    s_idx = lax.broadcasted_iota(jnp.int32, (lc * lc, 1), 0) // lc
    decay = jnp.where(l_idx >= s_idx, jnp.exp(a_l - a_s), 0.0)
    cb_prod = jnp.concatenate([cm * bm[s:s + 1, :] for s in range(lc)], axis=0)
    mix = _dot2(_dot2(cb_prod, eg_ref[...]) * decay, e)
    y = ea_e * 0.0
    for s in range(lc):
        y = y + mix[s * lc:(s + 1) * lc, :] * xdt[s:s + 1, :]

    s_prev = s0_ref[...].T
    s_b = s_prev.astype(BF16)
    bm_b = bm.astype(BF16)
    cm_b = cm.astype(BF16)
    y_off, s_parts = [], []
    for g in range(SSM_GROUPS):
        gs = slice(g * SSM_STATE, (g + 1) * SSM_STATE)
        gw = slice(g * SSM_GROUP_WIDTH, (g + 1) * SSM_GROUP_WIDTH)
        y_off.append(_dot(cm_b[:, gs], s_b[:, gw]))
        s_parts.append(lax.dot_general(bm_b[:, gs], xdte[:, gw], TN, preferred_element_type=F32))
    y = y + jnp.concatenate(y_off, axis=-1) * ea_e + dskip_ref[...] * xs
    s_new = s_prev * cd_e + jnp.concatenate(s_parts, axis=-1)
    nssm_ref[...] = s_new.T
    z = zx_ref[:, 0:SSM_D_INNER]
    y_ref[...] = _gate_norm(y, z, gn_ref[...])


def _ssd_sample(zx, dt, conv0, s0, params):
    b = conv0.shape[0]
    t_new = zx.shape[0] // b
    full = lambda x: pl.BlockSpec(x.shape, lambda bb: (0, 0))
    row = lambda w: pl.BlockSpec((t_new, w), lambda bb: (bb, 0))
    conv_spec = pl.BlockSpec((None, SSM_CONV - 1, SSM_CONV_DIM), lambda bb: (bb, 0, 0))
    state_spec = pl.BlockSpec((None, SSM_D_INNER, SSM_STATE), lambda bb: (bb, 0, 0))
    return pl.pallas_call(
        _ssd_sample_kernel,
        grid=(b,),
        in_specs=[row(SSM_MAIN), row(LANES), conv_spec, state_spec] + [full(x) for x in params],
        out_specs=[row(SSM_D_INNER), conv_spec, state_spec],
        out_shape=[jax.ShapeDtypeStruct((b * t_new, SSM_D_INNER), F32),
                   jax.ShapeDtypeStruct((b, SSM_CONV - 1, SSM_CONV_DIM), F32),
                   jax.ShapeDtypeStruct((b, SSM_D_INNER, SSM_STATE), F32)],
        scratch_shapes=[pltpu.VMEM((8 + t_new, SSM_CONV_DIM), F32)],
        compiler_params=_params(("parallel",)),
        name="ssd_sample",
    )(zx, dt, conv0, s0, *params)


def _one_hot_expand(n_rows, n_cols, group):
    r = lax.broadcasted_iota(jnp.int32, (n_rows, n_cols), 0)
    c = lax.broadcasted_iota(jnp.int32, (n_rows, n_cols), 1)
    return (c // group == r).astype(BF16)


def _kv_rows(feature_major, b):
    tokens = feature_major.shape[-1]
    x = feature_major.reshape(1, b, 2, N_HEADS, HEAD_DIM, tokens)
    return jnp.transpose(x, (0, 1, 5, 2, 3, 4))


def kernel(x_prompt, x_sample, cache_kv_g0, cache_kv_g1, cache_kv_g2, state_conv, state_ssm, attn_norm, attn_w_in, attn_q_gain, attn_k_gain, attn_w_out, ssm_norm, ssm_w_in, ssm_conv_w, ssm_conv_b, ssm_dt_bias, ssm_A_log, ssm_D, ssm_gate_norm, ssm_w_out):
    b, seq, dm = x_prompt.shape
    sb, st, _ = x_sample.shape
    xp = x_prompt.reshape(b * seq, dm)
    xs = x_sample.reshape(sb * st, dm)

    w_in = attn_w_in[0].astype(BF16)
    w_out = attn_w_out[0].astype(BF16)
    scale = HEAD_DIM ** -0.5
    q_gain3 = jnp.tile(attn_q_gain[0], (1, N_HEADS)) * scale
    k_gain3 = jnp.tile(attn_k_gain[0], (1, N_HEADS))

    tile_gain = jnp.ones((N_TILES, 1, ATT_WIDTH), F32)
    for g in range(len(DILATIONS)):
        tile_gain = tile_gain.at[Q_TILES[g], 0].set(q_gain3[g] * LOG2E).at[K_TILES[g], 0].set(k_gain3[g])
    qkvg = _attn_proj(xp, attn_norm[0], w_in, tile_gain, tm=1024, seq=seq)
    part2 = _attn_group(qkvg, 2, n_sub=2)
    part1 = _attn_group(qkvg, 1, n_sub=8)
    o_gated = _attn_group(qkvg, 0, n_sub=4, others=[part1, part2])
    y1p = _outproj(o_gated, w_out, xp, tm=1024, perm_seq=seq)
    kv_p = []
    for g, d in enumerate(DILATIONS):
        keep = min(d * WINDOW_STEPS, seq)
        block = min(max(keep, 16 * PERM), 1024)
        kv_p.append(_kv_rows(_kv_tail(qkvg, g, keep=keep, block=block), b))

    proj_s = _proj(xs, attn_norm[0], w_in, tm=sb * st, tn=1024, out_dtype=F32)
    caches_t = [jnp.transpose(c[0], (0, 2, 3, 4, 1)).reshape(sb, 2, ATT_WIDTH, c.shape[2])
                for c in (cache_kv_g0, cache_kv_g1, cache_kv_g2)]
    o_s, kn0, kn1, kn2 = _sample_attn(proj_s, caches_t, q_gain3[:, :LANES], k_gain3[:, :LANES])
    y1s = _outproj(o_s, w_out, xs, tm=sb * st)
    kv_s = []
    for g, kn in enumerate((kn0, kn1, kn2)):
        v = proj_s[:, (3 * g + 2) * ATT_WIDTH:(3 * g + 3) * ATT_WIDTH]
        kv_s.append(jnp.stack([kn, v], axis=1).reshape(1, sb, st, 2, N_HEADS, HEAD_DIM))

    w_in2 = ssm_w_in[0].astype(BF16)
    w_dt = jnp.pad(w_in2[:, SSM_MAIN:], ((0, 0), (0, LANES - SSM_HEADS)))
    w_out2 = ssm_w_out[0].astype(BF16)
    pad_h = lambda v: jnp.pad(v.astype(F32), (0, LANES - SSM_HEADS)).reshape(1, LANES)
    e32 = _one_hot_expand(LANES, SSM_D_INNER, SSM_D_INNER // SSM_HEADS)
    gate_gain = ssm_gate_norm[0].reshape(1, -1)
    scan_params = (ssm_conv_w[0], ssm_conv_b[0].reshape(1, -1), pad_h(ssm_dt_bias[0]),
                   pad_h(-jnp.exp(ssm_A_log[0].astype(F32))),
                   jnp.repeat(ssm_D[0].astype(F32), SSM_D_INNER // SSM_HEADS).reshape(1, -1))

    zx_p, dt_p = _proj(y1p, ssm_norm[0], w_in2, tm=1024, tn=SSM_MAIN // 2, n_out=SSM_MAIN, out_dtype=BF16,
                       w_dt=w_dt)
    yg_p, conv_p, ssm_p = _ssd_prompt(zx_p, dt_p, b, scan_params + (gate_gain, e32), chunks_per_step=4)
    y2p = _outproj(yg_p, w_out2, y1p, tm=1024)

    eg = (lax.broadcasted_iota(jnp.int32, (SSM_GROUPS * SSM_STATE, LANES), 0) // SSM_STATE
          == lax.broadcasted_iota(jnp.int32, (SSM_GROUPS * SSM_STATE, LANES), 1) // (SSM_HEADS // SSM_GROUPS))
    eg = (eg & (lax.broadcasted_iota(jnp.int32, eg.shape, 1) < SSM_HEADS)).astype(BF16)
    zx_s, dt_s = _proj(y1s, ssm_norm[0], w_in2, tm=sb * st, tn=SSM_MAIN // 2, n_out=SSM_MAIN, out_dtype=F32,
                       w_dt=w_dt)
    yg_s, conv_s, ssm_s = _ssd_sample(zx_s, dt_s, state_conv[0], state_ssm[0].reshape(sb, SSM_D_INNER, SSM_STATE),
                                      scan_params + (gate_gain, e32, eg))
    y2s = _outproj(yg_s, w_out2, y1s, tm=sb * st)

    hp = SSM_D_INNER // SSM_HEADS
    return (y2p.reshape(b, seq, dm), y2s.reshape(sb, st, dm),
            kv_p[0], kv_p[1], kv_p[2], kv_s[0], kv_s[1], kv_s[2],
            conv_p[None], conv_s[None],
            ssm_p.reshape(1, b, SSM_HEADS, hp, SSM_STATE), ssm_s.reshape(1, sb, SSM_HEADS, hp, SSM_STATE))
```

```python
import functools

import jax
import jax.numpy as jnp
from jax import lax
from jax.experimental import pallas as pl
from jax.experimental.pallas import tpu as pltpu

F32 = jnp.float32
BF16 = jnp.bfloat16

D_MODEL = 1024
N_HEADS = 16
HEAD_DIM = 64
ATT_WIDTH = N_HEADS * HEAD_DIM
DILATIONS = (1, 4, 16)
WINDOW_STEPS = 128
ATT_IN = 10 * ATT_WIDTH
PERM = 16

SSM_D_INNER = 2048
SSM_HEADS = 32
SSM_STATE = 128
SSM_GROUPS = 4
SSM_GROUP_WIDTH = SSM_D_INNER // SSM_GROUPS
SSM_CONV = 4
SSM_CONV_DIM = SSM_D_INNER + 2 * SSM_GROUPS * SSM_STATE
SSM_MAIN = SSM_D_INNER + SSM_CONV_DIM
SSM_CHUNK = 128

NORM_EPS = 1e-6
GATE_NORM_EPS = 1e-5
MASKED = -1e30
LOG2E = 1.4426950408889634

LANES = 128
VMEM_LIMIT = 56 * 1024 * 1024

NT = (((1,), (1,)), ((), ()))
TN = (((0,), (0,)), ((), ()))


def _params(semantics):
    return pltpu.CompilerParams(dimension_semantics=semantics, vmem_limit_bytes=VMEM_LIMIT)


def _rms(x, w, eps):
    return x * lax.rsqrt(jnp.mean(x * x, axis=-1, keepdims=True) + eps) * w


def _sigmoid(x):
    return 1.0 / (1.0 + jnp.exp(-x))


def _split2(v):
    hi = v.astype(BF16)
    lo = (v - hi.astype(F32)).astype(BF16)
    return hi, lo


def _dot(a, b):
    return jnp.dot(a, b, preferred_element_type=F32)


def _dot_nt(a, b):
    return lax.dot_general(a, b, NT, preferred_element_type=F32)


def _dot2(v, e):
    hi, lo = _split2(v)
    return _dot(hi, e) + _dot(lo, e)


def _low_half():
    return lax.broadcasted_iota(jnp.int32, (1, LANES), 1) < HEAD_DIM


def _head_norm(x, eps):
    lo = _low_half()
    parts = []
    for j in range(x.shape[1] // LANES):
        t = x[:, j * LANES:(j + 1) * LANES]
        t2 = t * t
        s_lo = jnp.sum(jnp.where(lo, t2, 0.0), axis=-1, keepdims=True)
        s_hi = jnp.sum(jnp.where(lo, 0.0, t2), axis=-1, keepdims=True)
        r = jnp.where(lo, lax.rsqrt(s_lo * (1.0 / HEAD_DIM) + eps), lax.rsqrt(s_hi * (1.0 / HEAD_DIM) + eps))
        parts.append(t * r)
    return jnp.concatenate(parts, axis=-1)


def _split_heads(qp):
    lo = _low_half()
    zero = jnp.zeros_like(qp)
    return jnp.concatenate([jnp.where(lo, qp, zero), jnp.where(lo, zero, qp)], axis=0)


def _join_heads(x2):
    rows = x2.shape[0] // 2
    return jnp.where(_low_half(), x2[0:rows], x2[rows:])


def _proj_kernel(x_ref, nw_ref, w_ref, *rest, with_dt):
    rest = list(rest)
    wdt_ref = rest.pop(0) if with_dt else None
    o_ref = rest.pop(0)
    dt_ref = rest.pop(0) if with_dt else None
    h_ref = rest.pop(0)

    @pl.when(pl.program_id(1) == 0)
    def _():
        h_ref[...] = _rms(x_ref[...], nw_ref[...], NORM_EPS).astype(BF16)
        if with_dt:
            dt_ref[...] = _dot(h_ref[...], wdt_ref[...])

    o_ref[...] = _dot(h_ref[...], w_ref[...]).astype(o_ref.dtype)


def _proj(x, norm_w, w, *, tm, tn, out_dtype, n_out=None, w_dt=None):
    m, dm = x.shape
    n = w.shape[1] if n_out is None else n_out
    assert n % tn == 0 and m % tm == 0
    in_specs = [pl.BlockSpec((tm, dm), lambda i, j: (i, 0)),
                pl.BlockSpec((1, dm), lambda i, j: (0, 0)),
                pl.BlockSpec((dm, tn), lambda i, j: (0, j))]
    args = [x, norm_w.reshape(1, dm), w]
    out_shape = [jax.ShapeDtypeStruct((m, n), out_dtype)]
    out_specs = [pl.BlockSpec((tm, tn), lambda i, j: (i, j))]
    if w_dt is not None:
        in_specs.append(pl.BlockSpec((dm, LANES), lambda i, j: (0, 0)))
        args.append(w_dt)
        out_shape.append(jax.ShapeDtypeStruct((m, LANES), F32))
        out_specs.append(pl.BlockSpec((tm, LANES), lambda i, j: (i, 0)))
    outs = pl.pallas_call(
        functools.partial(_proj_kernel, with_dt=w_dt is not None),
        grid=(m // tm, n // tn), in_specs=in_specs, out_specs=out_specs, out_shape=out_shape,
        scratch_shapes=[pltpu.VMEM((tm, dm), BF16)],
        compiler_params=_params(("parallel", "arbitrary")),
        name="norm_proj",
    )(*args)
    return outs if w_dt is not None else outs[0]


N_TILES = ATT_IN // ATT_WIDTH
Q_TILES, K_TILES, V_TILES, GATE_TILE = (0, 3, 6), (1, 4, 7), (2, 5, 8), 9
STREAM_A = (0, 1, 3, 4, 6)
STREAM_B = (2, 5, 8, 9, 7)


def _lookup(j, table):
    out = jnp.int32(table[0])
    for t, v in enumerate(table[1:], 1):
        out = jnp.where(j == t, jnp.int32(v), out)
    return out


def _attn_proj_kernel(x_ref, nw_ref, wa_ref, wb_ref, ga_ref, gb_ref, oa_ref, ob_ref, h_ref, slab_ref):
    j = pl.program_id(1)

    @pl.when(j == 0)
    def _():
        xn = _rms(x_ref[...], nw_ref[...], NORM_EPS)
        tm = x_ref.shape[0]
        rows = tm // PERM
        step = 4
        assert PERM == step * step
        for c in range(x_ref.shape[1] // LANES):
            cols = slice(c * LANES, (c + 1) * LANES)
            slab_ref[0] = xn[:, cols]
            for ra in range(step):
                slab_ref[1, ra * (tm // step):(ra + 1) * (tm // step), :] = slab_ref[0, pl.ds(ra, tm // step, stride=step), :]
            for ra in range(step):
                for rb in range(step):
                    r = ra + step * rb
                    piece = slab_ref[1, pl.ds(ra * (tm // step) + rb, rows, stride=step), :]
                    h_ref[r * rows:(r + 1) * rows, cols] = piece.astype(BF16)

    def tile(w_ref, gain_ref, o_ref, normed):
        res = _dot(h_ref[...], w_ref[...])
        if normed:
            res = _head_norm(res, NORM_EPS) * gain_ref[0]
        o_ref[...] = res.astype(o_ref.dtype).reshape(o_ref.shape)

    last = pl.num_programs(1) - 1

    @pl.when(j < last)
    def _():
        tile(wa_ref, ga_ref, oa_ref, True)
        tile(wb_ref, gb_ref, ob_ref, False)

    @pl.when(j == last)
    def _():
        tile(wa_ref, ga_ref, oa_ref, True)
        tile(wb_ref, gb_ref, ob_ref, True)


def _attn_proj(x, norm_w, w, tile_gain, *, tm, seq):
    m, dm = x.shape
    per_b = seq // tm
    rows = tm // PERM
    n_steps = len(STREAM_A)
    w_spec = lambda table: pl.BlockSpec((dm, ATT_WIDTH), lambda i, j: (0, _lookup(j, table)))
    g_spec = lambda table: pl.BlockSpec((1, 1, ATT_WIDTH), lambda i, j: (_lookup(j, table), 0, 0))
    out_spec = pl.BlockSpec((None, PERM, rows, ATT_WIDTH), lambda i, j: (i // per_b, 0, i % per_b, j))
    out_sds = jax.ShapeDtypeStruct((m // seq, PERM, seq // PERM, n_steps * ATT_WIDTH), BF16)
    return pl.pallas_call(
        _attn_proj_kernel,
        grid=(m // tm, n_steps),
        in_specs=[pl.BlockSpec((tm, dm), lambda i, j: (i, 0)), pl.BlockSpec((1, dm), lambda i, j: (0, 0)),
                  w_spec(STREAM_A), w_spec(STREAM_B), g_spec(STREAM_A), g_spec(STREAM_B)],
        out_specs=[out_spec, out_spec], out_shape=[out_sds, out_sds],
        scratch_shapes=[pltpu.VMEM((tm, dm), BF16), pltpu.VMEM((2, tm, LANES), F32)],
        compiler_params=_params(("parallel", "arbitrary")),
        name="attn_proj",
    )(x, norm_w.reshape(1, dm), w, w, tile_gain, tile_gain)


def _rows_in_order(slab_ref, x):
    n = x.shape[0]
    run = n // PERM
    step = 4
    assert PERM == step * step
    quarter = n // step
    for ra in range(step):
        for rb in range(step):
            r = ra + step * rb
            slab_ref[0, pl.ds(ra * quarter + rb, run, stride=step), :] = x[r * run:(r + 1) * run]
    for ra in range(step):
        slab_ref[1, pl.ds(ra, quarter, stride=step), :] = slab_ref[0, ra * quarter:(ra + 1) * quarter, :]
    return slab_ref[1, 0:n, :]


def _outproj_kernel(a_ref, w_ref, x_ref, o_ref, *scratch, unperm):
    tm = o_ref.shape[0]
    a = a_ref[...].reshape(tm, a_ref.shape[-1]).astype(BF16)
    res = _dot(a, w_ref[...])
    if unperm:
        slab_ref, = scratch
        for c in range(o_ref.shape[1] // LANES):
            cols = slice(c * LANES, (c + 1) * LANES)
            o_ref[:, cols] = x_ref[:, cols] + _rows_in_order(slab_ref, res[:, cols])
    else:
        o_ref[...] = x_ref[...] + res


def _outproj(a, w, x, *, tm, perm_seq=None):
    m, n = x.shape
    k = w.shape[0]
    if perm_seq is not None:
        per_b = perm_seq // tm
        a_spec = pl.BlockSpec((None, PERM, tm // PERM, k), lambda i: (i // per_b, 0, i % per_b, 0))
    else:
        a_spec = pl.BlockSpec((tm, k), lambda i: (i, 0))
    return pl.pallas_call(
        functools.partial(_outproj_kernel, unperm=perm_seq is not None),
        grid=(m // tm,),
        in_specs=[a_spec, pl.BlockSpec((k, n), lambda i: (0, 0)), pl.BlockSpec((tm, n), lambda i: (i, 0))],
        out_specs=pl.BlockSpec((tm, n), lambda i: (i, 0)),
        out_shape=jax.ShapeDtypeStruct((m, n), F32),
        scratch_shapes=[pltpu.VMEM((2, tm, LANES), F32)] if perm_seq else [],
        compiler_params=_params(("parallel",)),
        name="out_proj",
    )(a, w, x)


ATT_SUB = 128


def _attn_group_kernel(*refs, n_chunks, n_sub, merge):
    it = iter(refs)
    q_ref, k_ref, v_ref = next(it), next(it), next(it)
    if merge:
        gate_ref = next(it)
        others = [(next(it), next(it)) for _ in range(len(DILATIONS) - 1)]
        o_out = next(it)
    else:
        o_out, lse_out = next(it), next(it)
    ks_ref, vs_ref = next(it), next(it)

    rc = q_ref.shape[1] // n_sub
    assert n_chunks * rc == ATT_SUB
    n = pl.program_id(2)
    narrow = rc % 16 != 0

    def piece(ref, u, cols=slice(None)):
        if narrow and ref.dtype == BF16:
            pair = 2 * (u // 2) * rc
            x = ref[:, pair:pair + 2 * rc, cols].astype(F32)[:, (u % 2) * rc:(u % 2 + 1) * rc, :]
        else:
            x = ref[:, u * rc:(u + 1) * rc, cols]
        return x.reshape(ATT_SUB, x.shape[-1])

    def unsubs(parts, dtype):
        parts = [p.reshape(n_chunks, rc, p.shape[-1]) for p in parts]
        if narrow:
            return jnp.concatenate(parts, axis=1).astype(dtype)
        return jnp.concatenate([p.astype(dtype) for p in parts], axis=1)

    @pl.when(n == 0)
    def _():
        ks_ref[0:ATT_SUB, :] = jnp.zeros((ATT_SUB, ATT_WIDTH), BF16)
        vs_ref[0:ATT_SUB, :] = jnp.zeros((ATT_SUB, ATT_WIDTH), BF16)

    for u in range(n_sub):
        ks_ref[(u + 1) * ATT_SUB:(u + 2) * ATT_SUB, :] = piece(k_ref, u).astype(BF16)
        vs_ref[(u + 1) * ATT_SUB:(u + 2) * ATT_SUB, :] = piece(v_ref, u).astype(BF16)

    qi = lax.broadcasted_iota(jnp.int32, (ATT_SUB, 2 * ATT_SUB), 0)
    kj = lax.broadcasted_iota(jnp.int32, (ATT_SUB, 2 * ATT_SUB), 1)
    kc = jnp.where(kj >= ATT_SUB, kj - ATT_SUB, kj)
    tq = n_chunks * (qi % rc) + qi // rc
    tk = n_chunks * (kc % rc) + kc // rc + jnp.where(kj >= ATT_SUB, 0, -ATT_SUB)
    dist = tq - tk
    in_band = (dist >= 0) & (dist <= WINDOW_STEPS)
    band = jnp.where(in_band, 0.0, MASKED)
    band_first = jnp.where(in_band & ((kj >= ATT_SUB) | (n > 0)), 0.0, MASKED)
    bias = [jnp.concatenate([x, x], axis=0) for x in (band_first, band)]
    ones = jnp.ones((2 * ATT_SUB, LANES), BF16)

    for j in range(N_HEADS // 2):
        cols = slice(j * LANES, (j + 1) * LANES)
        o_parts, lse_parts = [], []
        for u in range(n_sub):
            keys = slice(u * ATT_SUB, (u + 2) * ATT_SUB)
            q2 = _split_heads(piece(q_ref, u, cols).astype(BF16))
            s = _dot_nt(q2, ks_ref[keys, cols]) + bias[min(u, 1)]
            m = jnp.max(s, axis=-1, keepdims=True)
            p = jnp.exp2(s - m).astype(BF16)
            pv = _dot(p, jnp.concatenate([vs_ref[keys, cols], ones], axis=1))
            l = pv[:, LANES:]
            o = _join_heads(pv[:, :LANES] / l)
            lse = _join_heads(m + jnp.log2(l))
            if merge:
                o_g = [o] + [piece(o_ref, u, cols).astype(F32) for o_ref, _ in others]
                lse_g = [lse] + [piece(l_ref, u, cols) for _, l_ref in others]
                top = jnp.maximum(jnp.maximum(lse_g[0], lse_g[1]), lse_g[2])
                w = [jnp.exp2(x - top) for x in lse_g]
                o = (w[0] * o_g[0] + w[1] * o_g[1] + w[2] * o_g[2]) / (w[0] + w[1] + w[2])
                g = piece(gate_ref, u, cols).astype(F32)
                o = o * (g * _sigmoid(g))
            o_parts.append(o)
            lse_parts.append(lse)
        o_out[:, :, cols] = unsubs(o_parts, BF16)
        if not merge:
            lse_out[:, :, cols] = unsubs(lse_parts, F32)
    ks_ref[0:ATT_SUB, :] = ks_ref[n_sub * ATT_SUB:(n_sub + 1) * ATT_SUB, :]
    vs_ref[0:ATT_SUB, :] = vs_ref[n_sub * ATT_SUB:(n_sub + 1) * ATT_SUB, :]


def _attn_residues_kernel(*refs, n_res, **kwargs):
    blocks, scratch = refs[:-2], refs[-2:]
    for rr in range(n_res):
        _attn_group_kernel(*[r.at[:, rr] for r in blocks], *scratch, **kwargs)


def _attn_group(qkvg, g, *, n_sub, n_res=1, others=None):
    b, _, t, _ = qkvg[0].shape
    d = DILATIONS[g]
    n_chunks = PERM // d
    rc = ATT_SUB // n_chunks * n_sub
    nb = t // rc
    assert n_res == 1 or nb == 1
    merge = others is not None
    view = lambda a: a.reshape(b, n_chunks, d, t, a.shape[-1])

    def spec(col=0):
        res_dim = None if n_res == 1 else n_res
        return pl.BlockSpec((None, n_chunks, res_dim, rc, ATT_WIDTH), lambda bb, r, n: (bb, 0, r, n, col))

    in_specs, args = [], []

    def add_tile(tile):
        stream, table = (0, STREAM_A) if tile in STREAM_A else (1, STREAM_B)
        in_specs.append(spec(table.index(tile)))
        args.append(view(qkvg[stream]))

    for tile in (Q_TILES[g], K_TILES[g], V_TILES[g]):
        add_tile(tile)
    full = jax.ShapeDtypeStruct((b, n_chunks, d, t, ATT_WIDTH), BF16)
    if merge:
        add_tile(GATE_TILE)
        for o, lse in others:
            in_specs += [spec(), spec()]
            args += [view(o), view(lse)]
        out_shape, out_specs = [full], [spec()]
    else:
        out_shape = [full, jax.ShapeDtypeStruct(full.shape, F32)]
        out_specs = [spec(), spec()]
    body = functools.partial(_attn_group_kernel, n_chunks=n_chunks, n_sub=n_sub, merge=merge)
    if n_res > 1:
        body = functools.partial(_attn_residues_kernel, n_res=n_res, n_chunks=n_chunks, n_sub=n_sub, merge=merge)
    outs = pl.pallas_call(
        body,
        grid=(b, d // n_res, nb), in_specs=in_specs, out_specs=out_specs, out_shape=out_shape,
        scratch_shapes=[pltpu.VMEM(((n_sub + 1) * ATT_SUB, ATT_WIDTH), BF16)] * 2,
        compiler_params=_params(("parallel", "parallel", "arbitrary")),
        name=f"attn_group{g}",
    )(*args)
    outs = [a.reshape(b, PERM, t, ATT_WIDTH) for a in outs]
    return outs[0] if merge else tuple(outs)


def _kv_tail_kernel(k_ref, v_ref, o_ref, slab_ref):
    n = PERM * k_ref.shape[1]
    t_out = o_ref.shape[1]
    for half, ref in enumerate((k_ref, v_ref)):
        for c in range(ATT_WIDTH // LANES):
            x = ref[:, :, c * LANES:(c + 1) * LANES].astype(F32).reshape(n, LANES)
            first = half * ATT_WIDTH + c * LANES
            o_ref[first:first + LANES, :] = _rows_in_order(slab_ref, x).T[:, n - t_out:]


def _kv_tail(qkvg, g, *, keep, block):
    b, _, t, _ = qkvg[0].shape
    run = block // PERM
    t_out = min(block, keep)
    first = (t * PERM - max(keep, block)) // block

    def tile(idx):
        stream, table = (0, STREAM_A) if idx in STREAM_A else (1, STREAM_B)
        col = table.index(idx)
        return qkvg[stream], pl.BlockSpec((None, PERM, run, ATT_WIDTH), lambda bb, i: (bb, 0, first + i, col))

    (k_arr, k_spec), (v_arr, v_spec) = tile(K_TILES[g]), tile(V_TILES[g])
    return pl.pallas_call(
        _kv_tail_kernel,
        grid=(b, keep // t_out),
        in_specs=[k_spec, v_spec],
        out_specs=pl.BlockSpec((None, 2 * ATT_WIDTH, t_out), lambda bb, i: (bb, 0, i)),
        out_shape=jax.ShapeDtypeStruct((b, 2 * ATT_WIDTH, keep), F32),
        scratch_shapes=[pltpu.VMEM((2, block, LANES), F32)],
        compiler_params=_params(("parallel", "parallel")),
        name="kv_tail",
    )(k_arr, v_arr)


SAMPLE_HEADS = 8


def _sample_attn_kernel(*refs):
    n_g = len(DILATIONS)
    qkv = [refs[3 * g:3 * g + 3] for g in range(n_g)]
    gate_ref = refs[3 * n_g]
    c_refs = refs[3 * n_g + 1:4 * n_g + 1]
    qg_ref, kg_ref = refs[4 * n_g + 1:4 * n_g + 3]
    o_ref = refs[4 * n_g + 3]
    kn_refs = refs[4 * n_g + 4:]
    t_new = o_ref.shape[0]
    t_row = lax.broadcasted_iota(jnp.int32, (2 * t_new, 1), 0) % t_new
    t_col = lax.broadcasted_iota(jnp.int32, (1, t_new), 1)

    valid_c, valid_n = [], []
    for g, d in enumerate(DILATIONS):
        assert d & (d - 1) == 0
        length = c_refs[g].shape[2]
        back = length + t_row - lax.broadcasted_iota(jnp.int32, (1, length), 1)
        valid_c.append(((back & (d - 1)) == 0) & (back <= WINDOW_STEPS * d))
        back = t_row - t_col
        valid_n.append((back >= 0) & ((back & (d - 1)) == 0) & (back <= WINDOW_STEPS * d))
    ones_rows = jnp.ones((LANES, c_refs[-1].shape[2]), BF16)
    ones_new = jnp.ones((t_new, LANES), BF16)

    o_cols = []
    kn_cols = [[] for _ in DILATIONS]
    for j in range(SAMPLE_HEADS // 2):
        cols = slice(j * LANES, (j + 1) * LANES)
        o_g, m_g, l_g = [], [], []
        for g, d in enumerate(DILATIONS):
            q_ref, k_ref, v_ref = qkv[g]
            qm = _split_heads(_head_norm(q_ref[:, cols], NORM_EPS) * qg_ref[g:g + 1, :]).astype(BF16)
            kn = _head_norm(k_ref[:, cols], NORM_EPS) * kg_ref[g:g + 1, :]
            kn_cols[g].append(kn)
            length = c_refs[g].shape[2]
            s_c = jnp.where(valid_c[g], _dot(qm, c_refs[g][0, cols, :].astype(BF16)), MASKED)
            s_n = jnp.where(valid_n[g], _dot_nt(qm, kn.astype(BF16)), MASKED)
            m = jnp.maximum(jnp.max(s_c, axis=-1, keepdims=True), jnp.max(s_n, axis=-1, keepdims=True))
            vt = jnp.concatenate([c_refs[g][1, cols, :].astype(BF16), ones_rows[:, 0:length]], axis=0)
            vn = jnp.concatenate([v_ref[:, cols].astype(BF16), ones_new], axis=1)
            pv = (_dot_nt(jnp.exp(s_c - m).astype(BF16), vt)
                  + _dot(jnp.exp(s_n - m).astype(BF16), vn))
            o_g.append(pv[:, :LANES])
            l_g.append(pv[:, LANES:])
            m_g.append(m)
        top = jnp.maximum(jnp.maximum(m_g[0], m_g[1]), m_g[2])
        f = [jnp.exp(m - top) for m in m_g]
        den = f[0] * l_g[0] + f[1] * l_g[1] + f[2] * l_g[2]
        o = _join_heads((f[0] * o_g[0] + f[1] * o_g[1] + f[2] * o_g[2]) / den)
        gate = gate_ref[:, cols]
        o_cols.append(o * (gate * _sigmoid(gate)))
    o_ref[...] = jnp.concatenate(o_cols, axis=-1)
    for g in range(n_g):
        kn_refs[g][...] = jnp.concatenate(kn_cols[g], axis=-1)


def _sample_attn(proj, caches_t, q_gain, k_gain):
    b = caches_t[0].shape[0]
    t_new = proj.shape[0] // b
    width = SAMPLE_HEADS * HEAD_DIM
    per_chunk = ATT_WIDTH // width
    col = lambda c: pl.BlockSpec((t_new, width), lambda bb, hc: (bb, c * per_chunk + hc))
    in_specs, args = [], []
    for g in range(len(DILATIONS)):
        in_specs += [col(3 * g), col(3 * g + 1), col(3 * g + 2)]
        args += [proj, proj, proj]
    in_specs.append(col(ATT_IN // ATT_WIDTH - 1))
    args.append(proj)
    for c in caches_t:
        in_specs.append(pl.BlockSpec((None, 2, width, c.shape[3]), lambda bb, hc: (bb, 0, hc, 0)))
        args.append(c)
    gain_spec = pl.BlockSpec((len(DILATIONS), LANES), lambda bb, hc: (0, 0))
    out_spec = pl.BlockSpec((t_new, width), lambda bb, hc: (bb, hc))
    out_sds = jax.ShapeDtypeStruct((b * t_new, ATT_WIDTH), F32)
    return pl.pallas_call(
        _sample_attn_kernel,
        grid=(b, per_chunk),
        in_specs=in_specs + [gain_spec, gain_spec],
        out_specs=[out_spec] * 4,
        out_shape=[out_sds] * 4,
        compiler_params=_params(("parallel", "parallel")),
        name="sample_attn",
    )(*args, q_gain, k_gain)


def _softplus(x):
    return jnp.maximum(x, 0.0) + jnp.log(1.0 + jnp.exp(-jnp.abs(x)))


def _conv_silu(zx_ref, xpad_ref, cw_ref, cb_ref, lc):
    pad = 8
    xpad_ref[pad:pad + lc, :] = zx_ref[:, SSM_D_INNER:SSM_MAIN].astype(F32)
    conv = cb_ref[...] + xpad_ref[pad - 3:pad - 3 + lc, :] * cw_ref[0:1, :]
    for j in range(1, SSM_CONV):
        conv = conv + xpad_ref[pad - 3 + j:pad - 3 + j + lc, :] * cw_ref[j:j + 1, :]
    tail = xpad_ref[pad + lc - 3:pad + lc, :]
    xpad_ref[pad - 3:pad, :] = tail
    return conv * _sigmoid(conv), tail


def _gate_norm(y, z, gn):
    y = y * (z * _sigmoid(z))
    parts = []
    for g in range(SSM_GROUPS):
        yg = y[:, g * SSM_GROUP_WIDTH:(g + 1) * SSM_GROUP_WIDTH]
        parts.append(yg * lax.rsqrt(jnp.mean(yg * yg, axis=-1, keepdims=True) + GATE_NORM_EPS))
    return jnp.concatenate(parts, axis=-1) * gn


def _cumsum_rows(x):
    rows = x.shape[0]
    tril = (lax.broadcasted_iota(jnp.int32, (rows, rows), 0)
            >= lax.broadcasted_iota(jnp.int32, (rows, rows), 1)).astype(BF16)
    h1 = x.astype(BF16)
    r1 = x - h1.astype(F32)
    h2 = r1.astype(BF16)
    h3 = (r1 - h2.astype(F32)).astype(BF16)
    return _dot(tril, h1) + _dot(tril, h2) + _dot(tril, h3)


def _ssd_prompt_kernel(zx_ref, dt_ref, cw_ref, cb_ref, dtb_ref, a_ref, dskip_ref, gn_ref, e_ref,
                       y_ref, nconv_ref, nssm_ref, carry_ref, st_ref):
    c = pl.program_id(1)
    last = pl.num_programs(1) - 1

    @pl.when(c == 0)
    def _():
        carry_ref[...] = jnp.zeros(carry_ref.shape, BF16)
        st_ref[...] = jnp.zeros(st_ref.shape, F32)

    for u in range(zx_ref.shape[0] // SSM_CHUNK):
        rows = pl.ds(u * SSM_CHUNK, SSM_CHUNK)
        _ssd_chunk(zx_ref.at[rows], dt_ref.at[rows], cw_ref, cb_ref, dtb_ref, a_ref, dskip_ref, gn_ref, e_ref,
                   y_ref.at[rows], carry_ref, st_ref)

    @pl.when(c == last)
    def _():
        n_carry = carry_ref.shape[0]
        nconv_ref[...] = carry_ref[...].astype(F32)[n_carry - (SSM_CONV - 1):n_carry]
        nssm_ref[...] = st_ref[...].T


def _ssd_chunk(zx_ref, dt_ref, cw_ref, cb_ref, dtb_ref, a_ref, dskip_ref, gn_ref, e_ref, y_ref, carry_ref, st_ref):
    lc = zx_ref.shape[0]
    n_carry = carry_ref.shape[0]
    src = lax.broadcasted_iota(jnp.int32, (lc, n_carry + lc), 1) - n_carry
    dst = lax.broadcasted_iota(jnp.int32, (lc, n_carry + lc), 0)
    taps = SSM_CONV - 1
    shift = jnp.concatenate([(src == dst - (taps - j)).astype(BF16) for j in range(taps)], axis=0)

    def conv_silu(start, width):
        cols = slice(start, start + width)
        raw = zx_ref[:, SSM_D_INNER + start:SSM_D_INNER + start + width]
        shifted = _dot(shift, jnp.concatenate([carry_ref[:, cols], raw], axis=0))
        conv = cb_ref[:, cols] + shifted[0:lc] * cw_ref[0:1, cols]
        for j in range(1, taps):
            conv = conv + shifted[j * lc:(j + 1) * lc] * cw_ref[j:j + 1, cols]
        conv = conv + raw.astype(F32) * cw_ref[taps:taps + 1, cols]
        return conv * _sigmoid(conv)

    dt = _softplus(dt_ref[...] + dtb_ref[...])
    a_cs = _cumsum_rows(dt * a_ref[...])
    a_log2 = a_cs * LOG2E
    a_log2_t = a_log2.T
    dt_t = dt.T
    a_last = a_cs[lc - 1:lc, :]
    to_end = (dt * jnp.exp(a_last - a_cs)).astype(BF16)
    from_start = jnp.exp(a_cs).astype(BF16)
    chunk_decay = jnp.broadcast_to(jnp.exp(a_last), (8, LANES))
    causal = (lax.broadcasted_iota(jnp.int32, (lc, lc), 0) >= lax.broadcasted_iota(jnp.int32, (lc, lc), 1))
    heads_per_group = SSM_HEADS // SSM_GROUPS
    head_w = SSM_D_INNER // SSM_HEADS
    quad = 4
    quad_w = quad * head_w
    lane_head = lax.broadcasted_iota(jnp.int32, (1, quad_w), 1) // head_w
    n_bc = SSM_GROUPS * SSM_STATE
    for g in range(SSM_GROUPS):
        gw = slice(g * SSM_GROUP_WIDTH, (g + 1) * SSM_GROUP_WIDTH)
        xs = conv_silu(g * SSM_GROUP_WIDTH, SSM_GROUP_WIDTH)
        bm = conv_silu(SSM_D_INNER + g * SSM_STATE, SSM_STATE).astype(BF16)
        cm = conv_silu(SSM_D_INNER + n_bc + g * SSM_STATE, SSM_STATE).astype(BF16)
        e = e_ref[:, gw]
        xs_b = xs.astype(BF16)
        xdte = (xs * _dot(to_end, e)).astype(BF16)
        s_prev = st_ref[:, gw]
        cb = _dot_nt(cm, bm)
        y = _dot(cm, s_prev.astype(BF16)) * _dot(from_start, e)
        st_ref[:, gw] = (s_prev * _dot2(chunk_decay, e)[0:1, :]
                         + lax.dot_general(bm, xdte, TN, preferred_element_type=F32))
        quad_out = []
        for qd in range(heads_per_group // quad):
            h0 = g * heads_per_group + quad * qd
            ms = []
            for h in range(h0, h0 + quad):
                seg = a_log2[:, h:h + 1] - a_log2_t[h:h + 1, :]
                ms.append((cb * jnp.where(causal, jnp.exp2(seg), 0.0) * dt_t[h:h + 1, :]).astype(BF16))
            xq = xs_b[:, qd * quad_w:(qd + 1) * quad_w]
            rhs = jnp.concatenate([jnp.where(lane_head == i, xq, jnp.zeros_like(xq)) for i in range(quad)], axis=0)
            quad_out.append(_dot(jnp.concatenate(ms, axis=1), rhs))
        y = y + jnp.concatenate(quad_out, axis=-1) + dskip_ref[:, gw] * xs
        z = zx_ref[:, gw].astype(F32)
        y = y * (z * _sigmoid(z))
        y = y * lax.rsqrt(jnp.mean(y * y, axis=-1, keepdims=True) + GATE_NORM_EPS) * gn_ref[:, gw]
        y_ref[:, gw] = y.astype(y_ref.dtype)

    carry_ref[...] = zx_ref[lc - n_carry:lc, SSM_D_INNER:SSM_MAIN]


def _ssd_prompt(zx, dt, b, params, *, chunks_per_step):
    m = zx.shape[0]
    rows = chunks_per_step * SSM_CHUNK
    nc = m // b // rows
    full = lambda x: pl.BlockSpec(x.shape, lambda bb, c: (0, 0))
    row = lambda w: pl.BlockSpec((rows, w), lambda bb, c: (bb * nc + c, 0))
    return pl.pallas_call(
        _ssd_prompt_kernel,
        grid=(b, nc),
        in_specs=[row(SSM_MAIN), row(LANES)] + [full(x) for x in params],
        out_specs=[row(SSM_D_INNER),
                   pl.BlockSpec((None, SSM_CONV - 1, SSM_CONV_DIM), lambda bb, c: (bb, 0, 0)),
                   pl.BlockSpec((None, SSM_D_INNER, SSM_STATE), lambda bb, c: (bb, 0, 0))],
        out_shape=[jax.ShapeDtypeStruct((m, SSM_D_INNER), BF16),
                   jax.ShapeDtypeStruct((b, SSM_CONV - 1, SSM_CONV_DIM), F32),
                   jax.ShapeDtypeStruct((b, SSM_D_INNER, SSM_STATE), F32)],
        scratch_shapes=[pltpu.VMEM((16, SSM_CONV_DIM), BF16),
                        pltpu.VMEM((SSM_STATE, SSM_D_INNER), F32)],
        compiler_params=_params(("parallel", "arbitrary")),
        name="ssd_prompt",
    )(zx, dt, *params)


def _ssd_sample_kernel(zx_ref, dt_ref, conv0_ref, s0_ref, cw_ref, cb_ref, dtb_ref, a_ref, dskip_ref, gn_ref,
                       e_ref, eg_ref, y_ref, nconv_ref, nssm_ref, xpad_ref):
    lc = zx_ref.shape[0]
    xpad_ref[0:5, :] = jnp.zeros((5, SSM_CONV_DIM), F32)
    xpad_ref[5:8, :] = conv0_ref[...]
    xbc, tail = _conv_silu(zx_ref, xpad_ref, cw_ref, cb_ref, lc)
    nconv_ref[...] = tail

    xs = xbc[:, 0:SSM_D_INNER]
    bm = xbc[:, SSM_D_INNER:SSM_D_INNER + SSM_GROUPS * SSM_STATE]
    cm = xbc[:, SSM_D_INNER + SSM_GROUPS * SSM_STATE:]
    e = e_ref[...]
    dt = _softplus(dt_ref[...] + dtb_ref[...])
    dta = dt * a_ref[...]
    rows = [dta[0:1, :]]
    for i in range(1, lc):
        rows.append(rows[-1] + dta[i:i + 1, :])
    a_cs = jnp.concatenate(rows, axis=0)
    a_last = rows[-1]
    xdt = xs * _dot2(dt, e)
    xdte = (xdt * _dot2(jnp.exp(a_last - a_cs), e)).astype(BF16)
    ea_e = _dot2(jnp.exp(a_cs), e)
    cd_e = _dot2(jnp.broadcast_to(jnp.exp(a_last), (8, LANES)), e)[0:1, :]

    a_l = jnp.concatenate([a_cs] * lc, axis=0)
    a_s = jnp.concatenate([jnp.broadcast_to(a_cs[s:s + 1, :], (lc, LANES)) for s in range(lc)], axis=0)
    l_idx = lax.broadcasted_iota(jnp.int32, (lc * lc, 1), 0) % lc
    s_idx = lax.broadcasted_iota(jnp.int32, (lc * lc, 1), 0) // lc
    decay = jnp.where(l_idx >= s_idx, jnp.exp(a_l - a_s), 0.0)
    cb_prod = jnp.concatenate([cm * bm[s:s + 1, :] for s in range(lc)], axis=0)
    mix = _dot2(_dot2(cb_prod, eg_ref[...]) * decay, e)
    y = ea_e * 0.0
    for s in range(lc):
        y = y + mix[s * lc:(s + 1) * lc, :] * xdt[s:s + 1, :]

    s_prev = s0_ref[...].T
    s_b = s_prev.astype(BF16)
    bm_b = bm.astype(BF16)
    cm_b = cm.astype(BF16)
    y_off, s_parts = [], []
    for g in range(SSM_GROUPS):
        gs = slice(g * SSM_STATE, (g + 1) * SSM_STATE)
        gw = slice(g * SSM_GROUP_WIDTH, (g + 1) * SSM_GROUP_WIDTH)
        y_off.append(_dot(cm_b[:, gs], s_b[:, gw]))
        s_parts.append(lax.dot_general(bm_b[:, gs], xdte[:, gw], TN, preferred_element_type=F32))
    y = y + jnp.concatenate(y_off, axis=-1) * ea_e + dskip_ref[...] * xs
    s_new = s_prev * cd_e + jnp.concatenate(s_parts, axis=-1)
    nssm_ref[...] = s_new.T
    z = zx_ref[:, 0:SSM_D_INNER]
    y_ref[...] = _gate_norm(y, z, gn_ref[...])


def _ssd_sample(zx, dt, conv0, s0, params):
    b = conv0.shape[0]
    t_new = zx.shape[0] // b
    full = lambda x: pl.BlockSpec(x.shape, lambda bb: (0, 0))
    row = lambda w: pl.BlockSpec((t_new, w), lambda bb: (bb, 0))
    conv_spec = pl.BlockSpec((None, SSM_CONV - 1, SSM_CONV_DIM), lambda bb: (bb, 0, 0))
    state_spec = pl.BlockSpec((None, SSM_D_INNER, SSM_STATE), lambda bb: (bb, 0, 0))
    return pl.pallas_call(
        _ssd_sample_kernel,
        grid=(b,),
        in_specs=[row(SSM_MAIN), row(LANES), conv_spec, state_spec] + [full(x) for x in params],
        out_specs=[row(SSM_D_INNER), conv_spec, state_spec],
        out_shape=[jax.ShapeDtypeStruct((b * t_new, SSM_D_INNER), F32),
                   jax.ShapeDtypeStruct((b, SSM_CONV - 1, SSM_CONV_DIM), F32),
                   jax.ShapeDtypeStruct((b, SSM_D_INNER, SSM_STATE), F32)],
        scratch_shapes=[pltpu.VMEM((8 + t_new, SSM_CONV_DIM), F32)],
        compiler_params=_params(("parallel",)),
        name="ssd_sample",
    )(zx, dt, conv0, s0, *params)


def _one_hot_expand(n_rows, n_cols, group):
    r = lax.broadcasted_iota(jnp.int32, (n_rows, n_cols), 0)
    c = lax.broadcasted_iota(jnp.int32, (n_rows, n_cols), 1)
    return (c // group == r).astype(BF16)


def _kv_rows(feature_major, b):
    tokens = feature_major.shape[-1]
    x = feature_major.reshape(1, b, 2, N_HEADS, HEAD_DIM, tokens)
    return jnp.transpose(x, (0, 1, 5, 2, 3, 4))


def kernel(x_prompt, x_sample, cache_kv_g0, cache_kv_g1, cache_kv_g2, state_conv, state_ssm, attn_norm, attn_w_in, attn_q_gain, attn_k_gain, attn_w_out, ssm_norm, ssm_w_in, ssm_conv_w, ssm_conv_b, ssm_dt_bias, ssm_A_log, ssm_D, ssm_gate_norm, ssm_w_out):
    b, seq, dm = x_prompt.shape
    sb, st, _ = x_sample.shape
    xp = x_prompt.reshape(b * seq, dm)
    xs = x_sample.reshape(sb * st, dm)

    w_in = attn_w_in[0].astype(BF16)
    w_out = attn_w_out[0].astype(BF16)
    scale = HEAD_DIM ** -0.5
    q_gain3 = jnp.tile(attn_q_gain[0], (1, N_HEADS)) * scale
    k_gain3 = jnp.tile(attn_k_gain[0], (1, N_HEADS))

    tile_gain = jnp.ones((N_TILES, 1, ATT_WIDTH), F32)
    for g in range(len(DILATIONS)):
        tile_gain = tile_gain.at[Q_TILES[g], 0].set(q_gain3[g] * LOG2E).at[K_TILES[g], 0].set(k_gain3[g])
    qkvg = _attn_proj(xp, attn_norm[0], w_in, tile_gain, tm=1024, seq=seq)
    part2 = _attn_group(qkvg, 2, n_sub=2, n_res=4)
    part1 = _attn_group(qkvg, 1, n_sub=8)
    o_gated = _attn_group(qkvg, 0, n_sub=4, others=[part1, part2])
    y1p = _outproj(o_gated, w_out, xp, tm=1024, perm_seq=seq)
    kv_p = []
    for g, d in enumerate(DILATIONS):
        keep = min(d * WINDOW_STEPS, seq)
        block = min(max(keep, 16 * PERM), 1024)
        kv_p.append(_kv_rows(_kv_tail(qkvg, g, keep=keep, block=block), b))

    proj_s = _proj(xs, attn_norm[0], w_in, tm=sb * st, tn=1024, out_dtype=F32)
    caches_t = [jnp.transpose(c[0], (0, 2, 3, 4, 1)).reshape(sb, 2, ATT_WIDTH, c.shape[2])
                for c in (cache_kv_g0, cache_kv_g1, cache_kv_g2)]
    o_s, kn0, kn1, kn2 = _sample_attn(proj_s, caches_t, q_gain3[:, :LANES], k_gain3[:, :LANES])
    y1s = _outproj(o_s, w_out, xs, tm=sb * st)
    kv_s = []
    for g, kn in enumerate((kn0, kn1, kn2)):
        v = proj_s[:, (3 * g + 2) * ATT_WIDTH:(3 * g + 3) * ATT_WIDTH]
        kv_s.append(jnp.stack([kn, v], axis=1).reshape(1, sb, st, 2, N_HEADS, HEAD_DIM))

    w_in2 = ssm_w_in[0].astype(BF16)
    w_dt = jnp.pad(w_in2[:, SSM_MAIN:], ((0, 0), (0, LANES - SSM_HEADS)))
    w_out2 = ssm_w_out[0].astype(BF16)
    pad_h = lambda v: jnp.pad(v.astype(F32), (0, LANES - SSM_HEADS)).reshape(1, LANES)
    e32 = _one_hot_expand(LANES, SSM_D_INNER, SSM_D_INNER // SSM_HEADS)
    gate_gain = ssm_gate_norm[0].reshape(1, -1)
    scan_params = (ssm_conv_w[0], ssm_conv_b[0].reshape(1, -1), pad_h(ssm_dt_bias[0]),
                   pad_h(-jnp.exp(ssm_A_log[0].astype(F32))),
                   jnp.repeat(ssm_D[0].astype(F32), SSM_D_INNER // SSM_HEADS).reshape(1, -1))

    zx_p, dt_p = _proj(y1p, ssm_norm[0], w_in2, tm=1024, tn=SSM_MAIN // 2, n_out=SSM_MAIN, out_dtype=BF16,
                       w_dt=w_dt)
    yg_p, conv_p, ssm_p = _ssd_prompt(zx_p, dt_p, b, scan_params + (gate_gain, e32), chunks_per_step=8)
    y2p = _outproj(yg_p, w_out2, y1p, tm=1024)

    eg = (lax.broadcasted_iota(jnp.int32, (SSM_GROUPS * SSM_STATE, LANES), 0) // SSM_STATE
          == lax.broadcasted_iota(jnp.int32, (SSM_GROUPS * SSM_STATE, LANES), 1) // (SSM_HEADS // SSM_GROUPS))
    eg = (eg & (lax.broadcasted_iota(jnp.int32, eg.shape, 1) < SSM_HEADS)).astype(BF16)
    zx_s, dt_s = _proj(y1s, ssm_norm[0], w_in2, tm=sb * st, tn=SSM_MAIN // 2, n_out=SSM_MAIN, out_dtype=F32,
                       w_dt=w_dt)
    yg_s, conv_s, ssm_s = _ssd_sample(zx_s, dt_s, state_conv[0], state_ssm[0].reshape(sb, SSM_D_INNER, SSM_STATE),
                                      scan_params + (gate_gain, e32, eg))
    y2s = _outproj(yg_s, w_out2, y1s, tm=sb * st)

    hp = SSM_D_INNER // SSM_HEADS
    return (y2p.reshape(b, seq, dm), y2s.reshape(sb, st, dm),
            kv_p[0], kv_p[1], kv_p[2], kv_s[0], kv_s[1], kv_s[2],
            conv_p[None], conv_s[None],
            ssm_p.reshape(1, b, SSM_HEADS, hp, SSM_STATE), ssm_s.reshape(1, sb, SSM_HEADS, hp, SSM_STATE))
```

```python
import functools

import jax
import jax.numpy as jnp
from jax import lax
from jax.experimental import pallas as pl
from jax.experimental.pallas import tpu as pltpu

F32 = jnp.float32
BF16 = jnp.bfloat16

D_MODEL = 1024
N_HEADS = 16
HEAD_DIM = 64
ATT_WIDTH = N_HEADS * HEAD_DIM
DILATIONS = (1, 4, 16)
WINDOW_STEPS = 128
ATT_IN = 10 * ATT_WIDTH
PERM = 16

SSM_D_INNER = 2048
SSM_HEADS = 32
SSM_STATE = 128
SSM_GROUPS = 4
SSM_GROUP_WIDTH = SSM_D_INNER // SSM_GROUPS
SSM_CONV = 4
SSM_CONV_DIM = SSM_D_INNER + 2 * SSM_GROUPS * SSM_STATE
SSM_MAIN = SSM_D_INNER + SSM_CONV_DIM
SSM_CHUNK = 128

NORM_EPS = 1e-6
GATE_NORM_EPS = 1e-5
MASKED = -1e30
LOG2E = 1.4426950408889634

LANES = 128
VMEM_LIMIT = 56 * 1024 * 1024

PROJ_ROWS = 1024
SAMPLE_PROJ_COLS = 2048
ATT_SUBS = (4, 8, 2)
ATT_RESIDUES = (1, 1, 4)
SSD_CHUNKS = 4
KV_TAIL_TOKENS = 1024

NT = (((1,), (1,)), ((), ()))
TN = (((0,), (0,)), ((), ()))


def _params(semantics):
    return pltpu.CompilerParams(dimension_semantics=semantics, vmem_limit_bytes=VMEM_LIMIT)


def _rms(x, w, eps):
    return x * lax.rsqrt(jnp.mean(x * x, axis=-1, keepdims=True) + eps) * w


def _sigmoid(x):
    return 1.0 / (1.0 + jnp.exp(-x))


def _split2(v):
    hi = v.astype(BF16)
    lo = (v - hi.astype(F32)).astype(BF16)
    return hi, lo


def _dot(a, b):
    return jnp.dot(a, b, preferred_element_type=F32)


def _dot_nt(a, b):
    return lax.dot_general(a, b, NT, preferred_element_type=F32)


def _dot2(v, e):
    hi, lo = _split2(v)
    return _dot(hi, e) + _dot(lo, e)


def _low_half():
    return lax.broadcasted_iota(jnp.int32, (1, LANES), 1) < HEAD_DIM


def _head_norm(x, eps):
    lo = _low_half()
    parts = []
    for j in range(x.shape[1] // LANES):
        t = x[:, j * LANES:(j + 1) * LANES]
        t2 = t * t
        s_lo = jnp.sum(jnp.where(lo, t2, 0.0), axis=-1, keepdims=True)
        s_hi = jnp.sum(jnp.where(lo, 0.0, t2), axis=-1, keepdims=True)
        r = jnp.where(lo, lax.rsqrt(s_lo * (1.0 / HEAD_DIM) + eps), lax.rsqrt(s_hi * (1.0 / HEAD_DIM) + eps))
        parts.append(t * r)
    return jnp.concatenate(parts, axis=-1)


def _split_heads(qp):
    lo = _low_half()
    zero = jnp.zeros_like(qp)
    return jnp.concatenate([jnp.where(lo, qp, zero), jnp.where(lo, zero, qp)], axis=0)


def _join_heads(x2):
    rows = x2.shape[0] // 2
    return jnp.where(_low_half(), x2[0:rows], x2[rows:])


def _proj_kernel(x_ref, nw_ref, w_ref, *rest, with_dt):
    rest = list(rest)
    wdt_ref = rest.pop(0) if with_dt else None
    o_ref = rest.pop(0)
    dt_ref = rest.pop(0) if with_dt else None
    h_ref = rest.pop(0)

    @pl.when(pl.program_id(1) == 0)
    def _():
        h_ref[...] = _rms(x_ref[...], nw_ref[...], NORM_EPS).astype(BF16)
        if with_dt:
            dt_ref[...] = _dot(h_ref[...], wdt_ref[...])

    o_ref[...] = _dot(h_ref[...], w_ref[...]).astype(o_ref.dtype)


def _proj(x, norm_w, w, *, tm, tn, out_dtype, n_out=None, w_dt=None):
    m, dm = x.shape
    n = w.shape[1] if n_out is None else n_out
    assert n % tn == 0 and m % tm == 0
    in_specs = [pl.BlockSpec((tm, dm), lambda i, j: (i, 0)),
                pl.BlockSpec((1, dm), lambda i, j: (0, 0)),
                pl.BlockSpec((dm, tn), lambda i, j: (0, j))]
    args = [x, norm_w.reshape(1, dm), w]
    out_shape = [jax.ShapeDtypeStruct((m, n), out_dtype)]
    out_specs = [pl.BlockSpec((tm, tn), lambda i, j: (i, j))]
    if w_dt is not None:
        in_specs.append(pl.BlockSpec((dm, LANES), lambda i, j: (0, 0)))
        args.append(w_dt)
        out_shape.append(jax.ShapeDtypeStruct((m, LANES), F32))
        out_specs.append(pl.BlockSpec((tm, LANES), lambda i, j: (i, 0)))
    outs = pl.pallas_call(
        functools.partial(_proj_kernel, with_dt=w_dt is not None),
        grid=(m // tm, n // tn), in_specs=in_specs, out_specs=out_specs, out_shape=out_shape,
        scratch_shapes=[pltpu.VMEM((tm, dm), BF16)],
        compiler_params=_params(("parallel", "arbitrary")),
        name="norm_proj",
    )(*args)
    return outs if w_dt is not None else outs[0]


N_TILES = ATT_IN // ATT_WIDTH
Q_TILES, K_TILES, V_TILES, GATE_TILE = (0, 3, 6), (1, 4, 7), (2, 5, 8), 9
STREAM_A = (0, 1, 3, 4, 6)
STREAM_B = (2, 5, 8, 9, 7)


def _lookup(j, table):
    out = jnp.int32(table[0])
    for t, v in enumerate(table[1:], 1):
        out = jnp.where(j == t, jnp.int32(v), out)
    return out


def _attn_proj_kernel(x_ref, nw_ref, wa_ref, wb_ref, ga_ref, gb_ref, oa_ref, ob_ref, h_ref, slab_ref):
    j = pl.program_id(1)

    @pl.when(j == 0)
    def _():
        xn = _rms(x_ref[...], nw_ref[...], NORM_EPS)
        tm = x_ref.shape[0]
        rows = tm // PERM
        step = 4
        assert PERM == step * step
        for c in range(x_ref.shape[1] // LANES):
            cols = slice(c * LANES, (c + 1) * LANES)
            slab_ref[0] = xn[:, cols]
            for ra in range(step):
                slab_ref[1, ra * (tm // step):(ra + 1) * (tm // step), :] = slab_ref[0, pl.ds(ra, tm // step, stride=step), :]
            for ra in range(step):
                for rb in range(step):
                    r = ra + step * rb
                    piece = slab_ref[1, pl.ds(ra * (tm // step) + rb, rows, stride=step), :]
                    h_ref[r * rows:(r + 1) * rows, cols] = piece.astype(BF16)

    def tile(w_ref, gain_ref, o_ref, normed):
        res = _dot(h_ref[...], w_ref[...])
        if normed:
            res = _head_norm(res, NORM_EPS) * gain_ref[0]
        o_ref[...] = res.astype(o_ref.dtype).reshape(o_ref.shape)

    last = pl.num_programs(1) - 1

    @pl.when(j < last)
    def _():
        tile(wa_ref, ga_ref, oa_ref, True)
        tile(wb_ref, gb_ref, ob_ref, False)

    @pl.when(j == last)
    def _():
        tile(wa_ref, ga_ref, oa_ref, True)
        tile(wb_ref, gb_ref, ob_ref, True)


def _attn_proj(x, norm_w, w, tile_gain, *, tm, seq):
    m, dm = x.shape
    per_b = seq // tm
    rows = tm // PERM
    n_steps = len(STREAM_A)
    w_spec = lambda table: pl.BlockSpec((dm, ATT_WIDTH), lambda i, j: (0, _lookup(j, table)))
    g_spec = lambda table: pl.BlockSpec((1, 1, ATT_WIDTH), lambda i, j: (_lookup(j, table), 0, 0))
    out_spec = pl.BlockSpec((None, PERM, rows, ATT_WIDTH), lambda i, j: (i // per_b, 0, i % per_b, j))
    out_sds = jax.ShapeDtypeStruct((m // seq, PERM, seq // PERM, n_steps * ATT_WIDTH), BF16)
    return pl.pallas_call(
        _attn_proj_kernel,
        grid=(m // tm, n_steps),
        in_specs=[pl.BlockSpec((tm, dm), lambda i, j: (i, 0)), pl.BlockSpec((1, dm), lambda i, j: (0, 0)),
                  w_spec(STREAM_A), w_spec(STREAM_B), g_spec(STREAM_A), g_spec(STREAM_B)],
        out_specs=[out_spec, out_spec], out_shape=[out_sds, out_sds],
        scratch_shapes=[pltpu.VMEM((tm, dm), BF16), pltpu.VMEM((2, tm, LANES), F32)],
        compiler_params=_params(("parallel", "arbitrary")),
        name="attn_proj",
    )(x, norm_w.reshape(1, dm), w, w, tile_gain, tile_gain)


def _rows_in_order(slab_ref, x):
    n = x.shape[0]
    run = n // PERM
    step = 4
    assert PERM == step * step
    quarter = n // step
    for ra in range(step):
        for rb in range(step):
            r = ra + step * rb
            slab_ref[0, pl.ds(ra * quarter + rb, run, stride=step), :] = x[r * run:(r + 1) * run]
    for ra in range(step):
        slab_ref[1, pl.ds(ra, quarter, stride=step), :] = slab_ref[0, ra * quarter:(ra + 1) * quarter, :]
    return slab_ref[1, 0:n, :]


def _outproj_kernel(a_ref, w_ref, x_ref, o_ref, *scratch, unperm):
    tm = o_ref.shape[0]
    a = a_ref[...].reshape(tm, a_ref.shape[-1]).astype(BF16)
    res = _dot(a, w_ref[...])
    if unperm:
        slab_ref, = scratch
        for c in range(o_ref.shape[1] // LANES):
            cols = slice(c * LANES, (c + 1) * LANES)
            o_ref[:, cols] = x_ref[:, cols] + _rows_in_order(slab_ref, res[:, cols])
    else:
        o_ref[...] = x_ref[...] + res


def _outproj(a, w, x, *, tm, perm_seq=None):
    m, n = x.shape
    k = w.shape[0]
    if perm_seq is not None:
        per_b = perm_seq // tm
        a_spec = pl.BlockSpec((None, PERM, tm // PERM, k), lambda i: (i // per_b, 0, i % per_b, 0))
    else:
        a_spec = pl.BlockSpec((tm, k), lambda i: (i, 0))
    return pl.pallas_call(
        functools.partial(_outproj_kernel, unperm=perm_seq is not None),
        grid=(m // tm,),
        in_specs=[a_spec, pl.BlockSpec((k, n), lambda i: (0, 0)), pl.BlockSpec((tm, n), lambda i: (i, 0))],
        out_specs=pl.BlockSpec((tm, n), lambda i: (i, 0)),
        out_shape=jax.ShapeDtypeStruct((m, n), F32),
        scratch_shapes=[pltpu.VMEM((2, tm, LANES), F32)] if perm_seq else [],
        compiler_params=_params(("parallel",)),
        name="out_proj",
    )(a, w, x)


ATT_SUB = 128


def _attn_group_kernel(*refs, n_chunks, n_sub, merge):
    it = iter(refs)
    q_ref, k_ref, v_ref = next(it), next(it), next(it)
    if merge:
        gate_ref = next(it)
        others = [(next(it), next(it)) for _ in range(len(DILATIONS) - 1)]
        o_out = next(it)
    else:
        o_out, lse_out = next(it), next(it)
    ks_ref, vs_ref = next(it), next(it)

    rc = q_ref.shape[1] // n_sub
    assert n_chunks * rc == ATT_SUB
    n = pl.program_id(2)
    narrow = rc % 16 != 0

    def piece(ref, u, cols=slice(None)):
        if narrow and ref.dtype == BF16:
            pair = 2 * (u // 2) * rc
            x = ref[:, pair:pair + 2 * rc, cols].astype(F32)[:, (u % 2) * rc:(u % 2 + 1) * rc, :]
        else:
            x = ref[:, u * rc:(u + 1) * rc, cols]
        return x.reshape(ATT_SUB, x.shape[-1])

    def unsubs(parts, dtype):
        parts = [p.reshape(n_chunks, rc, p.shape[-1]) for p in parts]
        if narrow:
            return jnp.concatenate(parts, axis=1).astype(dtype)
        return jnp.concatenate([p.astype(dtype) for p in parts], axis=1)

    @pl.when(n == 0)
    def _():
        ks_ref[0:ATT_SUB, :] = jnp.zeros((ATT_SUB, ATT_WIDTH), BF16)
        vs_ref[0:ATT_SUB, :] = jnp.zeros((ATT_SUB, ATT_WIDTH), BF16)

    for u in range(n_sub):
        ks_ref[(u + 1) * ATT_SUB:(u + 2) * ATT_SUB, :] = piece(k_ref, u).astype(BF16)
        vs_ref[(u + 1) * ATT_SUB:(u + 2) * ATT_SUB, :] = piece(v_ref, u).astype(BF16)

    qi = lax.broadcasted_iota(jnp.int32, (ATT_SUB, 2 * ATT_SUB), 0)
    kj = lax.broadcasted_iota(jnp.int32, (ATT_SUB, 2 * ATT_SUB), 1)
    kc = jnp.where(kj >= ATT_SUB, kj - ATT_SUB, kj)
    tq = n_chunks * (qi % rc) + qi // rc
    tk = n_chunks * (kc % rc) + kc // rc + jnp.where(kj >= ATT_SUB, 0, -ATT_SUB)
    dist = tq - tk
    in_band = (dist >= 0) & (dist <= WINDOW_STEPS)
    band = jnp.where(in_band, 0.0, MASKED)
    band_first = jnp.where(in_band & ((kj >= ATT_SUB) | (n > 0)), 0.0, MASKED)
    bias = [jnp.concatenate([x, x], axis=0) for x in (band_first, band)]
    ones = jnp.ones((2 * ATT_SUB, LANES), BF16)

    for j in range(N_HEADS // 2):
        cols = slice(j * LANES, (j + 1) * LANES)
        o_parts, lse_parts = [], []
        for u in range(n_sub):
            keys = slice(u * ATT_SUB, (u + 2) * ATT_SUB)
            q2 = _split_heads(piece(q_ref, u, cols).astype(BF16))
            s = _dot_nt(q2, ks_ref[keys, cols]) + bias[min(u, 1)]
            m = jnp.max(s, axis=-1, keepdims=True)
            p = jnp.exp2(s - m).astype(BF16)
            pv = _dot(p, jnp.concatenate([vs_ref[keys, cols], ones], axis=1))
            l = pv[:, LANES:]
            o = _join_heads(pv[:, :LANES] / l)
            lse = _join_heads(m + jnp.log2(l))
            if merge:
                o_g = [o] + [piece(o_ref, u, cols).astype(F32) for o_ref, _ in others]
                lse_g = [lse] + [piece(l_ref, u, cols) for _, l_ref in others]
                top = jnp.maximum(jnp.maximum(lse_g[0], lse_g[1]), lse_g[2])
                w = [jnp.exp2(x - top) for x in lse_g]
                o = (w[0] * o_g[0] + w[1] * o_g[1] + w[2] * o_g[2]) / (w[0] + w[1] + w[2])
                g = piece(gate_ref, u, cols).astype(F32)
                o = o * (g * _sigmoid(g))
            o_parts.append(o)
            lse_parts.append(lse)
        o_out[:, :, cols] = unsubs(o_parts, BF16)
        if not merge:
            lse_out[:, :, cols] = unsubs(lse_parts, F32)
    ks_ref[0:ATT_SUB, :] = ks_ref[n_sub * ATT_SUB:(n_sub + 1) * ATT_SUB, :]
    vs_ref[0:ATT_SUB, :] = vs_ref[n_sub * ATT_SUB:(n_sub + 1) * ATT_SUB, :]


def _attn_residues_kernel(*refs, n_res, **kwargs):
    blocks, scratch = refs[:-2], refs[-2:]
    for rr in range(n_res):
        _attn_group_kernel(*[r.at[:, rr] for r in blocks], *scratch, **kwargs)


def _attn_group(qkvg, g, *, n_sub, n_res=1, others=None):
    b, _, t, _ = qkvg[0].shape
    d = DILATIONS[g]
    n_chunks = PERM // d
    rc = ATT_SUB // n_chunks * n_sub
    nb = t // rc
    assert n_res == 1 or nb == 1
    merge = others is not None
    view = lambda a: a.reshape(b, n_chunks, d, t, a.shape[-1])

    def spec(col=0):
        res_dim = None if n_res == 1 else n_res
        return pl.BlockSpec((None, n_chunks, res_dim, rc, ATT_WIDTH), lambda bb, r, n: (bb, 0, r, n, col))

    in_specs, args = [], []

    def add_tile(tile):
        stream, table = (0, STREAM_A) if tile in STREAM_A else (1, STREAM_B)
        in_specs.append(spec(table.index(tile)))
        args.append(view(qkvg[stream]))

    for tile in (Q_TILES[g], K_TILES[g], V_TILES[g]):
        add_tile(tile)
    full = jax.ShapeDtypeStruct((b, n_chunks, d, t, ATT_WIDTH), BF16)
    if merge:
        add_tile(GATE_TILE)
        for o, lse in others:
            in_specs += [spec(), spec()]
            args += [view(o), view(lse)]
        out_shape, out_specs = [full], [spec()]
    else:
        out_shape = [full, jax.ShapeDtypeStruct(full.shape, F32)]
        out_specs = [spec(), spec()]
    body = functools.partial(_attn_group_kernel, n_chunks=n_chunks, n_sub=n_sub, merge=merge)
    if n_res > 1:
        body = functools.partial(_attn_residues_kernel, n_res=n_res, n_chunks=n_chunks, n_sub=n_sub, merge=merge)
    outs = pl.pallas_call(
        body,
        grid=(b, d // n_res, nb), in_specs=in_specs, out_specs=out_specs, out_shape=out_shape,
        scratch_shapes=[pltpu.VMEM(((n_sub + 1) * ATT_SUB, ATT_WIDTH), BF16)] * 2,
        compiler_params=_params(("parallel", "parallel", "arbitrary")),
        name=f"attn_group{g}",
    )(*args)
    outs = [a.reshape(b, PERM, t, ATT_WIDTH) for a in outs]
    return outs[0] if merge else tuple(outs)


def _kv_tail_kernel(k_ref, v_ref, o_ref, slab_ref):
    n = PERM * k_ref.shape[1]
    t_out = o_ref.shape[1]
    for half, ref in enumerate((k_ref, v_ref)):
        for c in range(ATT_WIDTH // LANES):
            x = ref[:, :, c * LANES:(c + 1) * LANES].astype(F32).reshape(n, LANES)
            first = half * ATT_WIDTH + c * LANES
            o_ref[first:first + LANES, :] = _rows_in_order(slab_ref, x).T[:, n - t_out:]


def _kv_tail(qkvg, g, *, keep, block):
    b, _, t, _ = qkvg[0].shape
    run = block // PERM
    t_out = min(block, keep)
    first = (t * PERM - max(keep, block)) // block

    def tile(idx):
        stream, table = (0, STREAM_A) if idx in STREAM_A else (1, STREAM_B)
        col = table.index(idx)
        return qkvg[stream], pl.BlockSpec((None, PERM, run, ATT_WIDTH), lambda bb, i: (bb, 0, first + i, col))

    (k_arr, k_spec), (v_arr, v_spec) = tile(K_TILES[g]), tile(V_TILES[g])
    return pl.pallas_call(
        _kv_tail_kernel,
        grid=(b, keep // t_out),
        in_specs=[k_spec, v_spec],
        out_specs=pl.BlockSpec((None, 2 * ATT_WIDTH, t_out), lambda bb, i: (bb, 0, i)),
        out_shape=jax.ShapeDtypeStruct((b, 2 * ATT_WIDTH, keep), F32),
        scratch_shapes=[pltpu.VMEM((2, block, LANES), F32)],
        compiler_params=_params(("parallel", "parallel")),
        name="kv_tail",
    )(k_arr, v_arr)


SAMPLE_HEADS = 8


def _sample_attn_kernel(*refs):
    n_g = len(DILATIONS)
    qkv = [refs[3 * g:3 * g + 3] for g in range(n_g)]
    gate_ref = refs[3 * n_g]
    c_refs = refs[3 * n_g + 1:4 * n_g + 1]
    qg_ref, kg_ref = refs[4 * n_g + 1:4 * n_g + 3]
    o_ref = refs[4 * n_g + 3]
    kn_refs = refs[4 * n_g + 4:]
    t_new = o_ref.shape[0]
    t_row = lax.broadcasted_iota(jnp.int32, (2 * t_new, 1), 0) % t_new
    t_col = lax.broadcasted_iota(jnp.int32, (1, t_new), 1)

    valid_c, valid_n = [], []
    for g, d in enumerate(DILATIONS):
        assert d & (d - 1) == 0
        length = c_refs[g].shape[2]
        back = length + t_row - lax.broadcasted_iota(jnp.int32, (1, length), 1)
        valid_c.append(((back & (d - 1)) == 0) & (back <= WINDOW_STEPS * d))
        back = t_row - t_col
        valid_n.append((back >= 0) & ((back & (d - 1)) == 0) & (back <= WINDOW_STEPS * d))
    ones_rows = jnp.ones((LANES, c_refs[-1].shape[2]), BF16)
    ones_new = jnp.ones((t_new, LANES), BF16)

    o_cols = []
    kn_cols = [[] for _ in DILATIONS]
    for j in range(SAMPLE_HEADS // 2):
        cols = slice(j * LANES, (j + 1) * LANES)
        o_g, m_g, l_g = [], [], []
        for g, d in enumerate(DILATIONS):
            q_ref, k_ref, v_ref = qkv[g]
            qm = _split_heads(_head_norm(q_ref[:, cols], NORM_EPS) * qg_ref[g:g + 1, :]).astype(BF16)
            kn = _head_norm(k_ref[:, cols], NORM_EPS) * kg_ref[g:g + 1, :]
            kn_cols[g].append(kn)
            length = c_refs[g].shape[2]
            s_c = jnp.where(valid_c[g], _dot(qm, c_refs[g][0, cols, :].astype(BF16)), MASKED)
            s_n = jnp.where(valid_n[g], _dot_nt(qm, kn.astype(BF16)), MASKED)
            m = jnp.maximum(jnp.max(s_c, axis=-1, keepdims=True), jnp.max(s_n, axis=-1, keepdims=True))
            vt = jnp.concatenate([c_refs[g][1, cols, :].astype(BF16), ones_rows[:, 0:length]], axis=0)
            vn = jnp.concatenate([v_ref[:, cols].astype(BF16), ones_new], axis=1)
            pv = (_dot_nt(jnp.exp(s_c - m).astype(BF16), vt)
                  + _dot(jnp.exp(s_n - m).astype(BF16), vn))
            o_g.append(pv[:, :LANES])
            l_g.append(pv[:, LANES:])
            m_g.append(m)
        top = jnp.maximum(jnp.maximum(m_g[0], m_g[1]), m_g[2])
        f = [jnp.exp(m - top) for m in m_g]
        den = f[0] * l_g[0] + f[1] * l_g[1] + f[2] * l_g[2]
        o = _join_heads((f[0] * o_g[0] + f[1] * o_g[1] + f[2] * o_g[2]) / den)
        gate = gate_ref[:, cols]
        o_cols.append(o * (gate * _sigmoid(gate)))
    o_ref[...] = jnp.concatenate(o_cols, axis=-1)
    for g in range(n_g):
        kn_refs[g][...] = jnp.concatenate(kn_cols[g], axis=-1)


def _sample_attn(proj, caches_t, q_gain, k_gain):
    b = caches_t[0].shape[0]
    t_new = proj.shape[0] // b
    width = SAMPLE_HEADS * HEAD_DIM
    per_chunk = ATT_WIDTH // width
    col = lambda c: pl.BlockSpec((t_new, width), lambda bb, hc: (bb, c * per_chunk + hc))
    in_specs, args = [], []
    for g in range(len(DILATIONS)):
        in_specs += [col(3 * g), col(3 * g + 1), col(3 * g + 2)]
        args += [proj, proj, proj]
    in_specs.append(col(ATT_IN // ATT_WIDTH - 1))
    args.append(proj)
    for c in caches_t:
        in_specs.append(pl.BlockSpec((None, 2, width, c.shape[3]), lambda bb, hc: (bb, 0, hc, 0)))
        args.append(c)
    gain_spec = pl.BlockSpec((len(DILATIONS), LANES), lambda bb, hc: (0, 0))
    out_spec = pl.BlockSpec((t_new, width), lambda bb, hc: (bb, hc))
    out_sds = jax.ShapeDtypeStruct((b * t_new, ATT_WIDTH), F32)
    return pl.pallas_call(
        _sample_attn_kernel,
        grid=(b, per_chunk),
        in_specs=in_specs + [gain_spec, gain_spec],
        out_specs=[out_spec] * 4,
        out_shape=[out_sds] * 4,
        compiler_params=_params(("parallel", "parallel")),
        name="sample_attn",
    )(*args, q_gain, k_gain)


def _softplus(x):
    return jnp.maximum(x, 0.0) + jnp.log(1.0 + jnp.exp(-jnp.abs(x)))


def _conv_silu(zx_ref, xpad_ref, cw_ref, cb_ref, lc):
    pad = 8
    xpad_ref[pad:pad + lc, :] = zx_ref[:, SSM_D_INNER:SSM_MAIN].astype(F32)
    conv = cb_ref[...] + xpad_ref[pad - 3:pad - 3 + lc, :] * cw_ref[0:1, :]
    for j in range(1, SSM_CONV):
        conv = conv + xpad_ref[pad - 3 + j:pad - 3 + j + lc, :] * cw_ref[j:j + 1, :]
    tail = xpad_ref[pad + lc - 3:pad + lc, :]
    xpad_ref[pad - 3:pad, :] = tail
    return conv * _sigmoid(conv), tail


def _gate_norm(y, z, gn):
    y = y * (z * _sigmoid(z))
    parts = []
    for g in range(SSM_GROUPS):
        yg = y[:, g * SSM_GROUP_WIDTH:(g + 1) * SSM_GROUP_WIDTH]
        parts.append(yg * lax.rsqrt(jnp.mean(yg * yg, axis=-1, keepdims=True) + GATE_NORM_EPS))
    return jnp.concatenate(parts, axis=-1) * gn


def _cumsum_rows(x):
    rows = x.shape[0]
    tril = (lax.broadcasted_iota(jnp.int32, (rows, rows), 0)
            >= lax.broadcasted_iota(jnp.int32, (rows, rows), 1)).astype(BF16)
    h1 = x.astype(BF16)
    r1 = x - h1.astype(F32)
    h2 = r1.astype(BF16)
    h3 = (r1 - h2.astype(F32)).astype(BF16)
    return _dot(tril, h1) + _dot(tril, h2) + _dot(tril, h3)


def _ssd_prompt_kernel(zx_ref, dt_ref, cw_ref, cb_ref, dtb_ref, a_ref, dskip_ref, gn_ref, e_ref,
                       y_ref, nconv_ref, nssm_ref, carry_ref, st_ref):
    c = pl.program_id(1)
    last = pl.num_programs(1) - 1

    @pl.when(c == 0)
    def _():
        carry_ref[...] = jnp.zeros(carry_ref.shape, BF16)
        st_ref[...] = jnp.zeros(st_ref.shape, F32)

    for u in range(zx_ref.shape[0] // SSM_CHUNK):
        rows = pl.ds(u * SSM_CHUNK, SSM_CHUNK)
        _ssd_chunk(zx_ref.at[rows], dt_ref.at[rows], cw_ref, cb_ref, dtb_ref, a_ref, dskip_ref, gn_ref, e_ref,
                   y_ref.at[rows], carry_ref, st_ref)

    @pl.when(c == last)
    def _():
        n_carry = carry_ref.shape[0]
        nconv_ref[...] = carry_ref[...].astype(F32)[n_carry - (SSM_CONV - 1):n_carry]
        nssm_ref[...] = st_ref[...].T


def _ssd_chunk(zx_ref, dt_ref, cw_ref, cb_ref, dtb_ref, a_ref, dskip_ref, gn_ref, e_ref, y_ref, carry_ref, st_ref):
    lc = zx_ref.shape[0]
    n_carry = carry_ref.shape[0]
    src = lax.broadcasted_iota(jnp.int32, (lc, n_carry + lc), 1) - n_carry
    dst = lax.broadcasted_iota(jnp.int32, (lc, n_carry + lc), 0)
    taps = SSM_CONV - 1
    shift = jnp.concatenate([(src == dst - (taps - j)).astype(BF16) for j in range(taps)], axis=0)

    def conv_silu(start, width):
        cols = slice(start, start + width)
        raw = zx_ref[:, SSM_D_INNER + start:SSM_D_INNER + start + width]
        shifted = _dot(shift, jnp.concatenate([carry_ref[:, cols], raw], axis=0))
        conv = cb_ref[:, cols] + shifted[0:lc] * cw_ref[0:1, cols]
        for j in range(1, taps):
            conv = conv + shifted[j * lc:(j + 1) * lc] * cw_ref[j:j + 1, cols]
        conv = conv + raw.astype(F32) * cw_ref[taps:taps + 1, cols]
        return conv * _sigmoid(conv)

    dt = _softplus(dt_ref[...] + dtb_ref[...])
    a_cs = _cumsum_rows(dt * a_ref[...])
    a_log2 = a_cs * LOG2E
    a_log2_t = a_log2.T
    dt_t = dt.T
    a_last = a_cs[lc - 1:lc, :]
    to_end = (dt * jnp.exp(a_last - a_cs)).astype(BF16)
    from_start = jnp.exp(a_cs).astype(BF16)
    chunk_decay = jnp.broadcast_to(jnp.exp(a_last), (8, LANES))
    causal = (lax.broadcasted_iota(jnp.int32, (lc, lc), 0) >= lax.broadcasted_iota(jnp.int32, (lc, lc), 1))
    heads_per_group = SSM_HEADS // SSM_GROUPS
    head_w = SSM_D_INNER // SSM_HEADS
    quad = 4
    quad_w = quad * head_w
    lane_head = lax.broadcasted_iota(jnp.int32, (1, quad_w), 1) // head_w
    n_bc = SSM_GROUPS * SSM_STATE
    for g in range(SSM_GROUPS):
        gw = slice(g * SSM_GROUP_WIDTH, (g + 1) * SSM_GROUP_WIDTH)
        xs = conv_silu(g * SSM_GROUP_WIDTH, SSM_GROUP_WIDTH)
        bm = conv_silu(SSM_D_INNER + g * SSM_STATE, SSM_STATE).astype(BF16)
        cm = conv_silu(SSM_D_INNER + n_bc + g * SSM_STATE, SSM_STATE).astype(BF16)
        e = e_ref[:, gw]
        xs_b = xs.astype(BF16)
        xdte = (xs * _dot(to_end, e)).astype(BF16)
        s_prev = st_ref[:, gw]
        cb = _dot_nt(cm, bm)
        y = _dot(cm, s_prev.astype(BF16)) * _dot(from_start, e)
        st_ref[:, gw] = (s_prev * _dot2(chunk_decay, e)[0:1, :]
                         + lax.dot_general(bm, xdte, TN, preferred_element_type=F32))
        quad_out = []
        for qd in range(heads_per_group // quad):
            h0 = g * heads_per_group + quad * qd
            ms = []
            for h in range(h0, h0 + quad):
                seg = a_log2[:, h:h + 1] - a_log2_t[h:h + 1, :]
                ms.append((cb * jnp.where(causal, jnp.exp2(seg), 0.0) * dt_t[h:h + 1, :]).astype(BF16))
            xq = xs_b[:, qd * quad_w:(qd + 1) * quad_w]
            rhs = jnp.concatenate([jnp.where(lane_head == i, xq, jnp.zeros_like(xq)) for i in range(quad)], axis=0)
            quad_out.append(_dot(jnp.concatenate(ms, axis=1), rhs))
        y = y + jnp.concatenate(quad_out, axis=-1) + dskip_ref[:, gw] * xs
        z = zx_ref[:, gw].astype(F32)
        y = y * (z * _sigmoid(z))
        y = y * lax.rsqrt(jnp.mean(y * y, axis=-1, keepdims=True) + GATE_NORM_EPS) * gn_ref[:, gw]
        y_ref[:, gw] = y.astype(y_ref.dtype)

    carry_ref[...] = zx_ref[lc - n_carry:lc, SSM_D_INNER:SSM_MAIN]


def _ssd_prompt(zx, dt, b, params, *, chunks_per_step):
    m = zx.shape[0]
    rows = chunks_per_step * SSM_CHUNK
    nc = m // b // rows
    full = lambda x: pl.BlockSpec(x.shape, lambda bb, c: (0, 0))
    row = lambda w: pl.BlockSpec((rows, w), lambda bb, c: (bb * nc + c, 0))
    return pl.pallas_call(
        _ssd_prompt_kernel,
        grid=(b, nc),
        in_specs=[row(SSM_MAIN), row(LANES)] + [full(x) for x in params],
        out_specs=[row(SSM_D_INNER),
                   pl.BlockSpec((None, SSM_CONV - 1, SSM_CONV_DIM), lambda bb, c: (bb, 0, 0)),
                   pl.BlockSpec((None, SSM_D_INNER, SSM_STATE), lambda bb, c: (bb, 0, 0))],
        out_shape=[jax.ShapeDtypeStruct((m, SSM_D_INNER), BF16),
                   jax.ShapeDtypeStruct((b, SSM_CONV - 1, SSM_CONV_DIM), F32),
                   jax.ShapeDtypeStruct((b, SSM_D_INNER, SSM_STATE), F32)],
        scratch_shapes=[pltpu.VMEM((16, SSM_CONV_DIM), BF16),
                        pltpu.VMEM((SSM_STATE, SSM_D_INNER), F32)],
        compiler_params=_params(("parallel", "arbitrary")),
        name="ssd_prompt",
    )(zx, dt, *params)


def _ssd_sample_kernel(zx_ref, dt_ref, conv0_ref, s0_ref, cw_ref, cb_ref, dtb_ref, a_ref, dskip_ref, gn_ref,
                       e_ref, eg_ref, y_ref, nconv_ref, nssm_ref, xpad_ref):
    lc = zx_ref.shape[0]
    xpad_ref[0:5, :] = jnp.zeros((5, SSM_CONV_DIM), F32)
    xpad_ref[5:8, :] = conv0_ref[...]
    xbc, tail = _conv_silu(zx_ref, xpad_ref, cw_ref, cb_ref, lc)
    nconv_ref[...] = tail

    xs = xbc[:, 0:SSM_D_INNER]
    bm = xbc[:, SSM_D_INNER:SSM_D_INNER + SSM_GROUPS * SSM_STATE]
    cm = xbc[:, SSM_D_INNER + SSM_GROUPS * SSM_STATE:]
    e = e_ref[...]
    dt = _softplus(dt_ref[...] + dtb_ref[...])
    dta = dt * a_ref[...]
    rows = [dta[0:1, :]]
    for i in range(1, lc):
        rows.append(rows[-1] + dta[i:i + 1, :])
    a_cs = jnp.concatenate(rows, axis=0)
    a_last = rows[-1]
    xdt = xs * _dot2(dt, e)
    xdte = (xdt * _dot2(jnp.exp(a_last - a_cs), e)).astype(BF16)
    ea_e = _dot2(jnp.exp(a_cs), e)
    cd_e = _dot2(jnp.broadcast_to(jnp.exp(a_last), (8, LANES)), e)[0:1, :]

    a_l = jnp.concatenate([a_cs] * lc, axis=0)
    a_s = jnp.concatenate([jnp.broadcast_to(a_cs[s:s + 1, :], (lc, LANES)) for s in range(lc)], axis=0)
    l_idx = lax.broadcasted_iota(jnp.int32, (lc * lc, 1), 0) % lc
    s_idx = lax.broadcasted_iota(jnp.int32, (lc * lc, 1), 0) // lc
    decay = jnp.where(l_idx >= s_idx, jnp.exp(a_l - a_s), 0.0)
    cb_prod = jnp.concatenate([cm * bm[s:s + 1, :] for s in range(lc)], axis=0)
    mix = _dot2(_dot2(cb_prod, eg_ref[...]) * decay, e)
    y = ea_e * 0.0
    for s in range(lc):
        y = y + mix[s * lc:(s + 1) * lc, :] * xdt[s:s + 1, :]

    s_prev = s0_ref[...].T
    s_b = s_prev.astype(BF16)
    bm_b = bm.astype(BF16)
    cm_b = cm.astype(BF16)
    y_off, s_parts = [], []
    for g in range(SSM_GROUPS):
        gs = slice(g * SSM_STATE, (g + 1) * SSM_STATE)
        gw = slice(g * SSM_GROUP_WIDTH, (g + 1) * SSM_GROUP_WIDTH)
        y_off.append(_dot(cm_b[:, gs], s_b[:, gw]))
        s_parts.append(lax.dot_general(bm_b[:, gs], xdte[:, gw], TN, preferred_element_type=F32))
    y = y + jnp.concatenate(y_off, axis=-1) * ea_e + dskip_ref[...] * xs
    s_new = s_prev * cd_e + jnp.concatenate(s_parts, axis=-1)
    nssm_ref[...] = s_new.T
    z = zx_ref[:, 0:SSM_D_INNER]
    y_ref[...] = _gate_norm(y, z, gn_ref[...])


def _ssd_sample(zx, dt, conv0, s0, params):
    b = conv0.shape[0]
    t_new = zx.shape[0] // b
    full = lambda x: pl.BlockSpec(x.shape, lambda bb: (0, 0))
    row = lambda w: pl.BlockSpec((t_new, w), lambda bb: (bb, 0))
    conv_spec = pl.BlockSpec((None, SSM_CONV - 1, SSM_CONV_DIM), lambda bb: (bb, 0, 0))
    state_spec = pl.BlockSpec((None, SSM_D_INNER, SSM_STATE), lambda bb: (bb, 0, 0))
    return pl.pallas_call(
        _ssd_sample_kernel,
        grid=(b,),
        in_specs=[row(SSM_MAIN), row(LANES), conv_spec, state_spec] + [full(x) for x in params],
        out_specs=[row(SSM_D_INNER), conv_spec, state_spec],
        out_shape=[jax.ShapeDtypeStruct((b * t_new, SSM_D_INNER), F32),
                   jax.ShapeDtypeStruct((b, SSM_CONV - 1, SSM_CONV_DIM), F32),
                   jax.ShapeDtypeStruct((b, SSM_D_INNER, SSM_STATE), F32)],
        scratch_shapes=[pltpu.VMEM((8 + t_new, SSM_CONV_DIM), F32)],
        compiler_params=_params(("parallel",)),
        name="ssd_sample",
    )(zx, dt, conv0, s0, *params)


def _one_hot_expand(n_rows, n_cols, group):
    r = lax.broadcasted_iota(jnp.int32, (n_rows, n_cols), 0)
    c = lax.broadcasted_iota(jnp.int32, (n_rows, n_cols), 1)
    return (c // group == r).astype(BF16)


def _kv_rows(feature_major, b):
    tokens = feature_major.shape[-1]
    x = feature_major.reshape(1, b, 2, N_HEADS, HEAD_DIM, tokens)
    return jnp.transpose(x, (0, 1, 5, 2, 3, 4))


def kernel(x_prompt, x_sample, cache_kv_g0, cache_kv_g1, cache_kv_g2, state_conv, state_ssm, attn_norm, attn_w_in, attn_q_gain, attn_k_gain, attn_w_out, ssm_norm, ssm_w_in, ssm_conv_w, ssm_conv_b, ssm_dt_bias, ssm_A_log, ssm_D, ssm_gate_norm, ssm_w_out):
    b, seq, dm = x_prompt.shape
    sb, st, _ = x_sample.shape
    xp = x_prompt.reshape(b * seq, dm)
    xs = x_sample.reshape(sb * st, dm)

    w_in = attn_w_in[0].astype(BF16)
    w_out = attn_w_out[0].astype(BF16)
    scale = HEAD_DIM ** -0.5
    q_gain3 = jnp.tile(attn_q_gain[0], (1, N_HEADS)) * scale
    k_gain3 = jnp.tile(attn_k_gain[0], (1, N_HEADS))

    tile_gain = jnp.ones((N_TILES, 1, ATT_WIDTH), F32)
    for g in range(len(DILATIONS)):
        tile_gain = tile_gain.at[Q_TILES[g], 0].set(q_gain3[g] * LOG2E).at[K_TILES[g], 0].set(k_gain3[g])
    qkvg = _attn_proj(xp, attn_norm[0], w_in, tile_gain, tm=PROJ_ROWS, seq=seq)
    part2 = _attn_group(qkvg, 2, n_sub=ATT_SUBS[2], n_res=ATT_RESIDUES[2])
    part1 = _attn_group(qkvg, 1, n_sub=ATT_SUBS[1], n_res=ATT_RESIDUES[1])
    o_gated = _attn_group(qkvg, 0, n_sub=ATT_SUBS[0], n_res=ATT_RESIDUES[0], others=[part1, part2])
    y1p = _outproj(o_gated, w_out, xp, tm=PROJ_ROWS, perm_seq=seq)
    kv_p = []
    for g, d in enumerate(DILATIONS):
        keep = min(d * WINDOW_STEPS, seq)
        block = min(max(keep, 16 * PERM), KV_TAIL_TOKENS)
        kv_p.append(_kv_rows(_kv_tail(qkvg, g, keep=keep, block=block), b))

    proj_s = _proj(xs, attn_norm[0], w_in, tm=sb * st, tn=SAMPLE_PROJ_COLS, out_dtype=F32)
    caches_t = [jnp.transpose(c[0], (0, 2, 3, 4, 1)).reshape(sb, 2, ATT_WIDTH, c.shape[2])
                for c in (cache_kv_g0, cache_kv_g1, cache_kv_g2)]
    o_s, kn0, kn1, kn2 = _sample_attn(proj_s, caches_t, q_gain3[:, :LANES], k_gain3[:, :LANES])
    y1s = _outproj(o_s, w_out, xs, tm=sb * st)
    kv_s = []
    for g, kn in enumerate((kn0, kn1, kn2)):
        v = proj_s[:, (3 * g + 2) * ATT_WIDTH:(3 * g + 3) * ATT_WIDTH]
        kv_s.append(jnp.stack([kn, v], axis=1).reshape(1, sb, st, 2, N_HEADS, HEAD_DIM))

    w_in2 = ssm_w_in[0].astype(BF16)
    w_dt = jnp.pad(w_in2[:, SSM_MAIN:], ((0, 0), (0, LANES - SSM_HEADS)))
    w_out2 = ssm_w_out[0].astype(BF16)
    pad_h = lambda v: jnp.pad(v.astype(F32), (0, LANES - SSM_HEADS)).reshape(1, LANES)
    e32 = _one_hot_expand(LANES, SSM_D_INNER, SSM_D_INNER // SSM_HEADS)
    gate_gain = ssm_gate_norm[0].reshape(1, -1)
    scan_params = (ssm_conv_w[0], ssm_conv_b[0].reshape(1, -1), pad_h(ssm_dt_bias[0]),
                   pad_h(-jnp.exp(ssm_A_log[0].astype(F32))),
                   jnp.repeat(ssm_D[0].astype(F32), SSM_D_INNER // SSM_HEADS).reshape(1, -1))

    zx_p, dt_p = _proj(y1p, ssm_norm[0], w_in2, tm=PROJ_ROWS, tn=SSM_MAIN // 2, n_out=SSM_MAIN, out_dtype=BF16,
                       w_dt=w_dt)
    yg_p, conv_p, ssm_p = _ssd_prompt(zx_p, dt_p, b, scan_params + (gate_gain, e32), chunks_per_step=SSD_CHUNKS)
    y2p = _outproj(yg_p, w_out2, y1p, tm=PROJ_ROWS)

    eg = (lax.broadcasted_iota(jnp.int32, (SSM_GROUPS * SSM_STATE, LANES), 0) // SSM_STATE
          == lax.broadcasted_iota(jnp.int32, (SSM_GROUPS * SSM_STATE, LANES), 1) // (SSM_HEADS // SSM_GROUPS))
    eg = (eg & (lax.broadcasted_iota(jnp.int32, eg.shape, 1) < SSM_HEADS)).astype(BF16)
    zx_s, dt_s = _proj(y1s, ssm_norm[0], w_in2, tm=sb * st, tn=SSM_MAIN // 2, n_out=SSM_MAIN, out_dtype=F32,
                       w_dt=w_dt)
    yg_s, conv_s, ssm_s = _ssd_sample(zx_s, dt_s, state_conv[0], state_ssm[0].reshape(sb, SSM_D_INNER, SSM_STATE),
                                      scan_params + (gate_gain, e32, eg))
    y2s = _outproj(yg_s, w_out2, y1s, tm=sb * st)

    hp = SSM_D_INNER // SSM_HEADS
    return (y2p.reshape(b, seq, dm), y2s.reshape(sb, st, dm),
            kv_p[0], kv_p[1], kv_p[2], kv_s[0], kv_s[1], kv_s[2],
            conv_p[None], conv_s[None],
            ssm_p.reshape(1, b, SSM_HEADS, hp, SSM_STATE), ssm_s.reshape(1, sb, SSM_HEADS, hp, SSM_STATE))
```

```python
import functools

import jax
import jax.numpy as jnp
from jax import lax
from jax.experimental import pallas as pl
from jax.experimental.pallas import tpu as pltpu

F32 = jnp.float32
BF16 = jnp.bfloat16

D_MODEL = 1024
N_HEADS = 16
HEAD_DIM = 64
ATT_WIDTH = N_HEADS * HEAD_DIM
DILATIONS = (1, 4, 16)
WINDOW_STEPS = 128
ATT_IN = 10 * ATT_WIDTH
PERM = 16

SSM_D_INNER = 2048
SSM_HEADS = 32
SSM_STATE = 128
SSM_GROUPS = 4
SSM_GROUP_WIDTH = SSM_D_INNER // SSM_GROUPS
SSM_CONV = 4
SSM_CONV_DIM = SSM_D_INNER + 2 * SSM_GROUPS * SSM_STATE
SSM_MAIN = SSM_D_INNER + SSM_CONV_DIM
SSM_CHUNK = 128

NORM_EPS = 1e-6
GATE_NORM_EPS = 1e-5
MASKED = -1e30
LOG2E = 1.4426950408889634

LANES = 128
VMEM_LIMIT = 56 * 1024 * 1024

PROJ_ROWS = 1024
SAMPLE_PROJ_COLS = 2048
ATT_SUBS = (4, 8, 2)
ATT_RESIDUES = (1, 1, 4)
SSD_CHUNKS = 4
KV_TAIL_TOKENS = 1024

NT = (((1,), (1,)), ((), ()))
TN = (((0,), (0,)), ((), ()))


def _params(semantics):
    return pltpu.CompilerParams(dimension_semantics=semantics, vmem_limit_bytes=VMEM_LIMIT)


def _rms(x, w, eps):
    return x * lax.rsqrt(jnp.mean(x * x, axis=-1, keepdims=True) + eps) * w


def _sigmoid(x):
    return 1.0 / (1.0 + jnp.exp(-x))


def _split2(v):
    hi = v.astype(BF16)
    lo = (v - hi.astype(F32)).astype(BF16)
    return hi, lo


def _dot(a, b):
    return jnp.dot(a, b, preferred_element_type=F32)


def _dot_nt(a, b):
    return lax.dot_general(a, b, NT, preferred_element_type=F32)


def _dot2(v, e):
    hi, lo = _split2(v)
    return _dot(hi, e) + _dot(lo, e)


def _low_half():
    return lax.broadcasted_iota(jnp.int32, (1, LANES), 1) < HEAD_DIM


def _head_norm(x, eps):
    lo = _low_half()
    parts = []
    for j in range(x.shape[1] // LANES):
        t = x[:, j * LANES:(j + 1) * LANES]
        t2 = t * t
        s_lo = jnp.sum(jnp.where(lo, t2, 0.0), axis=-1, keepdims=True)
        s_hi = jnp.sum(jnp.where(lo, 0.0, t2), axis=-1, keepdims=True)
        r = jnp.where(lo, lax.rsqrt(s_lo * (1.0 / HEAD_DIM) + eps), lax.rsqrt(s_hi * (1.0 / HEAD_DIM) + eps))
        parts.append(t * r)
    return jnp.concatenate(parts, axis=-1)


def _split_heads(qp):
    lo = _low_half()
    zero = jnp.zeros_like(qp)
    return jnp.concatenate([jnp.where(lo, qp, zero), jnp.where(lo, zero, qp)], axis=0)


def _join_heads(x2):
    rows = x2.shape[0] // 2
    return jnp.where(_low_half(), x2[0:rows], x2[rows:])


def _proj_kernel(x_ref, nw_ref, w_ref, *rest, with_dt):
    rest = list(rest)
    wdt_ref = rest.pop(0) if with_dt else None
    o_ref = rest.pop(0)
    dt_ref = rest.pop(0) if with_dt else None
    h_ref = rest.pop(0)

    @pl.when(pl.program_id(1) == 0)
    def _():
        h_ref[...] = _rms(x_ref[...], nw_ref[...], NORM_EPS).astype(BF16)
        if with_dt:
            dt_ref[...] = _dot(h_ref[...], wdt_ref[...])

    o_ref[...] = _dot(h_ref[...], w_ref[...]).astype(o_ref.dtype)


def _proj(x, norm_w, w, *, tm, tn, out_dtype, n_out=None, w_dt=None):
    m, dm = x.shape
    n = w.shape[1] if n_out is None else n_out
    assert n % tn == 0 and m % tm == 0
    in_specs = [pl.BlockSpec((tm, dm), lambda i, j: (i, 0)),
                pl.BlockSpec((1, dm), lambda i, j: (0, 0)),
                pl.BlockSpec((dm, tn), lambda i, j: (0, j))]
    args = [x, norm_w.reshape(1, dm), w]
    out_shape = [jax.ShapeDtypeStruct((m, n), out_dtype)]
    out_specs = [pl.BlockSpec((tm, tn), lambda i, j: (i, j))]
    if w_dt is not None:
        in_specs.append(pl.BlockSpec((dm, LANES), lambda i, j: (0, 0)))
        args.append(w_dt)
        out_shape.append(jax.ShapeDtypeStruct((m, LANES), F32))
        out_specs.append(pl.BlockSpec((tm, LANES), lambda i, j: (i, 0)))
    outs = pl.pallas_call(
        functools.partial(_proj_kernel, with_dt=w_dt is not None),
        grid=(m // tm, n // tn), in_specs=in_specs, out_specs=out_specs, out_shape=out_shape,
        scratch_shapes=[pltpu.VMEM((tm, dm), BF16)],
        compiler_params=_params(("parallel", "arbitrary")),
        name="norm_proj",
    )(*args)
    return outs if w_dt is not None else outs[0]


N_TILES = ATT_IN // ATT_WIDTH
Q_TILES, K_TILES, V_TILES, GATE_TILE = (0, 3, 6), (1, 4, 7), (2, 5, 8), 9
STREAM_A = (0, 1, 3, 4, 6)
STREAM_B = (2, 5, 8, 9, 7)


def _lookup(j, table):
    out = jnp.int32(table[0])
    for t, v in enumerate(table[1:], 1):
        out = jnp.where(j == t, jnp.int32(v), out)
    return out


def _attn_proj_kernel(x_ref, nw_ref, wa_ref, wb_ref, ga_ref, gb_ref, oa_ref, ob_ref, h_ref, slab_ref):
    j = pl.program_id(1)

    @pl.when(j == 0)
    def _():
        xn = _rms(x_ref[...], nw_ref[...], NORM_EPS)
        tm = x_ref.shape[0]
        rows = tm // PERM
        step = 4
        assert PERM == step * step
        for c in range(x_ref.shape[1] // LANES):
            cols = slice(c * LANES, (c + 1) * LANES)
            slab_ref[0] = xn[:, cols]
            for ra in range(step):
                slab_ref[1, ra * (tm // step):(ra + 1) * (tm // step), :] = slab_ref[0, pl.ds(ra, tm // step, stride=step), :]
            for ra in range(step):
                for rb in range(step):
                    r = ra + step * rb
                    piece = slab_ref[1, pl.ds(ra * (tm // step) + rb, rows, stride=step), :]
                    h_ref[r * rows:(r + 1) * rows, cols] = piece.astype(BF16)

    def tile(w_ref, gain_ref, o_ref, normed):
        res = _dot(h_ref[...], w_ref[...])
        if normed:
            res = _head_norm(res, NORM_EPS) * gain_ref[0]
        o_ref[...] = res.astype(o_ref.dtype).reshape(o_ref.shape)

    last = pl.num_programs(1) - 1

    @pl.when(j < last)
    def _():
        tile(wa_ref, ga_ref, oa_ref, True)
        tile(wb_ref, gb_ref, ob_ref, False)

    @pl.when(j == last)
    def _():
        tile(wa_ref, ga_ref, oa_ref, True)
        tile(wb_ref, gb_ref, ob_ref, True)


def _attn_proj(x, norm_w, w, tile_gain, *, tm, seq):
    m, dm = x.shape
    per_b = seq // tm
    rows = tm // PERM
    n_steps = len(STREAM_A)
    w_spec = lambda table: pl.BlockSpec((dm, ATT_WIDTH), lambda i, j: (0, _lookup(j, table)))
    g_spec = lambda table: pl.BlockSpec((1, 1, ATT_WIDTH), lambda i, j: (_lookup(j, table), 0, 0))
    out_spec = pl.BlockSpec((None, PERM, rows, ATT_WIDTH), lambda i, j: (i // per_b, 0, i % per_b, j))
    out_sds = jax.ShapeDtypeStruct((m // seq, PERM, seq // PERM, n_steps * ATT_WIDTH), BF16)
    return pl.pallas_call(
        _attn_proj_kernel,
        grid=(m // tm, n_steps),
        in_specs=[pl.BlockSpec((tm, dm), lambda i, j: (i, 0)), pl.BlockSpec((1, dm), lambda i, j: (0, 0)),
                  w_spec(STREAM_A), w_spec(STREAM_B), g_spec(STREAM_A), g_spec(STREAM_B)],
        out_specs=[out_spec, out_spec], out_shape=[out_sds, out_sds],
        scratch_shapes=[pltpu.VMEM((tm, dm), BF16), pltpu.VMEM((2, tm, LANES), F32)],
        compiler_params=_params(("parallel", "arbitrary")),
        name="attn_proj",
    )(x, norm_w.reshape(1, dm), w, w, tile_gain, tile_gain)


def _rows_in_order(slab_ref, x):
    n = x.shape[0]
    run = n // PERM
    step = 4
    assert PERM == step * step
    quarter = n // step
    for ra in range(step):
        for rb in range(step):
            r = ra + step * rb
            slab_ref[0, pl.ds(ra * quarter + rb, run, stride=step), :] = x[r * run:(r + 1) * run]
    for ra in range(step):
        slab_ref[1, pl.ds(ra, quarter, stride=step), :] = slab_ref[0, ra * quarter:(ra + 1) * quarter, :]
    return slab_ref[1, 0:n, :]


def _outproj_kernel(a_ref, w_ref, x_ref, o_ref, *scratch, unperm):
    tm = o_ref.shape[0]
    a = a_ref[...].reshape(tm, a_ref.shape[-1]).astype(BF16)
    res = _dot(a, w_ref[...])
    if unperm:
        slab_ref, = scratch
        for c in range(o_ref.shape[1] // LANES):
            cols = slice(c * LANES, (c + 1) * LANES)
            o_ref[:, cols] = x_ref[:, cols] + _rows_in_order(slab_ref, res[:, cols])
    else:
        o_ref[...] = x_ref[...] + res


def _outproj(a, w, x, *, tm, perm_seq=None):
    m, n = x.shape
    k = w.shape[0]
    if perm_seq is not None:
        per_b = perm_seq // tm
        a_spec = pl.BlockSpec((None, PERM, tm // PERM, k), lambda i: (i // per_b, 0, i % per_b, 0))
    else:
        a_spec = pl.BlockSpec((tm, k), lambda i: (i, 0))
    return pl.pallas_call(
        functools.partial(_outproj_kernel, unperm=perm_seq is not None),
        grid=(m // tm,),
        in_specs=[a_spec, pl.BlockSpec((k, n), lambda i: (0, 0)), pl.BlockSpec((tm, n), lambda i: (i, 0))],
        out_specs=pl.BlockSpec((tm, n), lambda i: (i, 0)),
        out_shape=jax.ShapeDtypeStruct((m, n), F32),
        scratch_shapes=[pltpu.VMEM((2, tm, LANES), F32)] if perm_seq else [],
        compiler_params=_params(("parallel",)),
        name="out_proj",
    )(a, w, x)


ATT_SUB = 128


def _attn_group_kernel(*refs, n_chunks, n_sub, merge):
    it = iter(refs)
    q_ref, k_ref, v_ref = next(it), next(it), next(it)
    if merge:
        gate_ref = next(it)
        others = [(next(it), next(it)) for _ in range(len(DILATIONS) - 1)]
        o_out = next(it)
    else:
        o_out, lse_out = next(it), next(it)
    ks_ref, vs_ref = next(it), next(it)

    rc = q_ref.shape[1] // n_sub
    assert n_chunks * rc == ATT_SUB
    n = pl.program_id(2)
    narrow = rc % 16 != 0

    def piece(ref, u, cols=slice(None)):
        if narrow and ref.dtype == BF16:
            pair = 2 * (u // 2) * rc
            x = ref[:, pair:pair + 2 * rc, cols].astype(F32)[:, (u % 2) * rc:(u % 2 + 1) * rc, :]
        else:
            x = ref[:, u * rc:(u + 1) * rc, cols]
        return x.reshape(ATT_SUB, x.shape[-1])

    def unsubs(parts, dtype):
        parts = [p.reshape(n_chunks, rc, p.shape[-1]) for p in parts]
        if narrow:
            return jnp.concatenate(parts, axis=1).astype(dtype)
        return jnp.concatenate([p.astype(dtype) for p in parts], axis=1)

    @pl.when(n == 0)
    def _():
        ks_ref[0:ATT_SUB, :] = jnp.zeros((ATT_SUB, ATT_WIDTH), BF16)
        vs_ref[0:ATT_SUB, :] = jnp.zeros((ATT_SUB, ATT_WIDTH), BF16)

    for u in range(n_sub):
        ks_ref[(u + 1) * ATT_SUB:(u + 2) * ATT_SUB, :] = piece(k_ref, u).astype(BF16)
        vs_ref[(u + 1) * ATT_SUB:(u + 2) * ATT_SUB, :] = piece(v_ref, u).astype(BF16)

    qi = lax.broadcasted_iota(jnp.int32, (ATT_SUB, 2 * ATT_SUB), 0)
    kj = lax.broadcasted_iota(jnp.int32, (ATT_SUB, 2 * ATT_SUB), 1)
    kc = jnp.where(kj >= ATT_SUB, kj - ATT_SUB, kj)
    tq = n_chunks * (qi % rc) + qi // rc
    tk = n_chunks * (kc % rc) + kc // rc + jnp.where(kj >= ATT_SUB, 0, -ATT_SUB)
    dist = tq - tk
    in_band = (dist >= 0) & (dist <= WINDOW_STEPS)
    band = jnp.where(in_band, 0.0, MASKED)
    band_first = jnp.where(in_band & ((kj >= ATT_SUB) | (n > 0)), 0.0, MASKED)
    bias = [jnp.concatenate([x, x], axis=0) for x in (band_first, band)]
    ones = jnp.ones((2 * ATT_SUB, LANES), BF16)

    for j in range(N_HEADS // 2):
        cols = slice(j * LANES, (j + 1) * LANES)
        o_parts, lse_parts = [], []
        for u in range(n_sub):
            keys = slice(u * ATT_SUB, (u + 2) * ATT_SUB)
            q2 = _split_heads(piece(q_ref, u, cols).astype(BF16))
            s = _dot_nt(q2, ks_ref[keys, cols]) + bias[min(u, 1)]
            m = jnp.max(s, axis=-1, keepdims=True)
            p = jnp.exp2(s - m).astype(BF16)
            pv = _dot(p, jnp.concatenate([vs_ref[keys, cols], ones], axis=1))
            l = pv[:, LANES:]
            o = _join_heads(pv[:, :LANES] / l)
            lse = _join_heads(m + jnp.log2(l))
            if merge:
                o_g = [o] + [piece(o_ref, u, cols).astype(F32) for o_ref, _ in others]
                lse_g = [lse] + [piece(l_ref, u, cols) for _, l_ref in others]
                top = jnp.maximum(jnp.maximum(lse_g[0], lse_g[1]), lse_g[2])
                w = [jnp.exp2(x - top) for x in lse_g]
                o = (w[0] * o_g[0] + w[1] * o_g[1] + w[2] * o_g[2]) / (w[0] + w[1] + w[2])
                g = piece(gate_ref, u, cols).astype(F32)
                o = o * (g * _sigmoid(g))
            o_parts.append(o)
            lse_parts.append(lse)
        o_out[:, :, cols] = unsubs(o_parts, BF16)
        if not merge:
            lse_out[:, :, cols] = unsubs(lse_parts, F32)
    ks_ref[0:ATT_SUB, :] = ks_ref[n_sub * ATT_SUB:(n_sub + 1) * ATT_SUB, :]
    vs_ref[0:ATT_SUB, :] = vs_ref[n_sub * ATT_SUB:(n_sub + 1) * ATT_SUB, :]


def _attn_residues_kernel(*refs, n_res, **kwargs):
    blocks, scratch = refs[:-2], refs[-2:]
    for rr in range(n_res):
        _attn_group_kernel(*[r.at[:, rr] for r in blocks], *scratch, **kwargs)


def _attn_group(qkvg, g, *, n_sub, n_res=1, others=None):
    b, _, t, _ = qkvg[0].shape
    d = DILATIONS[g]
    n_chunks = PERM // d
    rc = ATT_SUB // n_chunks * n_sub
    nb = t // rc
    assert n_res == 1 or nb == 1
    merge = others is not None
    view = lambda a: a.reshape(b, n_chunks, d, t, a.shape[-1])

    def spec(col=0):
        res_dim = None if n_res == 1 else n_res
        return pl.BlockSpec((None, n_chunks, res_dim, rc, ATT_WIDTH), lambda bb, r, n: (bb, 0, r, n, col))

    in_specs, args = [], []

    def add_tile(tile):
        stream, table = (0, STREAM_A) if tile in STREAM_A else (1, STREAM_B)
        in_specs.append(spec(table.index(tile)))
        args.append(view(qkvg[stream]))

    for tile in (Q_TILES[g], K_TILES[g], V_TILES[g]):
        add_tile(tile)
    full = jax.ShapeDtypeStruct((b, n_chunks, d, t, ATT_WIDTH), BF16)
    if merge:
        add_tile(GATE_TILE)
        for o, lse in others:
            in_specs += [spec(), spec()]
            args += [view(o), view(lse)]
        out_shape, out_specs = [full], [spec()]
    else:
        out_shape = [full, jax.ShapeDtypeStruct(full.shape, F32)]
        out_specs = [spec(), spec()]
    body = functools.partial(_attn_group_kernel, n_chunks=n_chunks, n_sub=n_sub, merge=merge)
    if n_res > 1:
        body = functools.partial(_attn_residues_kernel, n_res=n_res, n_chunks=n_chunks, n_sub=n_sub, merge=merge)
    outs = pl.pallas_call(
        body,
        grid=(b, d // n_res, nb), in_specs=in_specs, out_specs=out_specs, out_shape=out_shape,
        scratch_shapes=[pltpu.VMEM(((n_sub + 1) * ATT_SUB, ATT_WIDTH), BF16)] * 2,
        compiler_params=_params(("parallel", "parallel", "arbitrary")),
        name=f"attn_group{g}",
    )(*args)
    outs = [a.reshape(b, PERM, t, ATT_WIDTH) for a in outs]
    return outs[0] if merge else tuple(outs)


def _kv_tail_kernel(k_ref, v_ref, o_ref, slab_ref):
    n = PERM * k_ref.shape[1]
    t_out = o_ref.shape[1]
    for half, ref in enumerate((k_ref, v_ref)):
        for c in range(ATT_WIDTH // LANES):
            x = ref[:, :, c * LANES:(c + 1) * LANES].astype(F32).reshape(n, LANES)
            first = half * ATT_WIDTH + c * LANES
            o_ref[first:first + LANES, :] = _rows_in_order(slab_ref, x).T[:, n - t_out:]


def _kv_tail(qkvg, g, *, keep, block):
    b, _, t, _ = qkvg[0].shape
    run = block // PERM
    t_out = min(block, keep)
    first = (t * PERM - max(keep, block)) // block

    def tile(idx):
        stream, table = (0, STREAM_A) if idx in STREAM_A else (1, STREAM_B)
        col = table.index(idx)
        return qkvg[stream], pl.BlockSpec((None, PERM, run, ATT_WIDTH), lambda bb, i: (bb, 0, first + i, col))

    (k_arr, k_spec), (v_arr, v_spec) = tile(K_TILES[g]), tile(V_TILES[g])
    return pl.pallas_call(
        _kv_tail_kernel,
        grid=(b, keep // t_out),
        in_specs=[k_spec, v_spec],
        out_specs=pl.BlockSpec((None, 2 * ATT_WIDTH, t_out), lambda bb, i: (bb, 0, i)),
        out_shape=jax.ShapeDtypeStruct((b, 2 * ATT_WIDTH, keep), F32),
        scratch_shapes=[pltpu.VMEM((2, block, LANES), F32)],
        compiler_params=_params(("parallel", "parallel")),
        name="kv_tail",
    )(k_arr, v_arr)


SAMPLE_HEADS = 8


def _sample_attn_kernel(*refs):
    n_g = len(DILATIONS)
    qkv = [refs[3 * g:3 * g + 3] for g in range(n_g)]
    gate_ref = refs[3 * n_g]
    c_refs = refs[3 * n_g + 1:4 * n_g + 1]
    qg_ref, kg_ref = refs[4 * n_g + 1:4 * n_g + 3]
    o_ref = refs[4 * n_g + 3]
    kn_refs = refs[4 * n_g + 4:]
    t_new = o_ref.shape[0]
    t_row = lax.broadcasted_iota(jnp.int32, (2 * t_new, 1), 0) % t_new
    t_col = lax.broadcasted_iota(jnp.int32, (1, t_new), 1)

    valid_c, valid_n = [], []
    for g, d in enumerate(DILATIONS):
        assert d & (d - 1) == 0
        length = c_refs[g].shape[2]
        back = length + t_row - lax.broadcasted_iota(jnp.int32, (1, length), 1)
        valid_c.append(((back & (d - 1)) == 0) & (back <= WINDOW_STEPS * d))
        back = t_row - t_col
        valid_n.append((back >= 0) & ((back & (d - 1)) == 0) & (back <= WINDOW_STEPS * d))
    ones_rows = jnp.ones((LANES, c_refs[-1].shape[2]), BF16)
    ones_new = jnp.ones((t_new, LANES), BF16)

    o_cols = []
    kn_cols = [[] for _ in DILATIONS]
    for j in range(SAMPLE_HEADS // 2):
        cols = slice(j * LANES, (j + 1) * LANES)
        o_g, m_g, l_g = [], [], []
        for g, d in enumerate(DILATIONS):
            q_ref, k_ref, v_ref = qkv[g]
            qm = _split_heads(_head_norm(q_ref[:, cols], NORM_EPS) * qg_ref[g:g + 1, :]).astype(BF16)
            kn = _head_norm(k_ref[:, cols], NORM_EPS) * kg_ref[g:g + 1, :]
            kn_cols[g].append(kn)
            length = c_refs[g].shape[2]
            s_c = jnp.where(valid_c[g], _dot(qm, c_refs[g][0, cols, :].astype(BF16)), MASKED)
            s_n = jnp.where(valid_n[g], _dot_nt(qm, kn.astype(BF16)), MASKED)
            m = jnp.maximum(jnp.max(s_c, axis=-1, keepdims=True), jnp.max(s_n, axis=-1, keepdims=True))
            vt = jnp.concatenate([c_refs[g][1, cols, :].astype(BF16), ones_rows[:, 0:length]], axis=0)
            vn = jnp.concatenate([v_ref[:, cols].astype(BF16), ones_new], axis=1)
            pv = (_dot_nt(jnp.exp(s_c - m).astype(BF16), vt)
                  + _dot(jnp.exp(s_n - m).astype(BF16), vn))
            o_g.append(pv[:, :LANES])
            l_g.append(pv[:, LANES:])
            m_g.append(m)
        top = jnp.maximum(jnp.maximum(m_g[0], m_g[1]), m_g[2])
        f = [jnp.exp(m - top) for m in m_g]
        den = f[0] * l_g[0] + f[1] * l_g[1] + f[2] * l_g[2]
        o = _join_heads((f[0] * o_g[0] + f[1] * o_g[1] + f[2] * o_g[2]) / den)
        gate = gate_ref[:, cols]
        o_cols.append(o * (gate * _sigmoid(gate)))
    o_ref[...] = jnp.concatenate(o_cols, axis=-1)
    for g in range(n_g):
        kn_refs[g][...] = jnp.concatenate(kn_cols[g], axis=-1)


def _sample_attn(proj, caches_t, q_gain, k_gain):
    b = caches_t[0].shape[0]
    t_new = proj.shape[0] // b
    width = SAMPLE_HEADS * HEAD_DIM
    per_chunk = ATT_WIDTH // width
    col = lambda c: pl.BlockSpec((t_new, width), lambda bb, hc: (bb, c * per_chunk + hc))
    in_specs, args = [], []
    for g in range(len(DILATIONS)):
        in_specs += [col(3 * g), col(3 * g + 1), col(3 * g + 2)]
        args += [proj, proj, proj]
    in_specs.append(col(ATT_IN // ATT_WIDTH - 1))
    args.append(proj)
    for c in caches_t:
        in_specs.append(pl.BlockSpec((None, 2, width, c.shape[3]), lambda bb, hc: (bb, 0, hc, 0)))
        args.append(c)
    gain_spec = pl.BlockSpec((len(DILATIONS), LANES), lambda bb, hc: (0, 0))
    out_spec = pl.BlockSpec((t_new, width), lambda bb, hc: (bb, hc))
    out_sds = jax.ShapeDtypeStruct((b * t_new, ATT_WIDTH), F32)
    return pl.pallas_call(
        _sample_attn_kernel,
        grid=(b, per_chunk),
        in_specs=in_specs + [gain_spec, gain_spec],
        out_specs=[out_spec] * 4,
        out_shape=[out_sds] * 4,
        compiler_params=_params(("parallel", "parallel")),
        name="sample_attn",
    )(*args, q_gain, k_gain)


def _softplus(x):
    return jnp.maximum(x, 0.0) + jnp.log(1.0 + jnp.exp(-jnp.abs(x)))


def _conv_silu(zx_ref, xpad_ref, cw_ref, cb_ref, lc):
    pad = 8
    xpad_ref[pad:pad + lc, :] = zx_ref[:, SSM_D_INNER:SSM_MAIN].astype(F32)
    conv = cb_ref[...] + xpad_ref[pad - 3:pad - 3 + lc, :] * cw_ref[0:1, :]
    for j in range(1, SSM_CONV):
        conv = conv + xpad_ref[pad - 3 + j:pad - 3 + j + lc, :] * cw_ref[j:j + 1, :]
    tail = xpad_ref[pad + lc - 3:pad + lc, :]
    xpad_ref[pad - 3:pad, :] = tail
    return conv * _sigmoid(conv), tail


def _gate_norm(y, z, gn):
    y = y * (z * _sigmoid(z))
    parts = []
    for g in range(SSM_GROUPS):
        yg = y[:, g * SSM_GROUP_WIDTH:(g + 1) * SSM_GROUP_WIDTH]
        parts.append(yg * lax.rsqrt(jnp.mean(yg * yg, axis=-1, keepdims=True) + GATE_NORM_EPS))
    return jnp.concatenate(parts, axis=-1) * gn


def _cumsum_rows(x):
    rows = x.shape[0]
    tril = (lax.broadcasted_iota(jnp.int32, (rows, rows), 0)
            >= lax.broadcasted_iota(jnp.int32, (rows, rows), 1)).astype(BF16)
    h1 = x.astype(BF16)
    r1 = x - h1.astype(F32)
    h2 = r1.astype(BF16)
    h3 = (r1 - h2.astype(F32)).astype(BF16)
    return _dot(tril, h1) + _dot(tril, h2) + _dot(tril, h3)


def _ssd_prompt_kernel(zx_ref, dt_ref, cw_ref, cb_ref, dtb_ref, a_ref, dskip_ref, gn_ref, e_ref,
                       y_ref, nconv_ref, nssm_ref, carry_ref, st_ref):
    c = pl.program_id(1)
    last = pl.num_programs(1) - 1

    @pl.when(c == 0)
    def _():
        carry_ref[...] = jnp.zeros(carry_ref.shape, BF16)
        st_ref[...] = jnp.zeros(st_ref.shape, F32)

    for u in range(zx_ref.shape[0] // SSM_CHUNK):
        rows = pl.ds(u * SSM_CHUNK, SSM_CHUNK)
        _ssd_chunk(zx_ref.at[rows], dt_ref.at[rows], cw_ref, cb_ref, dtb_ref, a_ref, dskip_ref, gn_ref, e_ref,
                   y_ref.at[rows], carry_ref, st_ref)

    @pl.when(c == last)
    def _():
        n_carry = carry_ref.shape[0]
        nconv_ref[...] = carry_ref[...].astype(F32)[n_carry - (SSM_CONV - 1):n_carry]
        nssm_ref[...] = st_ref[...].T


def _ssd_chunk(zx_ref, dt_ref, cw_ref, cb_ref, dtb_ref, a_ref, dskip_ref, gn_ref, e_ref, y_ref, carry_ref, st_ref):
    lc = zx_ref.shape[0]
    n_carry = carry_ref.shape[0]
    src = lax.broadcasted_iota(jnp.int32, (lc, n_carry + lc), 1) - n_carry
    dst = lax.broadcasted_iota(jnp.int32, (lc, n_carry + lc), 0)
    taps = SSM_CONV - 1
    shift = jnp.concatenate([(src == dst - (taps - j)).astype(BF16) for j in range(taps)], axis=0)

    def conv_silu(start, width):
        cols = slice(start, start + width)
        raw = zx_ref[:, SSM_D_INNER + start:SSM_D_INNER + start + width]
        shifted = _dot(shift, jnp.concatenate([carry_ref[:, cols], raw], axis=0))
        conv = cb_ref[:, cols] + shifted[0:lc] * cw_ref[0:1, cols]
        for j in range(1, taps):
            conv = conv + shifted[j * lc:(j + 1) * lc] * cw_ref[j:j + 1, cols]
        conv = conv + raw.astype(F32) * cw_ref[taps:taps + 1, cols]
        return conv * _sigmoid(conv)

    dt = _softplus(dt_ref[...] + dtb_ref[...])
    a_cs = _cumsum_rows(dt * a_ref[...])
    a_log2 = a_cs * LOG2E
    a_log2_t = a_log2.T
    dt_t = dt.T
    a_last = a_cs[lc - 1:lc, :]
    to_end = (dt * jnp.exp(a_last - a_cs)).astype(BF16)
    from_start = jnp.exp(a_cs).astype(BF16)
    chunk_decay = jnp.broadcast_to(jnp.exp(a_last), (8, LANES))
    causal = (lax.broadcasted_iota(jnp.int32, (lc, lc), 0) >= lax.broadcasted_iota(jnp.int32, (lc, lc), 1))
    heads_per_group = SSM_HEADS // SSM_GROUPS
    head_w = SSM_D_INNER // SSM_HEADS
    quad = 4
    quad_w = quad * head_w
    lane_head = lax.broadcasted_iota(jnp.int32, (1, quad_w), 1) // head_w
    n_bc = SSM_GROUPS * SSM_STATE
    for g in range(SSM_GROUPS):
        gw = slice(g * SSM_GROUP_WIDTH, (g + 1) * SSM_GROUP_WIDTH)
        xs = conv_silu(g * SSM_GROUP_WIDTH, SSM_GROUP_WIDTH)
        bm = conv_silu(SSM_D_INNER + g * SSM_STATE, SSM_STATE).astype(BF16)
        cm = conv_silu(SSM_D_INNER + n_bc + g * SSM_STATE, SSM_STATE).astype(BF16)
        e = e_ref[:, gw]
        xs_b = xs.astype(BF16)
        xdte = (xs * _dot(to_end, e)).astype(BF16)
        s_prev = st_ref[:, gw]
        cb = _dot_nt(cm, bm)
        y = _dot(cm, s_prev.astype(BF16)) * _dot(from_start, e)
        st_ref[:, gw] = (s_prev * _dot2(chunk_decay, e)[0:1, :]
                         + lax.dot_general(bm, xdte, TN, preferred_element_type=F32))
        quad_out = []
        for qd in range(heads_per_group // quad):
            h0 = g * heads_per_group + quad * qd
            ms = []
            for h in range(h0, h0 + quad):
                seg = a_log2[:, h:h + 1] - a_log2_t[h:h + 1, :]
                ms.append((cb * jnp.where(causal, jnp.exp2(seg), 0.0) * dt_t[h:h + 1, :]).astype(BF16))
            xq = xs_b[:, qd * quad_w:(qd + 1) * quad_w]
            rhs = jnp.concatenate([jnp.where(lane_head == i, xq, jnp.zeros_like(xq)) for i in range(quad)], axis=0)
            quad_out.append(_dot(jnp.concatenate(ms, axis=1), rhs))
        y = y + jnp.concatenate(quad_out, axis=-1) + dskip_ref[:, gw] * xs
        z = zx_ref[:, gw].astype(F32)
        y = y * (z * _sigmoid(z))
        y = y * lax.rsqrt(jnp.mean(y * y, axis=-1, keepdims=True) + GATE_NORM_EPS) * gn_ref[:, gw]
        y_ref[:, gw] = y.astype(y_ref.dtype)

    carry_ref[...] = zx_ref[lc - n_carry:lc, SSM_D_INNER:SSM_MAIN]


def _ssd_prompt(zx, dt, b, params, *, chunks_per_step):
    m = zx.shape[0]
    rows = chunks_per_step * SSM_CHUNK
    nc = m // b // rows
    full = lambda x: pl.BlockSpec(x.shape, lambda bb, c: (0, 0))
    row = lambda w: pl.BlockSpec((rows, w), lambda bb, c: (bb * nc + c, 0))
    return pl.pallas_call(
        _ssd_prompt_kernel,
        grid=(b, nc),
        in_specs=[row(SSM_MAIN), row(LANES)] + [full(x) for x in params],
        out_specs=[row(SSM_D_INNER),
                   pl.BlockSpec((None, SSM_CONV - 1, SSM_CONV_DIM), lambda bb, c: (bb, 0, 0)),
                   pl.BlockSpec((None, SSM_D_INNER, SSM_STATE), lambda bb, c: (bb, 0, 0))],
        out_shape=[jax.ShapeDtypeStruct((m, SSM_D_INNER), BF16),
                   jax.ShapeDtypeStruct((b, SSM_CONV - 1, SSM_CONV_DIM), F32),
                   jax.ShapeDtypeStruct((b, SSM_D_INNER, SSM_STATE), F32)],
        scratch_shapes=[pltpu.VMEM((16, SSM_CONV_DIM), BF16),
                        pltpu.VMEM((SSM_STATE, SSM_D_INNER), F32)],
        compiler_params=_params(("parallel", "arbitrary")),
        name="ssd_prompt",
    )(zx, dt, *params)


def _ssd_sample_kernel(zx_ref, dt_ref, conv0_ref, s0_ref, cw_ref, cb_ref, dtb_ref, a_ref, dskip_ref, gn_ref,
                       e_ref, eg_ref, y_ref, nconv_ref, nssm_ref, xpad_ref):
    lc = zx_ref.shape[0]
    xpad_ref[0:5, :] = jnp.zeros((5, SSM_CONV_DIM), F32)
    xpad_ref[5:8, :] = conv0_ref[...]
    xbc, tail = _conv_silu(zx_ref, xpad_ref, cw_ref, cb_ref, lc)
    nconv_ref[...] = tail

    xs = xbc[:, 0:SSM_D_INNER]
    bm = xbc[:, SSM_D_INNER:SSM_D_INNER + SSM_GROUPS * SSM_STATE]
    cm = xbc[:, SSM_D_INNER + SSM_GROUPS * SSM_STATE:]
    e = e_ref[...]
    dt = _softplus(dt_ref[...] + dtb_ref[...])
    dta = dt * a_ref[...]
    rows = [dta[0:1, :]]
    for i in range(1, lc):
        rows.append(rows[-1] + dta[i:i + 1, :])
    a_cs = jnp.concatenate(rows, axis=0)
    a_last = rows[-1]
    xdt = xs * _dot2(dt, e)
    xdte = (xdt * _dot2(jnp.exp(a_last - a_cs), e)).astype(BF16)
    ea_e = _dot2(jnp.exp(a_cs), e)
    cd_e = _dot2(jnp.broadcast_to(jnp.exp(a_last), (8, LANES)), e)[0:1, :]

    a_l = jnp.concatenate([a_cs] * lc, axis=0)
    a_s = jnp.concatenate([jnp.broadcast_to(a_cs[s:s + 1, :], (lc, LANES)) for s in range(lc)], axis=0)
    l_idx = lax.broadcasted_iota(jnp.int32, (lc * lc, 1), 0) % lc
    s_idx = lax.broadcasted_iota(jnp.int32, (lc * lc, 1), 0) // lc
    decay = jnp.where(l_idx >= s_idx, jnp.exp(a_l - a_s), 0.0)
    cb_prod = jnp.concatenate([cm * bm[s:s + 1, :] for s in range(lc)], axis=0)
    mix = _dot2(_dot2(cb_prod, eg_ref[...]) * decay, e)
    y = ea_e * 0.0
    for s in range(lc):
        y = y + mix[s * lc:(s + 1) * lc, :] * xdt[s:s + 1, :]

    s_prev = s0_ref[...].T
    s_b = s_prev.astype(BF16)
    bm_b = bm.astype(BF16)
    cm_b = cm.astype(BF16)
    y_off, s_parts = [], []
    for g in range(SSM_GROUPS):
        gs = slice(g * SSM_STATE, (g + 1) * SSM_STATE)
        gw = slice(g * SSM_GROUP_WIDTH, (g + 1) * SSM_GROUP_WIDTH)
        y_off.append(_dot(cm_b[:, gs], s_b[:, gw]))
        s_parts.append(lax.dot_general(bm_b[:, gs], xdte[:, gw], TN, preferred_element_type=F32))
    y = y + jnp.concatenate(y_off, axis=-1) * ea_e + dskip_ref[...] * xs
    s_new = s_prev * cd_e + jnp.concatenate(s_parts, axis=-1)
    nssm_ref[...] = s_new.T
    z = zx_ref[:, 0:SSM_D_INNER]
    y_ref[...] = _gate_norm(y, z, gn_ref[...])


def _ssd_sample_multi_kernel(zx_ref, dt_ref, conv0_ref, s0_ref, *rest, n_seq):
    params, (y_ref, nconv_ref, nssm_ref, xpad_ref) = rest[:-4], rest[-4:]
    t_new = zx_ref.shape[0] // n_seq
    for s in range(n_seq):
        rows = pl.ds(s * t_new, t_new)
        _ssd_sample_kernel(zx_ref.at[rows], dt_ref.at[rows], conv0_ref.at[s], s0_ref.at[s], *params,
                           y_ref.at[rows], nconv_ref.at[s], nssm_ref.at[s], xpad_ref.at[s])


SAMPLE_SSD_SEQS = 4


def _ssd_sample(zx, dt, conv0, s0, params):
    b = conv0.shape[0]
    t_new = zx.shape[0] // b
    n_seq = SAMPLE_SSD_SEQS
    full = lambda x: pl.BlockSpec(x.shape, lambda bb: (0, 0))
    row = lambda w: pl.BlockSpec((n_seq * t_new, w), lambda bb: (bb, 0))
    conv_spec = pl.BlockSpec((n_seq, SSM_CONV - 1, SSM_CONV_DIM), lambda bb: (bb, 0, 0))
    state_spec = pl.BlockSpec((n_seq, SSM_D_INNER, SSM_STATE), lambda bb: (bb, 0, 0))
    return pl.pallas_call(
        functools.partial(_ssd_sample_multi_kernel, n_seq=n_seq),
        grid=(b // n_seq,),
        in_specs=[row(SSM_MAIN), row(LANES), conv_spec, state_spec] + [full(x) for x in params],
        out_specs=[row(SSM_D_INNER), conv_spec, state_spec],
        out_shape=[jax.ShapeDtypeStruct((b * t_new, SSM_D_INNER), F32),
                   jax.ShapeDtypeStruct((b, SSM_CONV - 1, SSM_CONV_DIM), F32),
                   jax.ShapeDtypeStruct((b, SSM_D_INNER, SSM_STATE), F32)],
        scratch_shapes=[pltpu.VMEM((n_seq, 8 + t_new, SSM_CONV_DIM), F32)],
        compiler_params=_params(("parallel",)),
        name="ssd_sample",
    )(zx, dt, conv0, s0, *params)


def _one_hot_expand(n_rows, n_cols, group):
    r = lax.broadcasted_iota(jnp.int32, (n_rows, n_cols), 0)
    c = lax.broadcasted_iota(jnp.int32, (n_rows, n_cols), 1)
    return (c // group == r).astype(BF16)


def _kv_rows(feature_major, b):
    tokens = feature_major.shape[-1]
    x = feature_major.reshape(1, b, 2, N_HEADS, HEAD_DIM, tokens)
    return jnp.transpose(x, (0, 1, 5, 2, 3, 4))


def kernel(x_prompt, x_sample, cache_kv_g0, cache_kv_g1, cache_kv_g2, state_conv, state_ssm, attn_norm, attn_w_in, attn_q_gain, attn_k_gain, attn_w_out, ssm_norm, ssm_w_in, ssm_conv_w, ssm_conv_b, ssm_dt_bias, ssm_A_log, ssm_D, ssm_gate_norm, ssm_w_out):
    b, seq, dm = x_prompt.shape
    sb, st, _ = x_sample.shape
    xp = x_prompt.reshape(b * seq, dm)
    xs = x_sample.reshape(sb * st, dm)

    w_in = attn_w_in[0].astype(BF16)
    w_out = attn_w_out[0].astype(BF16)
    scale = HEAD_DIM ** -0.5
    q_gain3 = jnp.tile(attn_q_gain[0], (1, N_HEADS)) * scale
    k_gain3 = jnp.tile(attn_k_gain[0], (1, N_HEADS))

    tile_gain = jnp.ones((N_TILES, 1, ATT_WIDTH), F32)
    for g in range(len(DILATIONS)):
        tile_gain = tile_gain.at[Q_TILES[g], 0].set(q_gain3[g] * LOG2E).at[K_TILES[g], 0].set(k_gain3[g])
    qkvg = _attn_proj(xp, attn_norm[0], w_in, tile_gain, tm=PROJ_ROWS, seq=seq)
    part2 = _attn_group(qkvg, 2, n_sub=ATT_SUBS[2], n_res=ATT_RESIDUES[2])
    part1 = _attn_group(qkvg, 1, n_sub=ATT_SUBS[1], n_res=ATT_RESIDUES[1])
    o_gated = _attn_group(qkvg, 0, n_sub=ATT_SUBS[0], n_res=ATT_RESIDUES[0], others=[part1, part2])
    y1p = _outproj(o_gated, w_out, xp, tm=PROJ_ROWS, perm_seq=seq)
    kv_p = []
    for g, d in enumerate(DILATIONS):
        keep = min(d * WINDOW_STEPS, seq)
        block = min(max(keep, 16 * PERM), KV_TAIL_TOKENS)
        kv_p.append(_kv_rows(_kv_tail(qkvg, g, keep=keep, block=block), b))

    proj_s = _proj(xs, attn_norm[0], w_in, tm=sb * st, tn=SAMPLE_PROJ_COLS, out_dtype=F32)
    caches_t = [jnp.transpose(c[0], (0, 2, 3, 4, 1)).reshape(sb, 2, ATT_WIDTH, c.shape[2])
                for c in (cache_kv_g0, cache_kv_g1, cache_kv_g2)]
    o_s, kn0, kn1, kn2 = _sample_attn(proj_s, caches_t, q_gain3[:, :LANES], k_gain3[:, :LANES])
    y1s = _outproj(o_s, w_out, xs, tm=sb * st)
    kv_s = []
    for g, kn in enumerate((kn0, kn1, kn2)):
        v = proj_s[:, (3 * g + 2) * ATT_WIDTH:(3 * g + 3) * ATT_WIDTH]
        kv_s.append(jnp.stack([kn, v], axis=1).reshape(1, sb, st, 2, N_HEADS, HEAD_DIM))

    w_in2 = ssm_w_in[0].astype(BF16)
    w_dt = jnp.pad(w_in2[:, SSM_MAIN:], ((0, 0), (0, LANES - SSM_HEADS)))
    w_out2 = ssm_w_out[0].astype(BF16)
    pad_h = lambda v: jnp.pad(v.astype(F32), (0, LANES - SSM_HEADS)).reshape(1, LANES)
    e32 = _one_hot_expand(LANES, SSM_D_INNER, SSM_D_INNER // SSM_HEADS)
    gate_gain = ssm_gate_norm[0].reshape(1, -1)
    scan_params = (ssm_conv_w[0], ssm_conv_b[0].reshape(1, -1), pad_h(ssm_dt_bias[0]),
                   pad_h(-jnp.exp(ssm_A_log[0].astype(F32))),
                   jnp.repeat(ssm_D[0].astype(F32), SSM_D_INNER // SSM_HEADS).reshape(1, -1))

    zx_p, dt_p = _proj(y1p, ssm_norm[0], w_in2, tm=PROJ_ROWS, tn=SSM_MAIN // 2, n_out=SSM_MAIN, out_dtype=BF16,
                       w_dt=w_dt)
    yg_p, conv_p, ssm_p = _ssd_prompt(zx_p, dt_p, b, scan_params + (gate_gain, e32), chunks_per_step=SSD_CHUNKS)
    y2p = _outproj(yg_p, w_out2, y1p, tm=PROJ_ROWS)

    eg = (lax.broadcasted_iota(jnp.int32, (SSM_GROUPS * SSM_STATE, LANES), 0) // SSM_STATE
          == lax.broadcasted_iota(jnp.int32, (SSM_GROUPS * SSM_STATE, LANES), 1) // (SSM_HEADS // SSM_GROUPS))
    eg = (eg & (lax.broadcasted_iota(jnp.int32, eg.shape, 1) < SSM_HEADS)).astype(BF16)
    zx_s, dt_s = _proj(y1s, ssm_norm[0], w_in2, tm=sb * st, tn=SSM_MAIN // 2, n_out=SSM_MAIN, out_dtype=F32,
                       w_dt=w_dt)
    yg_s, conv_s, ssm_s = _ssd_sample(zx_s, dt_s, state_conv[0], state_ssm[0].reshape(sb, SSM_D_INNER, SSM_STATE),
                                      scan_params + (gate_gain, e32, eg))
    y2s = _outproj(yg_s, w_out2, y1s, tm=sb * st)

    hp = SSM_D_INNER // SSM_HEADS
    return (y2p.reshape(b, seq, dm), y2s.reshape(sb, st, dm),
            kv_p[0], kv_p[1], kv_p[2], kv_s[0], kv_s[1], kv_s[2],
            conv_p[None], conv_s[None],
            ssm_p.reshape(1, b, SSM_HEADS, hp, SSM_STATE), ssm_s.reshape(1, sb, SSM_HEADS, hp, SSM_STATE))
```

```python
import functools

import jax
import jax.numpy as jnp
from jax import lax
from jax.experimental import pallas as pl
from jax.experimental.pallas import tpu as pltpu

F32 = jnp.float32
BF16 = jnp.bfloat16

D_MODEL = 1024
N_HEADS = 16
HEAD_DIM = 64
ATT_WIDTH = N_HEADS * HEAD_DIM
DILATIONS = (1, 4, 16)
WINDOW_STEPS = 128
ATT_IN = 10 * ATT_WIDTH
PERM = 16

SSM_D_INNER = 2048
SSM_HEADS = 32
SSM_STATE = 128
SSM_GROUPS = 4
SSM_GROUP_WIDTH = SSM_D_INNER // SSM_GROUPS
SSM_CONV = 4
SSM_CONV_DIM = SSM_D_INNER + 2 * SSM_GROUPS * SSM_STATE
SSM_MAIN = SSM_D_INNER + SSM_CONV_DIM
SSM_CHUNK = 128

NORM_EPS = 1e-6
GATE_NORM_EPS = 1e-5
MASKED = -1e30
LOG2E = 1.4426950408889634

LANES = 128
VMEM_LIMIT = 56 * 1024 * 1024

PROJ_ROWS = 1024
SAMPLE_PROJ_COLS = 2048
ATT_SUBS = (8, 8, 2)
ATT_RESIDUES = (1, 1, 4)
SSD_CHUNKS = 4
KV_TAIL_TOKENS = 1024

NT = (((1,), (1,)), ((), ()))
TN = (((0,), (0,)), ((), ()))


def _params(semantics):
    return pltpu.CompilerParams(dimension_semantics=semantics, vmem_limit_bytes=VMEM_LIMIT)


def _rms(x, w, eps):
    return x * lax.rsqrt(jnp.mean(x * x, axis=-1, keepdims=True) + eps) * w


def _sigmoid(x):
    return 1.0 / (1.0 + jnp.exp(-x))


def _split2(v):
    hi = v.astype(BF16)
    lo = (v - hi.astype(F32)).astype(BF16)
    return hi, lo


def _dot(a, b):
    return jnp.dot(a, b, preferred_element_type=F32)


def _dot_nt(a, b):
    return lax.dot_general(a, b, NT, preferred_element_type=F32)


def _dot2(v, e):
    hi, lo = _split2(v)
    return _dot(hi, e) + _dot(lo, e)


def _low_half():
    return lax.broadcasted_iota(jnp.int32, (1, LANES), 1) < HEAD_DIM


def _head_norm(x, eps):
    lo = _low_half()
    parts = []
    for j in range(x.shape[1] // LANES):
        t = x[:, j * LANES:(j + 1) * LANES]
        t2 = t * t
        s_lo = jnp.sum(jnp.where(lo, t2, 0.0), axis=-1, keepdims=True)
        s_hi = jnp.sum(jnp.where(lo, 0.0, t2), axis=-1, keepdims=True)
        r = jnp.where(lo, lax.rsqrt(s_lo * (1.0 / HEAD_DIM) + eps), lax.rsqrt(s_hi * (1.0 / HEAD_DIM) + eps))
        parts.append(t * r)
    return jnp.concatenate(parts, axis=-1)


def _split_heads(qp):
    lo = _low_half()
    zero = jnp.zeros_like(qp)
    return jnp.concatenate([jnp.where(lo, qp, zero), jnp.where(lo, zero, qp)], axis=0)


def _join_heads(x2):
    rows = x2.shape[0] // 2
    return jnp.where(_low_half(), x2[0:rows], x2[rows:])


def _proj_kernel(x_ref, nw_ref, w_ref, *rest, with_dt, keep_bf16):
    rest = list(rest)
    wdt_ref = rest.pop(0) if with_dt else None
    o_ref = rest.pop(0)
    dt_ref = rest.pop(0) if with_dt else None
    wb_ref = rest.pop(0) if keep_bf16 else None
    h_ref = rest.pop(0)

    @pl.when(pl.program_id(1) == 0)
    def _():
        h_ref[...] = _rms(x_ref[...], nw_ref[...], NORM_EPS).astype(BF16)
        if with_dt:
            dt_ref[...] = _dot(h_ref[...], wdt_ref[...])

    w = w_ref[...]
    if keep_bf16:
        w = w.astype(BF16)
        wb_ref[...] = w
    o_ref[...] = _dot(h_ref[...], w).astype(o_ref.dtype)


def _proj(x, norm_w, w, *, tm, tn, out_dtype, n_out=None, w_dt=None, keep_bf16=False):
    m, dm = x.shape
    n = w.shape[1] if n_out is None else n_out
    assert n % tn == 0 and m % tm == 0
    in_specs = [pl.BlockSpec((tm, dm), lambda i, j: (i, 0)),
                pl.BlockSpec((1, dm), lambda i, j: (0, 0)),
                pl.BlockSpec((dm, tn), lambda i, j: (0, j))]
    args = [x, norm_w.reshape(1, dm), w]
    out_shape = [jax.ShapeDtypeStruct((m, n), out_dtype)]
    out_specs = [pl.BlockSpec((tm, tn), lambda i, j: (i, j))]
    if w_dt is not None:
        in_specs.append(pl.BlockSpec((dm, LANES), lambda i, j: (0, 0)))
        args.append(w_dt)
        out_shape.append(jax.ShapeDtypeStruct((m, LANES), F32))
        out_specs.append(pl.BlockSpec((tm, LANES), lambda i, j: (i, 0)))
    if keep_bf16:
        assert m == tm and n == w.shape[1]
        out_shape.append(jax.ShapeDtypeStruct(w.shape, BF16))
        out_specs.append(pl.BlockSpec((dm, tn), lambda i, j: (0, j)))
    outs = pl.pallas_call(
        functools.partial(_proj_kernel, with_dt=w_dt is not None, keep_bf16=keep_bf16),
        grid=(m // tm, n // tn), in_specs=in_specs, out_specs=out_specs, out_shape=out_shape,
        scratch_shapes=[pltpu.VMEM((tm, dm), BF16)],
        compiler_params=_params(("parallel", "arbitrary")),
        name="norm_proj",
    )(*args)
    return outs if len(outs) > 1 else outs[0]


N_TILES = ATT_IN // ATT_WIDTH
Q_TILES, K_TILES, V_TILES, GATE_TILE = (0, 3, 6), (1, 4, 7), (2, 5, 8), 9
STREAM_A = (0, 1, 3, 4, 6)
STREAM_B = (2, 5, 8, 9, 7)


def _lookup(j, table):
    out = jnp.int32(table[0])
    for t, v in enumerate(table[1:], 1):
        out = jnp.where(j == t, jnp.int32(v), out)
    return out


def _attn_proj_kernel(x_ref, nw_ref, wa_ref, wb_ref, ga_ref, gb_ref, oa_ref, ob_ref, h_ref, slab_ref):
    j = pl.program_id(1)

    @pl.when(j == 0)
    def _():
        xn = _rms(x_ref[...], nw_ref[...], NORM_EPS)
        tm = x_ref.shape[0]
        rows = tm // PERM
        step = 4
        assert PERM == step * step
        for c in range(x_ref.shape[1] // LANES):
            cols = slice(c * LANES, (c + 1) * LANES)
            slab_ref[0] = xn[:, cols]
            for ra in range(step):
                slab_ref[1, ra * (tm // step):(ra + 1) * (tm // step), :] = slab_ref[0, pl.ds(ra, tm // step, stride=step), :]
            for ra in range(step):
                for rb in range(step):
                    r = ra + step * rb
                    piece = slab_ref[1, pl.ds(ra * (tm // step) + rb, rows, stride=step), :]
                    h_ref[r * rows:(r + 1) * rows, cols] = piece.astype(BF16)

    def tile(w_ref, gain_ref, o_ref, normed):
        res = _dot(h_ref[...], w_ref[...])
        if normed:
            res = _head_norm(res, NORM_EPS) * gain_ref[0]
        o_ref[...] = res.astype(o_ref.dtype).reshape(o_ref.shape)

    last = pl.num_programs(1) - 1

    @pl.when(j < last)
    def _():
        tile(wa_ref, ga_ref, oa_ref, True)
        tile(wb_ref, gb_ref, ob_ref, False)

    @pl.when(j == last)
    def _():
        tile(wa_ref, ga_ref, oa_ref, True)
        tile(wb_ref, gb_ref, ob_ref, True)


def _attn_proj(x, norm_w, w, tile_gain, *, tm, seq):
    m, dm = x.shape
    per_b = seq // tm
    rows = tm // PERM
    n_steps = len(STREAM_A)
    w_spec = lambda table: pl.BlockSpec((dm, ATT_WIDTH), lambda i, j: (0, _lookup(j, table)))
    g_spec = lambda table: pl.BlockSpec((1, 1, ATT_WIDTH), lambda i, j: (_lookup(j, table), 0, 0))
    out_spec = pl.BlockSpec((None, PERM, rows, ATT_WIDTH), lambda i, j: (i // per_b, 0, i % per_b, j))
    out_sds = jax.ShapeDtypeStruct((m // seq, PERM, seq // PERM, n_steps * ATT_WIDTH), BF16)
    return pl.pallas_call(
        _attn_proj_kernel,
        grid=(m // tm, n_steps),
        in_specs=[pl.BlockSpec((tm, dm), lambda i, j: (i, 0)), pl.BlockSpec((1, dm), lambda i, j: (0, 0)),
                  w_spec(STREAM_A), w_spec(STREAM_B), g_spec(STREAM_A), g_spec(STREAM_B)],
        out_specs=[out_spec, out_spec], out_shape=[out_sds, out_sds],
        scratch_shapes=[pltpu.VMEM((tm, dm), BF16), pltpu.VMEM((2, tm, LANES), F32)],
        compiler_params=_params(("parallel", "arbitrary")),
        name="attn_proj",
    )(x, norm_w.reshape(1, dm), w, w, tile_gain, tile_gain)


def _rows_in_order(slab_ref, x):
    n = x.shape[0]
    run = n // PERM
    step = 4
    assert PERM == step * step
    quarter = n // step
    for ra in range(step):
        for rb in range(step):
            r = ra + step * rb
            slab_ref[0, pl.ds(ra * quarter + rb, run, stride=step), :] = x[r * run:(r + 1) * run]
    for ra in range(step):
        slab_ref[1, pl.ds(ra, quarter, stride=step), :] = slab_ref[0, ra * quarter:(ra + 1) * quarter, :]
    return slab_ref[1, 0:n, :]


def _outproj_kernel(a_ref, w_ref, x_ref, o_ref, *scratch, unperm):
    tm = o_ref.shape[0]
    a = a_ref[...].reshape(tm, a_ref.shape[-1]).astype(BF16)
    res = _dot(a, w_ref[...])
    if unperm:
        slab_ref, = scratch
        for c in range(o_ref.shape[1] // LANES):
            cols = slice(c * LANES, (c + 1) * LANES)
            o_ref[:, cols] = x_ref[:, cols] + _rows_in_order(slab_ref, res[:, cols])
    else:
        o_ref[...] = x_ref[...] + res


def _outproj(a, w, x, *, tm, perm_seq=None):
    m, n = x.shape
    k = w.shape[0]
    if perm_seq is not None:
        per_b = perm_seq // tm
        a_spec = pl.BlockSpec((None, PERM, tm // PERM, k), lambda i: (i // per_b, 0, i % per_b, 0))
    else:
        a_spec = pl.BlockSpec((tm, k), lambda i: (i, 0))
    return pl.pallas_call(
        functools.partial(_outproj_kernel, unperm=perm_seq is not None),
        grid=(m // tm,),
        in_specs=[a_spec, pl.BlockSpec((k, n), lambda i: (0, 0)), pl.BlockSpec((tm, n), lambda i: (i, 0))],
        out_specs=pl.BlockSpec((tm, n), lambda i: (i, 0)),
        out_shape=jax.ShapeDtypeStruct((m, n), F32),
        scratch_shapes=[pltpu.VMEM((2, tm, LANES), F32)] if perm_seq else [],
        compiler_params=_params(("parallel",)),
        name="out_proj",
    )(a, w, x)


ATT_SUB = 128


def _attn_group_kernel(*refs, n_chunks, n_sub, merge):
    it = iter(refs)
    q_ref, k_ref, v_ref = next(it), next(it), next(it)
    if merge:
        gate_ref = next(it)
        others = [(next(it), next(it)) for _ in range(len(DILATIONS) - 1)]
        o_out = next(it)
    else:
        o_out, lse_out = next(it), next(it)
    ks_ref, vs_ref = next(it), next(it)

    rc = q_ref.shape[1] // n_sub
    assert n_chunks * rc == ATT_SUB
    n = pl.program_id(2)
    narrow = rc % 16 != 0

    def piece(ref, u, cols=slice(None)):
        if narrow and ref.dtype == BF16:
            pair = 2 * (u // 2) * rc
            x = ref[:, pair:pair + 2 * rc, cols].astype(F32)[:, (u % 2) * rc:(u % 2 + 1) * rc, :]
        else:
            x = ref[:, u * rc:(u + 1) * rc, cols]
        return x.reshape(ATT_SUB, x.shape[-1])

    def unsubs(parts, dtype):
        parts = [p.reshape(n_chunks, rc, p.shape[-1]) for p in parts]
        if narrow:
            return jnp.concatenate(parts, axis=1).astype(dtype)
        return jnp.concatenate([p.astype(dtype) for p in parts], axis=1)

    @pl.when(n == 0)
    def _():
        ks_ref[0:ATT_SUB, :] = jnp.zeros((ATT_SUB, ATT_WIDTH), BF16)
        vs_ref[0:ATT_SUB, :] = jnp.zeros((ATT_SUB, ATT_WIDTH), BF16)

    for u in range(n_sub):
        ks_ref[(u + 1) * ATT_SUB:(u + 2) * ATT_SUB, :] = piece(k_ref, u).astype(BF16)
        vs_ref[(u + 1) * ATT_SUB:(u + 2) * ATT_SUB, :] = piece(v_ref, u).astype(BF16)

    qi = lax.broadcasted_iota(jnp.int32, (ATT_SUB, 2 * ATT_SUB), 0)
    kj = lax.broadcasted_iota(jnp.int32, (ATT_SUB, 2 * ATT_SUB), 1)
    kc = jnp.where(kj >= ATT_SUB, kj - ATT_SUB, kj)
    tq = n_chunks * (qi % rc) + qi // rc
    tk = n_chunks * (kc % rc) + kc // rc + jnp.where(kj >= ATT_SUB, 0, -ATT_SUB)
    dist = tq - tk
    in_band = (dist >= 0) & (dist <= WINDOW_STEPS)
    band = jnp.where(in_band, 0.0, MASKED)
    band_first = jnp.where(in_band & ((kj >= ATT_SUB) | (n > 0)), 0.0, MASKED)
    bias = [jnp.concatenate([x, x], axis=0) for x in (band_first, band)]
    ones = jnp.ones((2 * ATT_SUB, LANES), BF16)

    for j in range(N_HEADS // 2):
        cols = slice(j * LANES, (j + 1) * LANES)
        o_parts, lse_parts = [], []
        for u in range(n_sub):
            keys = slice(u * ATT_SUB, (u + 2) * ATT_SUB)
            q2 = _split_heads(piece(q_ref, u, cols).astype(BF16))
            s = _dot_nt(q2, ks_ref[keys, cols]) + bias[min(u, 1)]
            m = jnp.max(s, axis=-1, keepdims=True)
            p = jnp.exp2(s - m).astype(BF16)
            pv = _dot(p, jnp.concatenate([vs_ref[keys, cols], ones], axis=1))
            l = pv[:, LANES:]
            o = _join_heads(pv[:, :LANES] / l)
            lse = _join_heads(m + jnp.log2(l))
            if merge:
                o_g = [o] + [piece(o_ref, u, cols).astype(F32) for o_ref, _ in others]
                lse_g = [lse] + [piece(l_ref, u, cols) for _, l_ref in others]
                top = jnp.maximum(jnp.maximum(lse_g[0], lse_g[1]), lse_g[2])
                w = [jnp.exp2(x - top) for x in lse_g]
                o = (w[0] * o_g[0] + w[1] * o_g[1] + w[2] * o_g[2]) / (w[0] + w[1] + w[2])
                g = piece(gate_ref, u, cols).astype(F32)
                o = o * (g * _sigmoid(g))
            o_parts.append(o)
            lse_parts.append(lse)
        o_out[:, :, cols] = unsubs(o_parts, BF16)
        if not merge:
            lse_out[:, :, cols] = unsubs(lse_parts, F32)
    ks_ref[0:ATT_SUB, :] = ks_ref[n_sub * ATT_SUB:(n_sub + 1) * ATT_SUB, :]
    vs_ref[0:ATT_SUB, :] = vs_ref[n_sub * ATT_SUB:(n_sub + 1) * ATT_SUB, :]


def _attn_residues_kernel(*refs, n_res, **kwargs):
    blocks, scratch = refs[:-2], refs[-2:]
    for rr in range(n_res):
        _attn_group_kernel(*[r.at[:, rr] for r in blocks], *scratch, **kwargs)


def _attn_group(qkvg, g, *, n_sub, n_res=1, others=None):
    b, _, t, _ = qkvg[0].shape
    d = DILATIONS[g]
    n_chunks = PERM // d
    rc = ATT_SUB // n_chunks * n_sub
    nb = t // rc
    assert n_res == 1 or nb == 1
    merge = others is not None
    view = lambda a: a.reshape(b, n_chunks, d, t, a.shape[-1])

    def spec(col=0):
        res_dim = None if n_res == 1 else n_res
        return pl.BlockSpec((None, n_chunks, res_dim, rc, ATT_WIDTH), lambda bb, r, n: (bb, 0, r, n, col))

    in_specs, args = [], []

    def add_tile(tile):
        stream, table = (0, STREAM_A) if tile in STREAM_A else (1, STREAM_B)
        in_specs.append(spec(table.index(tile)))
        args.append(view(qkvg[stream]))

    for tile in (Q_TILES[g], K_TILES[g], V_TILES[g]):
        add_tile(tile)
    full = jax.ShapeDtypeStruct((b, n_chunks, d, t, ATT_WIDTH), BF16)
    if merge:
        add_tile(GATE_TILE)
        for o, lse in others:
            in_specs += [spec(), spec()]
            args += [view(o), view(lse)]
        out_shape, out_specs = [full], [spec()]
    else:
        out_shape = [full, jax.ShapeDtypeStruct(full.shape, F32)]
        out_specs = [spec(), spec()]
    body = functools.partial(_attn_group_kernel, n_chunks=n_chunks, n_sub=n_sub, merge=merge)
    if n_res > 1:
        body = functools.partial(_attn_residues_kernel, n_res=n_res, n_chunks=n_chunks, n_sub=n_sub, merge=merge)
    outs = pl.pallas_call(
        body,
        grid=(b, d // n_res, nb), in_specs=in_specs, out_specs=out_specs, out_shape=out_shape,
        scratch_shapes=[pltpu.VMEM(((n_sub + 1) * ATT_SUB, ATT_WIDTH), BF16)] * 2,
        compiler_params=_params(("parallel", "parallel", "arbitrary")),
        name=f"attn_group{g}",
    )(*args)
    outs = [a.reshape(b, PERM, t, ATT_WIDTH) for a in outs]
    return outs[0] if merge else tuple(outs)


def _kv_tail_kernel(k_ref, v_ref, o_ref, slab_ref):
    n = PERM * k_ref.shape[1]
    t_out = o_ref.shape[1]
    for half, ref in enumerate((k_ref, v_ref)):
        for c in range(ATT_WIDTH // LANES):
            x = ref[:, :, c * LANES:(c + 1) * LANES].astype(F32).reshape(n, LANES)
            first = half * ATT_WIDTH + c * LANES
            o_ref[first:first + LANES, :] = _rows_in_order(slab_ref, x).T[:, n - t_out:]


def _kv_tail(qkvg, g, *, keep, block):
    b, _, t, _ = qkvg[0].shape
    run = block // PERM
    t_out = min(block, keep)
    first = (t * PERM - max(keep, block)) // block

    def tile(idx):
        stream, table = (0, STREAM_A) if idx in STREAM_A else (1, STREAM_B)
        col = table.index(idx)
        return qkvg[stream], pl.BlockSpec((None, PERM, run, ATT_WIDTH), lambda bb, i: (bb, 0, first + i, col))

    (k_arr, k_spec), (v_arr, v_spec) = tile(K_TILES[g]), tile(V_TILES[g])
    return pl.pallas_call(
        _kv_tail_kernel,
        grid=(b, keep // t_out),
        in_specs=[k_spec, v_spec],
        out_specs=pl.BlockSpec((None, 2 * ATT_WIDTH, t_out), lambda bb, i: (bb, 0, i)),
        out_shape=jax.ShapeDtypeStruct((b, 2 * ATT_WIDTH, keep), F32),
        scratch_shapes=[pltpu.VMEM((2, block, LANES), F32)],
        compiler_params=_params(("parallel", "parallel")),
        name="kv_tail",
    )(k_arr, v_arr)


SAMPLE_HEADS = 8


def _sample_attn_kernel(*refs):
    n_g = len(DILATIONS)
    qkv = [refs[3 * g:3 * g + 3] for g in range(n_g)]
    gate_ref = refs[3 * n_g]
    c_refs = refs[3 * n_g + 1:4 * n_g + 1]
    qg_ref, kg_ref = refs[4 * n_g + 1:4 * n_g + 3]
    o_ref = refs[4 * n_g + 3]
    kn_refs = refs[4 * n_g + 4:]
    t_new = o_ref.shape[0]
    t_row = lax.broadcasted_iota(jnp.int32, (2 * t_new, 1), 0) % t_new
    t_col = lax.broadcasted_iota(jnp.int32, (1, t_new), 1)

    valid_c, valid_n = [], []
    for g, d in enumerate(DILATIONS):
        assert d & (d - 1) == 0
        length = c_refs[g].shape[2]
        back = length + t_row - lax.broadcasted_iota(jnp.int32, (1, length), 1)
        valid_c.append(((back & (d - 1)) == 0) & (back <= WINDOW_STEPS * d))
        back = t_row - t_col
        valid_n.append((back >= 0) & ((back & (d - 1)) == 0) & (back <= WINDOW_STEPS * d))
    ones_rows = jnp.ones((LANES, c_refs[-1].shape[2]), BF16)
    ones_new = jnp.ones((t_new, LANES), BF16)

    o_cols = []
    kn_cols = [[] for _ in DILATIONS]
    for j in range(SAMPLE_HEADS // 2):
        cols = slice(j * LANES, (j + 1) * LANES)
        o_g, m_g, l_g = [], [], []
        for g, d in enumerate(DILATIONS):
            q_ref, k_ref, v_ref = qkv[g]
            qm = _split_heads(_head_norm(q_ref[:, cols], NORM_EPS) * qg_ref[g:g + 1, :]).astype(BF16)
            kn = _head_norm(k_ref[:, cols], NORM_EPS) * kg_ref[g:g + 1, :]
            kn_cols[g].append(kn)
            length = c_refs[g].shape[2]
            s_c = jnp.where(valid_c[g], _dot(qm, c_refs[g][0, cols, :].astype(BF16)), MASKED)
            s_n = jnp.where(valid_n[g], _dot_nt(qm, kn.astype(BF16)), MASKED)
            m = jnp.maximum(jnp.max(s_c, axis=-1, keepdims=True), jnp.max(s_n, axis=-1, keepdims=True))
            vt = jnp.concatenate([c_refs[g][1, cols, :].astype(BF16), ones_rows[:, 0:length]], axis=0)
            vn = jnp.concatenate([v_ref[:, cols].astype(BF16), ones_new], axis=1)
            pv = (_dot_nt(jnp.exp(s_c - m).astype(BF16), vt)
                  + _dot(jnp.exp(s_n - m).astype(BF16), vn))
            o_g.append(pv[:, :LANES])
            l_g.append(pv[:, LANES:])
            m_g.append(m)
        top = jnp.maximum(jnp.maximum(m_g[0], m_g[1]), m_g[2])
        f = [jnp.exp(m - top) for m in m_g]
        den = f[0] * l_g[0] + f[1] * l_g[1] + f[2] * l_g[2]
        o = _join_heads((f[0] * o_g[0] + f[1] * o_g[1] + f[2] * o_g[2]) / den)
        gate = gate_ref[:, cols]
        o_cols.append(o * (gate * _sigmoid(gate)))
    o_ref[...] = jnp.concatenate(o_cols, axis=-1)
    for g in range(n_g):
        kn_refs[g][...] = jnp.concatenate(kn_cols[g], axis=-1)


def _sample_attn(proj, caches_t, q_gain, k_gain):
    b = caches_t[0].shape[0]
    t_new = proj.shape[0] // b
    width = SAMPLE_HEADS * HEAD_DIM
    per_chunk = ATT_WIDTH // width
    col = lambda c: pl.BlockSpec((t_new, width), lambda bb, hc: (bb, c * per_chunk + hc))
    in_specs, args = [], []
    for g in range(len(DILATIONS)):
        in_specs += [col(3 * g), col(3 * g + 1), col(3 * g + 2)]
        args += [proj, proj, proj]
    in_specs.append(col(ATT_IN // ATT_WIDTH - 1))
    args.append(proj)
    for c in caches_t:
        in_specs.append(pl.BlockSpec((None, 2, width, c.shape[3]), lambda bb, hc: (bb, 0, hc, 0)))
        args.append(c)
    gain_spec = pl.BlockSpec((len(DILATIONS), LANES), lambda bb, hc: (0, 0))
    out_spec = pl.BlockSpec((t_new, width), lambda bb, hc: (bb, hc))
    out_sds = jax.ShapeDtypeStruct((b * t_new, ATT_WIDTH), F32)
    return pl.pallas_call(
        _sample_attn_kernel,
        grid=(b, per_chunk),
        in_specs=in_specs + [gain_spec, gain_spec],
        out_specs=[out_spec] * 4,
        out_shape=[out_sds] * 4,
        compiler_params=_params(("parallel", "parallel")),
        name="sample_attn",
    )(*args, q_gain, k_gain)


def _softplus(x):
    return jnp.maximum(x, 0.0) + jnp.log(1.0 + jnp.exp(-jnp.abs(x)))


def _conv_silu(zx_ref, xpad_ref, cw_ref, cb_ref, lc):
    pad = 8
    xpad_ref[pad:pad + lc, :] = zx_ref[:, SSM_D_INNER:SSM_MAIN].astype(F32)
    conv = cb_ref[...] + xpad_ref[pad - 3:pad - 3 + lc, :] * cw_ref[0:1, :]
    for j in range(1, SSM_CONV):
        conv = conv + xpad_ref[pad - 3 + j:pad - 3 + j + lc, :] * cw_ref[j:j + 1, :]
    tail = xpad_ref[pad + lc - 3:pad + lc, :]
    xpad_ref[pad - 3:pad, :] = tail
    return conv * _sigmoid(conv), tail


def _gate_norm(y, z, gn):
    y = y * (z * _sigmoid(z))
    parts = []
    for g in range(SSM_GROUPS):
        yg = y[:, g * SSM_GROUP_WIDTH:(g + 1) * SSM_GROUP_WIDTH]
        parts.append(yg * lax.rsqrt(jnp.mean(yg * yg, axis=-1, keepdims=True) + GATE_NORM_EPS))
    return jnp.concatenate(parts, axis=-1) * gn


def _cumsum_rows(x):
    rows = x.shape[0]
    tril = (lax.broadcasted_iota(jnp.int32, (rows, rows), 0)
            >= lax.broadcasted_iota(jnp.int32, (rows, rows), 1)).astype(BF16)
    h1 = x.astype(BF16)
    r1 = x - h1.astype(F32)
    h2 = r1.astype(BF16)
    h3 = (r1 - h2.astype(F32)).astype(BF16)
    return _dot(tril, h1) + _dot(tril, h2) + _dot(tril, h3)


def _ssd_prompt_kernel(zx_ref, dt_ref, cw_ref, cb_ref, dtb_ref, a_ref, dskip_ref, gn_ref, e_ref,
                       y_ref, nconv_ref, nssm_ref, carry_ref, st_ref):
    c = pl.program_id(1)
    last = pl.num_programs(1) - 1

    @pl.when(c == 0)
    def _():
        carry_ref[...] = jnp.zeros(carry_ref.shape, BF16)
        st_ref[...] = jnp.zeros(st_ref.shape, F32)

    for u in range(zx_ref.shape[0] // SSM_CHUNK):
        rows = pl.ds(u * SSM_CHUNK, SSM_CHUNK)
        _ssd_chunk(zx_ref.at[rows], dt_ref.at[rows], cw_ref, cb_ref, dtb_ref, a_ref, dskip_ref, gn_ref, e_ref,
                   y_ref.at[rows], carry_ref, st_ref)

    @pl.when(c == last)
    def _():
        n_carry = carry_ref.shape[0]
        nconv_ref[...] = carry_ref[...].astype(F32)[n_carry - (SSM_CONV - 1):n_carry]
        nssm_ref[...] = st_ref[...].T


def _ssd_chunk(zx_ref, dt_ref, cw_ref, cb_ref, dtb_ref, a_ref, dskip_ref, gn_ref, e_ref, y_ref, carry_ref, st_ref):
    lc = zx_ref.shape[0]
    n_carry = carry_ref.shape[0]
    src = lax.broadcasted_iota(jnp.int32, (lc, n_carry + lc), 1) - n_carry
    dst = lax.broadcasted_iota(jnp.int32, (lc, n_carry + lc), 0)
    taps = SSM_CONV - 1
    shift = jnp.concatenate([(src == dst - (taps - j)).astype(BF16) for j in range(taps)], axis=0)

    def conv_silu(start, width):
        cols = slice(start, start + width)
        raw = zx_ref[:, SSM_D_INNER + start:SSM_D_INNER + start + width]
        shifted = _dot(shift, jnp.concatenate([carry_ref[:, cols], raw], axis=0))
        conv = cb_ref[:, cols] + shifted[0:lc] * cw_ref[0:1, cols]
        for j in range(1, taps):
            conv = conv + shifted[j * lc:(j + 1) * lc] * cw_ref[j:j + 1, cols]
        conv = conv + raw.astype(F32) * cw_ref[taps:taps + 1, cols]
        return conv * _sigmoid(conv)

    dt = _softplus(dt_ref[...] + dtb_ref[...])
    a_cs = _cumsum_rows(dt * a_ref[...])
    a_log2 = a_cs * LOG2E
    a_log2_t = a_log2.T
    dt_t = dt.T
    a_last = a_cs[lc - 1:lc, :]
    to_end = (dt * jnp.exp(a_last - a_cs)).astype(BF16)
    from_start = jnp.exp(a_cs).astype(BF16)
    chunk_decay = jnp.broadcast_to(jnp.exp(a_last), (8, LANES))
    causal = (lax.broadcasted_iota(jnp.int32, (lc, lc), 0) >= lax.broadcasted_iota(jnp.int32, (lc, lc), 1))
    heads_per_group = SSM_HEADS // SSM_GROUPS
    head_w = SSM_D_INNER // SSM_HEADS
    quad = 4
    quad_w = quad * head_w
    lane_head = lax.broadcasted_iota(jnp.int32, (1, quad_w), 1) // head_w
    n_bc = SSM_GROUPS * SSM_STATE
    for g in range(SSM_GROUPS):
        gw = slice(g * SSM_GROUP_WIDTH, (g + 1) * SSM_GROUP_WIDTH)
        xs = conv_silu(g * SSM_GROUP_WIDTH, SSM_GROUP_WIDTH)
        bm = conv_silu(SSM_D_INNER + g * SSM_STATE, SSM_STATE).astype(BF16)
        cm = conv_silu(SSM_D_INNER + n_bc + g * SSM_STATE, SSM_STATE).astype(BF16)
        e = e_ref[:, gw]
        xs_b = xs.astype(BF16)
        xdte = (xs * _dot(to_end, e)).astype(BF16)
        s_prev = st_ref[:, gw]
        cb = _dot_nt(cm, bm)
        y = _dot(cm, s_prev.astype(BF16)) * _dot(from_start, e)
        st_ref[:, gw] = (s_prev * _dot2(chunk_decay, e)[0:1, :]
                         + lax.dot_general(bm, xdte, TN, preferred_element_type=F32))
        quad_out = []
        for qd in range(heads_per_group // quad):
            h0 = g * heads_per_group + quad * qd
            ms = []
            for h in range(h0, h0 + quad):
                seg = a_log2[:, h:h + 1] - a_log2_t[h:h + 1, :]
                ms.append((cb * jnp.where(causal, jnp.exp2(seg), 0.0) * dt_t[h:h + 1, :]).astype(BF16))
            xq = xs_b[:, qd * quad_w:(qd + 1) * quad_w]
            rhs = jnp.concatenate([jnp.where(lane_head == i, xq, jnp.zeros_like(xq)) for i in range(quad)], axis=0)
            quad_out.append(_dot(jnp.concatenate(ms, axis=1), rhs))
        y = y + jnp.concatenate(quad_out, axis=-1) + dskip_ref[:, gw] * xs
        z = zx_ref[:, gw].astype(F32)
        y = y * (z * _sigmoid(z))
        y = y * lax.rsqrt(jnp.mean(y * y, axis=-1, keepdims=True) + GATE_NORM_EPS) * gn_ref[:, gw]
        y_ref[:, gw] = y.astype(y_ref.dtype)

    carry_ref[...] = zx_ref[lc - n_carry:lc, SSM_D_INNER:SSM_MAIN]


def _ssd_prompt(zx, dt, b, params, *, chunks_per_step):
    m = zx.shape[0]
    rows = chunks_per_step * SSM_CHUNK
    nc = m // b // rows
    full = lambda x: pl.BlockSpec(x.shape, lambda bb, c: (0, 0))
    row = lambda w: pl.BlockSpec((rows, w), lambda bb, c: (bb * nc + c, 0))
    return pl.pallas_call(
        _ssd_prompt_kernel,
        grid=(b, nc),
        in_specs=[row(SSM_MAIN), row(LANES)] + [full(x) for x in params],
        out_specs=[row(SSM_D_INNER),
                   pl.BlockSpec((None, SSM_CONV - 1, SSM_CONV_DIM), lambda bb, c: (bb, 0, 0)),
                   pl.BlockSpec((None, SSM_D_INNER, SSM_STATE), lambda bb, c: (bb, 0, 0))],
        out_shape=[jax.ShapeDtypeStruct((m, SSM_D_INNER), BF16),
                   jax.ShapeDtypeStruct((b, SSM_CONV - 1, SSM_CONV_DIM), F32),
                   jax.ShapeDtypeStruct((b, SSM_D_INNER, SSM_STATE), F32)],
        scratch_shapes=[pltpu.VMEM((16, SSM_CONV_DIM), BF16),
                        pltpu.VMEM((SSM_STATE, SSM_D_INNER), F32)],
        compiler_params=_params(("parallel", "arbitrary")),
        name="ssd_prompt",
    )(zx, dt, *params)


def _ssd_sample_kernel(zx_ref, dt_ref, conv0_ref, s0_ref, cw_ref, cb_ref, dtb_ref, a_ref, dskip_ref, gn_ref,
                       e_ref, eg_ref, y_ref, nconv_ref, nssm_ref, xpad_ref):
    lc = zx_ref.shape[0]
    xpad_ref[0:5, :] = jnp.zeros((5, SSM_CONV_DIM), F32)
    xpad_ref[5:8, :] = conv0_ref[...]
    xbc, tail = _conv_silu(zx_ref, xpad_ref, cw_ref, cb_ref, lc)
    nconv_ref[...] = tail

    xs = xbc[:, 0:SSM_D_INNER]
    bm = xbc[:, SSM_D_INNER:SSM_D_INNER + SSM_GROUPS * SSM_STATE]
    cm = xbc[:, SSM_D_INNER + SSM_GROUPS * SSM_STATE:]
    e = e_ref[...]
    dt = _softplus(dt_ref[...] + dtb_ref[...])
    dta = dt * a_ref[...]
    rows = [dta[0:1, :]]
    for i in range(1, lc):
        rows.append(rows[-1] + dta[i:i + 1, :])
    a_cs = jnp.concatenate(rows, axis=0)
    a_last = rows[-1]
    xdt = xs * _dot2(dt, e)
    xdte = (xdt * _dot2(jnp.exp(a_last - a_cs), e)).astype(BF16)
    ea_e = _dot2(jnp.exp(a_cs), e)
    cd_e = _dot2(jnp.broadcast_to(jnp.exp(a_last), (8, LANES)), e)[0:1, :]

    a_l = jnp.concatenate([a_cs] * lc, axis=0)
    a_s = jnp.concatenate([jnp.broadcast_to(a_cs[s:s + 1, :], (lc, LANES)) for s in range(lc)], axis=0)
    l_idx = lax.broadcasted_iota(jnp.int32, (lc * lc, 1), 0) % lc
    s_idx = lax.broadcasted_iota(jnp.int32, (lc * lc, 1), 0) // lc
    decay = jnp.where(l_idx >= s_idx, jnp.exp(a_l - a_s), 0.0)
    cb_prod = jnp.concatenate([cm * bm[s:s + 1, :] for s in range(lc)], axis=0)
    mix = _dot2(_dot2(cb_prod, eg_ref[...]) * decay, e)
    y = ea_e * 0.0
    for s in range(lc):
        y = y + mix[s * lc:(s + 1) * lc, :] * xdt[s:s + 1, :]

    s_prev = s0_ref[...].T
    s_b = s_prev.astype(BF16)
    bm_b = bm.astype(BF16)
    cm_b = cm.astype(BF16)
    y_off, s_parts = [], []
    for g in range(SSM_GROUPS):
        gs = slice(g * SSM_STATE, (g + 1) * SSM_STATE)
        gw = slice(g * SSM_GROUP_WIDTH, (g + 1) * SSM_GROUP_WIDTH)
        y_off.append(_dot(cm_b[:, gs], s_b[:, gw]))
        s_parts.append(lax.dot_general(bm_b[:, gs], xdte[:, gw], TN, preferred_element_type=F32))
    y = y + jnp.concatenate(y_off, axis=-1) * ea_e + dskip_ref[...] * xs
    s_new = s_prev * cd_e + jnp.concatenate(s_parts, axis=-1)
    nssm_ref[...] = s_new.T
    z = zx_ref[:, 0:SSM_D_INNER]
    y_ref[...] = _gate_norm(y, z, gn_ref[...])


def _ssd_sample_multi_kernel(zx_ref, dt_ref, conv0_ref, s0_ref, *rest, n_seq):
    params, (y_ref, nconv_ref, nssm_ref, xpad_ref) = rest[:-4], rest[-4:]
    t_new = zx_ref.shape[0] // n_seq
    for s in range(n_seq):
        rows = pl.ds(s * t_new, t_new)
        _ssd_sample_kernel(zx_ref.at[rows], dt_ref.at[rows], conv0_ref.at[s], s0_ref.at[s], *params,
                           y_ref.at[rows], nconv_ref.at[s], nssm_ref.at[s], xpad_ref.at[s])


SAMPLE_SSD_SEQS = 4


def _ssd_sample(zx, dt, conv0, s0, params):
    b = conv0.shape[0]
    t_new = zx.shape[0] // b
    n_seq = SAMPLE_SSD_SEQS
    full = lambda x: pl.BlockSpec(x.shape, lambda bb: (0, 0))
    row = lambda w: pl.BlockSpec((n_seq * t_new, w), lambda bb: (bb, 0))
    conv_spec = pl.BlockSpec((n_seq, SSM_CONV - 1, SSM_CONV_DIM), lambda bb: (bb, 0, 0))
    state_spec = pl.BlockSpec((n_seq, SSM_D_INNER, SSM_STATE), lambda bb: (bb, 0, 0))
    return pl.pallas_call(
        functools.partial(_ssd_sample_multi_kernel, n_seq=n_seq),
        grid=(b // n_seq,),
        in_specs=[row(SSM_MAIN), row(LANES), conv_spec, state_spec] + [full(x) for x in params],
        out_specs=[row(SSM_D_INNER), conv_spec, state_spec],
        out_shape=[jax.ShapeDtypeStruct((b * t_new, SSM_D_INNER), F32),
                   jax.ShapeDtypeStruct((b, SSM_CONV - 1, SSM_CONV_DIM), F32),
                   jax.ShapeDtypeStruct((b, SSM_D_INNER, SSM_STATE), F32)],
        scratch_shapes=[pltpu.VMEM((n_seq, 8 + t_new, SSM_CONV_DIM), F32)],
        compiler_params=_params(("parallel",)),
        name="ssd_sample",
    )(zx, dt, conv0, s0, *params)


def _one_hot_expand(n_rows, n_cols, group):
    r = lax.broadcasted_iota(jnp.int32, (n_rows, n_cols), 0)
    c = lax.broadcasted_iota(jnp.int32, (n_rows, n_cols), 1)
    return (c // group == r).astype(BF16)


def _kv_rows(feature_major, b):
    tokens = feature_major.shape[-1]
    x = feature_major.reshape(1, b, 2, N_HEADS, HEAD_DIM, tokens)
    return jnp.transpose(x, (0, 1, 5, 2, 3, 4))


def kernel(x_prompt, x_sample, cache_kv_g0, cache_kv_g1, cache_kv_g2, state_conv, state_ssm, attn_norm, attn_w_in, attn_q_gain, attn_k_gain, attn_w_out, ssm_norm, ssm_w_in, ssm_conv_w, ssm_conv_b, ssm_dt_bias, ssm_A_log, ssm_D, ssm_gate_norm, ssm_w_out):
    b, seq, dm = x_prompt.shape
    sb, st, _ = x_sample.shape
    xp = x_prompt.reshape(b * seq, dm)
    xs = x_sample.reshape(sb * st, dm)

    proj_s, w_in = _proj(xs, attn_norm[0], attn_w_in[0], tm=sb * st, tn=SAMPLE_PROJ_COLS, out_dtype=F32,
                         keep_bf16=True)
    w_out = attn_w_out[0].astype(BF16)
    scale = HEAD_DIM ** -0.5
    q_gain3 = jnp.tile(attn_q_gain[0], (1, N_HEADS)) * scale
    k_gain3 = jnp.tile(attn_k_gain[0], (1, N_HEADS))

    tile_gain = jnp.ones((N_TILES, 1, ATT_WIDTH), F32)
    for g in range(len(DILATIONS)):
        tile_gain = tile_gain.at[Q_TILES[g], 0].set(q_gain3[g] * LOG2E).at[K_TILES[g], 0].set(k_gain3[g])
    qkvg = _attn_proj(xp, attn_norm[0], w_in, tile_gain, tm=PROJ_ROWS, seq=seq)
    part2 = _attn_group(qkvg, 2, n_sub=ATT_SUBS[2], n_res=ATT_RESIDUES[2])
    part1 = _attn_group(qkvg, 1, n_sub=ATT_SUBS[1], n_res=ATT_RESIDUES[1])
    o_gated = _attn_group(qkvg, 0, n_sub=ATT_SUBS[0], n_res=ATT_RESIDUES[0], others=[part1, part2])
    y1p = _outproj(o_gated, w_out, xp, tm=PROJ_ROWS, perm_seq=seq)
    kv_p = []
    for g, d in enumerate(DILATIONS):
        keep = min(d * WINDOW_STEPS, seq)
        block = min(max(keep, 16 * PERM), KV_TAIL_TOKENS)
        kv_p.append(_kv_rows(_kv_tail(qkvg, g, keep=keep, block=block), b))

    caches_t = [jnp.transpose(c[0], (0, 2, 3, 4, 1)).reshape(sb, 2, ATT_WIDTH, c.shape[2])
                for c in (cache_kv_g0, cache_kv_g1, cache_kv_g2)]
    o_s, kn0, kn1, kn2 = _sample_attn(proj_s, caches_t, q_gain3[:, :LANES], k_gain3[:, :LANES])
    y1s = _outproj(o_s, w_out, xs, tm=sb * st)
    kv_s = []
    for g, kn in enumerate((kn0, kn1, kn2)):
        v = proj_s[:, (3 * g + 2) * ATT_WIDTH:(3 * g + 3) * ATT_WIDTH]
        kv_s.append(jnp.stack([kn, v], axis=1).reshape(1, sb, st, 2, N_HEADS, HEAD_DIM))

    w_in2 = ssm_w_in[0].astype(BF16)
    w_dt = jnp.pad(w_in2[:, SSM_MAIN:], ((0, 0), (0, LANES - SSM_HEADS)))
    w_out2 = ssm_w_out[0].astype(BF16)
    pad_h = lambda v: jnp.pad(v.astype(F32), (0, LANES - SSM_HEADS)).reshape(1, LANES)
    e32 = _one_hot_expand(LANES, SSM_D_INNER, SSM_D_INNER // SSM_HEADS)
    gate_gain = ssm_gate_norm[0].reshape(1, -1)
    scan_params = (ssm_conv_w[0], ssm_conv_b[0].reshape(1, -1), pad_h(ssm_dt_bias[0]),
                   pad_h(-jnp.exp(ssm_A_log[0].astype(F32))),
                   jnp.repeat(ssm_D[0].astype(F32), SSM_D_INNER // SSM_HEADS).reshape(1, -1))

    zx_p, dt_p = _proj(y1p, ssm_norm[0], w_in2, tm=PROJ_ROWS, tn=SSM_MAIN // 2, n_out=SSM_MAIN, out_dtype=BF16,
                       w_dt=w_dt)
    yg_p, conv_p, ssm_p = _ssd_prompt(zx_p, dt_p, b, scan_params + (gate_gain, e32), chunks_per_step=SSD_CHUNKS)
    y2p = _outproj(yg_p, w_out2, y1p, tm=PROJ_ROWS)

    eg = (lax.broadcasted_iota(jnp.int32, (SSM_GROUPS * SSM_STATE, LANES), 0) // SSM_STATE
          == lax.broadcasted_iota(jnp.int32, (SSM_GROUPS * SSM_STATE, LANES), 1) // (SSM_HEADS // SSM_GROUPS))
    eg = (eg & (lax.broadcasted_iota(jnp.int32, eg.shape, 1) < SSM_HEADS)).astype(BF16)
    zx_s, dt_s = _proj(y1s, ssm_norm[0], w_in2, tm=sb * st, tn=SSM_MAIN // 2, n_out=SSM_MAIN, out_dtype=F32,
                       w_dt=w_dt)
    yg_s, conv_s, ssm_s = _ssd_sample(zx_s, dt_s, state_conv[0], state_ssm[0].reshape(sb, SSM_D_INNER, SSM_STATE),
                                      scan_params + (gate_gain, e32, eg))
    y2s = _outproj(yg_s, w_out2, y1s, tm=sb * st)

    hp = SSM_D_INNER // SSM_HEADS
    return (y2p.reshape(b, seq, dm), y2s.reshape(sb, st, dm),
            kv_p[0], kv_p[1], kv_p[2], kv_s[0], kv_s[1], kv_s[2],
            conv_p[None], conv_s[None],
            ssm_p.reshape(1, b, SSM_HEADS, hp, SSM_STATE), ssm_s.reshape(1, sb, SSM_HEADS, hp, SSM_STATE))
```

```python
import functools

import jax
import jax.numpy as jnp
from jax import lax
from jax.experimental import pallas as pl
from jax.experimental.pallas import tpu as pltpu

F32 = jnp.float32
BF16 = jnp.bfloat16

D_MODEL = 1024
N_HEADS = 16
HEAD_DIM = 64
ATT_WIDTH = N_HEADS * HEAD_DIM
DILATIONS = (1, 4, 16)
WINDOW_STEPS = 128
ATT_IN = 10 * ATT_WIDTH
PERM = 16

SSM_D_INNER = 2048
SSM_HEADS = 32
SSM_STATE = 128
SSM_GROUPS = 4
SSM_GROUP_WIDTH = SSM_D_INNER // SSM_GROUPS
SSM_CONV = 4
SSM_CONV_DIM = SSM_D_INNER + 2 * SSM_GROUPS * SSM_STATE
SSM_MAIN = SSM_D_INNER + SSM_CONV_DIM
SSM_CHUNK = 128

NORM_EPS = 1e-6
GATE_NORM_EPS = 1e-5
MASKED = -1e30
LOG2E = 1.4426950408889634

LANES = 128
VMEM_LIMIT = 56 * 1024 * 1024

PROJ_ROWS = 1024
SAMPLE_PROJ_COLS = 2048
ATT_SUBS = (8, 8, 2)
ATT_RESIDUES = (1, 1, 4)
SSD_CHUNKS = 4
KV_TAIL_TOKENS = 1024

NT = (((1,), (1,)), ((), ()))
TN = (((0,), (0,)), ((), ()))


def _params(semantics):
    return pltpu.CompilerParams(dimension_semantics=semantics, vmem_limit_bytes=VMEM_LIMIT)


def _rms(x, w, eps):
    return x * lax.rsqrt(jnp.mean(x * x, axis=-1, keepdims=True) + eps) * w


def _sigmoid(x):
    return 1.0 / (1.0 + jnp.exp(-x))


def _split2(v):
    hi = v.astype(BF16)
    lo = (v - hi.astype(F32)).astype(BF16)
    return hi, lo


def _dot(a, b):
    return jnp.dot(a, b, preferred_element_type=F32)


def _dot_nt(a, b):
    return lax.dot_general(a, b, NT, preferred_element_type=F32)


def _dot2(v, e):
    hi, lo = _split2(v)
    return _dot(hi, e) + _dot(lo, e)


def _low_half():
    return lax.broadcasted_iota(jnp.int32, (1, LANES), 1) < HEAD_DIM


def _head_norm(x, eps):
    lo = _low_half()
    parts = []
    for j in range(x.shape[1] // LANES):
        t = x[:, j * LANES:(j + 1) * LANES]
        t2 = t * t
        s_lo = jnp.sum(jnp.where(lo, t2, 0.0), axis=-1, keepdims=True)
        s_hi = jnp.sum(jnp.where(lo, 0.0, t2), axis=-1, keepdims=True)
        r = jnp.where(lo, lax.rsqrt(s_lo * (1.0 / HEAD_DIM) + eps), lax.rsqrt(s_hi * (1.0 / HEAD_DIM) + eps))
        parts.append(t * r)
    return jnp.concatenate(parts, axis=-1)


def _split_heads(qp):
    lo = _low_half()
    zero = jnp.zeros_like(qp)
    return jnp.concatenate([jnp.where(lo, qp, zero), jnp.where(lo, zero, qp)], axis=0)


def _join_heads(x2):
    rows = x2.shape[0] // 2
    return jnp.where(_low_half(), x2[0:rows], x2[rows:])


def _proj_kernel(x_ref, nw_ref, w_ref, *rest, with_dt, keep_bf16):
    rest = list(rest)
    wdt_ref = rest.pop(0) if with_dt else None
    o_ref = rest.pop(0)
    dt_ref = rest.pop(0) if with_dt else None
    wb_ref = rest.pop(0) if keep_bf16 else None
    h_ref = rest.pop(0)

    @pl.when(pl.program_id(1) == 0)
    def _():
        h_ref[...] = _rms(x_ref[...], nw_ref[...], NORM_EPS).astype(BF16)
        if with_dt:
            dt_ref[...] = _dot(h_ref[...], wdt_ref[...])

    w = w_ref[...]
    if keep_bf16:
        w = w.astype(BF16)
        wb_ref[...] = w
    o_ref[...] = _dot(h_ref[...], w).astype(o_ref.dtype)


def _proj(x, norm_w, w, *, tm, tn, out_dtype, n_out=None, w_dt=None, keep_bf16=False):
    m, dm = x.shape
    n = w.shape[1] if n_out is None else n_out
    assert n % tn == 0 and m % tm == 0
    in_specs = [pl.BlockSpec((tm, dm), lambda i, j: (i, 0)),
                pl.BlockSpec((1, dm), lambda i, j: (0, 0)),
                pl.BlockSpec((dm, tn), lambda i, j: (0, j))]
    args = [x, norm_w.reshape(1, dm), w]
    out_shape = [jax.ShapeDtypeStruct((m, n), out_dtype)]
    out_specs = [pl.BlockSpec((tm, tn), lambda i, j: (i, j))]
    if w_dt is not None:
        in_specs.append(pl.BlockSpec((dm, LANES), lambda i, j: (0, 0)))
        args.append(w_dt)
        out_shape.append(jax.ShapeDtypeStruct((m, LANES), F32))
        out_specs.append(pl.BlockSpec((tm, LANES), lambda i, j: (i, 0)))
    if keep_bf16:
        assert m == tm and n == w.shape[1]
        out_shape.append(jax.ShapeDtypeStruct(w.shape, BF16))
        out_specs.append(pl.BlockSpec((dm, tn), lambda i, j: (0, j)))
    outs = pl.pallas_call(
        functools.partial(_proj_kernel, with_dt=w_dt is not None, keep_bf16=keep_bf16),
        grid=(m // tm, n // tn), in_specs=in_specs, out_specs=out_specs, out_shape=out_shape,
        scratch_shapes=[pltpu.VMEM((tm, dm), BF16)],
        compiler_params=_params(("parallel", "arbitrary")),
        name="norm_proj",
    )(*args)
    return outs if len(outs) > 1 else outs[0]


N_TILES = ATT_IN // ATT_WIDTH
Q_TILES, K_TILES, V_TILES, GATE_TILE = (0, 3, 6), (1, 4, 7), (2, 5, 8), 9
STREAM_A = (0, 1, 3, 4, 6)
STREAM_B = (2, 5, 8, 9, 7)


def _lookup(j, table):
    out = jnp.int32(table[0])
    for t, v in enumerate(table[1:], 1):
        out = jnp.where(j == t, jnp.int32(v), out)
    return out


def _attn_proj_kernel(x_ref, nw_ref, wa_ref, wb_ref, ga_ref, gb_ref, oa_ref, ob_ref, h_ref, slab_ref):
    j = pl.program_id(1)

    @pl.when(j == 0)
    def _():
        xn = _rms(x_ref[...], nw_ref[...], NORM_EPS)
        tm = x_ref.shape[0]
        rows = tm // PERM
        step = 4
        assert PERM == step * step
        for c in range(x_ref.shape[1] // LANES):
            cols = slice(c * LANES, (c + 1) * LANES)
            slab_ref[0] = xn[:, cols]
            for ra in range(step):
                slab_ref[1, ra * (tm // step):(ra + 1) * (tm // step), :] = slab_ref[0, pl.ds(ra, tm // step, stride=step), :]
            for ra in range(step):
                for rb in range(step):
                    r = ra + step * rb
                    piece = slab_ref[1, pl.ds(ra * (tm // step) + rb, rows, stride=step), :]
                    h_ref[r * rows:(r + 1) * rows, cols] = piece.astype(BF16)

    def tile(w_ref, gain_ref, o_ref, normed):
        res = _dot(h_ref[...], w_ref[...])
        if normed:
            res = _head_norm(res, NORM_EPS) * gain_ref[0]
        o_ref[...] = res.astype(o_ref.dtype).reshape(o_ref.shape)

    last = pl.num_programs(1) - 1

    @pl.when(j < last)
    def _():
        tile(wa_ref, ga_ref, oa_ref, True)
        tile(wb_ref, gb_ref, ob_ref, False)

    @pl.when(j == last)
    def _():
        tile(wa_ref, ga_ref, oa_ref, True)
        tile(wb_ref, gb_ref, ob_ref, True)


def _attn_proj(x, norm_w, w, tile_gain, *, tm, seq):
    m, dm = x.shape
    per_b = seq // tm
    rows = tm // PERM
    n_steps = len(STREAM_A)
    w_spec = lambda table: pl.BlockSpec((dm, ATT_WIDTH), lambda i, j: (0, _lookup(j, table)))
    g_spec = lambda table: pl.BlockSpec((1, 1, ATT_WIDTH), lambda i, j: (_lookup(j, table), 0, 0))
    out_spec = pl.BlockSpec((None, PERM, rows, ATT_WIDTH), lambda i, j: (i // per_b, 0, i % per_b, j))
    out_sds = jax.ShapeDtypeStruct((m // seq, PERM, seq // PERM, n_steps * ATT_WIDTH), BF16)
    return pl.pallas_call(
        _attn_proj_kernel,
        grid=(m // tm, n_steps),
        in_specs=[pl.BlockSpec((tm, dm), lambda i, j: (i, 0)), pl.BlockSpec((1, dm), lambda i, j: (0, 0)),
                  w_spec(STREAM_A), w_spec(STREAM_B), g_spec(STREAM_A), g_spec(STREAM_B)],
        out_specs=[out_spec, out_spec], out_shape=[out_sds, out_sds],
        scratch_shapes=[pltpu.VMEM((tm, dm), BF16), pltpu.VMEM((2, tm, LANES), F32)],
        compiler_params=_params(("parallel", "arbitrary")),
        name="attn_proj",
    )(x, norm_w.reshape(1, dm), w, w, tile_gain, tile_gain)


def _rows_in_order(slab_ref, x):
    n = x.shape[0]
    run = n // PERM
    step = 4
    assert PERM == step * step
    quarter = n // step
    for ra in range(step):
        for rb in range(step):
            r = ra + step * rb
            slab_ref[0, pl.ds(ra * quarter + rb, run, stride=step), :] = x[r * run:(r + 1) * run]
    for ra in range(step):
        slab_ref[1, pl.ds(ra, quarter, stride=step), :] = slab_ref[0, ra * quarter:(ra + 1) * quarter, :]
    return slab_ref[1, 0:n, :]


def _outproj_kernel(a_ref, w_ref, x_ref, o_ref, *scratch, unperm):
    tm = o_ref.shape[0]
    a = a_ref[...].reshape(tm, a_ref.shape[-1]).astype(BF16)
    res = _dot(a, w_ref[...])
    if unperm:
        slab_ref, = scratch
        for c in range(o_ref.shape[1] // LANES):
            cols = slice(c * LANES, (c + 1) * LANES)
            o_ref[:, cols] = x_ref[:, cols] + _rows_in_order(slab_ref, res[:, cols])
    else:
        o_ref[...] = x_ref[...] + res


def _outproj(a, w, x, *, tm, perm_seq=None):
    m, n = x.shape
    k = w.shape[0]
    if perm_seq is not None:
        per_b = perm_seq // tm
        a_spec = pl.BlockSpec((None, PERM, tm // PERM, k), lambda i: (i // per_b, 0, i % per_b, 0))
    else:
        a_spec = pl.BlockSpec((tm, k), lambda i: (i, 0))
    return pl.pallas_call(
        functools.partial(_outproj_kernel, unperm=perm_seq is not None),
        grid=(m // tm,),
        in_specs=[a_spec, pl.BlockSpec((k, n), lambda i: (0, 0)), pl.BlockSpec((tm, n), lambda i: (i, 0))],
        out_specs=pl.BlockSpec((tm, n), lambda i: (i, 0)),
        out_shape=jax.ShapeDtypeStruct((m, n), F32),
        scratch_shapes=[pltpu.VMEM((2, tm, LANES), F32)] if perm_seq else [],
        compiler_params=_params(("parallel",)),
        name="out_proj",
    )(a, w, x)


ATT_SUB = 128


def _attn_group_kernel(*refs, n_chunks, n_sub, merge):
    it = iter(refs)
    q_ref, k_ref, v_ref = next(it), next(it), next(it)
    if merge:
        gate_ref = next(it)
        others = [(next(it), next(it)) for _ in range(len(DILATIONS) - 1)]
        o_out = next(it)
    else:
        o_out, lse_out = next(it), next(it)
    ks_ref, vs_ref = next(it), next(it)

    rc = q_ref.shape[1] // n_sub
    assert n_chunks * rc == ATT_SUB
    n = pl.program_id(2)
    narrow = rc % 16 != 0

    def piece(ref, u, cols=slice(None)):
        if narrow and ref.dtype == BF16:
            pair = 2 * (u // 2) * rc
            x = ref[:, pair:pair + 2 * rc, cols].astype(F32)[:, (u % 2) * rc:(u % 2 + 1) * rc, :]
        else:
            x = ref[:, u * rc:(u + 1) * rc, cols]
        return x.reshape(ATT_SUB, x.shape[-1])

    def unsubs(parts, dtype):
        parts = [p.reshape(n_chunks, rc, p.shape[-1]) for p in parts]
        if narrow:
            return jnp.concatenate(parts, axis=1).astype(dtype)
        return jnp.concatenate([p.astype(dtype) for p in parts], axis=1)

    @pl.when(n == 0)
    def _():
        ks_ref[0:ATT_SUB, :] = jnp.zeros((ATT_SUB, ATT_WIDTH), BF16)
        vs_ref[0:ATT_SUB, :] = jnp.zeros((ATT_SUB, ATT_WIDTH), BF16)

    for u in range(n_sub):
        ks_ref[(u + 1) * ATT_SUB:(u + 2) * ATT_SUB, :] = piece(k_ref, u).astype(BF16)
        vs_ref[(u + 1) * ATT_SUB:(u + 2) * ATT_SUB, :] = piece(v_ref, u).astype(BF16)

    qi = lax.broadcasted_iota(jnp.int32, (ATT_SUB, 2 * ATT_SUB), 0)
    kj = lax.broadcasted_iota(jnp.int32, (ATT_SUB, 2 * ATT_SUB), 1)
    kc = jnp.where(kj >= ATT_SUB, kj - ATT_SUB, kj)
    tq = n_chunks * (qi % rc) + qi // rc
    tk = n_chunks * (kc % rc) + kc // rc + jnp.where(kj >= ATT_SUB, 0, -ATT_SUB)
    dist = tq - tk
    in_band = (dist >= 0) & (dist <= WINDOW_STEPS)
    band = jnp.where(in_band, 0.0, MASKED)
    band_first = jnp.where(in_band & ((kj >= ATT_SUB) | (n > 0)), 0.0, MASKED)
    bias = [jnp.concatenate([x, x], axis=0) for x in (band_first, band)]
    ones = jnp.ones((2 * ATT_SUB, LANES), BF16)

    for j in range(N_HEADS // 2):
        cols = slice(j * LANES, (j + 1) * LANES)
        o_parts, lse_parts = [], []
        for u in range(n_sub):
            keys = slice(u * ATT_SUB, (u + 2) * ATT_SUB)
            q2 = _split_heads(piece(q_ref, u, cols).astype(BF16))
            s = _dot_nt(q2, ks_ref[keys, cols]) + bias[min(u, 1)]
            m = jnp.max(s, axis=-1, keepdims=True)
            p = jnp.exp2(s - m).astype(BF16)
            pv = _dot(p, jnp.concatenate([vs_ref[keys, cols], ones], axis=1))
            l = pv[:, LANES:]
            o = _join_heads(pv[:, :LANES] / l)
            lse = _join_heads(m + jnp.log2(l))
            if merge:
                o_g = [o] + [piece(o_ref, u, cols).astype(F32) for o_ref, _ in others]
                lse_g = [lse] + [piece(l_ref, u, cols) for _, l_ref in others]
                top = jnp.maximum(jnp.maximum(lse_g[0], lse_g[1]), lse_g[2])
                w = [jnp.exp2(x - top) for x in lse_g]
                o = (w[0] * o_g[0] + w[1] * o_g[1] + w[2] * o_g[2]) / (w[0] + w[1] + w[2])
                g = piece(gate_ref, u, cols).astype(F32)
                o = o * (g * _sigmoid(g))
            o_parts.append(o)
            lse_parts.append(lse)
        o_out[:, :, cols] = unsubs(o_parts, BF16)
        if not merge:
            lse_out[:, :, cols] = unsubs(lse_parts, F32)
    ks_ref[0:ATT_SUB, :] = ks_ref[n_sub * ATT_SUB:(n_sub + 1) * ATT_SUB, :]
    vs_ref[0:ATT_SUB, :] = vs_ref[n_sub * ATT_SUB:(n_sub + 1) * ATT_SUB, :]


def _attn_residues_kernel(*refs, n_res, **kwargs):
    blocks, scratch = refs[:-2], refs[-2:]
    for rr in range(n_res):
        _attn_group_kernel(*[r.at[:, rr] for r in blocks], *scratch, **kwargs)


def _attn_group(qkvg, g, *, n_sub, n_res=1, others=None):
    b, _, t, _ = qkvg[0].shape
    d = DILATIONS[g]
    n_chunks = PERM // d
    rc = ATT_SUB // n_chunks * n_sub
    nb = t // rc
    assert n_res == 1 or nb == 1
    merge = others is not None
    view = lambda a: a.reshape(b, n_chunks, d, t, a.shape[-1])

    def spec(col=0):
        res_dim = None if n_res == 1 else n_res
        return pl.BlockSpec((None, n_chunks, res_dim, rc, ATT_WIDTH), lambda bb, r, n: (bb, 0, r, n, col))

    in_specs, args = [], []

    def add_tile(tile):
        stream, table = (0, STREAM_A) if tile in STREAM_A else (1, STREAM_B)
        in_specs.append(spec(table.index(tile)))
        args.append(view(qkvg[stream]))

    for tile in (Q_TILES[g], K_TILES[g], V_TILES[g]):
        add_tile(tile)
    full = jax.ShapeDtypeStruct((b, n_chunks, d, t, ATT_WIDTH), BF16)
    if merge:
        add_tile(GATE_TILE)
        for o, lse in others:
            in_specs += [spec(), spec()]
            args += [view(o), view(lse)]
        out_shape, out_specs = [full], [spec()]
    else:
        out_shape = [full, jax.ShapeDtypeStruct(full.shape, F32)]
        out_specs = [spec(), spec()]
    body = functools.partial(_attn_group_kernel, n_chunks=n_chunks, n_sub=n_sub, merge=merge)
    if n_res > 1:
        body = functools.partial(_attn_residues_kernel, n_res=n_res, n_chunks=n_chunks, n_sub=n_sub, merge=merge)
    outs = pl.pallas_call(
        body,
        grid=(b, d // n_res, nb), in_specs=in_specs, out_specs=out_specs, out_shape=out_shape,
        scratch_shapes=[pltpu.VMEM(((n_sub + 1) * ATT_SUB, ATT_WIDTH), BF16)] * 2,
        compiler_params=_params(("parallel", "parallel", "arbitrary")),
        name=f"attn_group{g}",
    )(*args)
    outs = [a.reshape(b, PERM, t, ATT_WIDTH) for a in outs]
    return outs[0] if merge else tuple(outs)


def _kv_tail_kernel(k_ref, v_ref, o_ref, slab_ref):
    n = PERM * k_ref.shape[1]
    t_out = o_ref.shape[1]
    for half, ref in enumerate((k_ref, v_ref)):
        for c in range(ATT_WIDTH // LANES):
            x = ref[:, :, c * LANES:(c + 1) * LANES].astype(F32).reshape(n, LANES)
            first = half * ATT_WIDTH + c * LANES
            o_ref[first:first + LANES, :] = _rows_in_order(slab_ref, x).T[:, n - t_out:]


def _kv_tail(qkvg, g, *, keep, block):
    b, _, t, _ = qkvg[0].shape
    run = block // PERM
    t_out = min(block, keep)
    first = (t * PERM - max(keep, block)) // block

    def tile(idx):
        stream, table = (0, STREAM_A) if idx in STREAM_A else (1, STREAM_B)
        col = table.index(idx)
        return qkvg[stream], pl.BlockSpec((None, PERM, run, ATT_WIDTH), lambda bb, i: (bb, 0, first + i, col))

    (k_arr, k_spec), (v_arr, v_spec) = tile(K_TILES[g]), tile(V_TILES[g])
    return pl.pallas_call(
        _kv_tail_kernel,
        grid=(b, keep // t_out),
        in_specs=[k_spec, v_spec],
        out_specs=pl.BlockSpec((None, 2 * ATT_WIDTH, t_out), lambda bb, i: (bb, 0, i)),
        out_shape=jax.ShapeDtypeStruct((b, 2 * ATT_WIDTH, keep), F32),
        scratch_shapes=[pltpu.VMEM((2, block, LANES), F32)],
        compiler_params=_params(("parallel", "parallel")),
        name="kv_tail",
    )(k_arr, v_arr)


SAMPLE_HEADS = 8


def _sample_attn_kernel(*refs):
    n_g = len(DILATIONS)
    qkv = [refs[3 * g:3 * g + 3] for g in range(n_g)]
    gate_ref = refs[3 * n_g]
    c_refs = refs[3 * n_g + 1:4 * n_g + 1]
    qg_ref, kg_ref = refs[4 * n_g + 1:4 * n_g + 3]
    o_ref = refs[4 * n_g + 3]
    kn_refs = refs[4 * n_g + 4:]
    t_new = o_ref.shape[0]
    t_row = lax.broadcasted_iota(jnp.int32, (2 * t_new, 1), 0) % t_new
    t_col = lax.broadcasted_iota(jnp.int32, (1, t_new), 1)

    valid_c, valid_n = [], []
    for g, d in enumerate(DILATIONS):
        assert d & (d - 1) == 0
        length = c_refs[g].shape[2]
        back = length + t_row - lax.broadcasted_iota(jnp.int32, (1, length), 1)
        valid_c.append(((back & (d - 1)) == 0) & (back <= WINDOW_STEPS * d))
        back = t_row - t_col
        valid_n.append((back >= 0) & ((back & (d - 1)) == 0) & (back <= WINDOW_STEPS * d))
    ones_rows = jnp.ones((LANES, c_refs[-1].shape[2]), BF16)
    ones_new = jnp.ones((t_new, LANES), BF16)

    o_cols = []
    kn_cols = [[] for _ in DILATIONS]
    for j in range(SAMPLE_HEADS // 2):
        cols = slice(j * LANES, (j + 1) * LANES)
        o_g, m_g, l_g = [], [], []
        for g, d in enumerate(DILATIONS):
            q_ref, k_ref, v_ref = qkv[g]
            qm = _split_heads(_head_norm(q_ref[:, cols], NORM_EPS) * qg_ref[g:g + 1, :]).astype(BF16)
            kn = _head_norm(k_ref[:, cols], NORM_EPS) * kg_ref[g:g + 1, :]
            kn_cols[g].append(kn)
            length = c_refs[g].shape[2]
            s_c = jnp.where(valid_c[g], _dot(qm, c_refs[g][0, cols, :].astype(BF16)), MASKED)
            s_n = jnp.where(valid_n[g], _dot_nt(qm, kn.astype(BF16)), MASKED)
            m = jnp.maximum(jnp.max(s_c, axis=-1, keepdims=True), jnp.max(s_n, axis=-1, keepdims=True))
            vt = jnp.concatenate([c_refs[g][1, cols, :].astype(BF16), ones_rows[:, 0:length]], axis=0)
            vn = jnp.concatenate([v_ref[:, cols].astype(BF16), ones_new], axis=1)
            pv = (_dot_nt(jnp.exp(s_c - m).astype(BF16), vt)
                  + _dot(jnp.exp(s_n - m).astype(BF16), vn))
            o_g.append(pv[:, :LANES])
            l_g.append(pv[:, LANES:])
            m_g.append(m)
        top = jnp.maximum(jnp.maximum(m_g[0], m_g[1]), m_g[2])
        f = [jnp.exp(m - top) for m in m_g]
        den = f[0] * l_g[0] + f[1] * l_g[1] + f[2] * l_g[2]
        o = _join_heads((f[0] * o_g[0] + f[1] * o_g[1] + f[2] * o_g[2]) / den)
        gate = gate_ref[:, cols]
        o_cols.append(o * (gate * _sigmoid(gate)))
    o_ref[...] = jnp.concatenate(o_cols, axis=-1)
    for g in range(n_g):
        kn_refs[g][...] = jnp.concatenate(kn_cols[g], axis=-1)


def _sample_attn(proj, caches_t, q_gain, k_gain):
    b = caches_t[0].shape[0]
    t_new = proj.shape[0] // b
    width = SAMPLE_HEADS * HEAD_DIM
    per_chunk = ATT_WIDTH // width
    col = lambda c: pl.BlockSpec((t_new, width), lambda bb, hc: (bb, c * per_chunk + hc))
    in_specs, args = [], []
    for g in range(len(DILATIONS)):
        in_specs += [col(3 * g), col(3 * g + 1), col(3 * g + 2)]
        args += [proj, proj, proj]
    in_specs.append(col(ATT_IN // ATT_WIDTH - 1))
    args.append(proj)
    for c in caches_t:
        in_specs.append(pl.BlockSpec((None, 2, width, c.shape[3]), lambda bb, hc: (bb, 0, hc, 0)))
        args.append(c)
    gain_spec = pl.BlockSpec((len(DILATIONS), LANES), lambda bb, hc: (0, 0))
    out_spec = pl.BlockSpec((t_new, width), lambda bb, hc: (bb, hc))
    out_sds = jax.ShapeDtypeStruct((b * t_new, ATT_WIDTH), F32)
    return pl.pallas_call(
        _sample_attn_kernel,
        grid=(b, per_chunk),
        in_specs=in_specs + [gain_spec, gain_spec],
        out_specs=[out_spec] * 4,
        out_shape=[out_sds] * 4,
        compiler_params=_params(("parallel", "parallel")),
        name="sample_attn",
    )(*args, q_gain, k_gain)


def _softplus(x):
    return jnp.maximum(x, 0.0) + jnp.log(1.0 + jnp.exp(-jnp.abs(x)))


def _conv_silu(zx_ref, xpad_ref, cw_ref, cb_ref, lc):
    pad = 8
    xpad_ref[pad:pad + lc, :] = zx_ref[:, SSM_D_INNER:SSM_MAIN].astype(F32)
    conv = cb_ref[...] + xpad_ref[pad - 3:pad - 3 + lc, :] * cw_ref[0:1, :]
    for j in range(1, SSM_CONV):
        conv = conv + xpad_ref[pad - 3 + j:pad - 3 + j + lc, :] * cw_ref[j:j + 1, :]
    tail = xpad_ref[pad + lc - 3:pad + lc, :]
    xpad_ref[pad - 3:pad, :] = tail
    return conv * _sigmoid(conv), tail


def _gate_norm(y, z, gn):
    y = y * (z * _sigmoid(z))
    parts = []
    for g in range(SSM_GROUPS):
        yg = y[:, g * SSM_GROUP_WIDTH:(g + 1) * SSM_GROUP_WIDTH]
        parts.append(yg * lax.rsqrt(jnp.mean(yg * yg, axis=-1, keepdims=True) + GATE_NORM_EPS))
    return jnp.concatenate(parts, axis=-1) * gn


def _cumsum_rows(x):
    rows = x.shape[0]
    tril = (lax.broadcasted_iota(jnp.int32, (rows, rows), 0)
            >= lax.broadcasted_iota(jnp.int32, (rows, rows), 1)).astype(BF16)
    h1 = x.astype(BF16)
    r1 = x - h1.astype(F32)
    h2 = r1.astype(BF16)
    h3 = (r1 - h2.astype(F32)).astype(BF16)
    return _dot(tril, h1) + _dot(tril, h2) + _dot(tril, h3)


def _ssd_prompt_kernel(zx_ref, dt_ref, cw_ref, cb_ref, dtb_ref, a_ref, dskip_ref, gn_ref, e_ref,
                       y_ref, nconv_ref, nssm_ref, carry_ref, st_ref):
    c = pl.program_id(1)
    last = pl.num_programs(1) - 1

    @pl.when(c == 0)
    def _():
        carry_ref[...] = jnp.zeros(carry_ref.shape, BF16)
        st_ref[...] = jnp.zeros(st_ref.shape, F32)

    for u in range(zx_ref.shape[0] // SSM_CHUNK):
        rows = pl.ds(u * SSM_CHUNK, SSM_CHUNK)
        _ssd_chunk(zx_ref.at[rows], dt_ref.at[rows], cw_ref, cb_ref, dtb_ref, a_ref, dskip_ref, gn_ref, e_ref,
                   y_ref.at[rows], carry_ref, st_ref)

    @pl.when(c == last)
    def _():
        n_carry = carry_ref.shape[0]
        nconv_ref[...] = carry_ref[...].astype(F32)[n_carry - (SSM_CONV - 1):n_carry]
        nssm_ref[...] = st_ref[...].T


def _ssd_chunk(zx_ref, dt_ref, cw_ref, cb_ref, dtb_ref, a_ref, dskip_ref, gn_ref, e_ref, y_ref, carry_ref, st_ref):
    lc = zx_ref.shape[0]
    n_carry = carry_ref.shape[0]
    src = lax.broadcasted_iota(jnp.int32, (lc, n_carry + lc), 1) - n_carry
    dst = lax.broadcasted_iota(jnp.int32, (lc, n_carry + lc), 0)
    taps = SSM_CONV - 1
    shift = jnp.concatenate([(src == dst - (taps - j)).astype(BF16) for j in range(taps)], axis=0)

    def conv_silu(start, width):
        cols = slice(start, start + width)
        raw = zx_ref[:, SSM_D_INNER + start:SSM_D_INNER + start + width]
        shifted = _dot(shift, jnp.concatenate([carry_ref[:, cols], raw], axis=0))
        conv = cb_ref[:, cols] + shifted[0:lc] * cw_ref[0:1, cols]
        for j in range(1, taps):
            conv = conv + shifted[j * lc:(j + 1) * lc] * cw_ref[j:j + 1, cols]
        conv = conv + raw.astype(F32) * cw_ref[taps:taps + 1, cols]
        return conv * _sigmoid(conv)

    dt = _softplus(dt_ref[...] + dtb_ref[...])
    a_cs = _cumsum_rows(dt * a_ref[...])
    a_log2 = a_cs * LOG2E
    a_log2_t = a_log2.T
    dt_t = dt.T
    a_last = a_cs[lc - 1:lc, :]
    to_end = (dt * jnp.exp(a_last - a_cs)).astype(BF16)
    from_start = jnp.exp(a_cs).astype(BF16)
    chunk_decay = jnp.broadcast_to(jnp.exp(a_last), (8, LANES))
    causal = (lax.broadcasted_iota(jnp.int32, (lc, lc), 0) >= lax.broadcasted_iota(jnp.int32, (lc, lc), 1))
    heads_per_group = SSM_HEADS // SSM_GROUPS
    head_w = SSM_D_INNER // SSM_HEADS
    quad = 4
    quad_w = quad * head_w
    lane_head = lax.broadcasted_iota(jnp.int32, (1, quad_w), 1) // head_w
    n_bc = SSM_GROUPS * SSM_STATE
    for g in range(SSM_GROUPS):
        gw = slice(g * SSM_GROUP_WIDTH, (g + 1) * SSM_GROUP_WIDTH)
        xs = conv_silu(g * SSM_GROUP_WIDTH, SSM_GROUP_WIDTH)
        bm = conv_silu(SSM_D_INNER + g * SSM_STATE, SSM_STATE).astype(BF16)
        cm = conv_silu(SSM_D_INNER + n_bc + g * SSM_STATE, SSM_STATE).astype(BF16)
        e = e_ref[:, gw]
        xs_b = xs.astype(BF16)
        xdte = (xs * _dot(to_end, e)).astype(BF16)
        s_prev = st_ref[:, gw]
        cb = _dot_nt(cm, bm)
        y = _dot(cm, s_prev.astype(BF16)) * _dot(from_start, e)
        st_ref[:, gw] = (s_prev * _dot2(chunk_decay, e)[0:1, :]
                         + lax.dot_general(bm, xdte, TN, preferred_element_type=F32))
        quad_out = []
        for qd in range(heads_per_group // quad):
            h0 = g * heads_per_group + quad * qd
            ms = []
            for h in range(h0, h0 + quad):
                seg = a_log2[:, h:h + 1] - a_log2_t[h:h + 1, :]
                ms.append((cb * jnp.where(causal, jnp.exp2(seg), 0.0) * dt_t[h:h + 1, :]).astype(BF16))
            xq = xs_b[:, qd * quad_w:(qd + 1) * quad_w]
            rhs = jnp.concatenate([jnp.where(lane_head == i, xq, jnp.zeros_like(xq)) for i in range(quad)], axis=0)
            quad_out.append(_dot(jnp.concatenate(ms, axis=1), rhs))
        y = y + jnp.concatenate(quad_out, axis=-1) + dskip_ref[:, gw] * xs
        z = zx_ref[:, gw].astype(F32)
        y = y * (z * _sigmoid(z))
        y = y * lax.rsqrt(jnp.mean(y * y, axis=-1, keepdims=True) + GATE_NORM_EPS) * gn_ref[:, gw]
        y_ref[:, gw] = y.astype(y_ref.dtype)

    carry_ref[...] = zx_ref[lc - n_carry:lc, SSM_D_INNER:SSM_MAIN]


def _ssd_prompt(zx, dt, b, params, *, chunks_per_step):
    m = zx.shape[0]
    rows = chunks_per_step * SSM_CHUNK
    nc = m // b // rows
    full = lambda x: pl.BlockSpec(x.shape, lambda bb, c: (0, 0))
    row = lambda w: pl.BlockSpec((rows, w), lambda bb, c: (bb * nc + c, 0))
    return pl.pallas_call(
        _ssd_prompt_kernel,
        grid=(b, nc),
        in_specs=[row(SSM_MAIN), row(LANES)] + [full(x) for x in params],
        out_specs=[row(SSM_D_INNER),
                   pl.BlockSpec((None, SSM_CONV - 1, SSM_CONV_DIM), lambda bb, c: (bb, 0, 0)),
                   pl.BlockSpec((None, SSM_D_INNER, SSM_STATE), lambda bb, c: (bb, 0, 0))],
        out_shape=[jax.ShapeDtypeStruct((m, SSM_D_INNER), BF16),
                   jax.ShapeDtypeStruct((b, SSM_CONV - 1, SSM_CONV_DIM), F32),
                   jax.ShapeDtypeStruct((b, SSM_D_INNER, SSM_STATE), F32)],
        scratch_shapes=[pltpu.VMEM((16, SSM_CONV_DIM), BF16),
                        pltpu.VMEM((SSM_STATE, SSM_D_INNER), F32)],
        compiler_params=_params(("parallel", "arbitrary")),
        name="ssd_prompt",
    )(zx, dt, *params)


def _ssd_sample_kernel(zx_ref, dt_ref, conv0_ref, s0_ref, cw_ref, cb_ref, dtb_ref, a_ref, dskip_ref, gn_ref,
                       e_ref, eg_ref, y_ref, nconv_ref, nssm_ref, xpad_ref):
    lc = zx_ref.shape[0]
    xpad_ref[0:5, :] = jnp.zeros((5, SSM_CONV_DIM), F32)
    xpad_ref[5:8, :] = conv0_ref[...]
    xbc, tail = _conv_silu(zx_ref, xpad_ref, cw_ref, cb_ref, lc)
    nconv_ref[...] = tail

    xs = xbc[:, 0:SSM_D_INNER]
    bm = xbc[:, SSM_D_INNER:SSM_D_INNER + SSM_GROUPS * SSM_STATE]
    cm = xbc[:, SSM_D_INNER + SSM_GROUPS * SSM_STATE:]
    e = e_ref[...]
    dt = _softplus(dt_ref[...] + dtb_ref[...])
    dta = dt * a_ref[...]
    rows = [dta[0:1, :]]
    for i in range(1, lc):
        rows.append(rows[-1] + dta[i:i + 1, :])
    a_cs = jnp.concatenate(rows, axis=0)
    a_last = rows[-1]
    xdt = xs * _dot2(dt, e)
    xdte = (xdt * _dot2(jnp.exp(a_last - a_cs), e)).astype(BF16)
    ea_e = _dot2(jnp.exp(a_cs), e)
    cd_e = _dot2(jnp.broadcast_to(jnp.exp(a_last), (8, LANES)), e)[0:1, :]

    a_l = jnp.concatenate([a_cs] * lc, axis=0)
    a_s = jnp.concatenate([jnp.broadcast_to(a_cs[s:s + 1, :], (lc, LANES)) for s in range(lc)], axis=0)
    l_idx = lax.broadcasted_iota(jnp.int32, (lc * lc, 1), 0) % lc
    s_idx = lax.broadcasted_iota(jnp.int32, (lc * lc, 1), 0) // lc
    decay = jnp.where(l_idx >= s_idx, jnp.exp(a_l - a_s), 0.0)
    cb_prod = jnp.concatenate([cm * bm[s:s + 1, :] for s in range(lc)], axis=0)
    mix = _dot2(_dot2(cb_prod, eg_ref[...]) * decay, e)
    y = ea_e * 0.0
    for s in range(lc):
        y = y + mix[s * lc:(s + 1) * lc, :] * xdt[s:s + 1, :]

    s_prev = s0_ref[...].T
    s_b = s_prev.astype(BF16)
    bm_b = bm.astype(BF16)
    cm_b = cm.astype(BF16)
    y_off, s_parts = [], []
    for g in range(SSM_GROUPS):
        gs = slice(g * SSM_STATE, (g + 1) * SSM_STATE)
        gw = slice(g * SSM_GROUP_WIDTH, (g + 1) * SSM_GROUP_WIDTH)
        y_off.append(_dot(cm_b[:, gs], s_b[:, gw]))
        s_parts.append(lax.dot_general(bm_b[:, gs], xdte[:, gw], TN, preferred_element_type=F32))
    y = y + jnp.concatenate(y_off, axis=-1) * ea_e + dskip_ref[...] * xs
    s_new = s_prev * cd_e + jnp.concatenate(s_parts, axis=-1)
    nssm_ref[...] = s_new.T
    z = zx_ref[:, 0:SSM_D_INNER]
    y_ref[...] = _gate_norm(y, z, gn_ref[...])


def _ssd_sample_multi_kernel(zx_ref, dt_ref, conv0_ref, s0_ref, *rest, n_seq):
    params, (y_ref, nconv_ref, nssm_ref, xpad_ref) = rest[:-4], rest[-4:]
    t_new = zx_ref.shape[0] // n_seq
    for s in range(n_seq):
        rows = pl.ds(s * t_new, t_new)
        _ssd_sample_kernel(zx_ref.at[rows], dt_ref.at[rows], conv0_ref.at[s], s0_ref.at[s], *params,
                           y_ref.at[rows], nconv_ref.at[s], nssm_ref.at[s], xpad_ref.at[s])


SAMPLE_SSD_SEQS = 8


def _ssd_sample(zx, dt, conv0, s0, params):
    b = conv0.shape[0]
    t_new = zx.shape[0] // b
    n_seq = SAMPLE_SSD_SEQS
    full = lambda x: pl.BlockSpec(x.shape, lambda bb: (0, 0))
    row = lambda w: pl.BlockSpec((n_seq * t_new, w), lambda bb: (bb, 0))
    conv_spec = pl.BlockSpec((n_seq, SSM_CONV - 1, SSM_CONV_DIM), lambda bb: (bb, 0, 0))
    state_spec = pl.BlockSpec((n_seq, SSM_D_INNER, SSM_STATE), lambda bb: (bb, 0, 0))
    return pl.pallas_call(
        functools.partial(_ssd_sample_multi_kernel, n_seq=n_seq),
        grid=(b // n_seq,),
        in_specs=[row(SSM_MAIN), row(LANES), conv_spec, state_spec] + [full(x) for x in params],
        out_specs=[row(SSM_D_INNER), conv_spec, state_spec],
        out_shape=[jax.ShapeDtypeStruct((b * t_new, SSM_D_INNER), F32),
                   jax.ShapeDtypeStruct((b, SSM_CONV - 1, SSM_CONV_DIM), F32),
                   jax.ShapeDtypeStruct((b, SSM_D_INNER, SSM_STATE), F32)],
        scratch_shapes=[pltpu.VMEM((n_seq, 8 + t_new, SSM_CONV_DIM), F32)],
        compiler_params=_params(("parallel",)),
        name="ssd_sample",
    )(zx, dt, conv0, s0, *params)


def _one_hot_expand(n_rows, n_cols, group):
    r = lax.broadcasted_iota(jnp.int32, (n_rows, n_cols), 0)
    c = lax.broadcasted_iota(jnp.int32, (n_rows, n_cols), 1)
    return (c // group == r).astype(BF16)


def _kv_rows(feature_major, b):
    tokens = feature_major.shape[-1]
    x = feature_major.reshape(1, b, 2, N_HEADS, HEAD_DIM, tokens)
    return jnp.transpose(x, (0, 1, 5, 2, 3, 4))


def kernel(x_prompt, x_sample, cache_kv_g0, cache_kv_g1, cache_kv_g2, state_conv, state_ssm, attn_norm, attn_w_in, attn_q_gain, attn_k_gain, attn_w_out, ssm_norm, ssm_w_in, ssm_conv_w, ssm_conv_b, ssm_dt_bias, ssm_A_log, ssm_D, ssm_gate_norm, ssm_w_out):
    b, seq, dm = x_prompt.shape
    sb, st, _ = x_sample.shape
    xp = x_prompt.reshape(b * seq, dm)
    xs = x_sample.reshape(sb * st, dm)

    proj_s, w_in = _proj(xs, attn_norm[0], attn_w_in[0], tm=sb * st, tn=SAMPLE_PROJ_COLS, out_dtype=F32,
                         keep_bf16=True)
    w_out = attn_w_out[0].astype(BF16)
    scale = HEAD_DIM ** -0.5
    q_gain3 = jnp.tile(attn_q_gain[0], (1, N_HEADS)) * scale
    k_gain3 = jnp.tile(attn_k_gain[0], (1, N_HEADS))

    tile_gain = jnp.ones((N_TILES, 1, ATT_WIDTH), F32)
    for g in range(len(DILATIONS)):
        tile_gain = tile_gain.at[Q_TILES[g], 0].set(q_gain3[g] * LOG2E).at[K_TILES[g], 0].set(k_gain3[g])
    qkvg = _attn_proj(xp, attn_norm[0], w_in, tile_gain, tm=PROJ_ROWS, seq=seq)
    part2 = _attn_group(qkvg, 2, n_sub=ATT_SUBS[2], n_res=ATT_RESIDUES[2])
    part1 = _attn_group(qkvg, 1, n_sub=ATT_SUBS[1], n_res=ATT_RESIDUES[1])
    o_gated = _attn_group(qkvg, 0, n_sub=ATT_SUBS[0], n_res=ATT_RESIDUES[0], others=[part1, part2])
    y1p = _outproj(o_gated, w_out, xp, tm=PROJ_ROWS, perm_seq=seq)
    kv_p = []
    for g, d in enumerate(DILATIONS):
        keep = min(d * WINDOW_STEPS, seq)
        block = min(max(keep, 16 * PERM), KV_TAIL_TOKENS)
        kv_p.append(_kv_rows(_kv_tail(qkvg, g, keep=keep, block=block), b))

    caches_t = [jnp.transpose(c[0], (0, 2, 3, 4, 1)).reshape(sb, 2, ATT_WIDTH, c.shape[2])
                for c in (cache_kv_g0, cache_kv_g1, cache_kv_g2)]
    o_s, kn0, kn1, kn2 = _sample_attn(proj_s, caches_t, q_gain3[:, :LANES], k_gain3[:, :LANES])
    y1s = _outproj(o_s, w_out, xs, tm=sb * st)
    kv_s = []
    for g, kn in enumerate((kn0, kn1, kn2)):
        v = proj_s[:, (3 * g + 2) * ATT_WIDTH:(3 * g + 3) * ATT_WIDTH]
        kv_s.append(jnp.stack([kn, v], axis=1).reshape(1, sb, st, 2, N_HEADS, HEAD_DIM))

    w_in2 = ssm_w_in[0].astype(BF16)
    w_dt = jnp.pad(w_in2[:, SSM_MAIN:], ((0, 0), (0, LANES - SSM_HEADS)))
    w_out2 = ssm_w_out[0].astype(BF16)
    pad_h = lambda v: jnp.pad(v.astype(F32), (0, LANES - SSM_HEADS)).reshape(1, LANES)
    e32 = _one_hot_expand(LANES, SSM_D_INNER, SSM_D_INNER // SSM_HEADS)
    gate_gain = ssm_gate_norm[0].reshape(1, -1)
    scan_params = (ssm_conv_w[0], ssm_conv_b[0].reshape(1, -1), pad_h(ssm_dt_bias[0]),
                   pad_h(-jnp.exp(ssm_A_log[0].astype(F32))),
                   jnp.repeat(ssm_D[0].astype(F32), SSM_D_INNER // SSM_HEADS).reshape(1, -1))

    zx_p, dt_p = _proj(y1p, ssm_norm[0], w_in2, tm=PROJ_ROWS, tn=SSM_MAIN // 2, n_out=SSM_MAIN, out_dtype=BF16,
                       w_dt=w_dt)
    yg_p, conv_p, ssm_p = _ssd_prompt(zx_p, dt_p, b, scan_params + (gate_gain, e32), chunks_per_step=SSD_CHUNKS)
    y2p = _outproj(yg_p, w_out2, y1p, tm=PROJ_ROWS)

    eg = (lax.broadcasted_iota(jnp.int32, (SSM_GROUPS * SSM_STATE, LANES), 0) // SSM_STATE
          == lax.broadcasted_iota(jnp.int32, (SSM_GROUPS * SSM_STATE, LANES), 1) // (SSM_HEADS // SSM_GROUPS))
    eg = (eg & (lax.broadcasted_iota(jnp.int32, eg.shape, 1) < SSM_HEADS)).astype(BF16)
    zx_s, dt_s = _proj(y1s, ssm_norm[0], w_in2, tm=sb * st, tn=SSM_MAIN // 2, n_out=SSM_MAIN, out_dtype=F32,
                       w_dt=w_dt)
    yg_s, conv_s, ssm_s = _ssd_sample(zx_s, dt_s, state_conv[0], state_ssm[0].reshape(sb, SSM_D_INNER, SSM_STATE),
                                      scan_params + (gate_gain, e32, eg))
    y2s = _outproj(yg_s, w_out2, y1s, tm=sb * st)

    hp = SSM_D_INNER // SSM_HEADS
    return (y2p.reshape(b, seq, dm), y2s.reshape(sb, st, dm),
            kv_p[0], kv_p[1], kv_p[2], kv_s[0], kv_s[1], kv_s[2],
            conv_p[None], conv_s[None],
            ssm_p.reshape(1, b, SSM_HEADS, hp, SSM_STATE), ssm_s.reshape(1, sb, SSM_HEADS, hp, SSM_STATE))
```

```python
import functools

import jax
import jax.numpy as jnp
from jax import lax
from jax.experimental import pallas as pl
from jax.experimental.pallas import tpu as pltpu

F32 = jnp.float32
BF16 = jnp.bfloat16

D_MODEL = 1024
N_HEADS = 16
HEAD_DIM = 64
ATT_WIDTH = N_HEADS * HEAD_DIM
DILATIONS = (1, 4, 16)
WINDOW_STEPS = 128
ATT_IN = 10 * ATT_WIDTH
PERM = 16

SSM_D_INNER = 2048
SSM_HEADS = 32
SSM_STATE = 128
SSM_GROUPS = 4
SSM_GROUP_WIDTH = SSM_D_INNER // SSM_GROUPS
SSM_CONV = 4
SSM_CONV_DIM = SSM_D_INNER + 2 * SSM_GROUPS * SSM_STATE
SSM_MAIN = SSM_D_INNER + SSM_CONV_DIM
SSM_CHUNK = 128

NORM_EPS = 1e-6
GATE_NORM_EPS = 1e-5
MASKED = -1e30
LOG2E = 1.4426950408889634

LANES = 128
VMEM_LIMIT = 56 * 1024 * 1024

PROJ_ROWS = 1024
SAMPLE_PROJ_COLS = 2048
ATT_SUBS = (8, 8, 2)
ATT_RESIDUES = (1, 1, 4)
SSD_CHUNKS = 4
KV_TAIL_TOKENS = 1024

NT = (((1,), (1,)), ((), ()))
TN = (((0,), (0,)), ((), ()))


def _params(semantics):
    return pltpu.CompilerParams(dimension_semantics=semantics, vmem_limit_bytes=VMEM_LIMIT)


def _rms(x, w, eps):
    return x * lax.rsqrt(jnp.mean(x * x, axis=-1, keepdims=True) + eps) * w


def _sigmoid(x):
    return 1.0 / (1.0 + jnp.exp(-x))


def _split2(v):
    hi = v.astype(BF16)
    lo = (v - hi.astype(F32)).astype(BF16)
    return hi, lo


def _dot(a, b):
    return jnp.dot(a, b, preferred_element_type=F32)


def _dot_nt(a, b):
    return lax.dot_general(a, b, NT, preferred_element_type=F32)


def _dot2(v, e):
    hi, lo = _split2(v)
    return _dot(hi, e) + _dot(lo, e)


def _low_half():
    return lax.broadcasted_iota(jnp.int32, (1, LANES), 1) < HEAD_DIM


def _head_norm(x, eps):
    lo = _low_half()
    parts = []
    for j in range(x.shape[1] // LANES):
        t = x[:, j * LANES:(j + 1) * LANES]
        t2 = t * t
        s_lo = jnp.sum(jnp.where(lo, t2, 0.0), axis=-1, keepdims=True)
        s_hi = jnp.sum(jnp.where(lo, 0.0, t2), axis=-1, keepdims=True)
        r = jnp.where(lo, lax.rsqrt(s_lo * (1.0 / HEAD_DIM) + eps), lax.rsqrt(s_hi * (1.0 / HEAD_DIM) + eps))
        parts.append(t * r)
    return jnp.concatenate(parts, axis=-1)


def _split_heads(qp):
    lo = _low_half()
    zero = jnp.zeros_like(qp)
    return jnp.concatenate([jnp.where(lo, qp, zero), jnp.where(lo, zero, qp)], axis=0)


def _join_heads(x2):
    rows = x2.shape[0] // 2
    return jnp.where(_low_half(), x2[0:rows], x2[rows:])


def _proj_kernel(x_ref, nw_ref, w_ref, *rest, with_dt, keep_bf16):
    rest = list(rest)
    wdt_ref = rest.pop(0) if with_dt else None
    o_ref = rest.pop(0)
    dt_ref = rest.pop(0) if with_dt else None
    wb_ref = rest.pop(0) if keep_bf16 else None
    h_ref = rest.pop(0)

    @pl.when(pl.program_id(1) == 0)
    def _():
        h_ref[...] = _rms(x_ref[...], nw_ref[...], NORM_EPS).astype(BF16)
        if with_dt:
            dt_ref[...] = _dot(h_ref[...], wdt_ref[...])

    w = w_ref[...]
    if keep_bf16:
        w = w.astype(BF16)
        wb_ref[...] = w
    o_ref[...] = _dot(h_ref[...], w).astype(o_ref.dtype)


def _proj(x, norm_w, w, *, tm, tn, out_dtype, n_out=None, w_dt=None, keep_bf16=False):
    m, dm = x.shape
    n = w.shape[1] if n_out is None else n_out
    assert n % tn == 0 and m % tm == 0
    in_specs = [pl.BlockSpec((tm, dm), lambda i, j: (i, 0)),
                pl.BlockSpec((1, dm), lambda i, j: (0, 0)),
                pl.BlockSpec((dm, tn), lambda i, j: (0, j))]
    args = [x, norm_w.reshape(1, dm), w]
    out_shape = [jax.ShapeDtypeStruct((m, n), out_dtype)]
    out_specs = [pl.BlockSpec((tm, tn), lambda i, j: (i, j))]
    if w_dt is not None:
        in_specs.append(pl.BlockSpec((dm, LANES), lambda i, j: (0, 0)))
        args.append(w_dt)
        out_shape.append(jax.ShapeDtypeStruct((m, LANES), F32))
        out_specs.append(pl.BlockSpec((tm, LANES), lambda i, j: (i, 0)))
    if keep_bf16:
        assert m == tm and n == w.shape[1]
        out_shape.append(jax.ShapeDtypeStruct(w.shape, BF16))
        out_specs.append(pl.BlockSpec((dm, tn), lambda i, j: (0, j)))
    outs = pl.pallas_call(
        functools.partial(_proj_kernel, with_dt=w_dt is not None, keep_bf16=keep_bf16),
        grid=(m // tm, n // tn), in_specs=in_specs, out_specs=out_specs, out_shape=out_shape,
        scratch_shapes=[pltpu.VMEM((tm, dm), BF16)],
        compiler_params=_params(("parallel", "arbitrary")),
        name="norm_proj",
    )(*args)
    return outs if len(outs) > 1 else outs[0]


N_TILES = ATT_IN // ATT_WIDTH
Q_TILES, K_TILES, V_TILES, GATE_TILE = (0, 3, 6), (1, 4, 7), (2, 5, 8), 9
STREAM_A = (0, 1, 3, 4, 6)
STREAM_B = (2, 5, 8, 9, 7)


def _lookup(j, table):
    out = jnp.int32(table[0])
    for t, v in enumerate(table[1:], 1):
        out = jnp.where(j == t, jnp.int32(v), out)
    return out


def _attn_proj_kernel(x_ref, nw_ref, wa_ref, wb_ref, ga_ref, gb_ref, oa_ref, ob_ref, h_ref, slab_ref):
    j = pl.program_id(1)

    @pl.when(j == 0)
    def _():
        xn = _rms(x_ref[...], nw_ref[...], NORM_EPS)
        tm = x_ref.shape[0]
        rows = tm // PERM
        step = 4
        assert PERM == step * step
        for c in range(x_ref.shape[1] // LANES):
            cols = slice(c * LANES, (c + 1) * LANES)
            slab_ref[0] = xn[:, cols]
            for ra in range(step):
                slab_ref[1, ra * (tm // step):(ra + 1) * (tm // step), :] = slab_ref[0, pl.ds(ra, tm // step, stride=step), :]
            for ra in range(step):
                for rb in range(step):
                    r = ra + step * rb
                    piece = slab_ref[1, pl.ds(ra * (tm // step) + rb, rows, stride=step), :]
                    h_ref[r * rows:(r + 1) * rows, cols] = piece.astype(BF16)

    def tile(w_ref, gain_ref, o_ref, normed):
        res = _dot(h_ref[...], w_ref[...])
        if normed:
            res = _head_norm(res, NORM_EPS) * gain_ref[0]
        o_ref[...] = res.astype(o_ref.dtype).reshape(o_ref.shape)

    last = pl.num_programs(1) - 1

    @pl.when(j < last)
    def _():
        tile(wa_ref, ga_ref, oa_ref, True)
        tile(wb_ref, gb_ref, ob_ref, False)

    @pl.when(j == last)
    def _():
        tile(wa_ref, ga_ref, oa_ref, True)
        tile(wb_ref, gb_ref, ob_ref, True)


def _attn_proj(x, norm_w, w, tile_gain, *, tm, seq):
    m, dm = x.shape
    per_b = seq // tm
    rows = tm // PERM
    n_steps = len(STREAM_A)
    w_spec = lambda table: pl.BlockSpec((dm, ATT_WIDTH), lambda i, j: (0, _lookup(j, table)))
    g_spec = lambda table: pl.BlockSpec((1, 1, ATT_WIDTH), lambda i, j: (_lookup(j, table), 0, 0))
    out_spec = pl.BlockSpec((None, PERM, rows, ATT_WIDTH), lambda i, j: (i // per_b, 0, i % per_b, j))
    out_sds = jax.ShapeDtypeStruct((m // seq, PERM, seq // PERM, n_steps * ATT_WIDTH), BF16)
    return pl.pallas_call(
        _attn_proj_kernel,
        grid=(m // tm, n_steps),
        in_specs=[pl.BlockSpec((tm, dm), lambda i, j: (i, 0)), pl.BlockSpec((1, dm), lambda i, j: (0, 0)),
                  w_spec(STREAM_A), w_spec(STREAM_B), g_spec(STREAM_A), g_spec(STREAM_B)],
        out_specs=[out_spec, out_spec], out_shape=[out_sds, out_sds],
        scratch_shapes=[pltpu.VMEM((tm, dm), BF16), pltpu.VMEM((2, tm, LANES), F32)],
        compiler_params=_params(("parallel", "arbitrary")),
        name="attn_proj",
    )(x, norm_w.reshape(1, dm), w, w, tile_gain, tile_gain)


def _rows_in_order(slab_ref, x):
    n = x.shape[0]
    run = n // PERM
    step = 4
    assert PERM == step * step
    quarter = n // step
    for ra in range(step):
        for rb in range(step):
            r = ra + step * rb
            slab_ref[0, pl.ds(ra * quarter + rb, run, stride=step), :] = x[r * run:(r + 1) * run]
    for ra in range(step):
        slab_ref[1, pl.ds(ra, quarter, stride=step), :] = slab_ref[0, ra * quarter:(ra + 1) * quarter, :]
    return slab_ref[1, 0:n, :]


def _outproj_kernel(a_ref, w_ref, x_ref, o_ref, *scratch, unperm):
    tm = o_ref.shape[0]
    a = a_ref[...].reshape(tm, a_ref.shape[-1]).astype(BF16)
    res = _dot(a, w_ref[...])
    if unperm:
        slab_ref, = scratch
        for c in range(o_ref.shape[1] // LANES):
            cols = slice(c * LANES, (c + 1) * LANES)
            o_ref[:, cols] = x_ref[:, cols] + _rows_in_order(slab_ref, res[:, cols])
    else:
        o_ref[...] = x_ref[...] + res


def _outproj(a, w, x, *, tm, perm_seq=None):
    m, n = x.shape
    k = w.shape[0]
    if perm_seq is not None:
        per_b = perm_seq // tm
        a_spec = pl.BlockSpec((None, PERM, tm // PERM, k), lambda i: (i // per_b, 0, i % per_b, 0))
    else:
        a_spec = pl.BlockSpec((tm, k), lambda i: (i, 0))
    return pl.pallas_call(
        functools.partial(_outproj_kernel, unperm=perm_seq is not None),
        grid=(m // tm,),
        in_specs=[a_spec, pl.BlockSpec((k, n), lambda i: (0, 0)), pl.BlockSpec((tm, n), lambda i: (i, 0))],
        out_specs=pl.BlockSpec((tm, n), lambda i: (i, 0)),
        out_shape=jax.ShapeDtypeStruct((m, n), F32),
        scratch_shapes=[pltpu.VMEM((2, tm, LANES), F32)] if perm_seq else [],
        compiler_params=_params(("parallel",)),
        name="out_proj",
    )(a, w, x)


ATT_SUB = 128


def _attn_group_kernel(*refs, n_chunks, n_sub, merge):
    it = iter(refs)
    q_ref, k_ref, v_ref = next(it), next(it), next(it)
    if merge:
        gate_ref = next(it)
        others = [(next(it), next(it)) for _ in range(len(DILATIONS) - 1)]
        o_out = next(it)
    else:
        o_out, lse_out = next(it), next(it)
    ks_ref, vs_ref = next(it), next(it)

    rc = q_ref.shape[1] // n_sub
    assert n_chunks * rc == ATT_SUB
    n = pl.program_id(2)
    narrow = rc % 16 != 0

    def piece(ref, u, cols=slice(None)):
        if narrow and ref.dtype == BF16:
            pair = 2 * (u // 2) * rc
            x = ref[:, pair:pair + 2 * rc, cols].astype(F32)[:, (u % 2) * rc:(u % 2 + 1) * rc, :]
        else:
            x = ref[:, u * rc:(u + 1) * rc, cols]
        return x.reshape(ATT_SUB, x.shape[-1])

    def unsubs(parts, dtype):
        parts = [p.reshape(n_chunks, rc, p.shape[-1]) for p in parts]
        if narrow:
            return jnp.concatenate(parts, axis=1).astype(dtype)
        return jnp.concatenate([p.astype(dtype) for p in parts], axis=1)

    @pl.when(n == 0)
    def _():
        ks_ref[0:ATT_SUB, :] = jnp.zeros((ATT_SUB, ATT_WIDTH), BF16)
        vs_ref[0:ATT_SUB, :] = jnp.zeros((ATT_SUB, ATT_WIDTH), BF16)

    for u in range(n_sub):
        ks_ref[(u + 1) * ATT_SUB:(u + 2) * ATT_SUB, :] = piece(k_ref, u).astype(BF16)
        vs_ref[(u + 1) * ATT_SUB:(u + 2) * ATT_SUB, :] = piece(v_ref, u).astype(BF16)

    qi = lax.broadcasted_iota(jnp.int32, (ATT_SUB, 2 * ATT_SUB), 0)
    kj = lax.broadcasted_iota(jnp.int32, (ATT_SUB, 2 * ATT_SUB), 1)
    kc = jnp.where(kj >= ATT_SUB, kj - ATT_SUB, kj)
    tq = n_chunks * (qi % rc) + qi // rc
    tk = n_chunks * (kc % rc) + kc // rc + jnp.where(kj >= ATT_SUB, 0, -ATT_SUB)
    dist = tq - tk
    in_band = (dist >= 0) & (dist <= WINDOW_STEPS)
    band = jnp.where(in_band, 0.0, MASKED)
    band_first = jnp.where(in_band & ((kj >= ATT_SUB) | (n > 0)), 0.0, MASKED)
    bias = (band_first, band)
    ones = jnp.ones((2 * ATT_SUB, LANES), BF16)
    lo = _low_half()

    for j in range(N_HEADS // 2):
        cols = slice(j * LANES, (j + 1) * LANES)
        o_parts, lse_parts = [], []
        for u in range(n_sub):
            keys = slice(u * ATT_SUB, (u + 2) * ATT_SUB)
            qp = piece(q_ref, u, cols).astype(BF16)
            v1 = jnp.concatenate([vs_ref[keys, cols], ones], axis=1)
            o_h, lse_h = [], []
            for first_head in (True, False):
                qm = jnp.where(lo == first_head, qp, jnp.zeros_like(qp))
                s = _dot_nt(qm, ks_ref[keys, cols]) + bias[min(u, 1)]
                m = jnp.max(s, axis=-1, keepdims=True)
                pv = _dot(jnp.exp2(s - m).astype(BF16), v1)
                l = pv[:, LANES:]
                o_h.append(pv[:, :LANES] / l)
                lse_h.append(m + jnp.log2(l))
            o = jnp.where(lo, o_h[0], o_h[1])
            lse = jnp.where(lo, lse_h[0], lse_h[1])
            if merge:
                o_g = [o] + [piece(o_ref, u, cols).astype(F32) for o_ref, _ in others]
                lse_g = [lse] + [piece(l_ref, u, cols) for _, l_ref in others]
                top = jnp.maximum(jnp.maximum(lse_g[0], lse_g[1]), lse_g[2])
                w = [jnp.exp2(x - top) for x in lse_g]
                o = (w[0] * o_g[0] + w[1] * o_g[1] + w[2] * o_g[2]) / (w[0] + w[1] + w[2])
                g = piece(gate_ref, u, cols).astype(F32)
                o = o * (g * _sigmoid(g))
            o_parts.append(o)
            lse_parts.append(lse)
        o_out[:, :, cols] = unsubs(o_parts, BF16)
        if not merge:
            lse_out[:, :, cols] = unsubs(lse_parts, F32)
    ks_ref[0:ATT_SUB, :] = ks_ref[n_sub * ATT_SUB:(n_sub + 1) * ATT_SUB, :]
    vs_ref[0:ATT_SUB, :] = vs_ref[n_sub * ATT_SUB:(n_sub + 1) * ATT_SUB, :]


def _attn_residues_kernel(*refs, n_res, **kwargs):
    blocks, scratch = refs[:-2], refs[-2:]
    for rr in range(n_res):
        _attn_group_kernel(*[r.at[:, rr] for r in blocks], *scratch, **kwargs)


def _attn_group(qkvg, g, *, n_sub, n_res=1, others=None):
    b, _, t, _ = qkvg[0].shape
    d = DILATIONS[g]
    n_chunks = PERM // d
    rc = ATT_SUB // n_chunks * n_sub
    nb = t // rc
    assert n_res == 1 or nb == 1
    merge = others is not None
    view = lambda a: a.reshape(b, n_chunks, d, t, a.shape[-1])

    def spec(col=0):
        res_dim = None if n_res == 1 else n_res
        return pl.BlockSpec((None, n_chunks, res_dim, rc, ATT_WIDTH), lambda bb, r, n: (bb, 0, r, n, col))

    in_specs, args = [], []

    def add_tile(tile):
        stream, table = (0, STREAM_A) if tile in STREAM_A else (1, STREAM_B)
        in_specs.append(spec(table.index(tile)))
        args.append(view(qkvg[stream]))

    for tile in (Q_TILES[g], K_TILES[g], V_TILES[g]):
        add_tile(tile)
    full = jax.ShapeDtypeStruct((b, n_chunks, d, t, ATT_WIDTH), BF16)
    if merge:
        add_tile(GATE_TILE)
        for o, lse in others:
            in_specs += [spec(), spec()]
            args += [view(o), view(lse)]
        out_shape, out_specs = [full], [spec()]
    else:
        out_shape = [full, jax.ShapeDtypeStruct(full.shape, F32)]
        out_specs = [spec(), spec()]
    body = functools.partial(_attn_group_kernel, n_chunks=n_chunks, n_sub=n_sub, merge=merge)
    if n_res > 1:
        body = functools.partial(_attn_residues_kernel, n_res=n_res, n_chunks=n_chunks, n_sub=n_sub, merge=merge)
    outs = pl.pallas_call(
        body,
        grid=(b, d // n_res, nb), in_specs=in_specs, out_specs=out_specs, out_shape=out_shape,
        scratch_shapes=[pltpu.VMEM(((n_sub + 1) * ATT_SUB, ATT_WIDTH), BF16)] * 2,
        compiler_params=_params(("parallel", "parallel", "arbitrary")),
        name=f"attn_group{g}",
    )(*args)
    outs = [a.reshape(b, PERM, t, ATT_WIDTH) for a in outs]
    return outs[0] if merge else tuple(outs)


def _kv_tail_kernel(k_ref, v_ref, o_ref, slab_ref):
    n = PERM * k_ref.shape[1]
    t_out = o_ref.shape[1]
    for half, ref in enumerate((k_ref, v_ref)):
        for c in range(ATT_WIDTH // LANES):
            x = ref[:, :, c * LANES:(c + 1) * LANES].astype(F32).reshape(n, LANES)
            first = half * ATT_WIDTH + c * LANES
            o_ref[first:first + LANES, :] = _rows_in_order(slab_ref, x).T[:, n - t_out:]


def _kv_tail(qkvg, g, *, keep, block):
    b, _, t, _ = qkvg[0].shape
    run = block // PERM
    t_out = min(block, keep)
    first = (t * PERM - max(keep, block)) // block

    def tile(idx):
        stream, table = (0, STREAM_A) if idx in STREAM_A else (1, STREAM_B)
        col = table.index(idx)
        return qkvg[stream], pl.BlockSpec((None, PERM, run, ATT_WIDTH), lambda bb, i: (bb, 0, first + i, col))

    (k_arr, k_spec), (v_arr, v_spec) = tile(K_TILES[g]), tile(V_TILES[g])
    return pl.pallas_call(
        _kv_tail_kernel,
        grid=(b, keep // t_out),
        in_specs=[k_spec, v_spec],
        out_specs=pl.BlockSpec((None, 2 * ATT_WIDTH, t_out), lambda bb, i: (bb, 0, i)),
        out_shape=jax.ShapeDtypeStruct((b, 2 * ATT_WIDTH, keep), F32),
        scratch_shapes=[pltpu.VMEM((2, block, LANES), F32)],
        compiler_params=_params(("parallel", "parallel")),
        name="kv_tail",
    )(k_arr, v_arr)


SAMPLE_HEADS = 8


def _sample_attn_kernel(*refs):
    n_g = len(DILATIONS)
    qkv = [refs[3 * g:3 * g + 3] for g in range(n_g)]
    gate_ref = refs[3 * n_g]
    c_refs = refs[3 * n_g + 1:4 * n_g + 1]
    qg_ref, kg_ref = refs[4 * n_g + 1:4 * n_g + 3]
    o_ref = refs[4 * n_g + 3]
    kn_refs = refs[4 * n_g + 4:]
    t_new = o_ref.shape[0]
    t_row = lax.broadcasted_iota(jnp.int32, (2 * t_new, 1), 0) % t_new
    t_col = lax.broadcasted_iota(jnp.int32, (1, t_new), 1)

    valid_c, valid_n = [], []
    for g, d in enumerate(DILATIONS):
        assert d & (d - 1) == 0
        length = c_refs[g].shape[2]
        back = length + t_row - lax.broadcasted_iota(jnp.int32, (1, length), 1)
        valid_c.append(((back & (d - 1)) == 0) & (back <= WINDOW_STEPS * d))
        back = t_row - t_col
        valid_n.append((back >= 0) & ((back & (d - 1)) == 0) & (back <= WINDOW_STEPS * d))
    ones_rows = jnp.ones((LANES, c_refs[-1].shape[2]), BF16)
    ones_new = jnp.ones((t_new, LANES), BF16)

    o_cols = []
    kn_cols = [[] for _ in DILATIONS]
    for j in range(SAMPLE_HEADS // 2):
        cols = slice(j * LANES, (j + 1) * LANES)
        o_g, m_g, l_g = [], [], []
        for g, d in enumerate(DILATIONS):
            q_ref, k_ref, v_ref = qkv[g]
            qm = _split_heads(_head_norm(q_ref[:, cols], NORM_EPS) * qg_ref[g:g + 1, :]).astype(BF16)
            kn = _head_norm(k_ref[:, cols], NORM_EPS) * kg_ref[g:g + 1, :]
            kn_cols[g].append(kn)
            length = c_refs[g].shape[2]
            s_c = jnp.where(valid_c[g], _dot(qm, c_refs[g][0, cols, :].astype(BF16)), MASKED)
            s_n = jnp.where(valid_n[g], _dot_nt(qm, kn.astype(BF16)), MASKED)
            m = jnp.maximum(jnp.max(s_c, axis=-1, keepdims=True), jnp.max(s_n, axis=-1, keepdims=True))
            vt = jnp.concatenate([c_refs[g][1, cols, :].astype(BF16), ones_rows[:, 0:length]], axis=0)
            vn = jnp.concatenate([v_ref[:, cols].astype(BF16), ones_new], axis=1)
            pv = (_dot_nt(jnp.exp(s_c - m).astype(BF16), vt)
                  + _dot(jnp.exp(s_n - m).astype(BF16), vn))
            o_g.append(pv[:, :LANES])
            l_g.append(pv[:, LANES:])
            m_g.append(m)
        top = jnp.maximum(jnp.maximum(m_g[0], m_g[1]), m_g[2])
        f = [jnp.exp(m - top) for m in m_g]
        den = f[0] * l_g[0] + f[1] * l_g[1] + f[2] * l_g[2]
        o = _join_heads((f[0] * o_g[0] + f[1] * o_g[1] + f[2] * o_g[2]) / den)
        gate = gate_ref[:, cols]
        o_cols.append(o * (gate * _sigmoid(gate)))
    o_ref[...] = jnp.concatenate(o_cols, axis=-1)
    for g in range(n_g):
        kn_refs[g][...] = jnp.concatenate(kn_cols[g], axis=-1)


def _sample_attn(proj, caches_t, q_gain, k_gain):
    b = caches_t[0].shape[0]
    t_new = proj.shape[0] // b
    width = SAMPLE_HEADS * HEAD_DIM
    per_chunk = ATT_WIDTH // width
    col = lambda c: pl.BlockSpec((t_new, width), lambda bb, hc: (bb, c * per_chunk + hc))
    in_specs, args = [], []
    for g in range(len(DILATIONS)):
        in_specs += [col(3 * g), col(3 * g + 1), col(3 * g + 2)]
        args += [proj, proj, proj]
    in_specs.append(col(ATT_IN // ATT_WIDTH - 1))
    args.append(proj)
    for c in caches_t:
        in_specs.append(pl.BlockSpec((None, 2, width, c.shape[3]), lambda bb, hc: (bb, 0, hc, 0)))
        args.append(c)
    gain_spec = pl.BlockSpec((len(DILATIONS), LANES), lambda bb, hc: (0, 0))
    out_spec = pl.BlockSpec((t_new, width), lambda bb, hc: (bb, hc))
    out_sds = jax.ShapeDtypeStruct((b * t_new, ATT_WIDTH), F32)
    return pl.pallas_call(
        _sample_attn_kernel,
        grid=(b, per_chunk),
        in_specs=in_specs + [gain_spec, gain_spec],
        out_specs=[out_spec] * 4,
        out_shape=[out_sds] * 4,
        compiler_params=_params(("parallel", "parallel")),
        name="sample_attn",
    )(*args, q_gain, k_gain)


def _softplus(x):
    return jnp.maximum(x, 0.0) + jnp.log(1.0 + jnp.exp(-jnp.abs(x)))


def _conv_silu(zx_ref, xpad_ref, cw_ref, cb_ref, lc):
    pad = 8
    xpad_ref[pad:pad + lc, :] = zx_ref[:, SSM_D_INNER:SSM_MAIN].astype(F32)
    conv = cb_ref[...] + xpad_ref[pad - 3:pad - 3 + lc, :] * cw_ref[0:1, :]
    for j in range(1, SSM_CONV):
        conv = conv + xpad_ref[pad - 3 + j:pad - 3 + j + lc, :] * cw_ref[j:j + 1, :]
    tail = xpad_ref[pad + lc - 3:pad + lc, :]
    xpad_ref[pad - 3:pad, :] = tail
    return conv * _sigmoid(conv), tail


def _gate_norm(y, z, gn):
    y = y * (z * _sigmoid(z))
    parts = []
    for g in range(SSM_GROUPS):
        yg = y[:, g * SSM_GROUP_WIDTH:(g + 1) * SSM_GROUP_WIDTH]
        parts.append(yg * lax.rsqrt(jnp.mean(yg * yg, axis=-1, keepdims=True) + GATE_NORM_EPS))
    return jnp.concatenate(parts, axis=-1) * gn


def _cumsum_rows(x):
    rows = x.shape[0]
    tril = (lax.broadcasted_iota(jnp.int32, (rows, rows), 0)
            >= lax.broadcasted_iota(jnp.int32, (rows, rows), 1)).astype(BF16)
    h1 = x.astype(BF16)
    r1 = x - h1.astype(F32)
    h2 = r1.astype(BF16)
    h3 = (r1 - h2.astype(F32)).astype(BF16)
    return _dot(tril, h1) + _dot(tril, h2) + _dot(tril, h3)


def _ssd_prompt_kernel(zx_ref, dt_ref, cw_ref, cb_ref, dtb_ref, a_ref, dskip_ref, gn_ref, e_ref,
                       y_ref, nconv_ref, nssm_ref, carry_ref, st_ref):
    c = pl.program_id(1)
    last = pl.num_programs(1) - 1

    @pl.when(c == 0)
    def _():
        carry_ref[...] = jnp.zeros(carry_ref.shape, BF16)
        st_ref[...] = jnp.zeros(st_ref.shape, F32)

    for u in range(zx_ref.shape[0] // SSM_CHUNK):
        rows = pl.ds(u * SSM_CHUNK, SSM_CHUNK)
        _ssd_chunk(zx_ref.at[rows], dt_ref.at[rows], cw_ref, cb_ref, dtb_ref, a_ref, dskip_ref, gn_ref, e_ref,
                   y_ref.at[rows], carry_ref, st_ref)

    @pl.when(c == last)
    def _():
        n_carry = carry_ref.shape[0]
        nconv_ref[...] = carry_ref[...].astype(F32)[n_carry - (SSM_CONV - 1):n_carry]
        nssm_ref[...] = st_ref[...].T


def _ssd_chunk(zx_ref, dt_ref, cw_ref, cb_ref, dtb_ref, a_ref, dskip_ref, gn_ref, e_ref, y_ref, carry_ref, st_ref):
    lc = zx_ref.shape[0]
    n_carry = carry_ref.shape[0]
    src = lax.broadcasted_iota(jnp.int32, (lc, n_carry + lc), 1) - n_carry
    dst = lax.broadcasted_iota(jnp.int32, (lc, n_carry + lc), 0)
    taps = SSM_CONV - 1
    shift = jnp.concatenate([(src == dst - (taps - j)).astype(BF16) for j in range(taps)], axis=0)

    def conv_silu(start, width):
        cols = slice(start, start + width)
        raw = zx_ref[:, SSM_D_INNER + start:SSM_D_INNER + start + width]
        shifted = _dot(shift, jnp.concatenate([carry_ref[:, cols], raw], axis=0))
        conv = cb_ref[:, cols] + shifted[0:lc] * cw_ref[0:1, cols]
        for j in range(1, taps):
            conv = conv + shifted[j * lc:(j + 1) * lc] * cw_ref[j:j + 1, cols]
        conv = conv + raw.astype(F32) * cw_ref[taps:taps + 1, cols]
        return conv * _sigmoid(conv)

    dt = _softplus(dt_ref[...] + dtb_ref[...])
    a_cs = _cumsum_rows(dt * a_ref[...])
    a_log2 = a_cs * LOG2E
    a_log2_t = a_log2.T
    dt_t = dt.T
    a_last = a_cs[lc - 1:lc, :]
    to_end = (dt * jnp.exp(a_last - a_cs)).astype(BF16)
    from_start = jnp.exp(a_cs).astype(BF16)
    chunk_decay = jnp.broadcast_to(jnp.exp(a_last), (8, LANES))
    causal = (lax.broadcasted_iota(jnp.int32, (lc, lc), 0) >= lax.broadcasted_iota(jnp.int32, (lc, lc), 1))
    heads_per_group = SSM_HEADS // SSM_GROUPS
    head_w = SSM_D_INNER // SSM_HEADS
    quad = 4
    quad_w = quad * head_w
    lane_head = lax.broadcasted_iota(jnp.int32, (1, quad_w), 1) // head_w
    n_bc = SSM_GROUPS * SSM_STATE
    for g in range(SSM_GROUPS):
        gw = slice(g * SSM_GROUP_WIDTH, (g + 1) * SSM_GROUP_WIDTH)
        xs = conv_silu(g * SSM_GROUP_WIDTH, SSM_GROUP_WIDTH)
        bm = conv_silu(SSM_D_INNER + g * SSM_STATE, SSM_STATE).astype(BF16)
        cm = conv_silu(SSM_D_INNER + n_bc + g * SSM_STATE, SSM_STATE).astype(BF16)
        e = e_ref[:, gw]
        xs_b = xs.astype(BF16)
        xdte = (xs * _dot(to_end, e)).astype(BF16)
        s_prev = st_ref[:, gw]
        cb = _dot_nt(cm, bm)
        y = _dot(cm, s_prev.astype(BF16)) * _dot(from_start, e)
        st_ref[:, gw] = (s_prev * _dot2(chunk_decay, e)[0:1, :]
                         + lax.dot_general(bm, xdte, TN, preferred_element_type=F32))
        quad_out = []
        for qd in range(heads_per_group // quad):
            h0 = g * heads_per_group + quad * qd
            ms = []
            for h in range(h0, h0 + quad):
                seg = a_log2[:, h:h + 1] - a_log2_t[h:h + 1, :]
                ms.append((cb * jnp.where(causal, jnp.exp2(seg), 0.0) * dt_t[h:h + 1, :]).astype(BF16))
            xq = xs_b[:, qd * quad_w:(qd + 1) * quad_w]
            rhs = jnp.concatenate([jnp.where(lane_head == i, xq, jnp.zeros_like(xq)) for i in range(quad)], axis=0)
            quad_out.append(_dot(jnp.concatenate(ms, axis=1), rhs))
        y = y + jnp.concatenate(quad_out, axis=-1) + dskip_ref[:, gw] * xs
        z = zx_ref[:, gw].astype(F32)
        y = y * (z * _sigmoid(z))
        y = y * lax.rsqrt(jnp.mean(y * y, axis=-1, keepdims=True) + GATE_NORM_EPS) * gn_ref[:, gw]
        y_ref[:, gw] = y.astype(y_ref.dtype)

    carry_ref[...] = zx_ref[lc - n_carry:lc, SSM_D_INNER:SSM_MAIN]


def _ssd_prompt(zx, dt, b, params, *, chunks_per_step):
    m = zx.shape[0]
    rows = chunks_per_step * SSM_CHUNK
    nc = m // b // rows
    full = lambda x: pl.BlockSpec(x.shape, lambda bb, c: (0, 0))
    row = lambda w: pl.BlockSpec((rows, w), lambda bb, c: (bb * nc + c, 0))
    return pl.pallas_call(
        _ssd_prompt_kernel,
        grid=(b, nc),
        in_specs=[row(SSM_MAIN), row(LANES)] + [full(x) for x in params],
        out_specs=[row(SSM_D_INNER),
                   pl.BlockSpec((None, SSM_CONV - 1, SSM_CONV_DIM), lambda bb, c: (bb, 0, 0)),
                   pl.BlockSpec((None, SSM_D_INNER, SSM_STATE), lambda bb, c: (bb, 0, 0))],
        out_shape=[jax.ShapeDtypeStruct((m, SSM_D_INNER), BF16),
                   jax.ShapeDtypeStruct((b, SSM_CONV - 1, SSM_CONV_DIM), F32),
                   jax.ShapeDtypeStruct((b, SSM_D_INNER, SSM_STATE), F32)],
        scratch_shapes=[pltpu.VMEM((16, SSM_CONV_DIM), BF16),
                        pltpu.VMEM((SSM_STATE, SSM_D_INNER), F32)],
        compiler_params=_params(("parallel", "arbitrary")),
        name="ssd_prompt",
    )(zx, dt, *params)


def _ssd_sample_kernel(zx_ref, dt_ref, conv0_ref, s0_ref, cw_ref, cb_ref, dtb_ref, a_ref, dskip_ref, gn_ref,
                       e_ref, eg_ref, y_ref, nconv_ref, nssm_ref, xpad_ref):
    lc = zx_ref.shape[0]
    xpad_ref[0:5, :] = jnp.zeros((5, SSM_CONV_DIM), F32)
    xpad_ref[5:8, :] = conv0_ref[...]
    xbc, tail = _conv_silu(zx_ref, xpad_ref, cw_ref, cb_ref, lc)
    nconv_ref[...] = tail

    xs = xbc[:, 0:SSM_D_INNER]
    bm = xbc[:, SSM_D_INNER:SSM_D_INNER + SSM_GROUPS * SSM_STATE]
    cm = xbc[:, SSM_D_INNER + SSM_GROUPS * SSM_STATE:]
    e = e_ref[...]
    dt = _softplus(dt_ref[...] + dtb_ref[...])
    dta = dt * a_ref[...]
    rows = [dta[0:1, :]]
    for i in range(1, lc):
        rows.append(rows[-1] + dta[i:i + 1, :])
    a_cs = jnp.concatenate(rows, axis=0)
    a_last = rows[-1]
    xdt = xs * _dot2(dt, e)
    xdte = (xdt * _dot2(jnp.exp(a_last - a_cs), e)).astype(BF16)
    ea_e = _dot2(jnp.exp(a_cs), e)
    cd_e = _dot2(jnp.broadcast_to(jnp.exp(a_last), (8, LANES)), e)[0:1, :]

    a_l = jnp.concatenate([a_cs] * lc, axis=0)
    a_s = jnp.concatenate([jnp.broadcast_to(a_cs[s:s + 1, :], (lc, LANES)) for s in range(lc)], axis=0)
    l_idx = lax.broadcasted_iota(jnp.int32, (lc * lc, 1), 0) % lc
    s_idx = lax.broadcasted_iota(jnp.int32, (lc * lc, 1), 0) // lc
    decay = jnp.where(l_idx >= s_idx, jnp.exp(a_l - a_s), 0.0)
    cb_prod = jnp.concatenate([cm * bm[s:s + 1, :] for s in range(lc)], axis=0)
    mix = _dot2(_dot2(cb_prod, eg_ref[...]) * decay, e)
    y = ea_e * 0.0
    for s in range(lc):
        y = y + mix[s * lc:(s + 1) * lc, :] * xdt[s:s + 1, :]

    s_prev = s0_ref[...].T
    s_b = s_prev.astype(BF16)
    bm_b = bm.astype(BF16)
    cm_b = cm.astype(BF16)
    y_off, s_parts = [], []
    for g in range(SSM_GROUPS):
        gs = slice(g * SSM_STATE, (g + 1) * SSM_STATE)
        gw = slice(g * SSM_GROUP_WIDTH, (g + 1) * SSM_GROUP_WIDTH)
        y_off.append(_dot(cm_b[:, gs], s_b[:, gw]))
        s_parts.append(lax.dot_general(bm_b[:, gs], xdte[:, gw], TN, preferred_element_type=F32))
    y = y + jnp.concatenate(y_off, axis=-1) * ea_e + dskip_ref[...] * xs
    s_new = s_prev * cd_e + jnp.concatenate(s_parts, axis=-1)
    nssm_ref[...] = s_new.T
    z = zx_ref[:, 0:SSM_D_INNER]
    y_ref[...] = _gate_norm(y, z, gn_ref[...])


def _ssd_sample_multi_kernel(zx_ref, dt_ref, conv0_ref, s0_ref, *rest, n_seq):
    params, (y_ref, nconv_ref, nssm_ref, xpad_ref) = rest[:-4], rest[-4:]
    t_new = zx_ref.shape[0] // n_seq
    for s in range(n_seq):
        rows = pl.ds(s * t_new, t_new)
        _ssd_sample_kernel(zx_ref.at[rows], dt_ref.at[rows], conv0_ref.at[s], s0_ref.at[s], *params,
                           y_ref.at[rows], nconv_ref.at[s], nssm_ref.at[s], xpad_ref.at[s])


SAMPLE_SSD_SEQS = 4


def _ssd_sample(zx, dt, conv0, s0, params):
    b = conv0.shape[0]
    t_new = zx.shape[0] // b
    n_seq = SAMPLE_SSD_SEQS
    full = lambda x: pl.BlockSpec(x.shape, lambda bb: (0, 0))
    row = lambda w: pl.BlockSpec((n_seq * t_new, w), lambda bb: (bb, 0))
    conv_spec = pl.BlockSpec((n_seq, SSM_CONV - 1, SSM_CONV_DIM), lambda bb: (bb, 0, 0))
    state_spec = pl.BlockSpec((n_seq, SSM_D_INNER, SSM_STATE), lambda bb: (bb, 0, 0))
    return pl.pallas_call(
        functools.partial(_ssd_sample_multi_kernel, n_seq=n_seq),
        grid=(b // n_seq,),
        in_specs=[row(SSM_MAIN), row(LANES), conv_spec, state_spec] + [full(x) for x in params],
        out_specs=[row(SSM_D_INNER), conv_spec, state_spec],
        out_shape=[jax.ShapeDtypeStruct((b * t_new, SSM_D_INNER), F32),
                   jax.ShapeDtypeStruct((b, SSM_CONV - 1, SSM_CONV_DIM), F32),
                   jax.ShapeDtypeStruct((b, SSM_D_INNER, SSM_STATE), F32)],
        scratch_shapes=[pltpu.VMEM((n_seq, 8 + t_new, SSM_CONV_DIM), F32)],
        compiler_params=_params(("parallel",)),
        name="ssd_sample",
    )(zx, dt, conv0, s0, *params)


def _one_hot_expand(n_rows, n_cols, group):
    r = lax.broadcasted_iota(jnp.int32, (n_rows, n_cols), 0)
    c = lax.broadcasted_iota(jnp.int32, (n_rows, n_cols), 1)
    return (c // group == r).astype(BF16)


def _kv_rows(feature_major, b):
    tokens = feature_major.shape[-1]
    x = feature_major.reshape(1, b, 2, N_HEADS, HEAD_DIM, tokens)
    return jnp.transpose(x, (0, 1, 5, 2, 3, 4))


def kernel(x_prompt, x_sample, cache_kv_g0, cache_kv_g1, cache_kv_g2, state_conv, state_ssm, attn_norm, attn_w_in, attn_q_gain, attn_k_gain, attn_w_out, ssm_norm, ssm_w_in, ssm_conv_w, ssm_conv_b, ssm_dt_bias, ssm_A_log, ssm_D, ssm_gate_norm, ssm_w_out):
    b, seq, dm = x_prompt.shape
    sb, st, _ = x_sample.shape
    xp = x_prompt.reshape(b * seq, dm)
    xs = x_sample.reshape(sb * st, dm)

    proj_s, w_in = _proj(xs, attn_norm[0], attn_w_in[0], tm=sb * st, tn=SAMPLE_PROJ_COLS, out_dtype=F32,
                         keep_bf16=True)
    w_out = attn_w_out[0].astype(BF16)
    scale = HEAD_DIM ** -0.5
    q_gain3 = jnp.tile(attn_q_gain[0], (1, N_HEADS)) * scale
    k_gain3 = jnp.tile(attn_k_gain[0], (1, N_HEADS))

    tile_gain = jnp.ones((N_TILES, 1, ATT_WIDTH), F32)
    for g in range(len(DILATIONS)):
        tile_gain = tile_gain.at[Q_TILES[g], 0].set(q_gain3[g] * LOG2E).at[K_TILES[g], 0].set(k_gain3[g])
    qkvg = _attn_proj(xp, attn_norm[0], w_in, tile_gain, tm=PROJ_ROWS, seq=seq)
    part2 = _attn_group(qkvg, 2, n_sub=ATT_SUBS[2], n_res=ATT_RESIDUES[2])
    part1 = _attn_group(qkvg, 1, n_sub=ATT_SUBS[1], n_res=ATT_RESIDUES[1])
    o_gated = _attn_group(qkvg, 0, n_sub=ATT_SUBS[0], n_res=ATT_RESIDUES[0], others=[part1, part2])
    y1p = _outproj(o_gated, w_out, xp, tm=PROJ_ROWS, perm_seq=seq)
    kv_p = []
    for g, d in enumerate(DILATIONS):
        keep = min(d * WINDOW_STEPS, seq)
        block = min(max(keep, 16 * PERM), KV_TAIL_TOKENS)
        kv_p.append(_kv_rows(_kv_tail(qkvg, g, keep=keep, block=block), b))

    caches_t = [jnp.transpose(c[0], (0, 2, 3, 4, 1)).reshape(sb, 2, ATT_WIDTH, c.shape[2])
                for c in (cache_kv_g0, cache_kv_g1, cache_kv_g2)]
    o_s, kn0, kn1, kn2 = _sample_attn(proj_s, caches_t, q_gain3[:, :LANES], k_gain3[:, :LANES])
    y1s = _outproj(o_s, w_out, xs, tm=sb * st)
    kv_s = []
    for g, kn in enumerate((kn0, kn1, kn2)):
        v = proj_s[:, (3 * g + 2) * ATT_WIDTH:(3 * g + 3) * ATT_WIDTH]
        kv_s.append(jnp.stack([kn, v], axis=1).reshape(1, sb, st, 2, N_HEADS, HEAD_DIM))

    w_in2 = ssm_w_in[0].astype(BF16)
    w_dt = jnp.pad(w_in2[:, SSM_MAIN:], ((0, 0), (0, LANES - SSM_HEADS)))
    w_out2 = ssm_w_out[0].astype(BF16)
    pad_h = lambda v: jnp.pad(v.astype(F32), (0, LANES - SSM_HEADS)).reshape(1, LANES)
    e32 = _one_hot_expand(LANES, SSM_D_INNER, SSM_D_INNER // SSM_HEADS)
    gate_gain = ssm_gate_norm[0].reshape(1, -1)
    scan_params = (ssm_conv_w[0], ssm_conv_b[0].reshape(1, -1), pad_h(ssm_dt_bias[0]),
                   pad_h(-jnp.exp(ssm_A_log[0].astype(F32))),
                   jnp.repeat(ssm_D[0].astype(F32), SSM_D_INNER // SSM_HEADS).reshape(1, -1))

    zx_p, dt_p = _proj(y1p, ssm_norm[0], w_in2, tm=PROJ_ROWS, tn=SSM_MAIN // 2, n_out=SSM_MAIN, out_dtype=BF16,
                       w_dt=w_dt)
    yg_p, conv_p, ssm_p = _ssd_prompt(zx_p, dt_p, b, scan_params + (gate_gain, e32), chunks_per_step=SSD_CHUNKS)
    y2p = _outproj(yg_p, w_out2, y1p, tm=PROJ_ROWS)

    eg = (lax.broadcasted_iota(jnp.int32, (SSM_GROUPS * SSM_STATE, LANES), 0) // SSM_STATE
          == lax.broadcasted_iota(jnp.int32, (SSM_GROUPS * SSM_STATE, LANES), 1) // (SSM_HEADS // SSM_GROUPS))
    eg = (eg & (lax.broadcasted_iota(jnp.int32, eg.shape, 1) < SSM_HEADS)).astype(BF16)
    zx_s, dt_s = _proj(y1s, ssm_norm[0], w_in2, tm=sb * st, tn=SSM_MAIN // 2, n_out=SSM_MAIN, out_dtype=F32,
                       w_dt=w_dt)
    yg_s, conv_s, ssm_s = _ssd_sample(zx_s, dt_s, state_conv[0], state_ssm[0].reshape(sb, SSM_D_INNER, SSM_STATE),
                                      scan_params + (gate_gain, e32, eg))
    y2s = _outproj(yg_s, w_out2, y1s, tm=sb * st)

    hp = SSM_D_INNER // SSM_HEADS
    return (y2p.reshape(b, seq, dm), y2s.reshape(sb, st, dm),
            kv_p[0], kv_p[1], kv_p[2], kv_s[0], kv_s[1], kv_s[2],
            conv_p[None], conv_s[None],
            ssm_p.reshape(1, b, SSM_HEADS, hp, SSM_STATE), ssm_s.reshape(1, sb, SSM_HEADS, hp, SSM_STATE))
```
